```python
import jax, jax.numpy as jnp
from jax import lax
import numpy as np

D_MODEL = 2048
BATCH = 1
SEQ = 8192
DEPTH = 2
DEC_BATCH = 128
DEC_SEQ = 1
PAST_LEN = 8192
PAGE_SIZE = 128

MIX_WIDTH = D_MODEL
GROUP_WIDTH = MIX_WIDTH // 4
MLSTM_HEADS = 4
MLSTM_DK = GROUP_WIDTH // MLSTM_HEADS
MLSTM_DV = GROUP_WIDTH // MLSTM_HEADS
MLSTM_CHUNK = 64
CONV_CH = GROUP_WIDTH
CONV_WIDTH = 31
GLA_HEADS = 4
GLA_DK = GROUP_WIDTH // (2 * GLA_HEADS)
GLA_DV = GROUP_WIDTH // GLA_HEADS
GLA_LOWRANK = 16
GLA_TAU = 16.0
GLA_CHUNK = 64
SWA_HEADS = 8
SWA_KV_HEADS = 2
SWA_HEAD_DIM = GROUP_WIDTH // SWA_HEADS
SWA_REP = SWA_HEADS // SWA_KV_HEADS
WINDOW = 128
N_GROUPS = 4
EXPERTS_PER_GROUP = 4
N_EXPERTS = N_GROUPS * EXPERTS_PER_GROUP
TOP_K_IN_GROUP = 2
D_EXPERT = D_MODEL // 2
NORM_EPS = 1e-6
SPLIT_SIZES = (
    MLSTM_HEADS * MLSTM_DK, MLSTM_HEADS * MLSTM_DK, MLSTM_HEADS * MLSTM_DV, MLSTM_HEADS * MLSTM_DV,
    MLSTM_HEADS, MLSTM_HEADS,
    2 * CONV_CH,
    GLA_HEADS * GLA_DK, GLA_HEADS * GLA_DK, GLA_HEADS * GLA_DV, GLA_HEADS * GLA_DV, GLA_LOWRANK,
    SWA_HEADS * SWA_HEAD_DIM, SWA_KV_HEADS * SWA_HEAD_DIM, SWA_KV_HEADS * SWA_HEAD_DIM,
)
N_IN = sum(SPLIT_SIZES)

kernel_name = 'hymba_mlstm_conformer_gla_swa_hmoe_step'


def rms_norm(x, g):
    xf = x.astype(jnp.float32)
    y = xf * lax.rsqrt(jnp.mean(xf * xf, -1, keepdims=True) + NORM_EPS)
    return (y * g.astype(jnp.float32)).astype(x.dtype)


def head_rms_norm(y, g):
    y = y * lax.rsqrt(jnp.mean(y * y, -1, keepdims=True) + NORM_EPS)
    B, T = y.shape[:2]
    return y.reshape(B, T, -1) * g.astype(jnp.float32)


def chunk_len(T, C):
    return C if T % C == 0 else T


def to_chunks(x, L):
    B, T = x.shape[:2]
    x = x.reshape((B, T // L, L) + x.shape[2:])
    return x.transpose((1, 0, 3, 2) + tuple(range(4, x.ndim)))


def from_chunks(y):
    nc, B, H, L = y.shape[:4]
    y = y.transpose((1, 0, 3, 2) + tuple(range(4, y.ndim)))
    return y.reshape((B, nc * L, H) + y.shape[4:])


def mlstm_chunked(q, k, v, i_pre, log_f, C0, n0, m0):
    L = chunk_len(q.shape[1], MLSTM_CHUNK)
    tril = jnp.asarray(np.tril(np.ones((L, L), dtype=bool)))

    def step(carry, inp):
        C, n, m = carry
        qc, kc, vc, ic, fc = inp
        b = jnp.cumsum(fc, axis=-1)
        log_d = jnp.where(tril, b[..., :, None] - b[..., None, :] + ic[..., None, :], -jnp.inf)
        log_inter = b + m[..., None]
        m_t = jnp.maximum(log_inter, jnp.max(log_d, -1))
        w_intra = jnp.einsum('bhtd,bhsd->bhts', qc, kc) * jnp.exp(log_d - m_t[..., None])
        g_inter = jnp.exp(log_inter - m_t)
        num = g_inter[..., None] * jnp.einsum('bhtd,bhde->bhte', qc, C) + jnp.einsum('bhts,bhse->bhte', w_intra, vc)
        den = g_inter * jnp.einsum('bhtd,bhd->bht', qc, n) + jnp.sum(w_intra, -1)
        h = num / jnp.maximum(jnp.abs(den), jnp.exp(-m_t))[..., None]
        b_last = b[..., -1]
        m_last = m_t[..., -1]
        g_state = jnp.exp(b_last + m - m_last)
        w_k = jnp.exp(b_last[..., None] - b + ic - m_last[..., None])
        C_new = g_state[..., None, None] * C + jnp.einsum('bhs,bhsd,bhse->bhde', w_k, kc, vc)
        n_new = g_state[..., None] * n + jnp.einsum('bhs,bhsd->bhd', w_k, kc)
        return (C_new, n_new, m_last), h

    xs = (to_chunks(q, L), to_chunks(k, L), to_chunks(v, L), to_chunks(i_pre, L), to_chunks(log_f, L))
    (C1, n1, m1), h = lax.scan(step, (C0, n0, m0), xs)
    return from_chunks(h), C1, n1, m1


def gla_chunked(q, k, v, log_a, S0):
    L = chunk_len(q.shape[1], GLA_CHUNK)
    tril = jnp.asarray(np.tril(np.ones((L, L), dtype=bool)))[..., None]

    def step(S, inp):
        qc, kc, vc, ac = inp
        b = jnp.cumsum(ac, axis=-2)
        rel = jnp.where(tril, b[..., :, None, :] - b[..., None, :, :], -jnp.inf)
        att = jnp.einsum('bhtd,bhsd,bhtsd->bhts', qc, kc, jnp.exp(rel))
        o = jnp.einsum('bhtd,bhde->bhte', qc * jnp.exp(b), S) + jnp.einsum('bhts,bhse->bhte', att, vc)
        b_last = b[..., -1, :]
        S_new = jnp.exp(b_last)[..., None] * S + jnp.einsum('bhsd,bhse->bhde', kc * jnp.exp(b_last[..., None, :] - b), vc)
        return S_new, o

    xs = (to_chunks(q, L), to_chunks(k, L), to_chunks(v, L), to_chunks(log_a, L))
    S1, o = lax.scan(step, S0, xs)
    return from_chunks(o), S1


def conformer_conv(u, buf, w, bias, g, beta):
    a, gate = jnp.split(u, 2, axis=-1)
    glu = a * jax.nn.sigmoid(gate)
    padded = jnp.concatenate([buf, glu], axis=1)
    y = lax.conv_general_dilated(padded, w.astype(jnp.float32)[:, None, :], window_strides=(1,), padding='VALID',
                                 dimension_numbers=('NWC', 'WIO', 'NWC'), feature_group_count=CONV_CH)
    y = y + bias.astype(jnp.float32)
    mu = jnp.mean(y, -1, keepdims=True)
    var = jnp.mean(jnp.square(y - mu), -1, keepdims=True)
    yn = (y - mu) * lax.rsqrt(var + NORM_EPS) * g.astype(jnp.float32) + beta.astype(jnp.float32)
    return jax.nn.silu(yn), padded[:, -(CONV_WIDTH - 1):]


def sliding_window_attention(q, k, v, k_prev, v_prev, sinks, has_past):
    B, T = q.shape[:2]
    kf = jnp.concatenate([k_prev, k], axis=1)
    vf = jnp.concatenate([v_prev, v], axis=1)
    bq = WINDOW if T % WINDOW == 0 else T
    nb = T // bq
    key_idx = np.arange(nb)[:, None] * bq + np.arange(bq + WINDOW)[None, :]
    q_idx = WINDOW + np.arange(nb)[:, None] * bq + np.arange(bq)[None, :]
    dist = q_idx[:, :, None] - key_idx[:, None, :]
    valid = (dist >= 0) & (dist < WINDOW)
    if not has_past:
        valid = valid & (key_idx[:, None, :] >= WINDOW)
    kb = kf[:, key_idx]
    vb = vf[:, key_idx]
    qb = q.reshape(B, nb, bq, SWA_KV_HEADS, SWA_REP, SWA_HEAD_DIM)
    s = jnp.einsum('bnqgrd,bnkgd->bngrqk', qb, kb) * (SWA_HEAD_DIM ** -0.5)
    s = jnp.where(valid[None, :, None, None], s, -jnp.inf)
    sink = sinks.astype(jnp.float32).reshape(SWA_KV_HEADS, SWA_REP)[None, None, :, :, None, None]
    mx = jnp.maximum(jnp.max(s, -1, keepdims=True), sink)
    p = jnp.exp(s - mx)
    p = p / (jnp.sum(p, -1, keepdims=True) + jnp.exp(sink - mx))
    o = jnp.einsum('bngrqk,bnkgd->bnqgrd', p, vb).reshape(B, T, SWA_HEADS * SWA_HEAD_DIM)
    return o, kf[:, -WINDOW:], vf[:, -WINDOW:]


def token_mixers(h, state, prm, has_past):
    B, T, _ = h.shape
    f32 = jnp.float32
    proj = jnp.einsum('btd,dn->btn', h, prm['w_in']).astype(f32)
    cuts = [int(c) for c in np.cumsum(SPLIT_SIZES)[:-1]]
    (a_q, a_k, a_v, a_o, a_i, a_f, c_u, g_q, g_k, g_v, g_r, g_lr, s_q, s_k, s_v) = jnp.split(proj, cuts, axis=-1)
    C0, n0, m0, S0, conv0, k0, v0 = [s.astype(f32) for s in state]

    q = a_q.reshape(B, T, MLSTM_HEADS, MLSTM_DK)
    k = a_k.reshape(B, T, MLSTM_HEADS, MLSTM_DK) * (MLSTM_DK ** -0.5)
    v = a_v.reshape(B, T, MLSTM_HEADS, MLSTM_DV)
    i_pre = a_i + prm['mlstm_b_i'].astype(f32)
    log_f = jax.nn.log_sigmoid(a_f + prm['mlstm_b_f'].astype(f32))
    h_a, C1, n1, m1 = mlstm_chunked(q, k, v, i_pre, log_f, C0, n0, m0)
    h_a = jax.nn.sigmoid(a_o).reshape(B, T, MLSTM_HEADS, MLSTM_DV) * h_a
    y_a = head_rms_norm(h_a, prm['mlstm_norm_g'])

    y_b, conv1 = conformer_conv(c_u, conv0, prm['conv_w'], prm['conv_b'], prm['conv_norm_g'], prm['conv_norm_b'])

    gq = g_q.reshape(B, T, GLA_HEADS, GLA_DK) * (GLA_DK ** -0.5)
    gk = g_k.reshape(B, T, GLA_HEADS, GLA_DK)
    gv = g_v.reshape(B, T, GLA_HEADS, GLA_DV)
    gate_pre = jnp.einsum('btr,rn->btn', g_lr, prm['gla_w_gate'].astype(f32)) + prm['gla_b_gate'].astype(f32)
    log_a = jax.nn.log_sigmoid(gate_pre).reshape(B, T, GLA_HEADS, GLA_DK) / GLA_TAU
    o_c, S1 = gla_chunked(gq, gk, gv, log_a, S0)
    y_c = head_rms_norm(o_c, prm['gla_norm_g']) * jax.nn.silu(g_r)

    sq = s_q.reshape(B, T, SWA_HEADS, SWA_HEAD_DIM)
    sk = s_k.reshape(B, T, SWA_KV_HEADS, SWA_HEAD_DIM)
    sv = s_v.reshape(B, T, SWA_KV_HEADS, SWA_HEAD_DIM)
    y_d, k1, v1 = sliding_window_attention(sq, sk, sv, k0, v0, prm['swa_sinks'], has_past)

    mixed = jnp.concatenate([y_a, y_b, y_c, y_d], axis=-1).astype(h.dtype)
    out = jnp.einsum('btm,md->btd', mixed, prm['w_out'])
    return out, (C1, n1, m1, S1, conv1, k1, v1)


def hier_moe(h, prm):
    B, T, D = h.shape
    f32 = jnp.float32
    xt = h.reshape(B * T, D)
    g_logits = (xt @ prm['router_group_w']).astype(f32) + prm['router_group_b'].astype(f32)
    g_prob = jax.nn.softmax(g_logits, axis=-1)
    g_sel = jnp.argmax(g_logits, axis=-1)
    g_w = jnp.take_along_axis(g_prob, g_sel[:, None], axis=-1)
    e_logits = ((xt @ prm['router_expert_w']).astype(f32) + prm['router_expert_b'].astype(f32))
    e_logits = e_logits.reshape(-1, N_GROUPS, EXPERTS_PER_GROUP)
    e_in = jnp.take_along_axis(e_logits, g_sel[:, None, None], axis=1)[:, 0]
    e_prob = jax.nn.softmax(e_in, axis=-1)
    top_p, top_i = lax.top_k(e_prob, TOP_K_IN_GROUP)
    top_p = top_p / jnp.sum(top_p, -1, keepdims=True)
    expert_id = g_sel[:, None] * EXPERTS_PER_GROUP + top_i
    gates = jnp.einsum('nk,nke->ne', g_w * top_p, jax.nn.one_hot(expert_id, N_EXPERTS, dtype=f32))
    y = jnp.zeros((B * T, D), f32)
    for e in range(N_EXPERTS):
        a = xt @ prm['expert_w_gate'][e]
        u = xt @ prm['expert_w_up'][e]
        y = y + gates[:, e:e + 1] * ((jax.nn.silu(a) * u) @ prm['expert_w_down'][e]).astype(f32)
    return y.reshape(B, T, D).astype(h.dtype)


def layer(x, state, prm, has_past):
    mix, new_state = token_mixers(rms_norm(x, prm['norm_mix_g']), state, prm, has_past)
    x = x + mix
    x = x + hier_moe(rms_norm(x, prm['norm_ffn_g']), prm)
    return x, new_state


def empty_state(B):
    z = jnp.zeros
    return (z((B, MLSTM_HEADS, MLSTM_DK, MLSTM_DV), jnp.float32), z((B, MLSTM_HEADS, MLSTM_DK), jnp.float32),
            z((B, MLSTM_HEADS), jnp.float32), z((B, GLA_HEADS, GLA_DK, GLA_DV), jnp.float32),
            z((B, CONV_WIDTH - 1, CONV_CH), jnp.float32),
            z((B, WINDOW, SWA_KV_HEADS, SWA_HEAD_DIM), jnp.float32),
            z((B, WINDOW, SWA_KV_HEADS, SWA_HEAD_DIM), jnp.float32))


def stack_states(states):
    return [jnp.stack(parts) for parts in zip(*states)]


def setup_inputs(seed: int = 0) -> dict:
    key = jax.random.key(seed)
    ks = jax.random.split(key, 32)

    def nrm(k, shape, scale):
        return jax.random.normal(k, shape, jnp.float32) * scale

    return {
        'x_prompt': nrm(ks[0], (BATCH, SEQ, D_MODEL), 1.0),
        'x_sample': nrm(ks[1], (DEC_BATCH, DEC_SEQ, D_MODEL), 1.0),
        'state_mlstm_C': nrm(ks[2], (DEPTH, DEC_BATCH, MLSTM_HEADS, MLSTM_DK, MLSTM_DV), 0.3),
        'state_mlstm_n': nrm(ks[3], (DEPTH, DEC_BATCH, MLSTM_HEADS, MLSTM_DK), 0.3),
        'state_mlstm_m': nrm(ks[4], (DEPTH, DEC_BATCH, MLSTM_HEADS), 0.5),
        'state_gla_S': nrm(ks[5], (DEPTH, DEC_BATCH, GLA_HEADS, GLA_DK, GLA_DV), 0.3),
        'cache_conv': nrm(ks[6], (DEPTH, DEC_BATCH, CONV_WIDTH - 1, CONV_CH), 0.5),
        'cache_swa_k': nrm(ks[7], (DEPTH, DEC_BATCH, WINDOW, SWA_KV_HEADS, SWA_HEAD_DIM), 1.0),
        'cache_swa_v': nrm(ks[8], (DEPTH, DEC_BATCH, WINDOW, SWA_KV_HEADS, SWA_HEAD_DIM), 1.0),
        'norm_mix_g': 1.0 + nrm(ks[9], (DEPTH, D_MODEL), 0.02),
        'w_in': nrm(ks[10], (DEPTH, D_MODEL, N_IN), D_MODEL ** -0.5),
        'mlstm_b_i': nrm(ks[11], (DEPTH, MLSTM_HEADS), 0.1),
        'mlstm_b_f': 3.0 + nrm(ks[12], (DEPTH, MLSTM_HEADS), 0.5),
        'mlstm_norm_g': 1.0 + nrm(ks[13], (DEPTH, MLSTM_HEADS * MLSTM_DV), 0.02),
        'conv_w': nrm(ks[14], (DEPTH, CONV_WIDTH, CONV_CH), CONV_WIDTH ** -0.5),
        'conv_b': nrm(ks[15], (DEPTH, CONV_CH), 0.02),
        'conv_norm_g': 1.0 + nrm(ks[16], (DEPTH, CONV_CH), 0.02),
        'conv_norm_b': nrm(ks[17], (DEPTH, CONV_CH), 0.02),
        'gla_w_gate': nrm(ks[18], (DEPTH, GLA_LOWRANK, GLA_HEADS * GLA_DK), GLA_LOWRANK ** -0.5),
        'gla_b_gate': nrm(ks[19], (DEPTH, GLA_HEADS * GLA_DK), 0.1),
        'gla_norm_g': 1.0 + nrm(ks[20], (DEPTH, GLA_HEADS * GLA_DV), 0.02),
        'swa_sinks': nrm(ks[21], (DEPTH, SWA_HEADS), 0.5),
        'w_out': nrm(ks[22], (DEPTH, MIX_WIDTH, D_MODEL), MIX_WIDTH ** -0.5),
        'norm_ffn_g': 1.0 + nrm(ks[23], (DEPTH, D_MODEL), 0.02),
        'router_group_w': nrm(ks[24], (DEPTH, D_MODEL, N_GROUPS), D_MODEL ** -0.5),
        'router_group_b': nrm(ks[25], (DEPTH, N_GROUPS), 0.01),
        'router_expert_w': nrm(ks[26], (DEPTH, D_MODEL, N_EXPERTS), D_MODEL ** -0.5),
        'router_expert_b': nrm(ks[27], (DEPTH, N_EXPERTS), 0.01),
        'expert_w_gate': nrm(ks[28], (DEPTH, N_EXPERTS, D_MODEL, D_EXPERT), D_MODEL ** -0.5),
        'expert_w_up': nrm(ks[29], (DEPTH, N_EXPERTS, D_MODEL, D_EXPERT), D_MODEL ** -0.5),
        'expert_w_down': nrm(ks[30], (DEPTH, N_EXPERTS, D_EXPERT, D_MODEL), D_EXPERT ** -0.5),
        'final_norm_g': 1.0 + nrm(ks[31], (D_MODEL,), 0.02),
    }


def reference(x_prompt, x_sample, state_mlstm_C, state_mlstm_n, state_mlstm_m, state_gla_S, cache_conv,
              cache_swa_k, cache_swa_v, norm_mix_g, w_in, mlstm_b_i, mlstm_b_f, mlstm_norm_g, conv_w, conv_b,
              conv_norm_g, conv_norm_b, gla_w_gate, gla_b_gate, gla_norm_g, swa_sinks, w_out, norm_ffn_g,
              router_group_w, router_group_b, router_expert_w, router_expert_b, expert_w_gate, expert_w_up,
              expert_w_down, final_norm_g):
    xp, xs = x_prompt, x_sample
    new_p, new_s = [], []
    for l in range(DEPTH):
        prm = {
            'norm_mix_g': norm_mix_g[l], 'w_in': w_in[l], 'mlstm_b_i': mlstm_b_i[l], 'mlstm_b_f': mlstm_b_f[l],
            'mlstm_norm_g': mlstm_norm_g[l], 'conv_w': conv_w[l], 'conv_b': conv_b[l],
            'conv_norm_g': conv_norm_g[l], 'conv_norm_b': conv_norm_b[l], 'gla_w_gate': gla_w_gate[l],
            'gla_b_gate': gla_b_gate[l], 'gla_norm_g': gla_norm_g[l], 'swa_sinks': swa_sinks[l],
            'w_out': w_out[l], 'norm_ffn_g': norm_ffn_g[l], 'router_group_w': router_group_w[l],
            'router_group_b': router_group_b[l], 'router_expert_w': router_expert_w[l],
            'router_expert_b': router_expert_b[l], 'expert_w_gate': expert_w_gate[l],
            'expert_w_up': expert_w_up[l], 'expert_w_down': expert_w_down[l],
        }
        st_s = (state_mlstm_C[l], state_mlstm_n[l], state_mlstm_m[l], state_gla_S[l], cache_conv[l],
                cache_swa_k[l], cache_swa_v[l])
        xp, sp = layer(xp, empty_state(xp.shape[0]), prm, False)
        xs, ss = layer(xs, st_s, prm, True)
        new_p.append(sp)
        new_s.append(ss)
    y_prompt = rms_norm(xp, final_norm_g)
    y_sample = rms_norm(xs, final_norm_g)
    p_mlstm_C, p_mlstm_n, p_mlstm_m, p_gla_S, p_conv, p_swa_k, p_swa_v = stack_states(new_p)
    s_mlstm_C, s_mlstm_n, s_mlstm_m, s_gla_S, s_conv, s_swa_k, s_swa_v = stack_states(new_s)
    return (y_prompt, y_sample, p_mlstm_C, p_mlstm_n, p_mlstm_m, p_gla_S, p_conv, p_swa_k, p_swa_v,
            s_mlstm_C, s_mlstm_n, s_mlstm_m, s_gla_S, s_conv, s_swa_k, s_swa_v)
```

```python
import functools

import jax
import jax.numpy as jnp
from jax import lax
from jax.experimental import pallas as pl
from jax.experimental.pallas import tpu as pltpu

F32 = jnp.float32
BF16 = jnp.bfloat16
I32 = jnp.int32
HIGHEST = lax.Precision.HIGHEST

D_MODEL = 2048
GROUP_WIDTH = 512
HEAD_W = 128
MLSTM_HEADS = 4
GLA_HEADS = 4
GLA_DK = 64
GLA_LOWRANK = 16
GLA_TAU = 16.0
CONV_WIDTH = 31
SWA_HEADS = 8
SWA_KV_HEADS = 2
SWA_HEAD_DIM = 64
WINDOW = 128
N_GROUPS = 4
EXPERTS_PER_GROUP = 4
N_EXPERTS = 16
D_EXPERT = 1024
NORM_EPS = 1e-6
LANES = 128
SUBLANES = 8

A_Q, A_K, A_V, A_O = 0, 512, 1024, 1536
C_A, C_G = 2048, 2560
G_V, G_R = 3072, 3584
S_Q = 4096
G_Q, G_K = 4608, 4864
A_G, G_LR, S_K, S_V = 5120, 5248, 5376, 5504
N_PROJ = 5632
_W_IN_SEGMENTS = (
    (0, 512, A_Q), (512, 1024, A_K), (1024, 1536, A_V), (1536, 2048, A_O),
    (2048, 2056, A_G),
    (2056, 2568, C_A), (2568, 3080, C_G),
    (3080, 3336, G_Q), (3336, 3592, G_K), (3592, 4104, G_V), (4104, 4616, G_R),
    (4616, 4632, G_LR),
    (4632, 5144, S_Q), (5144, 5272, S_K), (5272, 5400, S_V),
)

MLSTM_CHUNK = 128
GLA_CHUNK = 64
GLA_SUB = 16
CONV_TILE = 256
CONV_ROWS = 64
CONV_HALO = 32
SAMPLE_BB = 8
MOE_TILE = 512
VMEM_LIMIT = 56 * 1024 * 1024


def _cparams(sem, vmem=VMEM_LIMIT):
    return pltpu.CompilerParams(dimension_semantics=sem, vmem_limit_bytes=vmem)


def _log_sigmoid(x):
    return jnp.minimum(x, 0.0) - jnp.log1p(jnp.exp(-jnp.abs(x)))


def _sigmoid(x):
    return 1.0 / (1.0 + jnp.exp(-x))


def _silu(x):
    return x * _sigmoid(x)


def _pick_tile(n, candidates):
    for c in candidates:
        if n % c == 0:
            return c
    raise ValueError(f"no tile for {n} in {candidates}")


def _proj_kernel(x_ref, g_ref, w_ref, o_ref, hn_ref):
    @pl.when(pl.program_id(1) == 0)
    def _():
        x = x_ref[...]
        ms = jnp.mean(x * x, axis=-1, keepdims=True)
        hn_ref[...] = (x * lax.rsqrt(ms + NORM_EPS) * g_ref[...]).astype(BF16)

    o_ref[...] = jnp.dot(hn_ref[...], w_ref[...], preferred_element_type=F32)


def _in_projection(x, g, w_bf16):
    n = x.shape[0]
    tm = _pick_tile(n, (640, 512, 256, 128, 64, 8))
    tn = 512
    return pl.pallas_call(
        _proj_kernel,
        grid=(n // tm, N_PROJ // tn),
        in_specs=[
            pl.BlockSpec((tm, D_MODEL), lambda i, j: (i, 0)),
            pl.BlockSpec((1, D_MODEL), lambda i, j: (0, 0)),
            pl.BlockSpec((D_MODEL, tn), lambda i, j: (0, j)),
        ],
        out_specs=pl.BlockSpec((tm, tn), lambda i, j: (i, j)),
        out_shape=jax.ShapeDtypeStruct((n, N_PROJ), F32),
        scratch_shapes=[pltpu.VMEM((tm, D_MODEL), BF16)],
        compiler_params=_cparams(("parallel", "arbitrary")),
        name="in_projection",
    )(x, g.reshape(1, D_MODEL), w_bf16)


def _mlstm_prompt_kernel(q_ref, k_ref, v_ref, o_ref, gt_ref, bias_ref, ng_ref,
                         y_ref, cn_ref, m_ref):
    L = q_ref.shape[0]

    @pl.when(pl.program_id(0) == 0)
    def _():
        cn_ref[...] = jnp.zeros_like(cn_ref)
        m_ref[...] = jnp.zeros_like(m_ref)

    pre = gt_ref[...] + bias_ref[...]
    lf = _log_sigmoid(pre)
    row = lax.broadcasted_iota(I32, (L, L), 0)
    col = lax.broadcasted_iota(I32, (L, L), 1)
    tri = col <= row
    b_all = jnp.dot(tri.astype(F32), lf, precision=HIGHEST, preferred_element_type=F32)
    pre_t = pre.T
    b_t = b_all.T
    lane = lax.broadcasted_iota(I32, (L, HEAD_W), 1)
    ones_col = (lane == 0).astype(BF16)
    for h in range(MLSTM_HEADS):
        sl = slice(HEAD_W * h, HEAD_W * (h + 1))
        q = q_ref[:, sl]
        k = k_ref[:, sl] * (HEAD_W ** -0.5)
        v = v_ref[:, sl]
        b_col = b_all[:, 4 + h:5 + h]
        i_col = pre[:, h:h + 1]
        b_row = b_t[4 + h:5 + h, :]
        i_row = pre_t[h:h + 1, :]
        m_prev = m_ref[h:h + 1, 0:1]
        log_d = jnp.where(tri, b_col - b_row + i_row, -jnp.inf)
        log_inter = b_col + m_prev
        m_t = jnp.maximum(log_inter, jnp.max(log_d, axis=-1, keepdims=True))
        d_mat = jnp.exp(log_d - m_t)
        g_inter = jnp.exp(log_inter - m_t)
        qb = q.astype(BF16)
        kb = k.astype(BF16)
        s = lax.dot_general(qb, kb, (((1,), (1,)), ((), ())), preferred_element_type=F32)
        w = (s * d_mat).astype(BF16)
        v1 = jnp.concatenate([v.astype(BF16), ones_col], axis=1)
        cn = cn_ref[h]
        nd = g_inter * jnp.dot(qb, cn.astype(BF16), preferred_element_type=F32)
        nd = nd + jnp.dot(w, v1, preferred_element_type=F32)
        num = nd[:, :HEAD_W]
        den = nd[:, HEAD_W:HEAD_W + 1]
        hh = num / jnp.maximum(jnp.abs(den), jnp.exp(-m_t))
        hh = _sigmoid(o_ref[:, sl]) * hh
        hh = hh * lax.rsqrt(jnp.mean(hh * hh, axis=-1, keepdims=True) + NORM_EPS) * ng_ref[:, sl]
        y_ref[:, sl] = hh.astype(y_ref.dtype)
        m_last = m_t[L - 1:L, :]
        b_last = b_col[L - 1:L, :]
        g_state = jnp.exp(b_last + m_prev - m_last)
        w_k = jnp.exp(b_last - b_col + i_col - m_last)
        kw = (k * w_k).astype(BF16)
        upd = lax.dot_general(kw, v1, (((0,), (0,)), ((), ())), preferred_element_type=F32)
        cn_ref[h] = g_state * cn + upd
        m_ref[h:h + 1, :] = jnp.broadcast_to(m_last, (1, LANES))


def _mlstm_prompt(proj, t_len, bias_row, norm_g):
    L = _pick_tile(t_len, (MLSTM_CHUNK, 64, 32, 16, 8))

    def col(off):
        return pl.BlockSpec((L, GROUP_WIDTH), lambda i, o=off: (i, o // GROUP_WIDTH))

    return pl.pallas_call(
        _mlstm_prompt_kernel,
        grid=(t_len // L,),
        in_specs=[
            col(A_Q), col(A_K), col(A_V), col(A_O),
            pl.BlockSpec((L, LANES), lambda i: (i, A_G // LANES)),
            pl.BlockSpec((1, LANES), lambda i: (0, 0)),
            pl.BlockSpec((1, GROUP_WIDTH), lambda i: (0, 0)),
        ],
        out_specs=[
            pl.BlockSpec((L, GROUP_WIDTH), lambda i: (i, 0)),
            pl.BlockSpec((MLSTM_HEADS, HEAD_W, 2 * HEAD_W), lambda i: (0, 0, 0)),
            pl.BlockSpec((SUBLANES, LANES), lambda i: (0, 0)),
        ],
        out_shape=[
            jax.ShapeDtypeStruct((t_len, GROUP_WIDTH), BF16),
            jax.ShapeDtypeStruct((MLSTM_HEADS, HEAD_W, 2 * HEAD_W), F32),
            jax.ShapeDtypeStruct((SUBLANES, LANES), F32),
        ],
        compiler_params=_cparams(("arbitrary",)),
        name="mlstm_prompt",
    )(proj, proj, proj, proj, proj, bias_row, norm_g.reshape(1, GROUP_WIDTH))


def _gla_prompt_kernel(q_ref, k_ref, v_ref, r_ref, lr_ref, wg_ref, bg_ref, ng_ref, y_ref, sp_ref):
    L = q_ref.shape[0]
    n_sub = L // GLA_SUB

    @pl.when(pl.program_id(0) == 0)
    def _():
        sp_ref[...] = jnp.zeros_like(sp_ref)

    gate_pre = jnp.dot(lr_ref[...].astype(BF16), wg_ref[...], preferred_element_type=F32) + bg_ref[...]
    log_a = _log_sigmoid(gate_pre) * (1.0 / GLA_TAU)
    row = lax.broadcasted_iota(I32, (L, L), 0)
    col = lax.broadcasted_iota(I32, (L, L), 1)
    b = jnp.dot((col <= row).astype(F32), log_a, precision=HIGHEST, preferred_element_type=F32)
    q = q_ref[...] * (GLA_DK ** -0.5)
    k = k_ref[...]
    b_last = b[L - 1:L, :]
    q_in = q * jnp.exp(b)
    k_dec = k * jnp.exp(b_last - b)
    lane16 = lax.broadcasted_iota(I32, (GLA_SUB, LANES), 1)
    lo16 = lane16 < GLA_DK
    srow = lax.broadcasted_iota(I32, (LANES, 2 * HEAD_W), 0)
    scol = lax.broadcasted_iota(I32, (LANES, 2 * HEAD_W), 1)
    block_diag = (srow < GLA_DK) == (scol < HEAD_W)
    for p in range(2):
        pls = slice(LANES * p, LANES * (p + 1))
        sp = sp_ref[p]
        vp = v_ref[:, 2 * HEAD_W * p:2 * HEAD_W * (p + 1)].astype(BF16)
        o_inter = jnp.dot(q_in[:, pls].astype(BF16), sp.astype(BF16), preferred_element_type=F32)
        rows = []
        for blk in range(n_sub):
            r0 = GLA_SUB * blk
            n = GLA_SUB * (blk + 1)
            if blk == 0:
                qs = q[r0:r0 + GLA_SUB, pls] * jnp.exp(b[r0:r0 + GLA_SUB, pls])
                ks = k[:n, pls] * jnp.exp(-b[:n, pls])
            else:
                anchor = b[r0 - 1:r0, pls]
                qs = q[r0:r0 + GLA_SUB, pls] * jnp.exp(b[r0:r0 + GLA_SUB, pls] - anchor)
                ks = k[:n, pls] * jnp.exp(anchor - b[:n, pls])
            qs2 = jnp.concatenate([jnp.where(lo16, qs, 0.0), jnp.where(lo16, 0.0, qs)], axis=0)
            att = lax.dot_general(qs2.astype(BF16), ks.astype(BF16), (((1,), (1,)), ((), ())),
                                  preferred_element_type=F32)
            trow = lax.broadcasted_iota(I32, (2 * GLA_SUB, n), 0)
            tcol = lax.broadcasted_iota(I32, (2 * GLA_SUB, n), 1)
            t_idx = r0 + jnp.where(trow >= GLA_SUB, trow - GLA_SUB, trow)
            att = jnp.where(tcol <= t_idx, att, 0.0)
            o2 = jnp.dot(att.astype(BF16), vp[:n], preferred_element_type=F32)
            rows.append(jnp.concatenate([o2[:GLA_SUB, :HEAD_W], o2[GLA_SUB:, HEAD_W:]], axis=1))
        o = o_inter + jnp.concatenate(rows, axis=0)
        for hh in range(2):
            head = 2 * p + hh
            hs = slice(HEAD_W * head, HEAD_W * (head + 1))
            oh = o[:, HEAD_W * hh:HEAD_W * (hh + 1)]
            oh = oh * lax.rsqrt(jnp.mean(oh * oh, axis=-1, keepdims=True) + NORM_EPS) * ng_ref[:, hs]
            y_ref[:, hs] = (oh * _silu(r_ref[:, hs])).astype(y_ref.dtype)
        dec_col = jnp.exp(b[:, pls].T[:, L - 1:L])
        upd = lax.dot_general(k_dec[:, pls].astype(BF16), vp, (((0,), (0,)), ((), ())),
                              preferred_element_type=F32)
        sp_ref[p] = jnp.where(block_diag, dec_col * sp + upd, 0.0)


def _gla_prompt(proj, t_len, w_gate_pad, b_gate, norm_g):
    L = _pick_tile(t_len, (GLA_CHUNK,))
    return pl.pallas_call(
        _gla_prompt_kernel,
        grid=(t_len // L,),
        in_specs=[
            pl.BlockSpec((L, 256), lambda i: (i, G_Q // 256)),
            pl.BlockSpec((L, 256), lambda i: (i, G_K // 256)),
            pl.BlockSpec((L, GROUP_WIDTH), lambda i: (i, G_V // GROUP_WIDTH)),
            pl.BlockSpec((L, GROUP_WIDTH), lambda i: (i, G_R // GROUP_WIDTH)),
            pl.BlockSpec((L, LANES), lambda i: (i, G_LR // LANES)),
            pl.BlockSpec((LANES, 256), lambda i: (0, 0)),
            pl.BlockSpec((1, 256), lambda i: (0, 0)),
            pl.BlockSpec((1, GROUP_WIDTH), lambda i: (0, 0)),
        ],
        out_specs=[
            pl.BlockSpec((L, GROUP_WIDTH), lambda i: (i, 0)),
            pl.BlockSpec((2, LANES, 2 * HEAD_W), lambda i: (0, 0, 0)),
        ],
        out_shape=[
            jax.ShapeDtypeStruct((t_len, GROUP_WIDTH), BF16),
            jax.ShapeDtypeStruct((2, LANES, 2 * HEAD_W), F32),
        ],
        compiler_params=_cparams(("arbitrary",)),
        name="gla_prompt",
    )(proj, proj, proj, proj, proj, w_gate_pad, b_gate.reshape(1, 256), norm_g.reshape(1, GROUP_WIDTH))


def _conv_norm_act(y, g_ref, be_ref):
    mu = jnp.mean(y, axis=-1, keepdims=True)
    yc = y - mu
    var = jnp.mean(yc * yc, axis=-1, keepdims=True)
    return _silu(yc * lax.rsqrt(var + NORM_EPS) * g_ref[...] + be_ref[...])


def _conv_prompt_kernel(ua_ref, ug_ref, ha_ref, hg_ref, w_ref, b_ref, g_ref, be_ref,
                        y_ref, tail_ref, buf_ref):
    tt = ua_ref.shape[0]
    halo = ha_ref[...] * _sigmoid(hg_ref[...])
    buf_ref[0:CONV_HALO, :] = jnp.where(pl.program_id(0) > 0, halo, 0.0)
    buf_ref[CONV_HALO:CONV_HALO + tt, :] = ua_ref[...] * _sigmoid(ug_ref[...])
    base = CONV_HALO - (CONV_WIDTH - 1)
    for r in range(tt // CONV_ROWS):
        acc = jnp.zeros((CONV_ROWS, GROUP_WIDTH), F32)
        for j in range(CONV_WIDTH):
            s0 = r * CONV_ROWS + base + j
            acc = acc + w_ref[j:j + 1, :] * buf_ref[s0:s0 + CONV_ROWS, :]
        y = _conv_norm_act(acc + b_ref[...], g_ref, be_ref)
        y_ref[r * CONV_ROWS:(r + 1) * CONV_ROWS, :] = y.astype(y_ref.dtype)
    tail_ref[...] = buf_ref[tt:tt + CONV_HALO, :]


def _conv_prompt(proj, t_len, w, b, g, beta):
    tt = _pick_tile(t_len, (CONV_TILE, 128, 64))
    ratio = tt // CONV_HALO
    vec = lambda: pl.BlockSpec((1, GROUP_WIDTH), lambda i: (0, 0))
    return pl.pallas_call(
        _conv_prompt_kernel,
        grid=(t_len // tt,),
        in_specs=[
            pl.BlockSpec((tt, GROUP_WIDTH), lambda i: (i, C_A // GROUP_WIDTH)),
            pl.BlockSpec((tt, GROUP_WIDTH), lambda i: (i, C_G // GROUP_WIDTH)),
            pl.BlockSpec((CONV_HALO, GROUP_WIDTH), lambda i: (jnp.maximum(i * ratio - 1, 0), C_A // GROUP_WIDTH)),
            pl.BlockSpec((CONV_HALO, GROUP_WIDTH), lambda i: (jnp.maximum(i * ratio - 1, 0), C_G // GROUP_WIDTH)),
            pl.BlockSpec((CONV_WIDTH, GROUP_WIDTH), lambda i: (0, 0)),
            vec(), vec(), vec(),
        ],
        out_specs=[
            pl.BlockSpec((tt, GROUP_WIDTH), lambda i: (i, 0)),
            pl.BlockSpec((CONV_HALO, GROUP_WIDTH), lambda i: (0, 0)),
        ],
        out_shape=[
            jax.ShapeDtypeStruct((t_len, GROUP_WIDTH), BF16),
            jax.ShapeDtypeStruct((CONV_HALO, GROUP_WIDTH), F32),
        ],
        scratch_shapes=[pltpu.VMEM((tt + CONV_HALO, GROUP_WIDTH), F32)],
        compiler_params=_cparams(("arbitrary",)),
        name="conv_prompt",
    )(proj, proj, proj, proj, w, b.reshape(1, -1), g.reshape(1, -1), beta.reshape(1, -1))


def _swa_prompt_kernel(sink_ref, q_ref, kc_ref, vc_ref, kp_ref, vp_ref, y_ref):
    bq = q_ref.shape[0]
    first = pl.program_id(0) == 0
    k_all = jnp.concatenate([kp_ref[...], kc_ref[...]], axis=0)
    v_all = jnp.concatenate([vp_ref[...], vc_ref[...]], axis=0)
    k_sw = pltpu.roll(k_all, SWA_HEAD_DIM, 1).astype(BF16)
    v_sw = pltpu.roll(v_all, SWA_HEAD_DIM, 1).astype(BF16)
    k_all = k_all.astype(BF16)
    v_all = v_all.astype(BF16)
    tq = lax.broadcasted_iota(I32, (bq, 2 * bq), 0)
    kj = lax.broadcasted_iota(I32, (bq, 2 * bq), 1)
    valid = (kj > tq) & (kj <= tq + WINDOW) & (kj >= jnp.where(first, bq, 0))
    lane = lax.broadcasted_iota(I32, (bq, LANES), 1)
    lo = lane < SWA_HEAD_DIM
    rep = SWA_HEADS // SWA_KV_HEADS
    for c in range(SWA_HEADS // 2):
        qc = q_ref[:, LANES * c:LANES * (c + 1)] * (SWA_HEAD_DIM ** -0.5)
        outs = []
        for hh in range(2):
            h = 2 * c + hh
            g = h // rep
            qm = jnp.where(lo if hh == 0 else jnp.logical_not(lo), qc, 0.0).astype(BF16)
            k_use = k_all if g == hh else k_sw
            v_use = v_all if g == hh else v_sw
            s = lax.dot_general(qm, k_use, (((1,), (1,)), ((), ())), preferred_element_type=F32)
            s = jnp.where(valid, s, -jnp.inf)
            sink = sink_ref[h]
            mx = jnp.maximum(jnp.max(s, axis=-1, keepdims=True), sink)
            p = jnp.exp(s - mx)
            den = jnp.sum(p, axis=-1, keepdims=True) + jnp.exp(sink - mx)
            p = (p / den).astype(BF16)
            outs.append(jnp.dot(p, v_use, preferred_element_type=F32))
        y_ref[:, LANES * c:LANES * (c + 1)] = jnp.where(lo, outs[0], outs[1]).astype(y_ref.dtype)


def _swa_prompt(proj, t_len, sinks):
    bq = WINDOW
    assert t_len % bq == 0
    kv = lambda off, prev: pl.BlockSpec(
        (bq, LANES), (lambda i: (jnp.maximum(i - 1, 0), off // LANES)) if prev else (lambda i: (i, off // LANES)))
    return pl.pallas_call(
        _swa_prompt_kernel,
        grid=(t_len // bq,),
        in_specs=[
            pl.BlockSpec(memory_space=pltpu.SMEM),
            pl.BlockSpec((bq, GROUP_WIDTH), lambda i: (i, S_Q // GROUP_WIDTH)),
            kv(S_K, False), kv(S_V, False), kv(S_K, True), kv(S_V, True),
        ],
        out_specs=pl.BlockSpec((bq, GROUP_WIDTH), lambda i: (i, 0)),
        out_shape=jax.ShapeDtypeStruct((t_len, GROUP_WIDTH), BF16),
        compiler_params=_cparams(("parallel",)),
        name="swa_prompt",
    )(sinks, proj, proj, proj, proj, proj)


_T_MK, _T_MQ = 0, 512
_T_GA, _T_GK, _T_GQ = 1024, 1280, 1536
_T_ROWS = 1792


def _sample_kernel(ps_ref, mm_ref, bias_ref, n0_ref, c0_ref, s0_ref, cv0_ref, k0_ref, v0_ref,
                   qm_ref, sink_ref, wg_ref, bg_ref, mng_ref, gng_ref, cw_ref, cb_ref, cg_ref, cbe_ref,
                   y_ref, od_ref, c1_ref, n1_ref, m1_ref, s1_ref, cv1_ref, k1_ref, v1_ref,
                   tt_ref, bc_ref, num_ref, go_ref, yc_ref):
    i = pl.program_id(0)
    nb = ps_ref.shape[0]
    bb = y_ref.shape[0]

    def gla_gate(lr):
        gp = jnp.dot(lr.astype(BF16), wg_ref[...], preferred_element_type=F32) + bg_ref[...]
        return jnp.exp(_log_sigmoid(gp) * (1.0 / GLA_TAU))

    @pl.when(i == 0)
    def _():
        for h in range(MLSTM_HEADS):
            kk = ps_ref[:, A_K + HEAD_W * h:A_K + HEAD_W * (h + 1)] * (HEAD_W ** -0.5)
            tt_ref[_T_MK + HEAD_W * h:_T_MK + HEAD_W * (h + 1), :] = kk.T.astype(BF16)
            qq = ps_ref[:, A_Q + HEAD_W * h:A_Q + HEAD_W * (h + 1)]
            tt_ref[_T_MQ + HEAD_W * h:_T_MQ + HEAD_W * (h + 1), :] = qq.T.astype(BF16)
        a_all = gla_gate(ps_ref[:, G_LR:G_LR + LANES])
        for p in range(2):
            pls = slice(LANES * p, LANES * (p + 1))
            tt_ref[_T_GA + LANES * p:_T_GA + LANES * (p + 1), :] = a_all[:, pls].T.astype(BF16)
            kk = ps_ref[:, G_K + LANES * p:G_K + LANES * (p + 1)]
            tt_ref[_T_GK + LANES * p:_T_GK + LANES * (p + 1), :] = kk.T.astype(BF16)
            qq = ps_ref[:, G_Q + LANES * p:G_Q + LANES * (p + 1)] * (GLA_DK ** -0.5)
            tt_ref[_T_GQ + LANES * p:_T_GQ + LANES * (p + 1), :] = qq.T.astype(BF16)

    r0 = pl.multiple_of(i * bb, bb)
    rows = pl.ds(r0, bb)

    pre = ps_ref[rows, A_G:A_G + LANES] + bias_ref[...]
    lfm = _log_sigmoid(pre) + mm_ref[...]
    f_al = pltpu.roll(lfm, LANES - MLSTM_HEADS, 1)
    m_t = jnp.maximum(f_al, pre)
    g_st = jnp.exp(f_al - m_t)
    w_k = jnp.exp(pre - m_t)
    m1_ref[...] = m_t
    n_new = []
    for h in range(MLSTM_HEADS):
        kk = ps_ref[rows, A_K + HEAD_W * h:A_K + HEAD_W * (h + 1)] * (HEAD_W ** -0.5)
        nn = g_st[:, h:h + 1] * n0_ref[:, HEAD_W * h:HEAD_W * (h + 1)] + w_k[:, h:h + 1] * kk
        n1_ref[:, HEAD_W * h:HEAD_W * (h + 1)] = nn
        n_new.append(nn)

    glu = ps_ref[rows, C_A:C_A + GROUP_WIDTH] * _sigmoid(ps_ref[rows, C_G:C_G + GROUP_WIDTH])
    yc_ref[...] = glu * cw_ref[CONV_WIDTH - 1:CONV_WIDTH, :]
    a_v = ps_ref[rows, A_V:A_V + GROUP_WIDTH]
    g_v = ps_ref[rows, G_V:G_V + GROUP_WIDTH]
    s_k = ps_ref[rows, S_K:S_K + LANES]
    s_v = ps_ref[rows, S_V:S_V + LANES]

    lane_b = lax.broadcasted_iota(I32, (nb, LANES), 0)
    key_row = lax.broadcasted_iota(I32, (SWA_HEADS, WINDOW), 1)
    sink_col = sink_ref[:, 0:1]

    for j in range(bb):
        onehot = (lane_b == r0 + j).astype(BF16)
        bc_ref[...] = jnp.dot(tt_ref[...], onehot, preferred_element_type=F32)
        jrow = slice(j, j + 1)
        for h in range(MLSTM_HEADS):
            hs = slice(HEAD_W * h, HEAD_W * (h + 1))
            kbc = bc_ref[_T_MK + HEAD_W * h:_T_MK + HEAD_W * (h + 1), :]
            qbc = bc_ref[_T_MQ + HEAD_W * h:_T_MQ + HEAD_W * (h + 1), :]
            g1 = g_st[jrow, h:h + 1]
            w1 = w_k[jrow, h:h + 1]
            v_row = a_v[jrow, hs]
            c_new = g1 * c0_ref[j, h] + kbc * (w1 * v_row)
            c1_ref[j, h] = c_new
            num_ref[jrow, hs] = jnp.sum(qbc * c_new, axis=0, keepdims=True)
        for p in range(2):
            abc = bc_ref[_T_GA + LANES * p:_T_GA + LANES * (p + 1), :]
            kbc = bc_ref[_T_GK + LANES * p:_T_GK + LANES * (p + 1), :]
            qbc = bc_ref[_T_GQ + LANES * p:_T_GQ + LANES * (p + 1), :]
            for hh in range(2):
                head = 2 * p + hh
                hs = slice(HEAD_W * head, HEAD_W * (head + 1))
                ds_ = slice(GLA_DK * hh, GLA_DK * (hh + 1))
                v_row = g_v[jrow, hs]
                s_new = abc[ds_, :] * s0_ref[j, head] + kbc[ds_, :] * v_row
                s1_ref[j, head] = s_new
                go_ref[jrow, hs] = jnp.sum(qbc[ds_, :] * s_new, axis=0, keepdims=True)
        cache = cv0_ref[j]
        yc_ref[jrow, :] = yc_ref[jrow, :] + jnp.sum(cache * cw_ref[0:CONV_WIDTH - 1, :], axis=0, keepdims=True)
        cv1_ref[j, 0:CONV_WIDTH - 2, :] = cv0_ref[j, 1:CONV_WIDTH - 1, :]
        cv1_ref[j, CONV_WIDTH - 2:CONV_WIDTH - 1, :] = glu[jrow, :]
        k_new = s_k[jrow, :]
        v_new = s_v[jrow, :]
        k1_ref[j, 0:WINDOW - 1, :] = k0_ref[j, 1:WINDOW, :]
        k1_ref[j, WINDOW - 1:WINDOW, :] = k_new
        v1_ref[j, 0:WINDOW - 1, :] = v0_ref[j, 1:WINDOW, :]
        v1_ref[j, WINDOW - 1:WINDOW, :] = v_new
        qmat = qm_ref[j] * (SWA_HEAD_DIM ** -0.5)
        s_old = lax.dot_general(qmat.astype(BF16), k0_ref[j].astype(BF16), (((1,), (1,)), ((), ())),
                                preferred_element_type=F32)
        s_old = jnp.where(key_row >= 1, s_old, -jnp.inf)
        s_cur = jnp.sum(qmat * k_new, axis=-1, keepdims=True)
        mx = jnp.maximum(jnp.maximum(jnp.max(s_old, axis=-1, keepdims=True), s_cur), sink_col)
        p_old = jnp.exp(s_old - mx)
        p_cur = jnp.exp(s_cur - mx)
        den = jnp.sum(p_old, axis=-1, keepdims=True) + p_cur + jnp.exp(sink_col - mx)
        o = jnp.dot((p_old / den).astype(BF16), v0_ref[j].astype(BF16), preferred_element_type=F32)
        od_ref[j] = o + (p_cur / den) * v_new

    for h in range(MLSTM_HEADS):
        hs = slice(HEAD_W * h, HEAD_W * (h + 1))
        qq = ps_ref[rows, A_Q + HEAD_W * h:A_Q + HEAD_W * (h + 1)]
        den = jnp.sum(qq * n_new[h], axis=-1, keepdims=True)
        hh = num_ref[:, hs] / jnp.maximum(jnp.abs(den), jnp.exp(-m_t[:, h:h + 1]))
        hh = _sigmoid(ps_ref[rows, A_O + HEAD_W * h:A_O + HEAD_W * (h + 1)]) * hh
        hh = hh * lax.rsqrt(jnp.mean(hh * hh, axis=-1, keepdims=True) + NORM_EPS) * mng_ref[:, hs]
        y_ref[:, hs] = hh.astype(y_ref.dtype)
    y_ref[:, GROUP_WIDTH:2 * GROUP_WIDTH] = _conv_norm_act(yc_ref[...] + cb_ref[...], cg_ref, cbe_ref).astype(y_ref.dtype)
    for head in range(GLA_HEADS):
        hs = slice(HEAD_W * head, HEAD_W * (head + 1))
        oh = go_ref[:, hs]
        oh = oh * lax.rsqrt(jnp.mean(oh * oh, axis=-1, keepdims=True) + NORM_EPS) * gng_ref[:, hs]
        gr = ps_ref[rows, G_R + HEAD_W * head:G_R + HEAD_W * (head + 1)]
        y_ref[:, 2 * GROUP_WIDTH + HEAD_W * head:2 * GROUP_WIDTH + HEAD_W * (head + 1)] = (
            oh * _silu(gr)).astype(y_ref.dtype)


def _sample_mixers(proj_s, state, prm):
    c0, n0, m0, s0, cv0, k0, v0 = state
    nb = proj_s.shape[0]
    bb = SAMPLE_BB
    assert nb == LANES and nb % bb == 0
    mm = jnp.concatenate([m0, m0, jnp.zeros((nb, LANES - 2 * MLSTM_HEADS), F32)], axis=1)
    n0f = n0.reshape(nb, GROUP_WIDTH)
    k0f = k0.reshape(nb, WINDOW, LANES)
    v0f = v0.reshape(nb, WINDOW, LANES)
    sq = proj_s[:, S_Q:S_Q + GROUP_WIDTH].reshape(nb, SWA_KV_HEADS, SWA_HEADS // SWA_KV_HEADS, SWA_HEAD_DIM)
    zq = jnp.zeros_like(sq[:, 0])
    qm = jnp.concatenate([jnp.concatenate([sq[:, 0], zq], axis=-1), jnp.concatenate([zq, sq[:, 1]], axis=-1)], axis=1)
    sink_b = jnp.broadcast_to(prm['swa_sinks'].astype(F32)[:, None], (SWA_HEADS, LANES))

    full = lambda shape: pl.BlockSpec(shape, lambda i: (0,) * len(shape))
    rowb = lambda w: pl.BlockSpec((bb, w), lambda i: (i, 0))
    in_specs = [
        full((nb, N_PROJ)), rowb(LANES), full((1, LANES)), rowb(GROUP_WIDTH),
        pl.BlockSpec((bb, MLSTM_HEADS, HEAD_W, HEAD_W), lambda i: (i, 0, 0, 0)),
        pl.BlockSpec((bb, GLA_HEADS, GLA_DK, HEAD_W), lambda i: (i, 0, 0, 0)),
        pl.BlockSpec((bb, CONV_WIDTH - 1, GROUP_WIDTH), lambda i: (i, 0, 0)),
        pl.BlockSpec((bb, WINDOW, LANES), lambda i: (i, 0, 0)),
        pl.BlockSpec((bb, WINDOW, LANES), lambda i: (i, 0, 0)),
        pl.BlockSpec((bb, SWA_HEADS, LANES), lambda i: (i, 0, 0)),
        full((SWA_HEADS, LANES)), full((LANES, 256)), full((1, 256)),
        full((1, GROUP_WIDTH)), full((1, GROUP_WIDTH)),
        full((CONV_WIDTH, GROUP_WIDTH)), full((1, GROUP_WIDTH)), full((1, GROUP_WIDTH)), full((1, GROUP_WIDTH)),
    ]
    out_specs = [
        rowb(3 * GROUP_WIDTH),
        pl.BlockSpec((bb, SWA_HEADS, LANES), lambda i: (i, 0, 0)),
        pl.BlockSpec((bb, MLSTM_HEADS, HEAD_W, HEAD_W), lambda i: (i, 0, 0, 0)),
        rowb(GROUP_WIDTH), rowb(LANES),
        pl.BlockSpec((bb, GLA_HEADS, GLA_DK, HEAD_W), lambda i: (i, 0, 0, 0)),
        pl.BlockSpec((bb, CONV_WIDTH - 1, GROUP_WIDTH), lambda i: (i, 0, 0)),
        pl.BlockSpec((bb, WINDOW, LANES), lambda i: (i, 0, 0)),
        pl.BlockSpec((bb, WINDOW, LANES), lambda i: (i, 0, 0)),
    ]
    out_shape = [
        jax.ShapeDtypeStruct((nb, 3 * GROUP_WIDTH), F32),
        jax.ShapeDtypeStruct((nb, SWA_HEADS, LANES), F32),
        jax.ShapeDtypeStruct(c0.shape, F32),
        jax.ShapeDtypeStruct((nb, GROUP_WIDTH), F32),
        jax.ShapeDtypeStruct((nb, LANES), F32),
        jax.ShapeDtypeStruct(s0.shape, F32),
        jax.ShapeDtypeStruct(cv0.shape, F32),
        jax.ShapeDtypeStruct((nb, WINDOW, LANES), F32),
        jax.ShapeDtypeStruct((nb, WINDOW, LANES), F32),
    ]
    scratch = [
        pltpu.VMEM((_T_ROWS, nb), BF16), pltpu.VMEM((_T_ROWS, LANES), F32),
        pltpu.VMEM((bb, GROUP_WIDTH), F32), pltpu.VMEM((bb, GROUP_WIDTH), F32), pltpu.VMEM((bb, GROUP_WIDTH), F32),
    ]
    y, od, c1, n1, m1, s1, cv1, k1, v1 = pl.pallas_call(
        _sample_kernel,
        grid=(nb // bb,),
        in_specs=in_specs, out_specs=out_specs, out_shape=out_shape, scratch_shapes=scratch,
        compiler_params=_cparams(("arbitrary",)),
        name="sample_mixers",
    )(proj_s, mm, prm['gate_bias'], n0f, c0, s0, cv0, k0f, v0f, qm, sink_b,
      prm['gla_w_gate_pad'], prm['gla_b_gate'].reshape(1, 256),
      prm['mlstm_norm_g'].reshape(1, -1), prm['gla_norm_g'].reshape(1, -1),
      prm['conv_w'], prm['conv_b'].reshape(1, -1), prm['conv_norm_g'].reshape(1, -1),
      prm['conv_norm_b'].reshape(1, -1))
    od4 = od.reshape(nb, SWA_KV_HEADS, SWA_HEADS // SWA_KV_HEADS, LANES)
    yd = jnp.concatenate([od4[:, 0, :, :SWA_HEAD_DIM], od4[:, 1, :, SWA_HEAD_DIM:]], axis=1).reshape(nb, GROUP_WIDTH)
    y = jnp.concatenate([y, yd], axis=1).astype(BF16)
    new_state = (c1, n1.reshape(n0.shape), m1[:, :MLSTM_HEADS], s1, cv1,
                 k1.reshape(k0.shape), v1.reshape(v0.shape))
    return y, new_state


def _outproj_router_kernel(x_ref, mix_ref, w_ref, g_ref, rw_ref, rb_ref,
                           x1_ref, hn_ref, ri_ref, rf_ref, cnt_ref):
    tm = x_ref.shape[0]

    @pl.when(pl.program_id(0) == 0)
    def _():
        cnt_ref[...] = jnp.zeros_like(cnt_ref)

    x1 = x_ref[...] + jnp.dot(mix_ref[...], w_ref[...], preferred_element_type=F32)
    x1_ref[...] = x1
    ms = jnp.mean(x1 * x1, axis=-1, keepdims=True)
    hn = x1 * lax.rsqrt(ms + NORM_EPS) * g_ref[...]
    hn_ref[...] = hn
    logits = jnp.dot(hn.astype(BF16), rw_ref[...], preferred_element_type=F32) + rb_ref[...]
    lane = lax.broadcasted_iota(I32, (tm, LANES), 1)
    big = jnp.int32(LANES)
    gl = jnp.where(lane < N_GROUPS, logits, -jnp.inf)
    gmax = jnp.max(gl, axis=-1, keepdims=True)
    g_sel = jnp.min(jnp.where(gl == gmax, lane, big), axis=-1, keepdims=True)
    g_w = 1.0 / jnp.sum(jnp.exp(gl - gmax), axis=-1, keepdims=True)
    e_lane = lane - N_GROUPS
    in_grp = (e_lane >= 0) & (e_lane < N_EXPERTS) & ((e_lane // EXPERTS_PER_GROUP) == g_sel)
    el = jnp.where(in_grp, logits, -jnp.inf)
    m1 = jnp.max(el, axis=-1, keepdims=True)
    i1 = jnp.min(jnp.where(el == m1, lane, big), axis=-1, keepdims=True)
    el2 = jnp.where(lane == i1, -jnp.inf, el)
    m2 = jnp.max(el2, axis=-1, keepdims=True)
    i2 = jnp.min(jnp.where(el2 == m2, lane, big), axis=-1, keepdims=True)
    r = jnp.exp(m2 - m1)
    p1 = 1.0 / (1.0 + r)
    gate1 = g_w * p1
    gate2 = g_w * (r * p1)
    sel1 = lane == i1
    sel2 = lane == i2
    onehot = jnp.where(sel1 | sel2, 1.0, 0.0)
    row = lax.broadcasted_iota(I32, (tm, tm), 0)
    col = lax.broadcasted_iota(I32, (tm, tm), 1)
    strict = jnp.where(col < row, 1.0, 0.0).astype(BF16)
    cum = jnp.dot(strict, onehot.astype(BF16), preferred_element_type=F32) + cnt_ref[0:1, :]
    rank1 = jnp.sum(jnp.where(sel1, cum, 0.0), axis=-1, keepdims=True).astype(I32)
    rank2 = jnp.sum(jnp.where(sel2, cum, 0.0), axis=-1, keepdims=True).astype(I32)
    cnt_ref[...] = cnt_ref[...] + jnp.sum(onehot, axis=0, keepdims=True)
    ri = jnp.where(lane == 0, i1 - N_GROUPS, jnp.where(lane == 1, i2 - N_GROUPS,
                   jnp.where(lane == 2, rank1, jnp.where(lane == 3, rank2, 0))))
    ri_ref[...] = ri
    rf_ref[...] = jnp.where(lane == 0, gate1, jnp.where(lane == 1, gate2, 0.0))


def _outproj_router(x, mixed, w_out_bf16, norm_g, rw_pad, rb_pad):
    n = x.shape[0]
    tm = _pick_tile(n, (320, 256, 128, 64, 16))
    full = lambda shape: pl.BlockSpec(shape, lambda i: (0,) * len(shape))
    rowb = lambda w: pl.BlockSpec((tm, w), lambda i: (i, 0))
    return pl.pallas_call(
        _outproj_router_kernel,
        grid=(n // tm,),
        in_specs=[rowb(D_MODEL), rowb(D_MODEL), full((D_MODEL, D_MODEL)), full((1, D_MODEL)),
                  full((D_MODEL, LANES)), full((1, LANES))],
        out_specs=[rowb(D_MODEL), rowb(D_MODEL), rowb(LANES), rowb(LANES), full((SUBLANES, LANES))],
        out_shape=[
            jax.ShapeDtypeStruct((n, D_MODEL), F32),
            jax.ShapeDtypeStruct((n, D_MODEL), F32),
            jax.ShapeDtypeStruct((n, LANES), I32),
            jax.ShapeDtypeStruct((n, LANES), F32),
            jax.ShapeDtypeStruct((SUBLANES, LANES), F32),
        ],
        compiler_params=_cparams(("arbitrary",)),
        name="outproj_router",
    )(x, mixed, w_out_bf16, norm_g.reshape(1, D_MODEL), rw_pad, rb_pad)


def _dispatch_kernel(pos_ref, hn_ref, xs_in_ref, xs_ref, sem):
    del xs_in_ref
    tm = hn_ref.shape[0]
    base = pl.program_id(0) * tm

    def copies(r):
        tok = base + r
        src = hn_ref.at[pl.ds(r, 1)]
        return (pltpu.make_async_copy(src, xs_ref.at[pl.ds(pos_ref[2 * tok], 1)], sem.at[0]),
                pltpu.make_async_copy(src, xs_ref.at[pl.ds(pos_ref[2 * tok + 1], 1)], sem.at[1]))

    def issue(r, carry):
        a, b = copies(r)
        a.start()
        b.start()
        return carry

    def drain(r, carry):
        a, b = copies(r)
        a.wait()
        b.wait()
        return carry

    lax.fori_loop(0, tm, issue, 0)
    lax.fori_loop(0, tm, drain, 0)


def _dispatch(pos, hn, n_rows):
    n = hn.shape[0]
    tm = _pick_tile(n, (256, 128, 64, 8))
    xs0 = jnp.zeros((n_rows, D_MODEL), F32)
    return pl.pallas_call(
        _dispatch_kernel,
        grid_spec=pltpu.PrefetchScalarGridSpec(
            num_scalar_prefetch=1,
            grid=(n // tm,),
            in_specs=[pl.BlockSpec((tm, D_MODEL), lambda i, p: (i, 0)), pl.BlockSpec(memory_space=pl.ANY)],
            out_specs=pl.BlockSpec(memory_space=pl.ANY),
            scratch_shapes=[pltpu.SemaphoreType.DMA((2,))],
        ),
        out_shape=jax.ShapeDtypeStruct((n_rows, D_MODEL), F32),
        input_output_aliases={2: 0},
        compiler_params=_cparams(("arbitrary",)),
        name="moe_dispatch",
    )(pos, hn, xs0)


def _expert_kernel(te_ref, nt_ref, x_ref, wg_ref, wu_ref, wd_ref, y_ref):
    del te_ref
    used = pl.program_id(0) < nt_ref[0]

    @pl.when(used)
    def _():
        x = x_ref[...].astype(BF16)
        a = jnp.dot(x, wg_ref[0], preferred_element_type=F32)
        u = jnp.dot(x, wu_ref[0], preferred_element_type=F32)
        hmid = (_silu(a) * u).astype(BF16)
        y_ref[...] = jnp.dot(hmid, wd_ref[0], preferred_element_type=F32)

    @pl.when(jnp.logical_not(used))
    def _():
        y_ref[...] = jnp.zeros_like(y_ref)


def _expert_mlp(tile_expert, n_tiles_used, xs, wg, wu, wd, tile):
    n_tiles = xs.shape[0] // tile

    def row_map(t, te, nt):
        return (jnp.minimum(t, nt[0] - 1), 0)

    def w_map(t, te, nt):
        return (te[jnp.minimum(t, nt[0] - 1)], 0, 0)

    return pl.pallas_call(
        _expert_kernel,
        grid_spec=pltpu.PrefetchScalarGridSpec(
            num_scalar_prefetch=2,
            grid=(n_tiles,),
            in_specs=[
                pl.BlockSpec((tile, D_MODEL), row_map),
                pl.BlockSpec((1, D_MODEL, D_EXPERT), w_map),
                pl.BlockSpec((1, D_MODEL, D_EXPERT), w_map),
                pl.BlockSpec((1, D_EXPERT, D_MODEL), w_map),
            ],
            out_specs=pl.BlockSpec((tile, D_MODEL), lambda t, te, nt: (t, 0)),
        ),
        out_shape=jax.ShapeDtypeStruct(xs.shape, F32),
        compiler_params=_cparams(("arbitrary",)),
        name="expert_mlp",
    )(tile_expert, n_tiles_used, xs, wg, wu, wd)


def _combine_kernel(pos_ref, x1_ref, rf_ref, fg_ref, ys_ref, o_ref, buf_a, buf_b, sem, *, final_norm):
    tm = x1_ref.shape[0]
    base = pl.program_id(0) * tm

    def copies(r):
        tok = base + r
        return (pltpu.make_async_copy(ys_ref.at[pl.ds(pos_ref[2 * tok], 1)], buf_a.at[pl.ds(r, 1)], sem.at[0]),
                pltpu.make_async_copy(ys_ref.at[pl.ds(pos_ref[2 * tok + 1], 1)], buf_b.at[pl.ds(r, 1)], sem.at[1]))

    def issue(r, carry):
        a, b = copies(r)
        a.start()
        b.start()
        return carry

    def drain(r, carry):
        a, b = copies(r)
        a.wait()
        b.wait()
        return carry

    lax.fori_loop(0, tm, issue, 0)
    lax.fori_loop(0, tm, drain, 0)
    rf = rf_ref[...]
    x2 = x1_ref[...] + rf[:, 0:1] * buf_a[...] + rf[:, 1:2] * buf_b[...]
    if final_norm:
        ms = jnp.mean(x2 * x2, axis=-1, keepdims=True)
        x2 = x2 * lax.rsqrt(ms + NORM_EPS) * fg_ref[...]
    o_ref[...] = x2


def _combine(pos, x1, rf, ys, final_g, final_norm):
    n = x1.shape[0]
    tm = _pick_tile(n, (256, 128, 64, 8))
    return pl.pallas_call(
        functools.partial(_combine_kernel, final_norm=final_norm),
        grid_spec=pltpu.PrefetchScalarGridSpec(
            num_scalar_prefetch=1,
            grid=(n // tm,),
            in_specs=[
                pl.BlockSpec((tm, D_MODEL), lambda i, p: (i, 0)),
                pl.BlockSpec((tm, LANES), lambda i, p: (i, 0)),
                pl.BlockSpec((1, D_MODEL), lambda i, p: (0, 0)),
                pl.BlockSpec(memory_space=pl.ANY),
            ],
            out_specs=pl.BlockSpec((tm, D_MODEL), lambda i, p: (i, 0)),
            scratch_shapes=[pltpu.VMEM((tm, D_MODEL), F32), pltpu.VMEM((tm, D_MODEL), F32),
                            pltpu.SemaphoreType.DMA((2,))],
        ),
        out_shape=jax.ShapeDtypeStruct((n, D_MODEL), F32),
        compiler_params=_cparams(("arbitrary",)),
        name="moe_combine",
    )(pos, x1, rf, final_g.reshape(1, D_MODEL), ys)


def _pad_w_in(w_in):
    parts = []
    at = 0
    for lo, hi, dst in sorted(_W_IN_SEGMENTS, key=lambda s: s[2]):
        if dst > at:
            parts.append(jnp.zeros((D_MODEL, dst - at), w_in.dtype))
        parts.append(w_in[:, lo:hi])
        at = dst + (hi - lo)
    if at < N_PROJ:
        parts.append(jnp.zeros((D_MODEL, N_PROJ - at), w_in.dtype))
    return jnp.concatenate(parts, axis=1)


def _moe(x1, hn, ri, rf, counts, prm, final_g, final_norm, tile):
    n = x1.shape[0]
    n_tiles = -(-2 * n // tile) + N_EXPERTS
    cnt = counts[0, N_GROUPS:N_GROUPS + N_EXPERTS].astype(I32)
    tiles_per = (cnt + tile - 1) // tile
    tile_end = jnp.cumsum(tiles_per)
    row_off = (tile_end - tiles_per) * tile
    pos = (row_off[ri[:, 0:2]] + ri[:, 2:4]).reshape(2 * n)
    tile_expert = jnp.minimum(
        jnp.searchsorted(tile_end, jnp.arange(n_tiles, dtype=I32), side='right'), N_EXPERTS - 1).astype(I32)
    n_used = tile_end[N_EXPERTS - 1:N_EXPERTS].astype(I32)
    xs = _dispatch(pos, hn, n_tiles * tile)
    ys = _expert_mlp(tile_expert, n_used, xs, prm['expert_w_gate'], prm['expert_w_up'], prm['expert_w_down'], tile)
    return _combine(pos, x1, rf, ys, final_g, final_norm)


def _layer(x, t_len, state_s, prm, final_g, final_norm, moe_tile):
    n = x.shape[0]
    proj = _in_projection(x, prm['norm_mix_g'], prm['w_in_pad'])
    ya, cn, m_p = _mlstm_prompt(proj, t_len, prm['gate_bias'], prm['mlstm_norm_g'])
    yb, conv_tail = _conv_prompt(proj, t_len, prm['conv_w'], prm['conv_b'], prm['conv_norm_g'], prm['conv_norm_b'])
    yc, sp = _gla_prompt(proj, t_len, prm['gla_w_gate_pad'], prm['gla_b_gate'], prm['gla_norm_g'])
    yd = _swa_prompt(proj, t_len, prm['swa_sinks'])
    y_s, new_s = _sample_mixers(proj[t_len:], state_s, prm)
    mixed = jnp.concatenate([jnp.concatenate([ya, yb, yc, yd], axis=1), y_s], axis=0)
    x1, hn, ri, rf, counts = _outproj_router(x, mixed, prm['w_out'], prm['norm_ffn_g'], prm['router_w'], prm['router_b'])
    x2 = _moe(x1, hn, ri, rf, counts, prm, final_g, final_norm, moe_tile)
    p_c = cn[None, :, :, :HEAD_W]
    p_n = cn[None, :, :, HEAD_W]
    p_m = m_p[None, :MLSTM_HEADS, 0]
    p_s = jnp.stack([sp[0, :GLA_DK, :HEAD_W], sp[0, GLA_DK:, HEAD_W:],
                     sp[1, :GLA_DK, :HEAD_W], sp[1, GLA_DK:, HEAD_W:]])[None]
    p_conv = conv_tail[None, CONV_HALO - (CONV_WIDTH - 1):]
    p_k = proj[t_len - WINDOW:t_len, S_K:S_K + LANES].reshape(1, WINDOW, SWA_KV_HEADS, SWA_HEAD_DIM)
    p_v = proj[t_len - WINDOW:t_len, S_V:S_V + LANES].reshape(1, WINDOW, SWA_KV_HEADS, SWA_HEAD_DIM)
    return x2, (p_c, p_n, p_m, p_s, p_conv, p_k, p_v), new_s


def _forward(x_prompt, x_sample, states, layer_params, final_norm_g, moe_tile=MOE_TILE):
    t_len = x_prompt.shape[1]
    x = jnp.concatenate([x_prompt[0], x_sample[:, 0]], axis=0)
    new_p, new_s = [], []
    depth = len(layer_params)
    for l, prm in enumerate(layer_params):
        x, sp, ss = _layer(x, t_len, states[l], prm, final_norm_g, l == depth - 1, moe_tile)
        new_p.append(sp)
        new_s.append(ss)
    y_prompt = x[None, :t_len]
    y_sample = x[t_len:, None]
    p_states = [jnp.stack(parts) for parts in zip(*new_p)]
    s_states = [jnp.stack(parts) for parts in zip(*new_s)]
    return (y_prompt, y_sample, *p_states, *s_states)


def _prep_layer_params(l, norm_mix_g, w_in, mlstm_b_i, mlstm_b_f, mlstm_norm_g, conv_w, conv_b, conv_norm_g,
                       conv_norm_b, gla_w_gate, gla_b_gate, gla_norm_g, swa_sinks, w_out, norm_ffn_g,
                       router_group_w, router_group_b, router_expert_w, router_expert_b, expert_w_gate,
                       expert_w_up, expert_w_down):
    gate_bias = jnp.concatenate([mlstm_b_i[l], mlstm_b_f[l], jnp.zeros((LANES - 2 * MLSTM_HEADS,), F32)])
    rw = jnp.concatenate([router_group_w[l], router_expert_w[l],
                          jnp.zeros((D_MODEL, LANES - N_GROUPS - N_EXPERTS), F32)], axis=1)
    rb = jnp.concatenate([router_group_b[l], router_expert_b[l],
                          jnp.zeros((LANES - N_GROUPS - N_EXPERTS,), F32)])
    return {
        'norm_mix_g': norm_mix_g[l],
        'w_in_pad': _pad_w_in(w_in[l]).astype(BF16),
        'gate_bias': gate_bias.reshape(1, LANES),
        'mlstm_norm_g': mlstm_norm_g[l],
        'conv_w': conv_w[l], 'conv_b': conv_b[l], 'conv_norm_g': conv_norm_g[l], 'conv_norm_b': conv_norm_b[l],
        'gla_w_gate_pad': jnp.concatenate(
            [gla_w_gate[l], jnp.zeros((LANES - GLA_LOWRANK, GLA_HEADS * GLA_DK), F32)], axis=0).astype(BF16),
        'gla_b_gate': gla_b_gate[l], 'gla_norm_g': gla_norm_g[l],
        'swa_sinks': swa_sinks[l],
        'w_out': w_out[l].astype(BF16),
        'norm_ffn_g': norm_ffn_g[l],
        'router_w': rw.astype(BF16), 'router_b': rb.reshape(1, LANES),
        'expert_w_gate': expert_w_gate[l].astype(BF16),
        'expert_w_up': expert_w_up[l].astype(BF16),
        'expert_w_down': expert_w_down[l].astype(BF16),
    }


def kernel(x_prompt, x_sample, state_mlstm_C, state_mlstm_n, state_mlstm_m, state_gla_S, cache_conv, cache_swa_k, cache_swa_v, norm_mix_g, w_in, mlstm_b_i, mlstm_b_f, mlstm_norm_g, conv_w, conv_b, conv_norm_g, conv_norm_b, gla_w_gate, gla_b_gate, gla_norm_g, swa_sinks, w_out, norm_ffn_g, router_group_w, router_group_b, router_expert_w, router_expert_b, expert_w_gate, expert_w_up, expert_w_down, final_norm_g):
    depth = w_in.shape[0]
    weights = (norm_mix_g, w_in, mlstm_b_i, mlstm_b_f, mlstm_norm_g, conv_w, conv_b, conv_norm_g, conv_norm_b,
               gla_w_gate, gla_b_gate, gla_norm_g, swa_sinks, w_out, norm_ffn_g, router_group_w, router_group_b,
               router_expert_w, router_expert_b, expert_w_gate, expert_w_up, expert_w_down)
    layer_params = [_prep_layer_params(l, *weights) for l in range(depth)]
    states = [(state_mlstm_C[l], state_mlstm_n[l], state_mlstm_m[l], state_gla_S[l], cache_conv[l],
               cache_swa_k[l], cache_swa_v[l]) for l in range(depth)]
    return _forward(x_prompt, x_sample, states, layer_params, final_norm_g)
```

```python
import functools

import jax
import jax.numpy as jnp
from jax import lax
from jax.experimental import pallas as pl
from jax.experimental.pallas import tpu as pltpu

F32 = jnp.float32
BF16 = jnp.bfloat16
I32 = jnp.int32
HIGHEST = lax.Precision.HIGHEST

D_MODEL = 2048
GROUP_WIDTH = 512
HEAD_W = 128
MLSTM_HEADS = 4
GLA_HEADS = 4
GLA_DK = 64
GLA_LOWRANK = 16
GLA_TAU = 16.0
CONV_WIDTH = 31
SWA_HEADS = 8
SWA_KV_HEADS = 2
SWA_HEAD_DIM = 64
WINDOW = 128
N_GROUPS = 4
EXPERTS_PER_GROUP = 4
N_EXPERTS = 16
D_EXPERT = 1024
NORM_EPS = 1e-6
LANES = 128
SUBLANES = 8

A_Q, A_K, A_V, A_O = 0, 512, 1024, 1536
C_A, C_G = 2048, 2560
G_V, G_R = 3072, 3584
S_Q = 4096
G_Q, G_K = 4608, 4864
A_G, G_LR, S_K, S_V = 5120, 5248, 5376, 5504
N_PROJ = 5632
_W_IN_SEGMENTS = (
    (0, 512, A_Q), (512, 1024, A_K), (1024, 1536, A_V), (1536, 2048, A_O),
    (2048, 2056, A_G),
    (2056, 2568, C_A), (2568, 3080, C_G),
    (3080, 3336, G_Q), (3336, 3592, G_K), (3592, 4104, G_V), (4104, 4616, G_R),
    (4616, 4632, G_LR),
    (4632, 5144, S_Q), (5144, 5272, S_K), (5272, 5400, S_V),
)

MLSTM_CHUNK = 128
GLA_CHUNK = 64
GLA_SUB = 16
CONV_TILE = 256
CONV_ROWS = 64
CONV_HALO = 32
SAMPLE_BB = 8
MOE_TILE = 512
VMEM_LIMIT = 56 * 1024 * 1024


def _cparams(sem, vmem=VMEM_LIMIT):
    return pltpu.CompilerParams(dimension_semantics=sem, vmem_limit_bytes=vmem)


def _log_sigmoid(x):
    return jnp.minimum(x, 0.0) - jnp.log1p(jnp.exp(-jnp.abs(x)))


def _sigmoid(x):
    return 1.0 / (1.0 + jnp.exp(-x))


def _silu(x):
    return x * _sigmoid(x)


def _pick_tile(n, candidates):
    for c in candidates:
        if n % c == 0:
            return c
    raise ValueError(f"no tile for {n} in {candidates}")


def _proj_kernel(x_ref, g_ref, w_ref, o_ref, hn_ref):
    @pl.when(pl.program_id(1) == 0)
    def _():
        x = x_ref[...]
        ms = jnp.mean(x * x, axis=-1, keepdims=True)
        hn_ref[...] = (x * lax.rsqrt(ms + NORM_EPS) * g_ref[...]).astype(BF16)

    o_ref[...] = jnp.dot(hn_ref[...], w_ref[...], preferred_element_type=F32)


def _in_projection(x, g, w_bf16):
    n = x.shape[0]
    tm = _pick_tile(n, (640, 512, 256, 128, 64, 8))
    tn = 512
    return pl.pallas_call(
        _proj_kernel,
        grid=(n // tm, N_PROJ // tn),
        in_specs=[
            pl.BlockSpec((tm, D_MODEL), lambda i, j: (i, 0)),
            pl.BlockSpec((1, D_MODEL), lambda i, j: (0, 0)),
            pl.BlockSpec((D_MODEL, tn), lambda i, j: (0, j)),
        ],
        out_specs=pl.BlockSpec((tm, tn), lambda i, j: (i, j)),
        out_shape=jax.ShapeDtypeStruct((n, N_PROJ), F32),
        scratch_shapes=[pltpu.VMEM((tm, D_MODEL), BF16)],
        compiler_params=_cparams(("parallel", "arbitrary")),
        name="in_projection",
    )(x, g.reshape(1, D_MODEL), w_bf16)


def _mlstm_prompt_kernel(q_ref, k_ref, v_ref, o_ref, gt_ref, bias_ref, ng_ref,
                         y_ref, cn_ref, m_ref):
    L = q_ref.shape[0]

    @pl.when(pl.program_id(0) == 0)
    def _():
        cn_ref[...] = jnp.zeros_like(cn_ref)
        m_ref[...] = jnp.zeros_like(m_ref)

    pre = gt_ref[...] + bias_ref[...]
    lf = _log_sigmoid(pre)
    row = lax.broadcasted_iota(I32, (L, L), 0)
    col = lax.broadcasted_iota(I32, (L, L), 1)
    tri = col <= row
    b_all = jnp.dot(tri.astype(F32), lf, precision=HIGHEST, preferred_element_type=F32)
    pre_t = pre.T
    b_t = b_all.T
    lane = lax.broadcasted_iota(I32, (L, HEAD_W), 1)
    ones_col = (lane == 0).astype(BF16)
    for h in range(MLSTM_HEADS):
        sl = slice(HEAD_W * h, HEAD_W * (h + 1))
        q = q_ref[:, sl]
        k = k_ref[:, sl] * (HEAD_W ** -0.5)
        v = v_ref[:, sl]
        b_col = b_all[:, 4 + h:5 + h]
        i_col = pre[:, h:h + 1]
        b_row = b_t[4 + h:5 + h, :]
        i_row = pre_t[h:h + 1, :]
        m_prev = m_ref[h:h + 1, 0:1]
        log_d = jnp.where(tri, b_col - b_row + i_row, -jnp.inf)
        log_inter = b_col + m_prev
        m_t = jnp.maximum(log_inter, jnp.max(log_d, axis=-1, keepdims=True))
        d_mat = jnp.exp(log_d - m_t)
        g_inter = jnp.exp(log_inter - m_t)
        qb = q.astype(BF16)
        kb = k.astype(BF16)
        s = lax.dot_general(qb, kb, (((1,), (1,)), ((), ())), preferred_element_type=F32)
        w = (s * d_mat).astype(BF16)
        v1 = jnp.concatenate([v.astype(BF16), ones_col], axis=1)
        cn = cn_ref[h]
        nd = g_inter * jnp.dot(qb, cn.astype(BF16), preferred_element_type=F32)
        nd = nd + jnp.dot(w, v1, preferred_element_type=F32)
        num = nd[:, :HEAD_W]
        den = nd[:, HEAD_W:HEAD_W + 1]
        hh = num / jnp.maximum(jnp.abs(den), jnp.exp(-m_t))
        hh = _sigmoid(o_ref[:, sl]) * hh
        hh = hh * lax.rsqrt(jnp.mean(hh * hh, axis=-1, keepdims=True) + NORM_EPS) * ng_ref[:, sl]
        y_ref[:, sl] = hh.astype(y_ref.dtype)
        m_last = m_t[L - 1:L, :]
        b_last = b_col[L - 1:L, :]
        g_state = jnp.exp(b_last + m_prev - m_last)
        w_k = jnp.exp(b_last - b_col + i_col - m_last)
        kw = (k * w_k).astype(BF16)
        upd = lax.dot_general(kw, v1, (((0,), (0,)), ((), ())), preferred_element_type=F32)
        cn_ref[h] = g_state * cn + upd
        m_ref[h:h + 1, :] = jnp.broadcast_to(m_last, (1, LANES))


def _mlstm_prompt(proj, t_len, bias_row, norm_g):
    L = _pick_tile(t_len, (MLSTM_CHUNK, 64, 32, 16, 8))

    def col(off):
        return pl.BlockSpec((L, GROUP_WIDTH), lambda i, o=off: (i, o // GROUP_WIDTH))

    return pl.pallas_call(
        _mlstm_prompt_kernel,
        grid=(t_len // L,),
        in_specs=[
            col(A_Q), col(A_K), col(A_V), col(A_O),
            pl.BlockSpec((L, LANES), lambda i: (i, A_G // LANES)),
            pl.BlockSpec((1, LANES), lambda i: (0, 0)),
            pl.BlockSpec((1, GROUP_WIDTH), lambda i: (0, 0)),
        ],
        out_specs=[
            pl.BlockSpec((L, GROUP_WIDTH), lambda i: (i, 0)),
            pl.BlockSpec((MLSTM_HEADS, HEAD_W, 2 * HEAD_W), lambda i: (0, 0, 0)),
            pl.BlockSpec((SUBLANES, LANES), lambda i: (0, 0)),
        ],
        out_shape=[
            jax.ShapeDtypeStruct((t_len, GROUP_WIDTH), BF16),
            jax.ShapeDtypeStruct((MLSTM_HEADS, HEAD_W, 2 * HEAD_W), F32),
            jax.ShapeDtypeStruct((SUBLANES, LANES), F32),
        ],
        compiler_params=_cparams(("arbitrary",)),
        name="mlstm_prompt",
    )(proj, proj, proj, proj, proj, bias_row, norm_g.reshape(1, GROUP_WIDTH))


def _gla_prompt_kernel(q_ref, k_ref, v_ref, r_ref, lr_ref, wg_ref, bg_ref, ng_ref, y_ref, sp_ref):
    L = q_ref.shape[0]
    n_sub = L // GLA_SUB

    @pl.when(pl.program_id(0) == 0)
    def _():
        sp_ref[...] = jnp.zeros_like(sp_ref)

    gate_pre = jnp.dot(lr_ref[...].astype(BF16), wg_ref[...], preferred_element_type=F32) + bg_ref[...]
    log_a = _log_sigmoid(gate_pre) * (1.0 / GLA_TAU)
    row = lax.broadcasted_iota(I32, (L, L), 0)
    col = lax.broadcasted_iota(I32, (L, L), 1)
    b = jnp.dot((col <= row).astype(F32), log_a, precision=HIGHEST, preferred_element_type=F32)
    q = q_ref[...] * (GLA_DK ** -0.5)
    k = k_ref[...]
    b_last = b[L - 1:L, :]
    q_in = q * jnp.exp(b)
    k_dec = k * jnp.exp(b_last - b)
    lane16 = lax.broadcasted_iota(I32, (GLA_SUB, LANES), 1)
    lo16 = lane16 < GLA_DK
    srow = lax.broadcasted_iota(I32, (LANES, 2 * HEAD_W), 0)
    scol = lax.broadcasted_iota(I32, (LANES, 2 * HEAD_W), 1)
    block_diag = (srow < GLA_DK) == (scol < HEAD_W)
    for p in range(2):
        pls = slice(LANES * p, LANES * (p + 1))
        sp = sp_ref[p]
        vp = v_ref[:, 2 * HEAD_W * p:2 * HEAD_W * (p + 1)].astype(BF16)
        o_inter = jnp.dot(q_in[:, pls].astype(BF16), sp.astype(BF16), preferred_element_type=F32)
        rows = []
        for blk in range(n_sub):
            r0 = GLA_SUB * blk
            n = GLA_SUB * (blk + 1)
            if blk == 0:
                qs = q[r0:r0 + GLA_SUB, pls] * jnp.exp(b[r0:r0 + GLA_SUB, pls])
                ks = k[:n, pls] * jnp.exp(-b[:n, pls])
            else:
                anchor = b[r0 - 1:r0, pls]
                qs = q[r0:r0 + GLA_SUB, pls] * jnp.exp(b[r0:r0 + GLA_SUB, pls] - anchor)
                ks = k[:n, pls] * jnp.exp(anchor - b[:n, pls])
            qs2 = jnp.concatenate([jnp.where(lo16, qs, 0.0), jnp.where(lo16, 0.0, qs)], axis=0)
            att = lax.dot_general(qs2.astype(BF16), ks.astype(BF16), (((1,), (1,)), ((), ())),
                                  preferred_element_type=F32)
            trow = lax.broadcasted_iota(I32, (2 * GLA_SUB, n), 0)
            tcol = lax.broadcasted_iota(I32, (2 * GLA_SUB, n), 1)
            t_idx = r0 + jnp.where(trow >= GLA_SUB, trow - GLA_SUB, trow)
            att = jnp.where(tcol <= t_idx, att, 0.0)
            o2 = jnp.dot(att.astype(BF16), vp[:n], preferred_element_type=F32)
            rows.append(jnp.concatenate([o2[:GLA_SUB, :HEAD_W], o2[GLA_SUB:, HEAD_W:]], axis=1))
        o = o_inter + jnp.concatenate(rows, axis=0)
        for hh in range(2):
            head = 2 * p + hh
            hs = slice(HEAD_W * head, HEAD_W * (head + 1))
            oh = o[:, HEAD_W * hh:HEAD_W * (hh + 1)]
            oh = oh * lax.rsqrt(jnp.mean(oh * oh, axis=-1, keepdims=True) + NORM_EPS) * ng_ref[:, hs]
            y_ref[:, hs] = (oh * _silu(r_ref[:, hs])).astype(y_ref.dtype)
        dec_col = jnp.exp(b[:, pls].T[:, L - 1:L])
        upd = lax.dot_general(k_dec[:, pls].astype(BF16), vp, (((0,), (0,)), ((), ())),
                              preferred_element_type=F32)
        sp_ref[p] = jnp.where(block_diag, dec_col * sp + upd, 0.0)


def _gla_prompt(proj, t_len, w_gate_pad, b_gate, norm_g):
    L = _pick_tile(t_len, (GLA_CHUNK,))
    return pl.pallas_call(
        _gla_prompt_kernel,
        grid=(t_len // L,),
        in_specs=[
            pl.BlockSpec((L, 256), lambda i: (i, G_Q // 256)),
            pl.BlockSpec((L, 256), lambda i: (i, G_K // 256)),
            pl.BlockSpec((L, GROUP_WIDTH), lambda i: (i, G_V // GROUP_WIDTH)),
            pl.BlockSpec((L, GROUP_WIDTH), lambda i: (i, G_R // GROUP_WIDTH)),
            pl.BlockSpec((L, LANES), lambda i: (i, G_LR // LANES)),
            pl.BlockSpec((LANES, 256), lambda i: (0, 0)),
            pl.BlockSpec((1, 256), lambda i: (0, 0)),
            pl.BlockSpec((1, GROUP_WIDTH), lambda i: (0, 0)),
        ],
        out_specs=[
            pl.BlockSpec((L, GROUP_WIDTH), lambda i: (i, 0)),
            pl.BlockSpec((2, LANES, 2 * HEAD_W), lambda i: (0, 0, 0)),
        ],
        out_shape=[
            jax.ShapeDtypeStruct((t_len, GROUP_WIDTH), BF16),
            jax.ShapeDtypeStruct((2, LANES, 2 * HEAD_W), F32),
        ],
        compiler_params=_cparams(("arbitrary",)),
        name="gla_prompt",
    )(proj, proj, proj, proj, proj, w_gate_pad, b_gate.reshape(1, 256), norm_g.reshape(1, GROUP_WIDTH))


def _conv_norm_act(y, g_ref, be_ref):
    mu = jnp.mean(y, axis=-1, keepdims=True)
    yc = y - mu
    var = jnp.mean(yc * yc, axis=-1, keepdims=True)
    return _silu(yc * lax.rsqrt(var + NORM_EPS) * g_ref[...] + be_ref[...])


def _conv_prompt_kernel(ua_ref, ug_ref, ha_ref, hg_ref, w_ref, b_ref, g_ref, be_ref,
                        y_ref, tail_ref, buf_ref):
    tt = ua_ref.shape[0]
    halo = ha_ref[...] * _sigmoid(hg_ref[...])
    buf_ref[0:CONV_HALO, :] = jnp.where(pl.program_id(0) > 0, halo, 0.0)
    buf_ref[CONV_HALO:CONV_HALO + tt, :] = ua_ref[...] * _sigmoid(ug_ref[...])
    base = CONV_HALO - (CONV_WIDTH - 1)
    for r in range(tt // CONV_ROWS):
        acc = jnp.zeros((CONV_ROWS, GROUP_WIDTH), F32)
        for j in range(CONV_WIDTH):
            s0 = r * CONV_ROWS + base + j
            acc = acc + w_ref[j:j + 1, :] * buf_ref[s0:s0 + CONV_ROWS, :]
        y = _conv_norm_act(acc + b_ref[...], g_ref, be_ref)
        y_ref[r * CONV_ROWS:(r + 1) * CONV_ROWS, :] = y.astype(y_ref.dtype)
    tail_ref[...] = buf_ref[tt:tt + CONV_HALO, :]


def _conv_prompt(proj, t_len, w, b, g, beta):
    tt = _pick_tile(t_len, (CONV_TILE, 128, 64))
    ratio = tt // CONV_HALO
    vec = lambda: pl.BlockSpec((1, GROUP_WIDTH), lambda i: (0, 0))
    return pl.pallas_call(
        _conv_prompt_kernel,
        grid=(t_len // tt,),
        in_specs=[
            pl.BlockSpec((tt, GROUP_WIDTH), lambda i: (i, C_A // GROUP_WIDTH)),
            pl.BlockSpec((tt, GROUP_WIDTH), lambda i: (i, C_G // GROUP_WIDTH)),
            pl.BlockSpec((CONV_HALO, GROUP_WIDTH), lambda i: (jnp.maximum(i * ratio - 1, 0), C_A // GROUP_WIDTH)),
            pl.BlockSpec((CONV_HALO, GROUP_WIDTH), lambda i: (jnp.maximum(i * ratio - 1, 0), C_G // GROUP_WIDTH)),
            pl.BlockSpec((CONV_WIDTH, GROUP_WIDTH), lambda i: (0, 0)),
            vec(), vec(), vec(),
        ],
        out_specs=[
            pl.BlockSpec((tt, GROUP_WIDTH), lambda i: (i, 0)),
            pl.BlockSpec((CONV_HALO, GROUP_WIDTH), lambda i: (0, 0)),
        ],
        out_shape=[
            jax.ShapeDtypeStruct((t_len, GROUP_WIDTH), BF16),
            jax.ShapeDtypeStruct((CONV_HALO, GROUP_WIDTH), F32),
        ],
        scratch_shapes=[pltpu.VMEM((tt + CONV_HALO, GROUP_WIDTH), F32)],
        compiler_params=_cparams(("arbitrary",)),
        name="conv_prompt",
    )(proj, proj, proj, proj, w, b.reshape(1, -1), g.reshape(1, -1), beta.reshape(1, -1))


def _swa_prompt_kernel(sink_ref, q_ref, kc_ref, vc_ref, kp_ref, vp_ref, y_ref):
    bq = q_ref.shape[0]
    first = pl.program_id(0) == 0
    k_all = jnp.concatenate([kp_ref[...], kc_ref[...]], axis=0)
    v_all = jnp.concatenate([vp_ref[...], vc_ref[...]], axis=0)
    k_sw = pltpu.roll(k_all, SWA_HEAD_DIM, 1).astype(BF16)
    v_sw = pltpu.roll(v_all, SWA_HEAD_DIM, 1).astype(BF16)
    k_all = k_all.astype(BF16)
    v_all = v_all.astype(BF16)
    tq = lax.broadcasted_iota(I32, (bq, 2 * bq), 0)
    kj = lax.broadcasted_iota(I32, (bq, 2 * bq), 1)
    valid = (kj > tq) & (kj <= tq + WINDOW) & (kj >= jnp.where(first, bq, 0))
    lane = lax.broadcasted_iota(I32, (bq, LANES), 1)
    lo = lane < SWA_HEAD_DIM
    rep = SWA_HEADS // SWA_KV_HEADS
    for c in range(SWA_HEADS // 2):
        qc = q_ref[:, LANES * c:LANES * (c + 1)] * (SWA_HEAD_DIM ** -0.5)
        outs = []
        for hh in range(2):
            h = 2 * c + hh
            g = h // rep
            qm = jnp.where(lo if hh == 0 else jnp.logical_not(lo), qc, 0.0).astype(BF16)
            k_use = k_all if g == hh else k_sw
            v_use = v_all if g == hh else v_sw
            s = lax.dot_general(qm, k_use, (((1,), (1,)), ((), ())), preferred_element_type=F32)
            s = jnp.where(valid, s, -jnp.inf)
            sink = sink_ref[h]
            mx = jnp.maximum(jnp.max(s, axis=-1, keepdims=True), sink)
            p = jnp.exp(s - mx)
            den = jnp.sum(p, axis=-1, keepdims=True) + jnp.exp(sink - mx)
            p = (p / den).astype(BF16)
            outs.append(jnp.dot(p, v_use, preferred_element_type=F32))
        y_ref[:, LANES * c:LANES * (c + 1)] = jnp.where(lo, outs[0], outs[1]).astype(y_ref.dtype)


def _swa_prompt(proj, t_len, sinks):
    bq = WINDOW
    assert t_len % bq == 0
    kv = lambda off, prev: pl.BlockSpec(
        (bq, LANES), (lambda i: (jnp.maximum(i - 1, 0), off // LANES)) if prev else (lambda i: (i, off // LANES)))
    return pl.pallas_call(
        _swa_prompt_kernel,
        grid=(t_len // bq,),
        in_specs=[
            pl.BlockSpec(memory_space=pltpu.SMEM),
            pl.BlockSpec((bq, GROUP_WIDTH), lambda i: (i, S_Q // GROUP_WIDTH)),
            kv(S_K, False), kv(S_V, False), kv(S_K, True), kv(S_V, True),
        ],
        out_specs=pl.BlockSpec((bq, GROUP_WIDTH), lambda i: (i, 0)),
        out_shape=jax.ShapeDtypeStruct((t_len, GROUP_WIDTH), BF16),
        compiler_params=_cparams(("parallel",)),
        name="swa_prompt",
    )(sinks, proj, proj, proj, proj, proj)


_T_MK, _T_MQ = 0, 512
_T_GA, _T_GK, _T_GQ = 1024, 1280, 1536
_T_ROWS = 1792


def _sample_kernel(ps_ref, mm_ref, bias_ref, n0_ref, c0_ref, s0_ref, cv0_ref, k0_ref, v0_ref,
                   qm_ref, sink_ref, wg_ref, bg_ref, mng_ref, gng_ref, cw_ref, cb_ref, cg_ref, cbe_ref,
                   y_ref, od_ref, c1_ref, n1_ref, m1_ref, s1_ref, cv1_ref, k1_ref, v1_ref,
                   tt_ref, bc_ref, num_ref, go_ref, yc_ref):
    i = pl.program_id(0)
    nb = ps_ref.shape[0]
    bb = y_ref.shape[0]

    def gla_gate(lr):
        gp = jnp.dot(lr.astype(BF16), wg_ref[...], preferred_element_type=F32) + bg_ref[...]
        return jnp.exp(_log_sigmoid(gp) * (1.0 / GLA_TAU))

    @pl.when(i == 0)
    def _():
        for h in range(MLSTM_HEADS):
            kk = ps_ref[:, A_K + HEAD_W * h:A_K + HEAD_W * (h + 1)] * (HEAD_W ** -0.5)
            tt_ref[_T_MK + HEAD_W * h:_T_MK + HEAD_W * (h + 1), :] = kk.T.astype(BF16)
            qq = ps_ref[:, A_Q + HEAD_W * h:A_Q + HEAD_W * (h + 1)]
            tt_ref[_T_MQ + HEAD_W * h:_T_MQ + HEAD_W * (h + 1), :] = qq.T.astype(BF16)
        a_all = gla_gate(ps_ref[:, G_LR:G_LR + LANES])
        for p in range(2):
            pls = slice(LANES * p, LANES * (p + 1))
            tt_ref[_T_GA + LANES * p:_T_GA + LANES * (p + 1), :] = a_all[:, pls].T.astype(BF16)
            kk = ps_ref[:, G_K + LANES * p:G_K + LANES * (p + 1)]
            tt_ref[_T_GK + LANES * p:_T_GK + LANES * (p + 1), :] = kk.T.astype(BF16)
            qq = ps_ref[:, G_Q + LANES * p:G_Q + LANES * (p + 1)] * (GLA_DK ** -0.5)
            tt_ref[_T_GQ + LANES * p:_T_GQ + LANES * (p + 1), :] = qq.T.astype(BF16)

    r0 = pl.multiple_of(i * bb, bb)
    rows = pl.ds(r0, bb)

    pre = ps_ref[rows, A_G:A_G + LANES] + bias_ref[...]
    lfm = _log_sigmoid(pre) + mm_ref[...]
    f_al = pltpu.roll(lfm, LANES - MLSTM_HEADS, 1)
    m_t = jnp.maximum(f_al, pre)
    g_st = jnp.exp(f_al - m_t)
    w_k = jnp.exp(pre - m_t)
    m1_ref[...] = m_t
    n_new = []
    for h in range(MLSTM_HEADS):
        kk = ps_ref[rows, A_K + HEAD_W * h:A_K + HEAD_W * (h + 1)] * (HEAD_W ** -0.5)
        nn = g_st[:, h:h + 1] * n0_ref[:, HEAD_W * h:HEAD_W * (h + 1)] + w_k[:, h:h + 1] * kk
        n1_ref[:, HEAD_W * h:HEAD_W * (h + 1)] = nn
        n_new.append(nn)

    glu = ps_ref[rows, C_A:C_A + GROUP_WIDTH] * _sigmoid(ps_ref[rows, C_G:C_G + GROUP_WIDTH])
    yc_ref[...] = glu * cw_ref[CONV_WIDTH - 1:CONV_WIDTH, :]
    a_v = ps_ref[rows, A_V:A_V + GROUP_WIDTH]
    g_v = ps_ref[rows, G_V:G_V + GROUP_WIDTH]
    s_k = ps_ref[rows, S_K:S_K + LANES]
    s_v = ps_ref[rows, S_V:S_V + LANES]

    lane_b = lax.broadcasted_iota(I32, (nb, LANES), 0)
    key_row = lax.broadcasted_iota(I32, (SWA_HEADS, WINDOW), 1)
    sink_col = sink_ref[:, 0:1]

    for j in range(bb):
        onehot = (lane_b == r0 + j).astype(BF16)
        bc_ref[...] = jnp.dot(tt_ref[...], onehot, preferred_element_type=F32)
        jrow = slice(j, j + 1)
        for h in range(MLSTM_HEADS):
            hs = slice(HEAD_W * h, HEAD_W * (h + 1))
            kbc = bc_ref[_T_MK + HEAD_W * h:_T_MK + HEAD_W * (h + 1), :]
            qbc = bc_ref[_T_MQ + HEAD_W * h:_T_MQ + HEAD_W * (h + 1), :]
            g1 = g_st[jrow, h:h + 1]
            w1 = w_k[jrow, h:h + 1]
            v_row = a_v[jrow, hs]
            c_new = g1 * c0_ref[j, h] + kbc * (w1 * v_row)
            c1_ref[j, h] = c_new
            num_ref[jrow, hs] = jnp.sum(qbc * c_new, axis=0, keepdims=True)
        for p in range(2):
            abc = bc_ref[_T_GA + LANES * p:_T_GA + LANES * (p + 1), :]
            kbc = bc_ref[_T_GK + LANES * p:_T_GK + LANES * (p + 1), :]
            qbc = bc_ref[_T_GQ + LANES * p:_T_GQ + LANES * (p + 1), :]
            for hh in range(2):
                head = 2 * p + hh
                hs = slice(HEAD_W * head, HEAD_W * (head + 1))
                ds_ = slice(GLA_DK * hh, GLA_DK * (hh + 1))
                v_row = g_v[jrow, hs]
                s_new = abc[ds_, :] * s0_ref[j, head] + kbc[ds_, :] * v_row
                s1_ref[j, head] = s_new
                go_ref[jrow, hs] = jnp.sum(qbc[ds_, :] * s_new, axis=0, keepdims=True)
        cache = cv0_ref[j]
        yc_ref[jrow, :] = yc_ref[jrow, :] + jnp.sum(cache * cw_ref[0:CONV_WIDTH - 1, :], axis=0, keepdims=True)
        cv1_ref[j, 0:CONV_WIDTH - 2, :] = cv0_ref[j, 1:CONV_WIDTH - 1, :]
        cv1_ref[j, CONV_WIDTH - 2:CONV_WIDTH - 1, :] = glu[jrow, :]
        k_new = s_k[jrow, :]
        v_new = s_v[jrow, :]
        k1_ref[j, 0:WINDOW - 1, :] = k0_ref[j, 1:WINDOW, :]
        k1_ref[j, WINDOW - 1:WINDOW, :] = k_new
        v1_ref[j, 0:WINDOW - 1, :] = v0_ref[j, 1:WINDOW, :]
        v1_ref[j, WINDOW - 1:WINDOW, :] = v_new
        qmat = qm_ref[j] * (SWA_HEAD_DIM ** -0.5)
        s_old = lax.dot_general(qmat.astype(BF16), k0_ref[j].astype(BF16), (((1,), (1,)), ((), ())),
                                preferred_element_type=F32)
        s_old = jnp.where(key_row >= 1, s_old, -jnp.inf)
        s_cur = jnp.sum(qmat * k_new, axis=-1, keepdims=True)
        mx = jnp.maximum(jnp.maximum(jnp.max(s_old, axis=-1, keepdims=True), s_cur), sink_col)
        p_old = jnp.exp(s_old - mx)
        p_cur = jnp.exp(s_cur - mx)
        den = jnp.sum(p_old, axis=-1, keepdims=True) + p_cur + jnp.exp(sink_col - mx)
        o = jnp.dot((p_old / den).astype(BF16), v0_ref[j].astype(BF16), preferred_element_type=F32)
        od_ref[j] = o + (p_cur / den) * v_new

    for h in range(MLSTM_HEADS):
        hs = slice(HEAD_W * h, HEAD_W * (h + 1))
        qq = ps_ref[rows, A_Q + HEAD_W * h:A_Q + HEAD_W * (h + 1)]
        den = jnp.sum(qq * n_new[h], axis=-1, keepdims=True)
        hh = num_ref[:, hs] / jnp.maximum(jnp.abs(den), jnp.exp(-m_t[:, h:h + 1]))
        hh = _sigmoid(ps_ref[rows, A_O + HEAD_W * h:A_O + HEAD_W * (h + 1)]) * hh
        hh = hh * lax.rsqrt(jnp.mean(hh * hh, axis=-1, keepdims=True) + NORM_EPS) * mng_ref[:, hs]
        y_ref[:, hs] = hh.astype(y_ref.dtype)
    y_ref[:, GROUP_WIDTH:2 * GROUP_WIDTH] = _conv_norm_act(yc_ref[...] + cb_ref[...], cg_ref, cbe_ref).astype(y_ref.dtype)
    for head in range(GLA_HEADS):
        hs = slice(HEAD_W * head, HEAD_W * (head + 1))
        oh = go_ref[:, hs]
        oh = oh * lax.rsqrt(jnp.mean(oh * oh, axis=-1, keepdims=True) + NORM_EPS) * gng_ref[:, hs]
        gr = ps_ref[rows, G_R + HEAD_W * head:G_R + HEAD_W * (head + 1)]
        y_ref[:, 2 * GROUP_WIDTH + HEAD_W * head:2 * GROUP_WIDTH + HEAD_W * (head + 1)] = (
            oh * _silu(gr)).astype(y_ref.dtype)


def _sample_mixers(proj_s, states, layer, prm):
    c_all, n_all, m_all, s_all, cv_all, k_all, v_all = states
    depth = c_all.shape[0]
    nb = proj_s.shape[0]
    bb = SAMPLE_BB
    assert nb == LANES and nb % bb == 0
    lb = layer * (nb // bb)
    n0, m0 = n_all[layer], m_all[layer]
    mm = jnp.concatenate([m0, m0, jnp.zeros((nb, LANES - 2 * MLSTM_HEADS), F32)], axis=1)
    n0f = n0.reshape(nb, GROUP_WIDTH)
    c0 = c_all.reshape((depth * nb,) + c_all.shape[2:])
    s0 = s_all.reshape((depth * nb,) + s_all.shape[2:])
    cv0 = cv_all.reshape((depth * nb,) + cv_all.shape[2:])
    k0f = k_all.reshape(depth * nb, WINDOW, LANES)
    v0f = v_all.reshape(depth * nb, WINDOW, LANES)
    sq = proj_s[:, S_Q:S_Q + GROUP_WIDTH].reshape(nb, SWA_KV_HEADS, SWA_HEADS // SWA_KV_HEADS, SWA_HEAD_DIM)
    zq = jnp.zeros_like(sq[:, 0])
    qm = jnp.concatenate([jnp.concatenate([sq[:, 0], zq], axis=-1), jnp.concatenate([zq, sq[:, 1]], axis=-1)], axis=1)
    sink_b = jnp.broadcast_to(prm['swa_sinks'].astype(F32)[:, None], (SWA_HEADS, LANES))

    full = lambda shape: pl.BlockSpec(shape, lambda i: (0,) * len(shape))
    rowb = lambda w: pl.BlockSpec((bb, w), lambda i: (i, 0))
    in_specs = [
        full((nb, N_PROJ)), rowb(LANES), full((1, LANES)), rowb(GROUP_WIDTH),
        pl.BlockSpec((bb, MLSTM_HEADS, HEAD_W, HEAD_W), lambda i: (lb + i, 0, 0, 0)),
        pl.BlockSpec((bb, GLA_HEADS, GLA_DK, HEAD_W), lambda i: (lb + i, 0, 0, 0)),
        pl.BlockSpec((bb, CONV_WIDTH - 1, GROUP_WIDTH), lambda i: (lb + i, 0, 0)),
        pl.BlockSpec((bb, WINDOW, LANES), lambda i: (lb + i, 0, 0)),
        pl.BlockSpec((bb, WINDOW, LANES), lambda i: (lb + i, 0, 0)),
        pl.BlockSpec((bb, SWA_HEADS, LANES), lambda i: (i, 0, 0)),
        full((SWA_HEADS, LANES)), full((LANES, 256)), full((1, 256)),
        full((1, GROUP_WIDTH)), full((1, GROUP_WIDTH)),
        full((CONV_WIDTH, GROUP_WIDTH)), full((1, GROUP_WIDTH)), full((1, GROUP_WIDTH)), full((1, GROUP_WIDTH)),
    ]
    out_specs = [
        rowb(3 * GROUP_WIDTH),
        pl.BlockSpec((bb, SWA_HEADS, LANES), lambda i: (i, 0, 0)),
        pl.BlockSpec((bb, MLSTM_HEADS, HEAD_W, HEAD_W), lambda i: (i, 0, 0, 0)),
        rowb(GROUP_WIDTH), rowb(LANES),
        pl.BlockSpec((bb, GLA_HEADS, GLA_DK, HEAD_W), lambda i: (i, 0, 0, 0)),
        pl.BlockSpec((bb, CONV_WIDTH - 1, GROUP_WIDTH), lambda i: (i, 0, 0)),
        pl.BlockSpec((bb, WINDOW, LANES), lambda i: (i, 0, 0)),
        pl.BlockSpec((bb, WINDOW, LANES), lambda i: (i, 0, 0)),
    ]
    out_shape = [
        jax.ShapeDtypeStruct((nb, 3 * GROUP_WIDTH), F32),
        jax.ShapeDtypeStruct((nb, SWA_HEADS, LANES), F32),
        jax.ShapeDtypeStruct(c_all.shape[1:], F32),
        jax.ShapeDtypeStruct((nb, GROUP_WIDTH), F32),
        jax.ShapeDtypeStruct((nb, LANES), F32),
        jax.ShapeDtypeStruct(s_all.shape[1:], F32),
        jax.ShapeDtypeStruct(cv_all.shape[1:], F32),
        jax.ShapeDtypeStruct((nb, WINDOW, LANES), F32),
        jax.ShapeDtypeStruct((nb, WINDOW, LANES), F32),
    ]
    scratch = [
        pltpu.VMEM((_T_ROWS, nb), BF16), pltpu.VMEM((_T_ROWS, LANES), F32),
        pltpu.VMEM((bb, GROUP_WIDTH), F32), pltpu.VMEM((bb, GROUP_WIDTH), F32), pltpu.VMEM((bb, GROUP_WIDTH), F32),
    ]
    y, od, c1, n1, m1, s1, cv1, k1, v1 = pl.pallas_call(
        _sample_kernel,
        grid=(nb // bb,),
        in_specs=in_specs, out_specs=out_specs, out_shape=out_shape, scratch_shapes=scratch,
        compiler_params=_cparams(("arbitrary",)),
        name="sample_mixers",
    )(proj_s, mm, prm['gate_bias'], n0f, c0, s0, cv0, k0f, v0f, qm, sink_b,
      prm['gla_w_gate_pad'], prm['gla_b_gate'].reshape(1, 256),
      prm['mlstm_norm_g'].reshape(1, -1), prm['gla_norm_g'].reshape(1, -1),
      prm['conv_w'], prm['conv_b'].reshape(1, -1), prm['conv_norm_g'].reshape(1, -1),
      prm['conv_norm_b'].reshape(1, -1))
    od4 = od.reshape(nb, SWA_KV_HEADS, SWA_HEADS // SWA_KV_HEADS, LANES)
    yd = jnp.concatenate([od4[:, 0, :, :SWA_HEAD_DIM], od4[:, 1, :, SWA_HEAD_DIM:]], axis=1).reshape(nb, GROUP_WIDTH)
    y = jnp.concatenate([y, yd], axis=1).astype(BF16)
    new_state = (c1, n1.reshape(n0.shape), m1[:, :MLSTM_HEADS], s1, cv1,
                 k1.reshape(k_all.shape[1:]), v1.reshape(v_all.shape[1:]))
    return y, new_state


def _outproj_router_kernel(x_ref, mix_ref, w_ref, g_ref, rw_ref, rb_ref,
                           x1_ref, hn_ref, ri_ref, rf_ref, cnt_ref):
    tm = x_ref.shape[0]

    @pl.when(pl.program_id(0) == 0)
    def _():
        cnt_ref[...] = jnp.zeros_like(cnt_ref)

    x1 = x_ref[...] + jnp.dot(mix_ref[...], w_ref[...], preferred_element_type=F32)
    x1_ref[...] = x1
    ms = jnp.mean(x1 * x1, axis=-1, keepdims=True)
    hn = x1 * lax.rsqrt(ms + NORM_EPS) * g_ref[...]
    hn_ref[...] = hn
    logits = jnp.dot(hn.astype(BF16), rw_ref[...], preferred_element_type=F32) + rb_ref[...]
    lane = lax.broadcasted_iota(I32, (tm, LANES), 1)
    big = jnp.int32(LANES)
    gl = jnp.where(lane < N_GROUPS, logits, -jnp.inf)
    gmax = jnp.max(gl, axis=-1, keepdims=True)
    g_sel = jnp.min(jnp.where(gl == gmax, lane, big), axis=-1, keepdims=True)
    g_w = 1.0 / jnp.sum(jnp.exp(gl - gmax), axis=-1, keepdims=True)
    e_lane = lane - N_GROUPS
    in_grp = (e_lane >= 0) & (e_lane < N_EXPERTS) & ((e_lane // EXPERTS_PER_GROUP) == g_sel)
    el = jnp.where(in_grp, logits, -jnp.inf)
    m1 = jnp.max(el, axis=-1, keepdims=True)
    i1 = jnp.min(jnp.where(el == m1, lane, big), axis=-1, keepdims=True)
    el2 = jnp.where(lane == i1, -jnp.inf, el)
    m2 = jnp.max(el2, axis=-1, keepdims=True)
    i2 = jnp.min(jnp.where(el2 == m2, lane, big), axis=-1, keepdims=True)
    r = jnp.exp(m2 - m1)
    p1 = 1.0 / (1.0 + r)
    gate1 = g_w * p1
    gate2 = g_w * (r * p1)
    sel1 = lane == i1
    sel2 = lane == i2
    onehot = jnp.where(sel1 | sel2, 1.0, 0.0)
    row = lax.broadcasted_iota(I32, (tm, tm), 0)
    col = lax.broadcasted_iota(I32, (tm, tm), 1)
    strict = jnp.where(col < row, 1.0, 0.0).astype(BF16)
    cum = jnp.dot(strict, onehot.astype(BF16), preferred_element_type=F32) + cnt_ref[0:1, :]
    rank1 = jnp.sum(jnp.where(sel1, cum, 0.0), axis=-1, keepdims=True).astype(I32)
    rank2 = jnp.sum(jnp.where(sel2, cum, 0.0), axis=-1, keepdims=True).astype(I32)
    cnt_ref[...] = cnt_ref[...] + jnp.sum(onehot, axis=0, keepdims=True)
    ri = jnp.where(lane == 0, i1 - N_GROUPS, jnp.where(lane == 1, i2 - N_GROUPS,
                   jnp.where(lane == 2, rank1, jnp.where(lane == 3, rank2, 0))))
    ri_ref[...] = ri
    rf_ref[...] = jnp.where(lane == 0, gate1, jnp.where(lane == 1, gate2, 0.0))


def _outproj_router(x, mixed, w_out_bf16, norm_g, rw_pad, rb_pad):
    n = x.shape[0]
    tm = _pick_tile(n, (320, 256, 128, 64, 16))
    full = lambda shape: pl.BlockSpec(shape, lambda i: (0,) * len(shape))
    rowb = lambda w: pl.BlockSpec((tm, w), lambda i: (i, 0))
    return pl.pallas_call(
        _outproj_router_kernel,
        grid=(n // tm,),
        in_specs=[rowb(D_MODEL), rowb(D_MODEL), full((D_MODEL, D_MODEL)), full((1, D_MODEL)),
                  full((D_MODEL, LANES)), full((1, LANES))],
        out_specs=[rowb(D_MODEL), rowb(D_MODEL), rowb(LANES), rowb(LANES), full((SUBLANES, LANES))],
        out_shape=[
            jax.ShapeDtypeStruct((n, D_MODEL), F32),
            jax.ShapeDtypeStruct((n, D_MODEL), F32),
            jax.ShapeDtypeStruct((n, LANES), I32),
            jax.ShapeDtypeStruct((n, LANES), F32),
            jax.ShapeDtypeStruct((SUBLANES, LANES), F32),
        ],
        compiler_params=_cparams(("arbitrary",)),
        name="outproj_router",
    )(x, mixed, w_out_bf16, norm_g.reshape(1, D_MODEL), rw_pad, rb_pad)


def _cast_kernel(x_ref, o_ref):
    o_ref[...] = x_ref[...].astype(o_ref.dtype)


def _cast_bf16(w_all, layer):
    depth, n_e, k, f = w_all.shape
    rows = n_e * k
    tr = _pick_tile(rows, (1024, 512, 256, 128, 16))
    steps = rows // tr
    out = pl.pallas_call(
        _cast_kernel,
        grid=(steps,),
        in_specs=[pl.BlockSpec((tr, f), lambda i: (layer * steps + i, 0))],
        out_specs=pl.BlockSpec((tr, f), lambda i: (i, 0)),
        out_shape=jax.ShapeDtypeStruct((rows, f), BF16),
        compiler_params=_cparams(("parallel",)),
        name="cast_bf16",
    )(w_all.reshape(depth * rows, f))
    return out.reshape(n_e, k, f)


def _expert_kernel(te_ref, nt_ref, src_ref, hn_ref, wg_ref, wu_ref, wd_ref, y_ref, xbuf, sem):
    del te_ref
    t = pl.program_id(0)
    n_used = nt_ref[0]
    tile = y_ref.shape[0]

    def gather(tile_idx, slot):
        base = tile_idx * tile

        def one(r, carry):
            pltpu.make_async_copy(hn_ref.at[pl.ds(src_ref[base + r], 1)], xbuf.at[slot, pl.ds(r, 1)],
                                  sem.at[slot]).start()
            return carry

        lax.fori_loop(0, tile, one, 0)

    @pl.when(t == 0)
    def _():
        gather(0, 0)

    @pl.when(t + 1 < n_used)
    def _():
        gather(t + 1, (t + 1) % 2)

    @pl.when(t < n_used)
    def _():
        slot = t % 2
        pltpu.make_async_copy(hn_ref.at[pl.ds(0, tile)], xbuf.at[slot], sem.at[slot]).wait()
        x = xbuf[slot].astype(BF16)
        a = jnp.dot(x, wg_ref[0], preferred_element_type=F32)
        u = jnp.dot(x, wu_ref[0], preferred_element_type=F32)
        hmid = (_silu(a) * u).astype(BF16)
        y_ref[...] = jnp.dot(hmid, wd_ref[0], preferred_element_type=F32)

    @pl.when(t >= n_used)
    def _():
        y_ref[...] = jnp.zeros_like(y_ref)


def _expert_mlp(tile_expert, n_tiles_used, src, hn, wg, wu, wd, tile):
    n_tiles = src.shape[0] // tile
    assert hn.shape[0] >= tile

    def w_map(t, te, nt, sr):
        return (te[jnp.minimum(t, nt[0] - 1)], 0, 0)

    return pl.pallas_call(
        _expert_kernel,
        grid_spec=pltpu.PrefetchScalarGridSpec(
            num_scalar_prefetch=3,
            grid=(n_tiles,),
            in_specs=[
                pl.BlockSpec(memory_space=pl.ANY),
                pl.BlockSpec((1, D_MODEL, D_EXPERT), w_map),
                pl.BlockSpec((1, D_MODEL, D_EXPERT), w_map),
                pl.BlockSpec((1, D_EXPERT, D_MODEL), w_map),
            ],
            out_specs=pl.BlockSpec((tile, D_MODEL), lambda t, te, nt, sr: (t, 0)),
            scratch_shapes=[pltpu.VMEM((2, tile, D_MODEL), F32), pltpu.SemaphoreType.DMA((2,))],
        ),
        out_shape=jax.ShapeDtypeStruct((n_tiles * tile, D_MODEL), F32),
        compiler_params=_cparams(("arbitrary",)),
        name="expert_mlp",
    )(tile_expert, n_tiles_used, src, hn, wg, wu, wd)


def _combine_kernel(pos_ref, x1_ref, rf_ref, fg_ref, ys_ref, o_ref, buf_a, buf_b, sem, *, final_norm):
    tm = x1_ref.shape[0]
    base = pl.program_id(0) * tm

    def issue(r, carry):
        tok = base + r
        pltpu.make_async_copy(ys_ref.at[pl.ds(pos_ref[2 * tok], 1)], buf_a.at[pl.ds(r, 1)], sem.at[0]).start()
        pltpu.make_async_copy(ys_ref.at[pl.ds(pos_ref[2 * tok + 1], 1)], buf_b.at[pl.ds(r, 1)], sem.at[1]).start()
        return carry

    lax.fori_loop(0, tm, issue, 0)
    pltpu.make_async_copy(ys_ref.at[pl.ds(0, tm)], buf_a, sem.at[0]).wait()
    pltpu.make_async_copy(ys_ref.at[pl.ds(0, tm)], buf_b, sem.at[1]).wait()
    rf = rf_ref[...]
    x2 = x1_ref[...] + rf[:, 0:1] * buf_a[...] + rf[:, 1:2] * buf_b[...]
    if final_norm:
        ms = jnp.mean(x2 * x2, axis=-1, keepdims=True)
        x2 = x2 * lax.rsqrt(ms + NORM_EPS) * fg_ref[...]
    o_ref[...] = x2


def _combine(pos, x1, rf, ys, final_g, final_norm):
    n = x1.shape[0]
    tm = _pick_tile(n, (256, 128, 64, 8))
    return pl.pallas_call(
        functools.partial(_combine_kernel, final_norm=final_norm),
        grid_spec=pltpu.PrefetchScalarGridSpec(
            num_scalar_prefetch=1,
            grid=(n // tm,),
            in_specs=[
                pl.BlockSpec((tm, D_MODEL), lambda i, p: (i, 0)),
                pl.BlockSpec((tm, LANES), lambda i, p: (i, 0)),
                pl.BlockSpec((1, D_MODEL), lambda i, p: (0, 0)),
                pl.BlockSpec(memory_space=pl.ANY),
            ],
            out_specs=pl.BlockSpec((tm, D_MODEL), lambda i, p: (i, 0)),
            scratch_shapes=[pltpu.VMEM((tm, D_MODEL), F32), pltpu.VMEM((tm, D_MODEL), F32),
                            pltpu.SemaphoreType.DMA((2,))],
        ),
        out_shape=jax.ShapeDtypeStruct((n, D_MODEL), F32),
        compiler_params=_cparams(("arbitrary",)),
        name="moe_combine",
    )(pos, x1, rf, final_g.reshape(1, D_MODEL), ys)


def _pad_w_in(w_in):
    parts = []
    at = 0
    for lo, hi, dst in sorted(_W_IN_SEGMENTS, key=lambda s: s[2]):
        if dst > at:
            parts.append(jnp.zeros((D_MODEL, dst - at), w_in.dtype))
        parts.append(w_in[:, lo:hi])
        at = dst + (hi - lo)
    if at < N_PROJ:
        parts.append(jnp.zeros((D_MODEL, N_PROJ - at), w_in.dtype))
    return jnp.concatenate(parts, axis=1)


def _moe(x1, hn, ri, rf, counts, prm, final_g, final_norm, tile):
    n = x1.shape[0]
    n_tiles = -(-2 * n // tile) + N_EXPERTS
    cnt = counts[0, N_GROUPS:N_GROUPS + N_EXPERTS].astype(I32)
    tiles_per = (cnt + tile - 1) // tile
    tile_end = jnp.cumsum(tiles_per)
    row_off = (tile_end - tiles_per) * tile
    pos = (row_off[ri[:, 0:2]] + ri[:, 2:4]).reshape(2 * n)
    tile_ids = jnp.arange(n_tiles, dtype=I32)
    tile_expert = jnp.minimum(jnp.sum((tile_ids[:, None] >= tile_end[None, :]).astype(I32), axis=1), N_EXPERTS - 1)
    n_used = tile_end[N_EXPERTS - 1:N_EXPERTS].astype(I32)
    src = jnp.zeros((n_tiles * tile,), I32).at[pos].set(jnp.arange(2 * n, dtype=I32) // 2)
    ys = _expert_mlp(tile_expert, n_used, src, hn, prm['expert_w_gate'], prm['expert_w_up'],
                     prm['expert_w_down'], tile)
    return _combine(pos, x1, rf, ys, final_g, final_norm)


def _layer(x, t_len, states, layer, prm, final_g, final_norm, moe_tile):
    proj = _in_projection(x, prm['norm_mix_g'], prm['w_in_pad'])
    ya, cn, m_p = _mlstm_prompt(proj, t_len, prm['gate_bias'], prm['mlstm_norm_g'])
    yb, conv_tail = _conv_prompt(proj, t_len, prm['conv_w'], prm['conv_b'], prm['conv_norm_g'], prm['conv_norm_b'])
    yc, sp = _gla_prompt(proj, t_len, prm['gla_w_gate_pad'], prm['gla_b_gate'], prm['gla_norm_g'])
    yd = _swa_prompt(proj, t_len, prm['swa_sinks'])
    y_s, new_s = _sample_mixers(proj[t_len:], states, layer, prm)
    mixed = jnp.concatenate([jnp.concatenate([ya, yb, yc, yd], axis=1), y_s], axis=0)
    x1, hn, ri, rf, counts = _outproj_router(x, mixed, prm['w_out'], prm['norm_ffn_g'], prm['router_w'], prm['router_b'])
    x2 = _moe(x1, hn, ri, rf, counts, prm, final_g, final_norm, moe_tile)
    p_c = cn[None, :, :, :HEAD_W]
    p_n = cn[None, :, :, HEAD_W]
    p_m = m_p[None, :MLSTM_HEADS, 0]
    p_s = jnp.stack([sp[0, :GLA_DK, :HEAD_W], sp[0, GLA_DK:, HEAD_W:],
                     sp[1, :GLA_DK, :HEAD_W], sp[1, GLA_DK:, HEAD_W:]])[None]
    p_conv = conv_tail[None, CONV_HALO - (CONV_WIDTH - 1):]
    p_k = proj[t_len - WINDOW:t_len, S_K:S_K + LANES].reshape(1, WINDOW, SWA_KV_HEADS, SWA_HEAD_DIM)
    p_v = proj[t_len - WINDOW:t_len, S_V:S_V + LANES].reshape(1, WINDOW, SWA_KV_HEADS, SWA_HEAD_DIM)
    return x2, (p_c, p_n, p_m, p_s, p_conv, p_k, p_v), new_s


def _forward(x_prompt, x_sample, states, layer_params, final_norm_g, moe_tile=MOE_TILE):
    t_len = x_prompt.shape[1]
    x = jnp.concatenate([x_prompt[0], x_sample[:, 0]], axis=0)
    new_p, new_s = [], []
    depth = len(layer_params)
    for l, prm in enumerate(layer_params):
        x, sp, ss = _layer(x, t_len, states, l, prm, final_norm_g, l == depth - 1, moe_tile)
        new_p.append(sp)
        new_s.append(ss)
    y_prompt = x[None, :t_len]
    y_sample = x[t_len:, None]
    p_states = [jnp.stack(parts) for parts in zip(*new_p)]
    s_states = [jnp.stack(parts) for parts in zip(*new_s)]
    return (y_prompt, y_sample, *p_states, *s_states)


def _prep_layer_params(l, norm_mix_g, w_in, mlstm_b_i, mlstm_b_f, mlstm_norm_g, conv_w, conv_b, conv_norm_g,
                       conv_norm_b, gla_w_gate, gla_b_gate, gla_norm_g, swa_sinks, w_out, norm_ffn_g,
                       router_group_w, router_group_b, router_expert_w, router_expert_b, expert_w_gate,
                       expert_w_up, expert_w_down):
    gate_bias = jnp.concatenate([mlstm_b_i[l], mlstm_b_f[l], jnp.zeros((LANES - 2 * MLSTM_HEADS,), F32)])
    rw = jnp.concatenate([router_group_w[l], router_expert_w[l],
                          jnp.zeros((D_MODEL, LANES - N_GROUPS - N_EXPERTS), F32)], axis=1)
    rb = jnp.concatenate([router_group_b[l], router_expert_b[l],
                          jnp.zeros((LANES - N_GROUPS - N_EXPERTS,), F32)])
    return {
        'norm_mix_g': norm_mix_g[l],
        'w_in_pad': _pad_w_in(w_in[l].astype(BF16)),
        'gate_bias': gate_bias.reshape(1, LANES),
        'mlstm_norm_g': mlstm_norm_g[l],
        'conv_w': conv_w[l], 'conv_b': conv_b[l], 'conv_norm_g': conv_norm_g[l], 'conv_norm_b': conv_norm_b[l],
        'gla_w_gate_pad': jnp.concatenate(
            [gla_w_gate[l], jnp.zeros((LANES - GLA_LOWRANK, GLA_HEADS * GLA_DK), F32)], axis=0).astype(BF16),
        'gla_b_gate': gla_b_gate[l], 'gla_norm_g': gla_norm_g[l],
        'swa_sinks': swa_sinks[l],
        'w_out': w_out[l].astype(BF16),
        'norm_ffn_g': norm_ffn_g[l],
        'router_w': rw.astype(BF16), 'router_b': rb.reshape(1, LANES),
        'expert_w_gate': _cast_bf16(expert_w_gate, l),
        'expert_w_up': _cast_bf16(expert_w_up, l),
        'expert_w_down': _cast_bf16(expert_w_down, l),
    }


def kernel(x_prompt, x_sample, state_mlstm_C, state_mlstm_n, state_mlstm_m, state_gla_S, cache_conv, cache_swa_k, cache_swa_v, norm_mix_g, w_in, mlstm_b_i, mlstm_b_f, mlstm_norm_g, conv_w, conv_b, conv_norm_g, conv_norm_b, gla_w_gate, gla_b_gate, gla_norm_g, swa_sinks, w_out, norm_ffn_g, router_group_w, router_group_b, router_expert_w, router_expert_b, expert_w_gate, expert_w_up, expert_w_down, final_norm_g):
    depth = w_in.shape[0]
    weights = (norm_mix_g, w_in, mlstm_b_i, mlstm_b_f, mlstm_norm_g, conv_w, conv_b, conv_norm_g, conv_norm_b,
               gla_w_gate, gla_b_gate, gla_norm_g, swa_sinks, w_out, norm_ffn_g, router_group_w, router_group_b,
               router_expert_w, router_expert_b, expert_w_gate, expert_w_up, expert_w_down)
    layer_params = [_prep_layer_params(l, *weights) for l in range(depth)]
    states = (state_mlstm_C, state_mlstm_n, state_mlstm_m, state_gla_S, cache_conv, cache_swa_k, cache_swa_v)
    return _forward(x_prompt, x_sample, states, layer_params, final_norm_g)
```

```python
import functools

import jax
import jax.numpy as jnp
from jax import lax
from jax.experimental import pallas as pl
from jax.experimental.pallas import tpu as pltpu

F32 = jnp.float32
BF16 = jnp.bfloat16
I32 = jnp.int32
HIGHEST = lax.Precision.HIGHEST

D_MODEL = 2048
GROUP_WIDTH = 512
HEAD_W = 128
MLSTM_HEADS = 4
GLA_HEADS = 4
GLA_DK = 64
GLA_LOWRANK = 16
GLA_TAU = 16.0
CONV_WIDTH = 31
SWA_HEADS = 8
SWA_KV_HEADS = 2
SWA_HEAD_DIM = 64
WINDOW = 128
N_GROUPS = 4
EXPERTS_PER_GROUP = 4
N_EXPERTS = 16
D_EXPERT = 1024
NORM_EPS = 1e-6
LANES = 128
SUBLANES = 8
TOK_ROWS = D_MODEL // LANES
PACK_ROWS = TOK_ROWS // 2

A_Q, A_K, A_V, A_O = 0, 512, 1024, 1536
C_A, C_G = 2048, 2560
G_V, G_R = 3072, 3584
S_Q = 4096
G_Q, G_K = 4608, 4864
A_G, G_LR, S_K, S_V = 5120, 5248, 5376, 5504
N_PROJ = 5632
_W_IN_SEGMENTS = (
    (0, 512, A_Q), (512, 1024, A_K), (1024, 1536, A_V), (1536, 2048, A_O),
    (2048, 2056, A_G),
    (2056, 2568, C_A), (2568, 3080, C_G),
    (3080, 3336, G_Q), (3336, 3592, G_K), (3592, 4104, G_V), (4104, 4616, G_R),
    (4616, 4632, G_LR),
    (4632, 5144, S_Q), (5144, 5272, S_K), (5272, 5400, S_V),
)

MLSTM_CHUNK = 128
GLA_CHUNK = 64
GLA_STEP_ROWS = 256
GLA_SUB = 16
CONV_TILE = 256
CONV_ROWS = 64
CONV_HALO = 32
SAMPLE_BB = 8
MOE_TILE = 512
VMEM_LIMIT = 56 * 1024 * 1024


def _cparams(sem, vmem=VMEM_LIMIT):
    return pltpu.CompilerParams(dimension_semantics=sem, vmem_limit_bytes=vmem)


def _log_sigmoid(x):
    return jnp.minimum(x, 0.0) - jnp.log1p(jnp.exp(-jnp.abs(x)))


def _sigmoid(x):
    return 1.0 / (1.0 + jnp.exp(-x))


def _silu(x):
    return x * _sigmoid(x)


def _pick_tile(n, candidates):
    for c in candidates:
        if n % c == 0:
            return c
    raise ValueError(f"no tile for {n} in {candidates}")


def _proj_kernel(x_ref, g_ref, w_ref, o_ref, hn_ref):
    @pl.when(pl.program_id(1) == 0)
    def _():
        x = x_ref[...]
        ms = jnp.mean(x * x, axis=-1, keepdims=True)
        hn_ref[...] = (x * lax.rsqrt(ms + NORM_EPS) * g_ref[...]).astype(BF16)

    o_ref[...] = jnp.dot(hn_ref[...], w_ref[...], preferred_element_type=F32)


def _in_projection(x, g, w_bf16):
    n = x.shape[0]
    tm = _pick_tile(n, (832, 640, 512, 256, 128, 64, 8))
    tn = N_PROJ // 4
    return pl.pallas_call(
        _proj_kernel,
        grid=(n // tm, N_PROJ // tn),
        in_specs=[
            pl.BlockSpec((tm, D_MODEL), lambda i, j: (i, 0)),
            pl.BlockSpec((1, D_MODEL), lambda i, j: (0, 0)),
            pl.BlockSpec((D_MODEL, tn), lambda i, j: (0, j)),
        ],
        out_specs=pl.BlockSpec((tm, tn), lambda i, j: (i, j)),
        out_shape=jax.ShapeDtypeStruct((n, N_PROJ), F32),
        scratch_shapes=[pltpu.VMEM((tm, D_MODEL), BF16)],
        compiler_params=_cparams(("parallel", "arbitrary")),
        name="in_projection",
    )(x, g.reshape(1, D_MODEL), w_bf16)


def _mlstm_prompt_kernel(q_ref, k_ref, v_ref, o_ref, gt_ref, bias_ref, ng_ref,
                         y_ref, cn_ref, m_ref):
    L = q_ref.shape[0]

    @pl.when(pl.program_id(0) == 0)
    def _():
        cn_ref[...] = jnp.zeros_like(cn_ref)
        m_ref[...] = jnp.zeros_like(m_ref)

    pre = gt_ref[...] + bias_ref[...]
    lf = _log_sigmoid(pre)
    row = lax.broadcasted_iota(I32, (L, L), 0)
    col = lax.broadcasted_iota(I32, (L, L), 1)
    tri = col <= row
    b_all = jnp.dot(tri.astype(F32), lf, precision=HIGHEST, preferred_element_type=F32)
    pre_t = pre.T
    b_t = b_all.T
    lane = lax.broadcasted_iota(I32, (L, HEAD_W), 1)
    ones_col = (lane == 0).astype(BF16)
    for h in range(MLSTM_HEADS):
        sl = slice(HEAD_W * h, HEAD_W * (h + 1))
        q = q_ref[:, sl]
        k = k_ref[:, sl] * (HEAD_W ** -0.5)
        v = v_ref[:, sl]
        b_col = b_all[:, 4 + h:5 + h]
        i_col = pre[:, h:h + 1]
        b_row = b_t[4 + h:5 + h, :]
        i_row = pre_t[h:h + 1, :]
        m_prev = m_ref[h:h + 1, 0:1]
        log_d = jnp.where(tri, b_col - b_row + i_row, -jnp.inf)
        log_inter = b_col + m_prev
        m_t = jnp.maximum(log_inter, jnp.max(log_d, axis=-1, keepdims=True))
        d_mat = jnp.exp(log_d - m_t)
        g_inter = jnp.exp(log_inter - m_t)
        qb = q.astype(BF16)
        kb = k.astype(BF16)
        s = lax.dot_general(qb, kb, (((1,), (1,)), ((), ())), preferred_element_type=F32)
        w = (s * d_mat).astype(BF16)
        v1 = jnp.concatenate([v.astype(BF16), ones_col], axis=1)
        cn = cn_ref[h]
        nd = g_inter * jnp.dot(qb, cn.astype(BF16), preferred_element_type=F32)
        nd = nd + jnp.dot(w, v1, preferred_element_type=F32)
        num = nd[:, :HEAD_W]
        den = nd[:, HEAD_W:HEAD_W + 1]
        hh = num / jnp.maximum(jnp.abs(den), jnp.exp(-m_t))
        hh = _sigmoid(o_ref[:, sl]) * hh
        hh = hh * lax.rsqrt(jnp.mean(hh * hh, axis=-1, keepdims=True) + NORM_EPS) * ng_ref[:, sl]
        y_ref[:, sl] = hh.astype(y_ref.dtype)
        m_last = m_t[L - 1:L, :]
        b_last = b_col[L - 1:L, :]
        g_state = jnp.exp(b_last + m_prev - m_last)
        w_k = jnp.exp(b_last - b_col + i_col - m_last)
        kw = (k * w_k).astype(BF16)
        upd = lax.dot_general(kw, v1, (((0,), (0,)), ((), ())), preferred_element_type=F32)
        cn_ref[h] = g_state * cn + upd
        m_ref[h:h + 1, :] = jnp.broadcast_to(m_last, (1, LANES))


def _mlstm_prompt(proj, t_len, bias_row, norm_g):
    L = _pick_tile(t_len, (MLSTM_CHUNK, 64, 32, 16, 8))

    def col(off):
        return pl.BlockSpec((L, GROUP_WIDTH), lambda i, o=off: (i, o // GROUP_WIDTH))

    return pl.pallas_call(
        _mlstm_prompt_kernel,
        grid=(t_len // L,),
        in_specs=[
            col(A_Q), col(A_K), col(A_V), col(A_O),
            pl.BlockSpec((L, LANES), lambda i: (i, A_G // LANES)),
            pl.BlockSpec((1, LANES), lambda i: (0, 0)),
            pl.BlockSpec((1, GROUP_WIDTH), lambda i: (0, 0)),
        ],
        out_specs=[
            pl.BlockSpec((L, GROUP_WIDTH), lambda i: (i, 0)),
            pl.BlockSpec((MLSTM_HEADS, HEAD_W, 2 * HEAD_W), lambda i: (0, 0, 0)),
            pl.BlockSpec((SUBLANES, LANES), lambda i: (0, 0)),
        ],
        out_shape=[
            jax.ShapeDtypeStruct((t_len, GROUP_WIDTH), BF16),
            jax.ShapeDtypeStruct((MLSTM_HEADS, HEAD_W, 2 * HEAD_W), F32),
            jax.ShapeDtypeStruct((SUBLANES, LANES), F32),
        ],
        compiler_params=_cparams(("arbitrary",)),
        name="mlstm_prompt",
    )(proj, proj, proj, proj, proj, bias_row, norm_g.reshape(1, GROUP_WIDTH))


def _gla_prompt_kernel(q_ref, k_ref, v_ref, r_ref, lr_ref, wg_ref, bg_ref, ng_ref, y_ref, sp_ref):
    rows_step = q_ref.shape[0]
    L = min(GLA_CHUNK, rows_step)
    n_chunks = rows_step // L
    n_sub = L // GLA_SUB

    @pl.when(pl.program_id(0) == 0)
    def _():
        sp_ref[...] = jnp.zeros_like(sp_ref)

    gate_pre = jnp.dot(lr_ref[...].astype(BF16), wg_ref[...], preferred_element_type=F32) + bg_ref[...]
    log_a = _log_sigmoid(gate_pre) * (1.0 / GLA_TAU)
    row = lax.broadcasted_iota(I32, (rows_step, rows_step), 0)
    col = lax.broadcasted_iota(I32, (rows_step, rows_step), 1)
    tri = ((row // L) == (col // L)) & (col <= row)
    b = jnp.dot(tri.astype(F32), log_a, precision=HIGHEST, preferred_element_type=F32)
    q = q_ref[...] * (GLA_DK ** -0.5)
    k = k_ref[...]
    b_last = jnp.concatenate(
        [jnp.broadcast_to(b[L * c + L - 1:L * c + L, :], (L, b.shape[1])) for c in range(n_chunks)], axis=0)
    q_in = q * jnp.exp(b)
    k_dec = k * jnp.exp(b_last - b)
    lane16 = lax.broadcasted_iota(I32, (GLA_SUB, LANES), 1)
    lo16 = lane16 < GLA_DK
    srow = lax.broadcasted_iota(I32, (LANES, 2 * HEAD_W), 0)
    scol = lax.broadcasted_iota(I32, (LANES, 2 * HEAD_W), 1)
    block_diag = (srow < GLA_DK) == (scol < HEAD_W)
    for p in range(2):
        pls = slice(LANES * p, LANES * (p + 1))
        vp = v_ref[:, 2 * HEAD_W * p:2 * HEAD_W * (p + 1)].astype(BF16)
        b_t = b[:, pls].T
        sp = sp_ref[p]
        states = []
        for c in range(n_chunks):
            cs = slice(L * c, L * (c + 1))
            states.append(sp)
            dec_col = jnp.exp(b_t[:, L * c + L - 1:L * c + L])
            upd = lax.dot_general(k_dec[cs, pls].astype(BF16), vp[cs], (((0,), (0,)), ((), ())),
                                  preferred_element_type=F32)
            sp = jnp.where(block_diag, dec_col * sp + upd, 0.0)
        sp_ref[p] = sp
        outs = []
        for c in range(n_chunks):
            c0 = L * c
            o_inter = jnp.dot(q_in[c0:c0 + L, pls].astype(BF16), states[c].astype(BF16),
                              preferred_element_type=F32)
            rows = []
            for blk in range(n_sub):
                r0 = GLA_SUB * blk
                n = GLA_SUB * (blk + 1)
                qrows = slice(c0 + r0, c0 + r0 + GLA_SUB)
                krows = slice(c0, c0 + n)
                if blk == 0:
                    qs = q[qrows, pls] * jnp.exp(b[qrows, pls])
                    ks = k[krows, pls] * jnp.exp(-b[krows, pls])
                else:
                    anchor = b[c0 + r0 - 1:c0 + r0, pls]
                    qs = q[qrows, pls] * jnp.exp(b[qrows, pls] - anchor)
                    ks = k[krows, pls] * jnp.exp(anchor - b[krows, pls])
                qs2 = jnp.concatenate([jnp.where(lo16, qs, 0.0), jnp.where(lo16, 0.0, qs)], axis=0)
                att = lax.dot_general(qs2.astype(BF16), ks.astype(BF16), (((1,), (1,)), ((), ())),
                                      preferred_element_type=F32)
                trow = lax.broadcasted_iota(I32, (2 * GLA_SUB, n), 0)
                tcol = lax.broadcasted_iota(I32, (2 * GLA_SUB, n), 1)
                t_idx = r0 + jnp.where(trow >= GLA_SUB, trow - GLA_SUB, trow)
                att = jnp.where(tcol <= t_idx, att, 0.0)
                o2 = jnp.dot(att.astype(BF16), vp[krows], preferred_element_type=F32)
                rows.append(jnp.concatenate([o2[:GLA_SUB, :HEAD_W], o2[GLA_SUB:, HEAD_W:]], axis=1))
            outs.append(o_inter + jnp.concatenate(rows, axis=0))
        o = jnp.concatenate(outs, axis=0)
        for hh in range(2):
            head = 2 * p + hh
            hs = slice(HEAD_W * head, HEAD_W * (head + 1))
            oh = o[:, HEAD_W * hh:HEAD_W * (hh + 1)]
            oh = oh * lax.rsqrt(jnp.mean(oh * oh, axis=-1, keepdims=True) + NORM_EPS) * ng_ref[:, hs]
            y_ref[:, hs] = (oh * _silu(r_ref[:, hs])).astype(y_ref.dtype)


def _gla_prompt(proj, t_len, w_gate_pad, b_gate, norm_g):
    L = _pick_tile(t_len, (GLA_STEP_ROWS, GLA_CHUNK))
    return pl.pallas_call(
        _gla_prompt_kernel,
        grid=(t_len // L,),
        in_specs=[
            pl.BlockSpec((L, 256), lambda i: (i, G_Q // 256)),
            pl.BlockSpec((L, 256), lambda i: (i, G_K // 256)),
            pl.BlockSpec((L, GROUP_WIDTH), lambda i: (i, G_V // GROUP_WIDTH)),
            pl.BlockSpec((L, GROUP_WIDTH), lambda i: (i, G_R // GROUP_WIDTH)),
            pl.BlockSpec((L, LANES), lambda i: (i, G_LR // LANES)),
            pl.BlockSpec((LANES, 256), lambda i: (0, 0)),
            pl.BlockSpec((1, 256), lambda i: (0, 0)),
            pl.BlockSpec((1, GROUP_WIDTH), lambda i: (0, 0)),
        ],
        out_specs=[
            pl.BlockSpec((L, GROUP_WIDTH), lambda i: (i, 0)),
            pl.BlockSpec((2, LANES, 2 * HEAD_W), lambda i: (0, 0, 0)),
        ],
        out_shape=[
            jax.ShapeDtypeStruct((t_len, GROUP_WIDTH), BF16),
            jax.ShapeDtypeStruct((2, LANES, 2 * HEAD_W), F32),
        ],
        compiler_params=_cparams(("arbitrary",)),
        name="gla_prompt",
    )(proj, proj, proj, proj, proj, w_gate_pad, b_gate.reshape(1, 256), norm_g.reshape(1, GROUP_WIDTH))


def _conv_norm_act(y, g_ref, be_ref):
    mu = jnp.mean(y, axis=-1, keepdims=True)
    yc = y - mu
    var = jnp.mean(yc * yc, axis=-1, keepdims=True)
    return _silu(yc * lax.rsqrt(var + NORM_EPS) * g_ref[...] + be_ref[...])


def _conv_prompt_kernel(ua_ref, ug_ref, ha_ref, hg_ref, w_ref, b_ref, g_ref, be_ref,
                        y_ref, tail_ref, buf_ref, sh_ref):
    tt = ua_ref.shape[0]
    span = tt + CONV_HALO
    halo = ha_ref[...] * _sigmoid(hg_ref[...])
    buf_ref[0:CONV_HALO, :] = jnp.where(pl.program_id(0) > 0, halo, 0.0)
    buf_ref[CONV_HALO:span, :] = ua_ref[...] * _sigmoid(ug_ref[...])
    buf_ref[span:span + SUBLANES, :] = jnp.zeros((SUBLANES, GROUP_WIDTH), F32)
    for k in range(1, SUBLANES):
        sh_ref[k] = buf_ref[k:k + span, :]
    base = CONV_HALO - (CONV_WIDTH - 1)
    for r in range(tt // CONV_ROWS):
        acc = jnp.zeros((CONV_ROWS, GROUP_WIDTH), F32)
        for j in range(CONV_WIDTH):
            s0 = r * CONV_ROWS + base + j
            k = s0 % SUBLANES
            a0 = s0 - k
            win = buf_ref[a0:a0 + CONV_ROWS, :] if k == 0 else sh_ref[k, a0:a0 + CONV_ROWS, :]
            acc = acc + w_ref[j:j + 1, :] * win
        y = _conv_norm_act(acc + b_ref[...], g_ref, be_ref)
        y_ref[r * CONV_ROWS:(r + 1) * CONV_ROWS, :] = y.astype(y_ref.dtype)
    tail_ref[...] = buf_ref[tt:span, :]


def _conv_prompt(proj, t_len, w, b, g, beta):
    tt = _pick_tile(t_len, (CONV_TILE, 128, 64))
    ratio = tt // CONV_HALO
    vec = lambda: pl.BlockSpec((1, GROUP_WIDTH), lambda i: (0, 0))
    return pl.pallas_call(
        _conv_prompt_kernel,
        grid=(t_len // tt,),
        in_specs=[
            pl.BlockSpec((tt, GROUP_WIDTH), lambda i: (i, C_A // GROUP_WIDTH)),
            pl.BlockSpec((tt, GROUP_WIDTH), lambda i: (i, C_G // GROUP_WIDTH)),
            pl.BlockSpec((CONV_HALO, GROUP_WIDTH), lambda i: (jnp.maximum(i * ratio - 1, 0), C_A // GROUP_WIDTH)),
            pl.BlockSpec((CONV_HALO, GROUP_WIDTH), lambda i: (jnp.maximum(i * ratio - 1, 0), C_G // GROUP_WIDTH)),
            pl.BlockSpec((CONV_WIDTH, GROUP_WIDTH), lambda i: (0, 0)),
            vec(), vec(), vec(),
        ],
        out_specs=[
            pl.BlockSpec((tt, GROUP_WIDTH), lambda i: (i, 0)),
            pl.BlockSpec((CONV_HALO, GROUP_WIDTH), lambda i: (0, 0)),
        ],
        out_shape=[
            jax.ShapeDtypeStruct((t_len, GROUP_WIDTH), BF16),
            jax.ShapeDtypeStruct((CONV_HALO, GROUP_WIDTH), F32),
        ],
        scratch_shapes=[pltpu.VMEM((tt + CONV_HALO + SUBLANES, GROUP_WIDTH), F32),
                        pltpu.VMEM((SUBLANES, tt + CONV_HALO, GROUP_WIDTH), F32)],
        compiler_params=_cparams(("arbitrary",)),
        name="conv_prompt",
    )(proj, proj, proj, proj, w, b.reshape(1, -1), g.reshape(1, -1), beta.reshape(1, -1))


def _swa_prompt_kernel(sink_ref, q_ref, kc_ref, vc_ref, kp_ref, vp_ref, y_ref):
    bq = q_ref.shape[0]
    first = pl.program_id(0) == 0
    k_all = jnp.concatenate([kp_ref[...], kc_ref[...]], axis=0)
    v_all = jnp.concatenate([vp_ref[...], vc_ref[...]], axis=0)
    k_sw = pltpu.roll(k_all, SWA_HEAD_DIM, 1).astype(BF16)
    v_sw = pltpu.roll(v_all, SWA_HEAD_DIM, 1).astype(BF16)
    k_all = k_all.astype(BF16)
    v_all = v_all.astype(BF16)
    tq = lax.broadcasted_iota(I32, (bq, 2 * bq), 0)
    kj = lax.broadcasted_iota(I32, (bq, 2 * bq), 1)
    valid = (kj > tq) & (kj <= tq + WINDOW) & (kj >= jnp.where(first, bq, 0))
    lane = lax.broadcasted_iota(I32, (bq, LANES), 1)
    lo = lane < SWA_HEAD_DIM
    rep = SWA_HEADS // SWA_KV_HEADS
    for c in range(SWA_HEADS // 2):
        qc = q_ref[:, LANES * c:LANES * (c + 1)] * (SWA_HEAD_DIM ** -0.5)
        outs = []
        for hh in range(2):
            h = 2 * c + hh
            g = h // rep
            qm = jnp.where(lo if hh == 0 else jnp.logical_not(lo), qc, 0.0).astype(BF16)
            k_use = k_all if g == hh else k_sw
            v_use = v_all if g == hh else v_sw
            s = lax.dot_general(qm, k_use, (((1,), (1,)), ((), ())), preferred_element_type=F32)
            s = jnp.where(valid, s, -jnp.inf)
            sink = sink_ref[h]
            mx = jnp.maximum(jnp.max(s, axis=-1, keepdims=True), sink)
            p = jnp.exp(s - mx)
            den = jnp.sum(p, axis=-1, keepdims=True) + jnp.exp(sink - mx)
            p = (p / den).astype(BF16)
            outs.append(jnp.dot(p, v_use, preferred_element_type=F32))
        y_ref[:, LANES * c:LANES * (c + 1)] = jnp.where(lo, outs[0], outs[1]).astype(y_ref.dtype)


def _swa_prompt(proj, t_len, sinks):
    bq = WINDOW
    assert t_len % bq == 0
    kv = lambda off, prev: pl.BlockSpec(
        (bq, LANES), (lambda i: (jnp.maximum(i - 1, 0), off // LANES)) if prev else (lambda i: (i, off // LANES)))
    return pl.pallas_call(
        _swa_prompt_kernel,
        grid=(t_len // bq,),
        in_specs=[
            pl.BlockSpec(memory_space=pltpu.SMEM),
            pl.BlockSpec((bq, GROUP_WIDTH), lambda i: (i, S_Q // GROUP_WIDTH)),
            kv(S_K, False), kv(S_V, False), kv(S_K, True), kv(S_V, True),
        ],
        out_specs=pl.BlockSpec((bq, GROUP_WIDTH), lambda i: (i, 0)),
        out_shape=jax.ShapeDtypeStruct((t_len, GROUP_WIDTH), BF16),
        compiler_params=_cparams(("parallel",)),
        name="swa_prompt",
    )(sinks, proj, proj, proj, proj, proj)


_T_MK, _T_MQ = 0, 512
_T_GA, _T_GK, _T_GQ = 1024, 1280, 1536
_T_ROWS = 1792


def _sample_kernel(ps_ref, mm_ref, bias_ref, n0_ref, c0_ref, s0_ref, cv0_ref, k0_ref, v0_ref,
                   qm_ref, sink_ref, wg_ref, bg_ref, mng_ref, gng_ref, cw_ref, cb_ref, cg_ref, cbe_ref,
                   y_ref, od_ref, c1_ref, n1_ref, m1_ref, s1_ref, cv1_ref, k1_ref, v1_ref,
                   tt_ref, bc_ref, num_ref, go_ref, yc_ref):
    i = pl.program_id(0)
    nb = ps_ref.shape[0]
    bb = y_ref.shape[0]

    def gla_gate(lr):
        gp = jnp.dot(lr.astype(BF16), wg_ref[...], preferred_element_type=F32) + bg_ref[...]
        return jnp.exp(_log_sigmoid(gp) * (1.0 / GLA_TAU))

    @pl.when(i == 0)
    def _():
        for h in range(MLSTM_HEADS):
            kk = ps_ref[:, A_K + HEAD_W * h:A_K + HEAD_W * (h + 1)] * (HEAD_W ** -0.5)
            tt_ref[_T_MK + HEAD_W * h:_T_MK + HEAD_W * (h + 1), :] = kk.T.astype(BF16)
            qq = ps_ref[:, A_Q + HEAD_W * h:A_Q + HEAD_W * (h + 1)]
            tt_ref[_T_MQ + HEAD_W * h:_T_MQ + HEAD_W * (h + 1), :] = qq.T.astype(BF16)
        a_all = gla_gate(ps_ref[:, G_LR:G_LR + LANES])
        for p in range(2):
            pls = slice(LANES * p, LANES * (p + 1))
            tt_ref[_T_GA + LANES * p:_T_GA + LANES * (p + 1), :] = a_all[:, pls].T.astype(BF16)
            kk = ps_ref[:, G_K + LANES * p:G_K + LANES * (p + 1)]
            tt_ref[_T_GK + LANES * p:_T_GK + LANES * (p + 1), :] = kk.T.astype(BF16)
            qq = ps_ref[:, G_Q + LANES * p:G_Q + LANES * (p + 1)] * (GLA_DK ** -0.5)
            tt_ref[_T_GQ + LANES * p:_T_GQ + LANES * (p + 1), :] = qq.T.astype(BF16)

    r0 = pl.multiple_of(i * bb, bb)
    rows = pl.ds(r0, bb)

    pre = ps_ref[rows, A_G:A_G + LANES] + bias_ref[...]
    lfm = _log_sigmoid(pre) + mm_ref[...]
    f_al = pltpu.roll(lfm, LANES - MLSTM_HEADS, 1)
    m_t = jnp.maximum(f_al, pre)
    g_st = jnp.exp(f_al - m_t)
    w_k = jnp.exp(pre - m_t)
    m1_ref[...] = m_t
    n_new = []
    for h in range(MLSTM_HEADS):
        kk = ps_ref[rows, A_K + HEAD_W * h:A_K + HEAD_W * (h + 1)] * (HEAD_W ** -0.5)
        nn = g_st[:, h:h + 1] * n0_ref[:, HEAD_W * h:HEAD_W * (h + 1)] + w_k[:, h:h + 1] * kk
        n1_ref[:, HEAD_W * h:HEAD_W * (h + 1)] = nn
        n_new.append(nn)

    glu = ps_ref[rows, C_A:C_A + GROUP_WIDTH] * _sigmoid(ps_ref[rows, C_G:C_G + GROUP_WIDTH])
    yc_ref[...] = glu * cw_ref[CONV_WIDTH - 1:CONV_WIDTH, :]
    a_v = ps_ref[rows, A_V:A_V + GROUP_WIDTH]
    g_v = ps_ref[rows, G_V:G_V + GROUP_WIDTH]
    s_k = ps_ref[rows, S_K:S_K + LANES]
    s_v = ps_ref[rows, S_V:S_V + LANES]

    lane_b = lax.broadcasted_iota(I32, (nb, LANES), 0)
    key_row = lax.broadcasted_iota(I32, (SWA_HEADS, WINDOW), 1)
    sink_col = sink_ref[:, 0:1]

    for j in range(bb):
        onehot = (lane_b == r0 + j).astype(BF16)
        bc_ref[...] = jnp.dot(tt_ref[...], onehot, preferred_element_type=F32)
        jrow = slice(j, j + 1)
        for h in range(MLSTM_HEADS):
            hs = slice(HEAD_W * h, HEAD_W * (h + 1))
            kbc = bc_ref[_T_MK + HEAD_W * h:_T_MK + HEAD_W * (h + 1), :]
            qbc = bc_ref[_T_MQ + HEAD_W * h:_T_MQ + HEAD_W * (h + 1), :]
            g1 = g_st[jrow, h:h + 1]
            w1 = w_k[jrow, h:h + 1]
            v_row = a_v[jrow, hs]
            c_new = g1 * c0_ref[j, h] + kbc * (w1 * v_row)
            c1_ref[j, h] = c_new
            num_ref[jrow, hs] = jnp.sum(qbc * c_new, axis=0, keepdims=True)
        for p in range(2):
            abc = bc_ref[_T_GA + LANES * p:_T_GA + LANES * (p + 1), :]
            kbc = bc_ref[_T_GK + LANES * p:_T_GK + LANES * (p + 1), :]
            qbc = bc_ref[_T_GQ + LANES * p:_T_GQ + LANES * (p + 1), :]
            for hh in range(2):
                head = 2 * p + hh
                hs = slice(HEAD_W * head, HEAD_W * (head + 1))
                ds_ = slice(GLA_DK * hh, GLA_DK * (hh + 1))
                v_row = g_v[jrow, hs]
                s_new = abc[ds_, :] * s0_ref[j, head] + kbc[ds_, :] * v_row
                s1_ref[j, head] = s_new
                go_ref[jrow, hs] = jnp.sum(qbc[ds_, :] * s_new, axis=0, keepdims=True)
        cache = cv0_ref[j]
        yc_ref[jrow, :] = yc_ref[jrow, :] + jnp.sum(cache * cw_ref[0:CONV_WIDTH - 1, :], axis=0, keepdims=True)
        cv1_ref[j, 0:CONV_WIDTH - 2, :] = cv0_ref[j, 1:CONV_WIDTH - 1, :]
        cv1_ref[j, CONV_WIDTH - 2:CONV_WIDTH - 1, :] = glu[jrow, :]
        k_new = s_k[jrow, :]
        v_new = s_v[jrow, :]
        k1_ref[j, 0:WINDOW - 1, :] = k0_ref[j, 1:WINDOW, :]
        k1_ref[j, WINDOW - 1:WINDOW, :] = k_new
        v1_ref[j, 0:WINDOW - 1, :] = v0_ref[j, 1:WINDOW, :]
        v1_ref[j, WINDOW - 1:WINDOW, :] = v_new
        qmat = qm_ref[j] * (SWA_HEAD_DIM ** -0.5)
        s_old = lax.dot_general(qmat.astype(BF16), k0_ref[j].astype(BF16), (((1,), (1,)), ((), ())),
                                preferred_element_type=F32)
        s_old = jnp.where(key_row >= 1, s_old, -jnp.inf)
        s_cur = jnp.sum(qmat * k_new, axis=-1, keepdims=True)
        mx = jnp.maximum(jnp.maximum(jnp.max(s_old, axis=-1, keepdims=True), s_cur), sink_col)
        p_old = jnp.exp(s_old - mx)
        p_cur = jnp.exp(s_cur - mx)
        den = jnp.sum(p_old, axis=-1, keepdims=True) + p_cur + jnp.exp(sink_col - mx)
        o = jnp.dot((p_old / den).astype(BF16), v0_ref[j].astype(BF16), preferred_element_type=F32)
        od_ref[j] = o + (p_cur / den) * v_new

    for h in range(MLSTM_HEADS):
        hs = slice(HEAD_W * h, HEAD_W * (h + 1))
        qq = ps_ref[rows, A_Q + HEAD_W * h:A_Q + HEAD_W * (h + 1)]
        den = jnp.sum(qq * n_new[h], axis=-1, keepdims=True)
        hh = num_ref[:, hs] / jnp.maximum(jnp.abs(den), jnp.exp(-m_t[:, h:h + 1]))
        hh = _sigmoid(ps_ref[rows, A_O + HEAD_W * h:A_O + HEAD_W * (h + 1)]) * hh
        hh = hh * lax.rsqrt(jnp.mean(hh * hh, axis=-1, keepdims=True) + NORM_EPS) * mng_ref[:, hs]
        y_ref[:, hs] = hh.astype(y_ref.dtype)
    y_ref[:, GROUP_WIDTH:2 * GROUP_WIDTH] = _conv_norm_act(yc_ref[...] + cb_ref[...], cg_ref, cbe_ref).astype(y_ref.dtype)
    for head in range(GLA_HEADS):
        hs = slice(HEAD_W * head, HEAD_W * (head + 1))
        oh = go_ref[:, hs]
        oh = oh * lax.rsqrt(jnp.mean(oh * oh, axis=-1, keepdims=True) + NORM_EPS) * gng_ref[:, hs]
        gr = ps_ref[rows, G_R + HEAD_W * head:G_R + HEAD_W * (head + 1)]
        y_ref[:, 2 * GROUP_WIDTH + HEAD_W * head:2 * GROUP_WIDTH + HEAD_W * (head + 1)] = (
            oh * _silu(gr)).astype(y_ref.dtype)


def _sample_mixers(proj_s, states, layer, prm):
    c_all, n_all, m_all, s_all, cv_all, k_all, v_all = states
    depth = c_all.shape[0]
    nb = proj_s.shape[0]
    bb = SAMPLE_BB
    assert nb == LANES and nb % bb == 0
    lb = layer * (nb // bb)
    n0, m0 = n_all[layer], m_all[layer]
    mm = jnp.concatenate([m0, m0, jnp.zeros((nb, LANES - 2 * MLSTM_HEADS), F32)], axis=1)
    n0f = n0.reshape(nb, GROUP_WIDTH)
    c0 = c_all.reshape((depth * nb,) + c_all.shape[2:])
    s0 = s_all.reshape((depth * nb,) + s_all.shape[2:])
    cv0 = cv_all.reshape((depth * nb,) + cv_all.shape[2:])
    k0f = k_all.reshape(depth * nb, WINDOW, LANES)
    v0f = v_all.reshape(depth * nb, WINDOW, LANES)
    sq = proj_s[:, S_Q:S_Q + GROUP_WIDTH].reshape(nb, SWA_KV_HEADS, SWA_HEADS // SWA_KV_HEADS, SWA_HEAD_DIM)
    zq = jnp.zeros_like(sq[:, 0])
    qm = jnp.concatenate([jnp.concatenate([sq[:, 0], zq], axis=-1), jnp.concatenate([zq, sq[:, 1]], axis=-1)], axis=1)
    sink_b = jnp.broadcast_to(prm['swa_sinks'].astype(F32)[:, None], (SWA_HEADS, LANES))

    full = lambda shape: pl.BlockSpec(shape, lambda i: (0,) * len(shape))
    rowb = lambda w: pl.BlockSpec((bb, w), lambda i: (i, 0))
    in_specs = [
        full((nb, N_PROJ)), rowb(LANES), full((1, LANES)), rowb(GROUP_WIDTH),
        pl.BlockSpec((bb, MLSTM_HEADS, HEAD_W, HEAD_W), lambda i: (lb + i, 0, 0, 0)),
        pl.BlockSpec((bb, GLA_HEADS, GLA_DK, HEAD_W), lambda i: (lb + i, 0, 0, 0)),
        pl.BlockSpec((bb, CONV_WIDTH - 1, GROUP_WIDTH), lambda i: (lb + i, 0, 0)),
        pl.BlockSpec((bb, WINDOW, LANES), lambda i: (lb + i, 0, 0)),
        pl.BlockSpec((bb, WINDOW, LANES), lambda i: (lb + i, 0, 0)),
        pl.BlockSpec((bb, SWA_HEADS, LANES), lambda i: (i, 0, 0)),
        full((SWA_HEADS, LANES)), full((LANES, 256)), full((1, 256)),
        full((1, GROUP_WIDTH)), full((1, GROUP_WIDTH)),
        full((CONV_WIDTH, GROUP_WIDTH)), full((1, GROUP_WIDTH)), full((1, GROUP_WIDTH)), full((1, GROUP_WIDTH)),
    ]
    out_specs = [
        rowb(3 * GROUP_WIDTH),
        pl.BlockSpec((bb, SWA_HEADS, LANES), lambda i: (i, 0, 0)),
        pl.BlockSpec((bb, MLSTM_HEADS, HEAD_W, HEAD_W), lambda i: (i, 0, 0, 0)),
        rowb(GROUP_WIDTH), rowb(LANES),
        pl.BlockSpec((bb, GLA_HEADS, GLA_DK, HEAD_W), lambda i: (i, 0, 0, 0)),
        pl.BlockSpec((bb, CONV_WIDTH - 1, GROUP_WIDTH), lambda i: (i, 0, 0)),
        pl.BlockSpec((bb, WINDOW, LANES), lambda i: (i, 0, 0)),
        pl.BlockSpec((bb, WINDOW, LANES), lambda i: (i, 0, 0)),
    ]
    out_shape = [
        jax.ShapeDtypeStruct((nb, 3 * GROUP_WIDTH), F32),
        jax.ShapeDtypeStruct((nb, SWA_HEADS, LANES), F32),
        jax.ShapeDtypeStruct(c_all.shape[1:], F32),
        jax.ShapeDtypeStruct((nb, GROUP_WIDTH), F32),
        jax.ShapeDtypeStruct((nb, LANES), F32),
        jax.ShapeDtypeStruct(s_all.shape[1:], F32),
        jax.ShapeDtypeStruct(cv_all.shape[1:], F32),
        jax.ShapeDtypeStruct((nb, WINDOW, LANES), F32),
        jax.ShapeDtypeStruct((nb, WINDOW, LANES), F32),
    ]
    scratch = [
        pltpu.VMEM((_T_ROWS, nb), BF16), pltpu.VMEM((_T_ROWS, LANES), F32),
        pltpu.VMEM((bb, GROUP_WIDTH), F32), pltpu.VMEM((bb, GROUP_WIDTH), F32), pltpu.VMEM((bb, GROUP_WIDTH), F32),
    ]
    y, od, c1, n1, m1, s1, cv1, k1, v1 = pl.pallas_call(
        _sample_kernel,
        grid=(nb // bb,),
        in_specs=in_specs, out_specs=out_specs, out_shape=out_shape, scratch_shapes=scratch,
        compiler_params=_cparams(("arbitrary",)),
        name="sample_mixers",
    )(proj_s, mm, prm['gate_bias'], n0f, c0, s0, cv0, k0f, v0f, qm, sink_b,
      prm['gla_w_gate_pad'], prm['gla_b_gate'].reshape(1, 256),
      prm['mlstm_norm_g'].reshape(1, -1), prm['gla_norm_g'].reshape(1, -1),
      prm['conv_w'], prm['conv_b'].reshape(1, -1), prm['conv_norm_g'].reshape(1, -1),
      prm['conv_norm_b'].reshape(1, -1))
    od4 = od.reshape(nb, SWA_KV_HEADS, SWA_HEADS // SWA_KV_HEADS, LANES)
    yd = jnp.concatenate([od4[:, 0, :, :SWA_HEAD_DIM], od4[:, 1, :, SWA_HEAD_DIM:]], axis=1).reshape(nb, GROUP_WIDTH)
    y = jnp.concatenate([y, yd], axis=1).astype(BF16)
    new_state = (c1, n1.reshape(n0.shape), m1[:, :MLSTM_HEADS], s1, cv1,
                 k1.reshape(k_all.shape[1:]), v1.reshape(v_all.shape[1:]))
    return y, new_state


def _outproj_router_kernel(x_ref, mix_ref, w_ref, g_ref, rw_ref, rb_ref,
                           x1_ref, hn_ref, ri_ref, rf_ref, cnt_ref):
    tm = x_ref.shape[0]

    @pl.when(pl.program_id(0) == 0)
    def _():
        cnt_ref[...] = jnp.zeros_like(cnt_ref)

    x1 = x_ref[...] + jnp.dot(mix_ref[...], w_ref[...], preferred_element_type=F32)
    x1_ref[...] = x1
    ms = jnp.mean(x1 * x1, axis=-1, keepdims=True)
    hn = x1 * lax.rsqrt(ms + NORM_EPS) * g_ref[...]
    hn_ref[...] = hn
    logits = jnp.dot(hn.astype(BF16), rw_ref[...], preferred_element_type=F32) + rb_ref[...]
    lane = lax.broadcasted_iota(I32, (tm, LANES), 1)
    big = jnp.int32(LANES)
    gl = jnp.where(lane < N_GROUPS, logits, -jnp.inf)
    gmax = jnp.max(gl, axis=-1, keepdims=True)
    g_sel = jnp.min(jnp.where(gl == gmax, lane, big), axis=-1, keepdims=True)
    g_w = 1.0 / jnp.sum(jnp.exp(gl - gmax), axis=-1, keepdims=True)
    e_lane = lane - N_GROUPS
    in_grp = (e_lane >= 0) & (e_lane < N_EXPERTS) & ((e_lane // EXPERTS_PER_GROUP) == g_sel)
    el = jnp.where(in_grp, logits, -jnp.inf)
    m1 = jnp.max(el, axis=-1, keepdims=True)
    i1 = jnp.min(jnp.where(el == m1, lane, big), axis=-1, keepdims=True)
    el2 = jnp.where(lane == i1, -jnp.inf, el)
    m2 = jnp.max(el2, axis=-1, keepdims=True)
    i2 = jnp.min(jnp.where(el2 == m2, lane, big), axis=-1, keepdims=True)
    r = jnp.exp(m2 - m1)
    p1 = 1.0 / (1.0 + r)
    gate1 = g_w * p1
    gate2 = g_w * (r * p1)
    sel1 = lane == i1
    sel2 = lane == i2
    onehot = jnp.where(sel1 | sel2, 1.0, 0.0)
    row = lax.broadcasted_iota(I32, (tm, tm), 0)
    col = lax.broadcasted_iota(I32, (tm, tm), 1)
    strict = jnp.where(col < row, 1.0, 0.0).astype(BF16)
    cum = jnp.dot(strict, onehot.astype(BF16), preferred_element_type=F32) + cnt_ref[0:1, :]
    rank1 = jnp.sum(jnp.where(sel1, cum, 0.0), axis=-1, keepdims=True).astype(I32)
    rank2 = jnp.sum(jnp.where(sel2, cum, 0.0), axis=-1, keepdims=True).astype(I32)
    cnt_ref[...] = cnt_ref[...] + jnp.sum(onehot, axis=0, keepdims=True)
    ri = jnp.where(lane == 0, i1 - N_GROUPS, jnp.where(lane == 1, i2 - N_GROUPS,
                   jnp.where(lane == 2, rank1, jnp.where(lane == 3, rank2, 0))))
    ri_ref[...] = ri
    rf_ref[...] = jnp.where(lane == 0, gate1, jnp.where(lane == 1, gate2, 0.0))


def _outproj_router(x, mixed, w_out_bf16, norm_g, rw_pad, rb_pad):
    n = x.shape[0]
    tm = _pick_tile(n, (320, 256, 128, 64, 16))
    full = lambda shape: pl.BlockSpec(shape, lambda i: (0,) * len(shape))
    rowb = lambda w: pl.BlockSpec((tm, w), lambda i: (i, 0))
    return pl.pallas_call(
        _outproj_router_kernel,
        grid=(n // tm,),
        in_specs=[rowb(D_MODEL), rowb(D_MODEL), full((D_MODEL, D_MODEL)), full((1, D_MODEL)),
                  full((D_MODEL, LANES)), full((1, LANES))],
        out_specs=[rowb(D_MODEL), rowb(D_MODEL), rowb(LANES), rowb(LANES), full((SUBLANES, LANES))],
        out_shape=[
            jax.ShapeDtypeStruct((n, D_MODEL), F32),
            jax.ShapeDtypeStruct((n, D_MODEL), F32),
            jax.ShapeDtypeStruct((n, LANES), I32),
            jax.ShapeDtypeStruct((n, LANES), F32),
            jax.ShapeDtypeStruct((SUBLANES, LANES), F32),
        ],
        compiler_params=_cparams(("arbitrary",)),
        name="outproj_router",
    )(x, mixed, w_out_bf16, norm_g.reshape(1, D_MODEL), rw_pad, rb_pad)


def _cast_kernel(x_ref, o_ref):
    o_ref[...] = x_ref[...].astype(o_ref.dtype)


def _cast_bf16(w_all, layer):
    depth, n_e, k, f = w_all.shape
    rows = n_e * k
    tr = _pick_tile(rows, (1024, 512, 256, 128, 16))
    steps = rows // tr
    out = pl.pallas_call(
        _cast_kernel,
        grid=(steps,),
        in_specs=[pl.BlockSpec((tr, f), lambda i: (layer * steps + i, 0))],
        out_specs=pl.BlockSpec((tr, f), lambda i: (i, 0)),
        out_shape=jax.ShapeDtypeStruct((rows, f), BF16),
        compiler_params=_cparams(("parallel",)),
        name="cast_bf16",
    )(w_all.reshape(depth * rows, f))
    return out.reshape(n_e, k, f)


def _dispatch_kernel(pos_ref, zt_ref, hn_ref, xs_ref, zbuf, sem, zsem):
    tm = hn_ref.shape[0]
    tile = zbuf.shape[0]
    base = pl.program_id(0) * tm

    @pl.when(pl.program_id(0) == 0)
    def _():
        zbuf[...] = jnp.zeros_like(zbuf)

        def zero_tile(k, carry):
            @pl.when(zt_ref[k] >= 0)
            def _():
                row = pl.multiple_of(zt_ref[k] * tile, tile)
                cp = pltpu.make_async_copy(zbuf, xs_ref.at[pl.ds(row, tile)], zsem.at[0])
                cp.start()
                cp.wait()
            return carry

        lax.fori_loop(0, zt_ref.shape[0], zero_tile, 0)

    def issue(r, carry):
        tok = base + r
        src = hn_ref.at[pl.ds(r, 1)]
        pltpu.make_async_copy(src, xs_ref.at[pl.ds(pos_ref[2 * tok], 1)], sem.at[0]).start()
        pltpu.make_async_copy(src, xs_ref.at[pl.ds(pos_ref[2 * tok + 1], 1)], sem.at[1]).start()
        return carry

    lax.fori_loop(0, tm, issue, 0, unroll=4)
    pltpu.make_async_copy(hn_ref, xs_ref.at[pl.ds(0, tm)], sem.at[0]).wait()
    pltpu.make_async_copy(hn_ref, xs_ref.at[pl.ds(0, tm)], sem.at[1]).wait()


def _dispatch(pos, zero_tiles, hn, n_rows, tile):
    n = hn.shape[0]
    tm = _pick_tile(n, (256, 128, 64, 8))
    return pl.pallas_call(
        _dispatch_kernel,
        grid_spec=pltpu.PrefetchScalarGridSpec(
            num_scalar_prefetch=2,
            grid=(n // tm,),
            in_specs=[pl.BlockSpec((tm, D_MODEL), lambda i, p, z: (i, 0))],
            out_specs=pl.BlockSpec(memory_space=pl.ANY),
            scratch_shapes=[pltpu.VMEM((tile, D_MODEL), F32), pltpu.SemaphoreType.DMA((2,)),
                            pltpu.SemaphoreType.DMA((1,))],
        ),
        out_shape=jax.ShapeDtypeStruct((n_rows, D_MODEL), F32),
        compiler_params=_cparams(("arbitrary",)),
        name="moe_dispatch",
    )(pos, zero_tiles, hn)


def _expert_kernel(te_ref, nt_ref, x_ref, wg_ref, wu_ref, wd_ref, y_ref):
    del te_ref
    used = pl.program_id(0) < nt_ref[0]

    @pl.when(used)
    def _():
        x = x_ref[...].astype(BF16)
        a = jnp.dot(x, wg_ref[0], preferred_element_type=F32)
        u = jnp.dot(x, wu_ref[0], preferred_element_type=F32)
        hmid = (_silu(a) * u).astype(BF16)
        y_ref[...] = jnp.dot(hmid, wd_ref[0], preferred_element_type=F32)

    @pl.when(jnp.logical_not(used))
    def _():
        y_ref[...] = jnp.zeros_like(y_ref)


def _expert_mlp(tile_expert, n_tiles_used, xs, wg, wu, wd, tile):
    n_tiles = xs.shape[0] // tile

    def row_map(t, te, nt):
        return (jnp.minimum(t, nt[0] - 1), 0)

    def w_map(t, te, nt):
        return (te[jnp.minimum(t, nt[0] - 1)], 0, 0)

    return pl.pallas_call(
        _expert_kernel,
        grid_spec=pltpu.PrefetchScalarGridSpec(
            num_scalar_prefetch=2,
            grid=(n_tiles,),
            in_specs=[
                pl.BlockSpec((tile, D_MODEL), row_map),
                pl.BlockSpec((1, D_MODEL, D_EXPERT), w_map),
                pl.BlockSpec((1, D_MODEL, D_EXPERT), w_map),
                pl.BlockSpec((1, D_EXPERT, D_MODEL), w_map),
            ],
            out_specs=pl.BlockSpec((tile, D_MODEL), lambda t, te, nt: (t, 0)),
        ),
        out_shape=jax.ShapeDtypeStruct(xs.shape, F32),
        compiler_params=_cparams(("arbitrary",)),
        name="expert_mlp",
    )(tile_expert, n_tiles_used, xs, wg, wu, wd)


def _combine_kernel(pos_ref, x1_ref, rf_ref, fg_ref, ys_ref, o_ref, buf_a, buf_b, sem, *, final_norm):
    tm = x1_ref.shape[0]
    base = pl.program_id(0) * tm

    def issue(r, carry):
        tok = base + r
        pltpu.make_async_copy(ys_ref.at[pl.ds(pos_ref[2 * tok], 1)], buf_a.at[pl.ds(r, 1)], sem.at[0]).start()
        pltpu.make_async_copy(ys_ref.at[pl.ds(pos_ref[2 * tok + 1], 1)], buf_b.at[pl.ds(r, 1)], sem.at[1]).start()
        return carry

    lax.fori_loop(0, tm, issue, 0, unroll=4)
    pltpu.make_async_copy(ys_ref.at[pl.ds(0, tm)], buf_a, sem.at[0]).wait()
    pltpu.make_async_copy(ys_ref.at[pl.ds(0, tm)], buf_b, sem.at[1]).wait()
    rf = rf_ref[...]
    x2 = x1_ref[...] + rf[:, 0:1] * buf_a[...] + rf[:, 1:2] * buf_b[...]
    if final_norm:
        ms = jnp.mean(x2 * x2, axis=-1, keepdims=True)
        x2 = x2 * lax.rsqrt(ms + NORM_EPS) * fg_ref[...]
    o_ref[...] = x2


def _combine(pos, x1, rf, ys, final_g, final_norm):
    n = x1.shape[0]
    tm = _pick_tile(n, (256, 128, 64, 8))
    return pl.pallas_call(
        functools.partial(_combine_kernel, final_norm=final_norm),
        grid_spec=pltpu.PrefetchScalarGridSpec(
            num_scalar_prefetch=1,
            grid=(n // tm,),
            in_specs=[
                pl.BlockSpec((tm, D_MODEL), lambda i, p: (i, 0)),
                pl.BlockSpec((tm, LANES), lambda i, p: (i, 0)),
                pl.BlockSpec((1, D_MODEL), lambda i, p: (0, 0)),
                pl.BlockSpec(memory_space=pl.ANY),
            ],
            out_specs=pl.BlockSpec((tm, D_MODEL), lambda i, p: (i, 0)),
            scratch_shapes=[pltpu.VMEM((tm, D_MODEL), F32), pltpu.VMEM((tm, D_MODEL), F32),
                            pltpu.SemaphoreType.DMA((2,))],
        ),
        out_shape=jax.ShapeDtypeStruct((n, D_MODEL), F32),
        compiler_params=_cparams(("arbitrary",)),
        name="moe_combine",
    )(pos, x1, rf, final_g.reshape(1, D_MODEL), ys)


def _pad_w_in(w_in):
    parts = []
    at = 0
    for lo, hi, dst in sorted(_W_IN_SEGMENTS, key=lambda s: s[2]):
        if dst > at:
            parts.append(jnp.zeros((D_MODEL, dst - at), w_in.dtype))
        parts.append(w_in[:, lo:hi])
        at = dst + (hi - lo)
    if at < N_PROJ:
        parts.append(jnp.zeros((D_MODEL, N_PROJ - at), w_in.dtype))
    return jnp.concatenate(parts, axis=1)


def _moe(x1, hn, ri, rf, counts, prm, final_g, final_norm, tile):
    n = x1.shape[0]
    n_tiles = -(-2 * n // tile) + N_EXPERTS
    cnt = counts[0, N_GROUPS:N_GROUPS + N_EXPERTS].astype(I32)
    tiles_per = (cnt + tile - 1) // tile
    tile_end = jnp.cumsum(tiles_per)
    row_off = (tile_end - tiles_per) * tile
    pos = (row_off[ri[:, 0:2]] + ri[:, 2:4]).reshape(2 * n)
    tile_ids = jnp.arange(n_tiles, dtype=I32)
    tile_expert = jnp.minimum(jnp.sum((tile_ids[:, None] >= tile_end[None, :]).astype(I32), axis=1), N_EXPERTS - 1)
    n_used = tile_end[N_EXPERTS - 1:N_EXPERTS].astype(I32)
    last_tile = jnp.where(tiles_per > 0, tile_end - 1, -1).astype(I32)
    zero_tiles = jnp.concatenate([last_tile, jnp.where(tile_ids >= n_used[0], tile_ids, -1)])
    xs = _dispatch(pos, zero_tiles, hn, n_tiles * tile, tile)
    ys = _expert_mlp(tile_expert, n_used, xs, prm['expert_w_gate'], prm['expert_w_up'],
                     prm['expert_w_down'], tile)
    return _combine(pos, x1, rf, ys, final_g, final_norm)


def _layer(x, t_len, states, layer, prm, final_g, final_norm, moe_tile):
    proj = _in_projection(x, prm['norm_mix_g'], prm['w_in_pad'])
    ya, cn, m_p = _mlstm_prompt(proj, t_len, prm['gate_bias'], prm['mlstm_norm_g'])
    yb, conv_tail = _conv_prompt(proj, t_len, prm['conv_w'], prm['conv_b'], prm['conv_norm_g'], prm['conv_norm_b'])
    yc, sp = _gla_prompt(proj, t_len, prm['gla_w_gate_pad'], prm['gla_b_gate'], prm['gla_norm_g'])
    yd = _swa_prompt(proj, t_len, prm['swa_sinks'])
    y_s, new_s = _sample_mixers(proj[t_len:], states, layer, prm)
    mixed = jnp.concatenate([jnp.concatenate([ya, yb, yc, yd], axis=1), y_s], axis=0)
    x1, hn, ri, rf, counts = _outproj_router(x, mixed, prm['w_out'], prm['norm_ffn_g'], prm['router_w'], prm['router_b'])
    x2 = _moe(x1, hn, ri, rf, counts, prm, final_g, final_norm, moe_tile)
    p_c = cn[None, :, :, :HEAD_W]
    p_n = cn[None, :, :, HEAD_W]
    p_m = m_p[None, :MLSTM_HEADS, 0]
    p_s = jnp.stack([sp[0, :GLA_DK, :HEAD_W], sp[0, GLA_DK:, HEAD_W:],
                     sp[1, :GLA_DK, :HEAD_W], sp[1, GLA_DK:, HEAD_W:]])[None]
    p_conv = conv_tail[None, CONV_HALO - (CONV_WIDTH - 1):]
    p_k = proj[t_len - WINDOW:t_len, S_K:S_K + LANES].reshape(1, WINDOW, SWA_KV_HEADS, SWA_HEAD_DIM)
    p_v = proj[t_len - WINDOW:t_len, S_V:S_V + LANES].reshape(1, WINDOW, SWA_KV_HEADS, SWA_HEAD_DIM)
    return x2, (p_c, p_n, p_m, p_s, p_conv, p_k, p_v), new_s


def _forward(x_prompt, x_sample, states, layer_params, final_norm_g, moe_tile=MOE_TILE):
    t_len = x_prompt.shape[1]
    x = jnp.concatenate([x_prompt[0], x_sample[:, 0]], axis=0)
    new_p, new_s = [], []
    depth = len(layer_params)
    for l, prm in enumerate(layer_params):
        x, sp, ss = _layer(x, t_len, states, l, prm, final_norm_g, l == depth - 1, moe_tile)
        new_p.append(sp)
        new_s.append(ss)
    y_prompt = x[None, :t_len]
    y_sample = x[t_len:, None]
    p_states = [jnp.stack(parts) for parts in zip(*new_p)]
    s_states = [jnp.stack(parts) for parts in zip(*new_s)]
    return (y_prompt, y_sample, *p_states, *s_states)


def _prep_layer_params(l, norm_mix_g, w_in, mlstm_b_i, mlstm_b_f, mlstm_norm_g, conv_w, conv_b, conv_norm_g,
                       conv_norm_b, gla_w_gate, gla_b_gate, gla_norm_g, swa_sinks, w_out, norm_ffn_g,
                       router_group_w, router_group_b, router_expert_w, router_expert_b, expert_w_gate,
                       expert_w_up, expert_w_down):
    gate_bias = jnp.concatenate([mlstm_b_i[l], mlstm_b_f[l], jnp.zeros((LANES - 2 * MLSTM_HEADS,), F32)])
    rw = jnp.concatenate([router_group_w[l], router_expert_w[l],
                          jnp.zeros((D_MODEL, LANES - N_GROUPS - N_EXPERTS), F32)], axis=1)
    rb = jnp.concatenate([router_group_b[l], router_expert_b[l],
                          jnp.zeros((LANES - N_GROUPS - N_EXPERTS,), F32)])
    return {
        'norm_mix_g': norm_mix_g[l],
        'w_in_pad': _pad_w_in(w_in[l].astype(BF16)),
        'gate_bias': gate_bias.reshape(1, LANES),
        'mlstm_norm_g': mlstm_norm_g[l],
        'conv_w': conv_w[l], 'conv_b': conv_b[l], 'conv_norm_g': conv_norm_g[l], 'conv_norm_b': conv_norm_b[l],
        'gla_w_gate_pad': jnp.concatenate(
            [gla_w_gate[l], jnp.zeros((LANES - GLA_LOWRANK, GLA_HEADS * GLA_DK), F32)], axis=0).astype(BF16),
        'gla_b_gate': gla_b_gate[l], 'gla_norm_g': gla_norm_g[l],
        'swa_sinks': swa_sinks[l],
        'w_out': w_out[l].astype(BF16),
        'norm_ffn_g': norm_ffn_g[l],
        'router_w': rw.astype(BF16), 'router_b': rb.reshape(1, LANES),
        'expert_w_gate': _cast_bf16(expert_w_gate, l),
        'expert_w_up': _cast_bf16(expert_w_up, l),
        'expert_w_down': _cast_bf16(expert_w_down, l),
    }


def kernel(x_prompt, x_sample, state_mlstm_C, state_mlstm_n, state_mlstm_m, state_gla_S, cache_conv, cache_swa_k, cache_swa_v, norm_mix_g, w_in, mlstm_b_i, mlstm_b_f, mlstm_norm_g, conv_w, conv_b, conv_norm_g, conv_norm_b, gla_w_gate, gla_b_gate, gla_norm_g, swa_sinks, w_out, norm_ffn_g, router_group_w, router_group_b, router_expert_w, router_expert_b, expert_w_gate, expert_w_up, expert_w_down, final_norm_g):
    depth = w_in.shape[0]
    weights = (norm_mix_g, w_in, mlstm_b_i, mlstm_b_f, mlstm_norm_g, conv_w, conv_b, conv_norm_g, conv_norm_b,
               gla_w_gate, gla_b_gate, gla_norm_g, swa_sinks, w_out, norm_ffn_g, router_group_w, router_group_b,
               router_expert_w, router_expert_b, expert_w_gate, expert_w_up, expert_w_down)
    layer_params = [_prep_layer_params(l, *weights) for l in range(depth)]
    states = (state_mlstm_C, state_mlstm_n, state_mlstm_m, state_gla_S, cache_conv, cache_swa_k, cache_swa_v)
    return _forward(x_prompt, x_sample, states, layer_params, final_norm_g)
```

```python
import functools

import jax
import jax.numpy as jnp
from jax import lax
from jax.experimental import pallas as pl
from jax.experimental.pallas import tpu as pltpu

F32 = jnp.float32
BF16 = jnp.bfloat16
I32 = jnp.int32
HIGHEST = lax.Precision.HIGHEST

D_MODEL = 2048
GROUP_WIDTH = 512
HEAD_W = 128
MLSTM_HEADS = 4
GLA_HEADS = 4
GLA_DK = 64
GLA_LOWRANK = 16
GLA_TAU = 16.0
CONV_WIDTH = 31
SWA_HEADS = 8
SWA_KV_HEADS = 2
SWA_HEAD_DIM = 64
WINDOW = 128
N_GROUPS = 4
EXPERTS_PER_GROUP = 4
N_EXPERTS = 16
D_EXPERT = 1024
NORM_EPS = 1e-6
LANES = 128
SUBLANES = 8
TOK_ROWS = D_MODEL // LANES
PACK_ROWS = TOK_ROWS // 2

A_Q, A_K, A_V, A_O = 0, 512, 1024, 1536
C_A, C_G = 2048, 2560
G_V, G_R = 3072, 3584
S_Q = 4096
G_Q, G_K = 4608, 4864
A_G, G_LR, S_K, S_V = 5120, 5248, 5376, 5504
N_PROJ = 5632
_W_IN_SEGMENTS = (
    (0, 512, A_Q), (512, 1024, A_K), (1024, 1536, A_V), (1536, 2048, A_O),
    (2048, 2056, A_G),
    (2056, 2568, C_A), (2568, 3080, C_G),
    (3080, 3336, G_Q), (3336, 3592, G_K), (3592, 4104, G_V), (4104, 4616, G_R),
    (4616, 4632, G_LR),
    (4632, 5144, S_Q), (5144, 5272, S_K), (5272, 5400, S_V),
)

MLSTM_CHUNK = 128
MLSTM_STEP_ROWS = 256
GLA_CHUNK = 64
GLA_STEP_ROWS = 256
GLA_SUB = 16
SWA_STEP_ROWS = 256
CONV_TILE = 256
CONV_ROWS = 64
CONV_HALO = 32
SAMPLE_BB = 8
MOE_TILE = 512
CAST_BLOCK_ELEMS = 2 * 1024 * 1024
VMEM_LIMIT = 56 * 1024 * 1024


def _cparams(sem, vmem=VMEM_LIMIT):
    return pltpu.CompilerParams(dimension_semantics=sem, vmem_limit_bytes=vmem)


def _log_sigmoid(x):
    return jnp.minimum(x, 0.0) - jnp.log1p(jnp.exp(-jnp.abs(x)))


def _sigmoid(x):
    return 1.0 / (1.0 + jnp.exp(-x))


def _silu(x):
    return x * _sigmoid(x)


def _masked_row_sums(mask, x):
    m = jnp.where(mask, 1.0, 0.0).astype(BF16)
    hi = x.astype(BF16)
    r1 = x - hi.astype(F32)
    mid = r1.astype(BF16)
    lo = (r1 - mid.astype(F32)).astype(BF16)
    out = jnp.dot(m, hi, preferred_element_type=F32)
    out = out + jnp.dot(m, mid, preferred_element_type=F32)
    return out + jnp.dot(m, lo, preferred_element_type=F32)


def _pick_tile(n, candidates):
    for c in candidates:
        if n % c == 0:
            return c
    raise ValueError(f"no tile for {n} in {candidates}")


def _proj_kernel(x_ref, g_ref, w_ref, o_ref, hn_ref):
    @pl.when(pl.program_id(1) == 0)
    def _():
        x = x_ref[...]
        ms = jnp.mean(x * x, axis=-1, keepdims=True)
        hn_ref[...] = (x * lax.rsqrt(ms + NORM_EPS) * g_ref[...]).astype(BF16)

    o_ref[...] = jnp.dot(hn_ref[...], w_ref[...], preferred_element_type=F32)


def _in_projection(x, g, w_bf16):
    n = x.shape[0]
    tm = _pick_tile(n, (832, 640, 512, 256, 128, 64, 8))
    tn = N_PROJ // 4
    return pl.pallas_call(
        _proj_kernel,
        grid=(n // tm, N_PROJ // tn),
        in_specs=[
            pl.BlockSpec((tm, D_MODEL), lambda i, j: (i, 0)),
            pl.BlockSpec((1, D_MODEL), lambda i, j: (0, 0)),
            pl.BlockSpec((D_MODEL, tn), lambda i, j: (0, j)),
        ],
        out_specs=pl.BlockSpec((tm, tn), lambda i, j: (i, j)),
        out_shape=jax.ShapeDtypeStruct((n, N_PROJ), F32),
        scratch_shapes=[pltpu.VMEM((tm, D_MODEL), BF16)],
        compiler_params=_cparams(("parallel", "arbitrary")),
        name="in_projection",
    )(x, g.reshape(1, D_MODEL), w_bf16)


def _mlstm_prompt_kernel(q_ref, k_ref, v_ref, o_ref, gt_ref, bias_ref, ng_ref,
                         y_ref, cn_ref, m_ref):
    rows_step = q_ref.shape[0]
    L = min(MLSTM_CHUNK, rows_step)
    n_chunks = rows_step // L

    @pl.when(pl.program_id(0) == 0)
    def _():
        cn_ref[...] = jnp.zeros_like(cn_ref)
        m_ref[...] = jnp.zeros_like(m_ref)

    pre = gt_ref[...] + bias_ref[...]
    lf = _log_sigmoid(pre)
    row = lax.broadcasted_iota(I32, (rows_step, rows_step), 0)
    col = lax.broadcasted_iota(I32, (rows_step, rows_step), 1)
    cum = ((row // L) == (col // L)) & (col <= row)
    b_all = _masked_row_sums(cum, lf)
    pre_t = pre.T
    b_t = b_all.T
    trow = lax.broadcasted_iota(I32, (L, L), 0)
    tcol = lax.broadcasted_iota(I32, (L, L), 1)
    tri = tcol <= trow
    lane = lax.broadcasted_iota(I32, (L, HEAD_W), 1)
    ones_col = (lane == 0).astype(BF16)
    for h in range(MLSTM_HEADS):
        sl = slice(HEAD_W * h, HEAD_W * (h + 1))
        m_prev = m_ref[h:h + 1, 0:1]
        cn = cn_ref[h]
        for c in range(n_chunks):
            cs = slice(L * c, L * (c + 1))
            q = q_ref[cs, sl]
            k = k_ref[cs, sl] * (HEAD_W ** -0.5)
            v = v_ref[cs, sl]
            b_col = b_all[cs, 4 + h:5 + h]
            i_col = pre[cs, h:h + 1]
            b_row = b_t[4 + h:5 + h, cs]
            i_row = pre_t[h:h + 1, cs]
            log_d = jnp.where(tri, b_col - b_row + i_row, -jnp.inf)
            log_inter = b_col + m_prev
            m_t = jnp.maximum(log_inter, jnp.max(log_d, axis=-1, keepdims=True))
            d_mat = jnp.exp(log_d - m_t)
            g_inter = jnp.exp(log_inter - m_t)
            qb = q.astype(BF16)
            kb = k.astype(BF16)
            s = lax.dot_general(qb, kb, (((1,), (1,)), ((), ())), preferred_element_type=F32)
            w = (s * d_mat).astype(BF16)
            v1 = jnp.concatenate([v.astype(BF16), ones_col], axis=1)
            nd = g_inter * jnp.dot(qb, cn.astype(BF16), preferred_element_type=F32)
            nd = nd + jnp.dot(w, v1, preferred_element_type=F32)
            num = nd[:, :HEAD_W]
            den = nd[:, HEAD_W:HEAD_W + 1]
            hh = num / jnp.maximum(jnp.abs(den), jnp.exp(-m_t))
            hh = _sigmoid(o_ref[cs, sl]) * hh
            hh = hh * lax.rsqrt(jnp.mean(hh * hh, axis=-1, keepdims=True) + NORM_EPS) * ng_ref[:, sl]
            y_ref[cs, sl] = hh.astype(y_ref.dtype)
            m_last = m_t[L - 1:L, :]
            b_last = b_col[L - 1:L, :]
            g_state = jnp.exp(b_last + m_prev - m_last)
            w_k = jnp.exp(b_last - b_col + i_col - m_last)
            kw = (k * w_k).astype(BF16)
            upd = lax.dot_general(kw, v1, (((0,), (0,)), ((), ())), preferred_element_type=F32)
            cn = g_state * cn + upd
            m_prev = m_last
        cn_ref[h] = cn
        m_ref[h:h + 1, :] = jnp.broadcast_to(m_prev, (1, LANES))


def _mlstm_prompt(proj, t_len, bias_row, norm_g):
    L = _pick_tile(t_len, (MLSTM_STEP_ROWS, MLSTM_CHUNK, 64, 32, 16, 8))

    def col(off):
        return pl.BlockSpec((L, GROUP_WIDTH), lambda i, o=off: (i, o // GROUP_WIDTH))

    return pl.pallas_call(
        _mlstm_prompt_kernel,
        grid=(t_len // L,),
        in_specs=[
            col(A_Q), col(A_K), col(A_V), col(A_O),
            pl.BlockSpec((L, LANES), lambda i: (i, A_G // LANES)),
            pl.BlockSpec((1, LANES), lambda i: (0, 0)),
            pl.BlockSpec((1, GROUP_WIDTH), lambda i: (0, 0)),
        ],
        out_specs=[
            pl.BlockSpec((L, GROUP_WIDTH), lambda i: (i, 0)),
            pl.BlockSpec((MLSTM_HEADS, HEAD_W, 2 * HEAD_W), lambda i: (0, 0, 0)),
            pl.BlockSpec((SUBLANES, LANES), lambda i: (0, 0)),
        ],
        out_shape=[
            jax.ShapeDtypeStruct((t_len, GROUP_WIDTH), BF16),
            jax.ShapeDtypeStruct((MLSTM_HEADS, HEAD_W, 2 * HEAD_W), F32),
            jax.ShapeDtypeStruct((SUBLANES, LANES), F32),
        ],
        compiler_params=_cparams(("arbitrary",)),
        name="mlstm_prompt",
    )(proj, proj, proj, proj, proj, bias_row, norm_g.reshape(1, GROUP_WIDTH))


def _gla_prompt_kernel(q_ref, k_ref, v_ref, r_ref, lr_ref, wg_ref, bg_ref, ng_ref, y_ref, sp_ref):
    rows_step = q_ref.shape[0]
    L = min(GLA_CHUNK, rows_step)
    n_chunks = rows_step // L
    n_sub = L // GLA_SUB

    @pl.when(pl.program_id(0) == 0)
    def _():
        sp_ref[...] = jnp.zeros_like(sp_ref)

    gate_pre = jnp.dot(lr_ref[...].astype(BF16), wg_ref[...], preferred_element_type=F32) + bg_ref[...]
    log_a = _log_sigmoid(gate_pre) * (1.0 / GLA_TAU)
    row = lax.broadcasted_iota(I32, (rows_step, rows_step), 0)
    col = lax.broadcasted_iota(I32, (rows_step, rows_step), 1)
    tri = ((row // L) == (col // L)) & (col <= row)
    b = _masked_row_sums(tri, log_a)
    q = q_ref[...] * (GLA_DK ** -0.5)
    k = k_ref[...]
    b_last = jnp.concatenate(
        [jnp.broadcast_to(b[L * c + L - 1:L * c + L, :], (L, b.shape[1])) for c in range(n_chunks)], axis=0)
    q_in = q * jnp.exp(b)
    k_dec = k * jnp.exp(b_last - b)
    lane16 = lax.broadcasted_iota(I32, (GLA_SUB, LANES), 1)
    lo16 = lane16 < GLA_DK
    srow = lax.broadcasted_iota(I32, (LANES, 2 * HEAD_W), 0)
    scol = lax.broadcasted_iota(I32, (LANES, 2 * HEAD_W), 1)
    block_diag = (srow < GLA_DK) == (scol < HEAD_W)
    for p in range(2):
        pls = slice(LANES * p, LANES * (p + 1))
        vp = v_ref[:, 2 * HEAD_W * p:2 * HEAD_W * (p + 1)].astype(BF16)
        b_t = b[:, pls].T
        sp = sp_ref[p]
        states = []
        for c in range(n_chunks):
            cs = slice(L * c, L * (c + 1))
            states.append(sp)
            dec_col = jnp.exp(b_t[:, L * c + L - 1:L * c + L])
            upd = lax.dot_general(k_dec[cs, pls].astype(BF16), vp[cs], (((0,), (0,)), ((), ())),
                                  preferred_element_type=F32)
            sp = jnp.where(block_diag, dec_col * sp + upd, 0.0)
        sp_ref[p] = sp
        outs = []
        for c in range(n_chunks):
            c0 = L * c
            o_inter = jnp.dot(q_in[c0:c0 + L, pls].astype(BF16), states[c].astype(BF16),
                              preferred_element_type=F32)
            rows = []
            for blk in range(n_sub):
                r0 = GLA_SUB * blk
                n = GLA_SUB * (blk + 1)
                qrows = slice(c0 + r0, c0 + r0 + GLA_SUB)
                krows = slice(c0, c0 + n)
                if blk == 0:
                    qs = q[qrows, pls] * jnp.exp(b[qrows, pls])
                    ks = k[krows, pls] * jnp.exp(-b[krows, pls])
                else:
                    anchor = b[c0 + r0 - 1:c0 + r0, pls]
                    qs = q[qrows, pls] * jnp.exp(b[qrows, pls] - anchor)
                    ks = k[krows, pls] * jnp.exp(anchor - b[krows, pls])
                qs2 = jnp.concatenate([jnp.where(lo16, qs, 0.0), jnp.where(lo16, 0.0, qs)], axis=0)
                att = lax.dot_general(qs2.astype(BF16), ks.astype(BF16), (((1,), (1,)), ((), ())),
                                      preferred_element_type=F32)
                trow = lax.broadcasted_iota(I32, (2 * GLA_SUB, n), 0)
                tcol = lax.broadcasted_iota(I32, (2 * GLA_SUB, n), 1)
                t_idx = r0 + jnp.where(trow >= GLA_SUB, trow - GLA_SUB, trow)
                att = jnp.where(tcol <= t_idx, att, 0.0)
                o2 = jnp.dot(att.astype(BF16), vp[krows], preferred_element_type=F32)
                rows.append(jnp.concatenate([o2[:GLA_SUB, :HEAD_W], o2[GLA_SUB:, HEAD_W:]], axis=1))
            outs.append(o_inter + jnp.concatenate(rows, axis=0))
        o = jnp.concatenate(outs, axis=0)
        for hh in range(2):
            head = 2 * p + hh
            hs = slice(HEAD_W * head, HEAD_W * (head + 1))
            oh = o[:, HEAD_W * hh:HEAD_W * (hh + 1)]
            oh = oh * lax.rsqrt(jnp.mean(oh * oh, axis=-1, keepdims=True) + NORM_EPS) * ng_ref[:, hs]
            y_ref[:, hs] = (oh * _silu(r_ref[:, hs])).astype(y_ref.dtype)


def _gla_prompt(proj, t_len, w_gate_pad, b_gate, norm_g):
    L = _pick_tile(t_len, (GLA_STEP_ROWS, GLA_CHUNK))
    return pl.pallas_call(
        _gla_prompt_kernel,
        grid=(t_len // L,),
        in_specs=[
            pl.BlockSpec((L, 256), lambda i: (i, G_Q // 256)),
            pl.BlockSpec((L, 256), lambda i: (i, G_K // 256)),
            pl.BlockSpec((L, GROUP_WIDTH), lambda i: (i, G_V // GROUP_WIDTH)),
            pl.BlockSpec((L, GROUP_WIDTH), lambda i: (i, G_R // GROUP_WIDTH)),
            pl.BlockSpec((L, LANES), lambda i: (i, G_LR // LANES)),
            pl.BlockSpec((LANES, 256), lambda i: (0, 0)),
            pl.BlockSpec((1, 256), lambda i: (0, 0)),
            pl.BlockSpec((1, GROUP_WIDTH), lambda i: (0, 0)),
        ],
        out_specs=[
            pl.BlockSpec((L, GROUP_WIDTH), lambda i: (i, 0)),
            pl.BlockSpec((2, LANES, 2 * HEAD_W), lambda i: (0, 0, 0)),
        ],
        out_shape=[
            jax.ShapeDtypeStruct((t_len, GROUP_WIDTH), BF16),
            jax.ShapeDtypeStruct((2, LANES, 2 * HEAD_W), F32),
        ],
        compiler_params=_cparams(("arbitrary",)),
        name="gla_prompt",
    )(proj, proj, proj, proj, proj, w_gate_pad, b_gate.reshape(1, 256), norm_g.reshape(1, GROUP_WIDTH))


def _conv_norm_act(y, g_ref, be_ref):
    mu = jnp.mean(y, axis=-1, keepdims=True)
    yc = y - mu
    var = jnp.mean(yc * yc, axis=-1, keepdims=True)
    return _silu(yc * lax.rsqrt(var + NORM_EPS) * g_ref[...] + be_ref[...])


def _conv_prompt_kernel(ua_ref, ug_ref, ha_ref, hg_ref, w_ref, b_ref, g_ref, be_ref,
                        y_ref, tail_ref, buf_ref, sh_ref):
    tt = ua_ref.shape[0]
    span = tt + CONV_HALO
    halo = ha_ref[...] * _sigmoid(hg_ref[...])
    buf_ref[0:CONV_HALO, :] = jnp.where(pl.program_id(0) > 0, halo, 0.0)
    buf_ref[CONV_HALO:span, :] = ua_ref[...] * _sigmoid(ug_ref[...])
    buf_ref[span:span + SUBLANES, :] = jnp.zeros((SUBLANES, GROUP_WIDTH), F32)
    for k in range(1, SUBLANES):
        sh_ref[k] = buf_ref[k:k + span, :]
    base = CONV_HALO - (CONV_WIDTH - 1)
    for r in range(tt // CONV_ROWS):
        acc = jnp.zeros((CONV_ROWS, GROUP_WIDTH), F32)
        for j in range(CONV_WIDTH):
            s0 = r * CONV_ROWS + base + j
            k = s0 % SUBLANES
            a0 = s0 - k
            win = buf_ref[a0:a0 + CONV_ROWS, :] if k == 0 else sh_ref[k, a0:a0 + CONV_ROWS, :]
            acc = acc + w_ref[j:j + 1, :] * win
        y = _conv_norm_act(acc + b_ref[...], g_ref, be_ref)
        y_ref[r * CONV_ROWS:(r + 1) * CONV_ROWS, :] = y.astype(y_ref.dtype)
    tail_ref[...] = buf_ref[tt:span, :]


def _conv_prompt(proj, t_len, w, b, g, beta):
    tt = _pick_tile(t_len, (CONV_TILE, 128, 64))
    ratio = tt // CONV_HALO
    vec = lambda: pl.BlockSpec((1, GROUP_WIDTH), lambda i: (0, 0))
    return pl.pallas_call(
        _conv_prompt_kernel,
        grid=(t_len // tt,),
        in_specs=[
            pl.BlockSpec((tt, GROUP_WIDTH), lambda i: (i, C_A // GROUP_WIDTH)),
            pl.BlockSpec((tt, GROUP_WIDTH), lambda i: (i, C_G // GROUP_WIDTH)),
            pl.BlockSpec((CONV_HALO, GROUP_WIDTH), lambda i: (jnp.maximum(i * ratio - 1, 0), C_A // GROUP_WIDTH)),
            pl.BlockSpec((CONV_HALO, GROUP_WIDTH), lambda i: (jnp.maximum(i * ratio - 1, 0), C_G // GROUP_WIDTH)),
            pl.BlockSpec((CONV_WIDTH, GROUP_WIDTH), lambda i: (0, 0)),
            vec(), vec(), vec(),
        ],
        out_specs=[
            pl.BlockSpec((tt, GROUP_WIDTH), lambda i: (i, 0)),
            pl.BlockSpec((CONV_HALO, GROUP_WIDTH), lambda i: (0, 0)),
        ],
        out_shape=[
            jax.ShapeDtypeStruct((t_len, GROUP_WIDTH), BF16),
            jax.ShapeDtypeStruct((CONV_HALO, GROUP_WIDTH), F32),
        ],
        scratch_shapes=[pltpu.VMEM((tt + CONV_HALO + SUBLANES, GROUP_WIDTH), F32),
                        pltpu.VMEM((SUBLANES, tt + CONV_HALO, GROUP_WIDTH), F32)],
        compiler_params=_cparams(("arbitrary",)),
        name="conv_prompt",
    )(proj, proj, proj, proj, w, b.reshape(1, -1), g.reshape(1, -1), beta.reshape(1, -1))


def _swa_prompt_kernel(sink_ref, q_ref, kc_ref, vc_ref, kp_ref, vp_ref, y_ref):
    bq = WINDOW
    n_blk = q_ref.shape[0] // bq
    first = pl.program_id(0) == 0
    k_full = jnp.concatenate([kp_ref[...], kc_ref[...]], axis=0)
    v_full = jnp.concatenate([vp_ref[...], vc_ref[...]], axis=0)
    k_sw_full = pltpu.roll(k_full, SWA_HEAD_DIM, 1).astype(BF16)
    v_sw_full = pltpu.roll(v_full, SWA_HEAD_DIM, 1).astype(BF16)
    k_full = k_full.astype(BF16)
    v_full = v_full.astype(BF16)
    tq = lax.broadcasted_iota(I32, (bq, 2 * bq), 0)
    kj = lax.broadcasted_iota(I32, (bq, 2 * bq), 1)
    band = (kj > tq) & (kj <= tq + WINDOW)
    lane = lax.broadcasted_iota(I32, (bq, LANES), 1)
    lo = lane < SWA_HEAD_DIM
    rep = SWA_HEADS // SWA_KV_HEADS
    for blk in range(n_blk):
        qs = slice(bq * blk, bq * (blk + 1))
        ks = slice(bq * blk, bq * (blk + 2))
        valid = band & (kj >= jnp.where(first, bq, 0)) if blk == 0 else band
        k_all, v_all, k_sw, v_sw = k_full[ks], v_full[ks], k_sw_full[ks], v_sw_full[ks]
        for c in range(SWA_HEADS // 2):
            qc = q_ref[qs, LANES * c:LANES * (c + 1)] * (SWA_HEAD_DIM ** -0.5)
            outs = []
            for hh in range(2):
                h = 2 * c + hh
                g = h // rep
                qm = jnp.where(lo if hh == 0 else jnp.logical_not(lo), qc, 0.0).astype(BF16)
                k_use = k_all if g == hh else k_sw
                v_use = v_all if g == hh else v_sw
                s = lax.dot_general(qm, k_use, (((1,), (1,)), ((), ())), preferred_element_type=F32)
                s = jnp.where(valid, s, -jnp.inf)
                sink = sink_ref[h]
                mx = jnp.maximum(jnp.max(s, axis=-1, keepdims=True), sink)
                p = jnp.exp(s - mx)
                den = jnp.sum(p, axis=-1, keepdims=True) + jnp.exp(sink - mx)
                p = (p / den).astype(BF16)
                outs.append(jnp.dot(p, v_use, preferred_element_type=F32))
            y_ref[qs, LANES * c:LANES * (c + 1)] = jnp.where(lo, outs[0], outs[1]).astype(y_ref.dtype)


def _swa_prompt(proj, t_len, sinks):
    bq = WINDOW
    rows = _pick_tile(t_len, (SWA_STEP_ROWS, bq))
    ratio = rows // bq
    cur = lambda off: pl.BlockSpec((rows, LANES), lambda i: (i, off // LANES))
    prev = lambda off: pl.BlockSpec((bq, LANES), lambda i: (jnp.maximum(i * ratio - 1, 0), off // LANES))
    return pl.pallas_call(
        _swa_prompt_kernel,
        grid=(t_len // rows,),
        in_specs=[
            pl.BlockSpec(memory_space=pltpu.SMEM),
            pl.BlockSpec((rows, GROUP_WIDTH), lambda i: (i, S_Q // GROUP_WIDTH)),
            cur(S_K), cur(S_V), prev(S_K), prev(S_V),
        ],
        out_specs=pl.BlockSpec((rows, GROUP_WIDTH), lambda i: (i, 0)),
        out_shape=jax.ShapeDtypeStruct((t_len, GROUP_WIDTH), BF16),
        compiler_params=_cparams(("parallel",)),
        name="swa_prompt",
    )(sinks, proj, proj, proj, proj, proj)


_T_MK, _T_MQ = 0, 512
_T_GA, _T_GK, _T_GQ = 1024, 1280, 1536
_T_ROWS = 1792


def _sample_kernel(ps_ref, mm_ref, bias_ref, n0_ref, c0_ref, s0_ref, cv0_ref, k0_ref, v0_ref,
                   qm_ref, sink_ref, wg_ref, bg_ref, mng_ref, gng_ref, cw_ref, cb_ref, cg_ref, cbe_ref,
                   y_ref, od_ref, c1_ref, n1_ref, m1_ref, s1_ref, cv1_ref, k1_ref, v1_ref,
                   tt_ref, bc_ref, num_ref, go_ref, yc_ref):
    i = pl.program_id(0)
    nb = ps_ref.shape[0]
    bb = y_ref.shape[0]

    def gla_gate(lr):
        gp = jnp.dot(lr.astype(BF16), wg_ref[...], preferred_element_type=F32) + bg_ref[...]
        return jnp.exp(_log_sigmoid(gp) * (1.0 / GLA_TAU))

    @pl.when(i == 0)
    def _():
        for h in range(MLSTM_HEADS):
            kk = ps_ref[:, A_K + HEAD_W * h:A_K + HEAD_W * (h + 1)] * (HEAD_W ** -0.5)
            tt_ref[_T_MK + HEAD_W * h:_T_MK + HEAD_W * (h + 1), :] = kk.T.astype(BF16)
            qq = ps_ref[:, A_Q + HEAD_W * h:A_Q + HEAD_W * (h + 1)]
            tt_ref[_T_MQ + HEAD_W * h:_T_MQ + HEAD_W * (h + 1), :] = qq.T.astype(BF16)
        a_all = gla_gate(ps_ref[:, G_LR:G_LR + LANES])
        for p in range(2):
            pls = slice(LANES * p, LANES * (p + 1))
            tt_ref[_T_GA + LANES * p:_T_GA + LANES * (p + 1), :] = a_all[:, pls].T.astype(BF16)
            kk = ps_ref[:, G_K + LANES * p:G_K + LANES * (p + 1)]
            tt_ref[_T_GK + LANES * p:_T_GK + LANES * (p + 1), :] = kk.T.astype(BF16)
            qq = ps_ref[:, G_Q + LANES * p:G_Q + LANES * (p + 1)] * (GLA_DK ** -0.5)
            tt_ref[_T_GQ + LANES * p:_T_GQ + LANES * (p + 1), :] = qq.T.astype(BF16)

    r0 = pl.multiple_of(i * bb, bb)
    rows = pl.ds(r0, bb)

    pre = ps_ref[rows, A_G:A_G + LANES] + bias_ref[...]
    lfm = _log_sigmoid(pre) + mm_ref[...]
    f_al = pltpu.roll(lfm, LANES - MLSTM_HEADS, 1)
    m_t = jnp.maximum(f_al, pre)
    g_st = jnp.exp(f_al - m_t)
    w_k = jnp.exp(pre - m_t)
    m1_ref[...] = m_t
    n_new = []
    for h in range(MLSTM_HEADS):
        kk = ps_ref[rows, A_K + HEAD_W * h:A_K + HEAD_W * (h + 1)] * (HEAD_W ** -0.5)
        nn = g_st[:, h:h + 1] * n0_ref[:, HEAD_W * h:HEAD_W * (h + 1)] + w_k[:, h:h + 1] * kk
        n1_ref[:, HEAD_W * h:HEAD_W * (h + 1)] = nn
        n_new.append(nn)

    glu = ps_ref[rows, C_A:C_A + GROUP_WIDTH] * _sigmoid(ps_ref[rows, C_G:C_G + GROUP_WIDTH])
    yc_ref[...] = glu * cw_ref[CONV_WIDTH - 1:CONV_WIDTH, :]
    a_v = ps_ref[rows, A_V:A_V + GROUP_WIDTH]
    g_v = ps_ref[rows, G_V:G_V + GROUP_WIDTH]
    s_k = ps_ref[rows, S_K:S_K + LANES]
    s_v = ps_ref[rows, S_V:S_V + LANES]

    lane_b = lax.broadcasted_iota(I32, (nb, LANES), 0)
    key_row = lax.broadcasted_iota(I32, (SWA_HEADS, WINDOW), 1)
    sink_col = sink_ref[:, 0:1]

    for j in range(bb):
        onehot = (lane_b == r0 + j).astype(BF16)
        bc_ref[...] = jnp.dot(tt_ref[...], onehot, preferred_element_type=F32)
        jrow = slice(j, j + 1)
        for h in range(MLSTM_HEADS):
            hs = slice(HEAD_W * h, HEAD_W * (h + 1))
            kbc = bc_ref[_T_MK + HEAD_W * h:_T_MK + HEAD_W * (h + 1), :]
            qbc = bc_ref[_T_MQ + HEAD_W * h:_T_MQ + HEAD_W * (h + 1), :]
            g1 = g_st[jrow, h:h + 1]
            w1 = w_k[jrow, h:h + 1]
            v_row = a_v[jrow, hs]
            c_new = g1 * c0_ref[j, h] + kbc * (w1 * v_row)
            c1_ref[j, h] = c_new
            num_ref[jrow, hs] = jnp.sum(qbc * c_new, axis=0, keepdims=True)
        for p in range(2):
            abc = bc_ref[_T_GA + LANES * p:_T_GA + LANES * (p + 1), :]
            kbc = bc_ref[_T_GK + LANES * p:_T_GK + LANES * (p + 1), :]
            qbc = bc_ref[_T_GQ + LANES * p:_T_GQ + LANES * (p + 1), :]
            for hh in range(2):
                head = 2 * p + hh
                hs = slice(HEAD_W * head, HEAD_W * (head + 1))
                ds_ = slice(GLA_DK * hh, GLA_DK * (hh + 1))
                v_row = g_v[jrow, hs]
                s_new = abc[ds_, :] * s0_ref[j, head] + kbc[ds_, :] * v_row
                s1_ref[j, head] = s_new
                go_ref[jrow, hs] = jnp.sum(qbc[ds_, :] * s_new, axis=0, keepdims=True)
        cache = cv0_ref[j]
        yc_ref[jrow, :] = yc_ref[jrow, :] + jnp.sum(cache * cw_ref[0:CONV_WIDTH - 1, :], axis=0, keepdims=True)
        cv1_ref[j, 0:CONV_WIDTH - 2, :] = cv0_ref[j, 1:CONV_WIDTH - 1, :]
        cv1_ref[j, CONV_WIDTH - 2:CONV_WIDTH - 1, :] = glu[jrow, :]
        k_new = s_k[jrow, :]
        v_new = s_v[jrow, :]
        k1_ref[j, 0:WINDOW - 1, :] = k0_ref[j, 1:WINDOW, :]
        k1_ref[j, WINDOW - 1:WINDOW, :] = k_new
        v1_ref[j, 0:WINDOW - 1, :] = v0_ref[j, 1:WINDOW, :]
        v1_ref[j, WINDOW - 1:WINDOW, :] = v_new
        qmat = qm_ref[j] * (SWA_HEAD_DIM ** -0.5)
        s_old = lax.dot_general(qmat.astype(BF16), k0_ref[j].astype(BF16), (((1,), (1,)), ((), ())),
                                preferred_element_type=F32)
        s_old = jnp.where(key_row >= 1, s_old, -jnp.inf)
        s_cur = jnp.sum(qmat * k_new, axis=-1, keepdims=True)
        mx = jnp.maximum(jnp.maximum(jnp.max(s_old, axis=-1, keepdims=True), s_cur), sink_col)
        p_old = jnp.exp(s_old - mx)
        p_cur = jnp.exp(s_cur - mx)
        den = jnp.sum(p_old, axis=-1, keepdims=True) + p_cur + jnp.exp(sink_col - mx)
        o = jnp.dot((p_old / den).astype(BF16), v0_ref[j].astype(BF16), preferred_element_type=F32)
        od_ref[j] = o + (p_cur / den) * v_new

    for h in range(MLSTM_HEADS):
        hs = slice(HEAD_W * h, HEAD_W * (h + 1))
        qq = ps_ref[rows, A_Q + HEAD_W * h:A_Q + HEAD_W * (h + 1)]
        den = jnp.sum(qq * n_new[h], axis=-1, keepdims=True)
        hh = num_ref[:, hs] / jnp.maximum(jnp.abs(den), jnp.exp(-m_t[:, h:h + 1]))
        hh = _sigmoid(ps_ref[rows, A_O + HEAD_W * h:A_O + HEAD_W * (h + 1)]) * hh
        hh = hh * lax.rsqrt(jnp.mean(hh * hh, axis=-1, keepdims=True) + NORM_EPS) * mng_ref[:, hs]
        y_ref[:, hs] = hh.astype(y_ref.dtype)
    y_ref[:, GROUP_WIDTH:2 * GROUP_WIDTH] = _conv_norm_act(yc_ref[...] + cb_ref[...], cg_ref, cbe_ref).astype(y_ref.dtype)
    for head in range(GLA_HEADS):
        hs = slice(HEAD_W * head, HEAD_W * (head + 1))
        oh = go_ref[:, hs]
        oh = oh * lax.rsqrt(jnp.mean(oh * oh, axis=-1, keepdims=True) + NORM_EPS) * gng_ref[:, hs]
        gr = ps_ref[rows, G_R + HEAD_W * head:G_R + HEAD_W * (head + 1)]
        y_ref[:, 2 * GROUP_WIDTH + HEAD_W * head:2 * GROUP_WIDTH + HEAD_W * (head + 1)] = (
            oh * _silu(gr)).astype(y_ref.dtype)


def _sample_mixers(proj_s, states, layer, prm):
    c_all, n_all, m_all, s_all, cv_all, k_all, v_all = states
    depth = c_all.shape[0]
    nb = proj_s.shape[0]
    bb = SAMPLE_BB
    assert nb == LANES and nb % bb == 0
    lb = layer * (nb // bb)
    n0, m0 = n_all[layer], m_all[layer]
    mm = jnp.concatenate([m0, m0, jnp.zeros((nb, LANES - 2 * MLSTM_HEADS), F32)], axis=1)
    n0f = n0.reshape(nb, GROUP_WIDTH)
    c0 = c_all.reshape((depth * nb,) + c_all.shape[2:])
    s0 = s_all.reshape((depth * nb,) + s_all.shape[2:])
    cv0 = cv_all.reshape((depth * nb,) + cv_all.shape[2:])
    k0f = k_all.reshape(depth * nb, WINDOW, LANES)
    v0f = v_all.reshape(depth * nb, WINDOW, LANES)
    sq = proj_s[:, S_Q:S_Q + GROUP_WIDTH].reshape(nb, SWA_KV_HEADS, SWA_HEADS // SWA_KV_HEADS, SWA_HEAD_DIM)
    zq = jnp.zeros_like(sq[:, 0])
    qm = jnp.concatenate([jnp.concatenate([sq[:, 0], zq], axis=-1), jnp.concatenate([zq, sq[:, 1]], axis=-1)], axis=1)
    sink_b = jnp.broadcast_to(prm['swa_sinks'].astype(F32)[:, None], (SWA_HEADS, LANES))

    full = lambda shape: pl.BlockSpec(shape, lambda i: (0,) * len(shape))
    rowb = lambda w: pl.BlockSpec((bb, w), lambda i: (i, 0))
    in_specs = [
        full((nb, N_PROJ)), rowb(LANES), full((1, LANES)), rowb(GROUP_WIDTH),
        pl.BlockSpec((bb, MLSTM_HEADS, HEAD_W, HEAD_W), lambda i: (lb + i, 0, 0, 0)),
        pl.BlockSpec((bb, GLA_HEADS, GLA_DK, HEAD_W), lambda i: (lb + i, 0, 0, 0)),
        pl.BlockSpec((bb, CONV_WIDTH - 1, GROUP_WIDTH), lambda i: (lb + i, 0, 0)),
        pl.BlockSpec((bb, WINDOW, LANES), lambda i: (lb + i, 0, 0)),
        pl.BlockSpec((bb, WINDOW, LANES), lambda i: (lb + i, 0, 0)),
        pl.BlockSpec((bb, SWA_HEADS, LANES), lambda i: (i, 0, 0)),
        full((SWA_HEADS, LANES)), full((LANES, 256)), full((1, 256)),
        full((1, GROUP_WIDTH)), full((1, GROUP_WIDTH)),
        full((CONV_WIDTH, GROUP_WIDTH)), full((1, GROUP_WIDTH)), full((1, GROUP_WIDTH)), full((1, GROUP_WIDTH)),
    ]
    out_specs = [
        rowb(3 * GROUP_WIDTH),
        pl.BlockSpec((bb, SWA_HEADS, LANES), lambda i: (i, 0, 0)),
        pl.BlockSpec((bb, MLSTM_HEADS, HEAD_W, HEAD_W), lambda i: (i, 0, 0, 0)),
        rowb(GROUP_WIDTH), rowb(LANES),
        pl.BlockSpec((bb, GLA_HEADS, GLA_DK, HEAD_W), lambda i: (i, 0, 0, 0)),
        pl.BlockSpec((bb, CONV_WIDTH - 1, GROUP_WIDTH), lambda i: (i, 0, 0)),
        pl.BlockSpec((bb, WINDOW, LANES), lambda i: (i, 0, 0)),
        pl.BlockSpec((bb, WINDOW, LANES), lambda i: (i, 0, 0)),
    ]
    out_shape = [
        jax.ShapeDtypeStruct((nb, 3 * GROUP_WIDTH), F32),
        jax.ShapeDtypeStruct((nb, SWA_HEADS, LANES), F32),
        jax.ShapeDtypeStruct(c_all.shape[1:], F32),
        jax.ShapeDtypeStruct((nb, GROUP_WIDTH), F32),
        jax.ShapeDtypeStruct((nb, LANES), F32),
        jax.ShapeDtypeStruct(s_all.shape[1:], F32),
        jax.ShapeDtypeStruct(cv_all.shape[1:], F32),
        jax.ShapeDtypeStruct((nb, WINDOW, LANES), F32),
        jax.ShapeDtypeStruct((nb, WINDOW, LANES), F32),
    ]
    scratch = [
        pltpu.VMEM((_T_ROWS, nb), BF16), pltpu.VMEM((_T_ROWS, LANES), F32),
        pltpu.VMEM((bb, GROUP_WIDTH), F32), pltpu.VMEM((bb, GROUP_WIDTH), F32), pltpu.VMEM((bb, GROUP_WIDTH), F32),
    ]
    y, od, c1, n1, m1, s1, cv1, k1, v1 = pl.pallas_call(
        _sample_kernel,
        grid=(nb // bb,),
        in_specs=in_specs, out_specs=out_specs, out_shape=out_shape, scratch_shapes=scratch,
        compiler_params=_cparams(("arbitrary",)),
        name="sample_mixers",
    )(proj_s, mm, prm['gate_bias'], n0f, c0, s0, cv0, k0f, v0f, qm, sink_b,
      prm['gla_w_gate_pad'], prm['gla_b_gate'].reshape(1, 256),
      prm['mlstm_norm_g'].reshape(1, -1), prm['gla_norm_g'].reshape(1, -1),
      prm['conv_w'], prm['conv_b'].reshape(1, -1), prm['conv_norm_g'].reshape(1, -1),
      prm['conv_norm_b'].reshape(1, -1))
    od4 = od.reshape(nb, SWA_KV_HEADS, SWA_HEADS // SWA_KV_HEADS, LANES)
    yd = jnp.concatenate([od4[:, 0, :, :SWA_HEAD_DIM], od4[:, 1, :, SWA_HEAD_DIM:]], axis=1).reshape(nb, GROUP_WIDTH)
    y = jnp.concatenate([y, yd], axis=1).astype(BF16)
    new_state = (c1, n1.reshape(n0.shape), m1[:, :MLSTM_HEADS], s1, cv1,
                 k1.reshape(k_all.shape[1:]), v1.reshape(v_all.shape[1:]))
    return y, new_state


def _outproj_router_kernel(x_ref, mix_ref, w_ref, g_ref, rw_ref, rb_ref,
                           x1_ref, hn_ref, ri_ref, rf_ref, cnt_ref):
    tm = x_ref.shape[0]

    @pl.when(pl.program_id(0) == 0)
    def _():
        cnt_ref[...] = jnp.zeros_like(cnt_ref)

    x1 = x_ref[...] + jnp.dot(mix_ref[...], w_ref[...], preferred_element_type=F32)
    x1_ref[...] = x1
    ms = jnp.mean(x1 * x1, axis=-1, keepdims=True)
    hn = x1 * lax.rsqrt(ms + NORM_EPS) * g_ref[...]
    hn_ref[...] = hn
    logits = jnp.dot(hn.astype(BF16), rw_ref[...], preferred_element_type=F32) + rb_ref[...]
    lane = lax.broadcasted_iota(I32, (tm, LANES), 1)
    big = jnp.int32(LANES)
    gl = jnp.where(lane < N_GROUPS, logits, -jnp.inf)
    gmax = jnp.max(gl, axis=-1, keepdims=True)
    g_sel = jnp.min(jnp.where(gl == gmax, lane, big), axis=-1, keepdims=True)
    g_w = 1.0 / jnp.sum(jnp.exp(gl - gmax), axis=-1, keepdims=True)
    e_lane = lane - N_GROUPS
    in_grp = (e_lane >= 0) & (e_lane < N_EXPERTS) & ((e_lane // EXPERTS_PER_GROUP) == g_sel)
    el = jnp.where(in_grp, logits, -jnp.inf)
    m1 = jnp.max(el, axis=-1, keepdims=True)
    i1 = jnp.min(jnp.where(el == m1, lane, big), axis=-1, keepdims=True)
    el2 = jnp.where(lane == i1, -jnp.inf, el)
    m2 = jnp.max(el2, axis=-1, keepdims=True)
    i2 = jnp.min(jnp.where(el2 == m2, lane, big), axis=-1, keepdims=True)
    r = jnp.exp(m2 - m1)
    p1 = 1.0 / (1.0 + r)
    gate1 = g_w * p1
    gate2 = g_w * (r * p1)
    sel1 = lane == i1
    sel2 = lane == i2
    onehot = jnp.where(sel1 | sel2, 1.0, 0.0)
    row = lax.broadcasted_iota(I32, (tm, tm), 0)
    col = lax.broadcasted_iota(I32, (tm, tm), 1)
    strict = jnp.where(col < row, 1.0, 0.0).astype(BF16)
    cum = jnp.dot(strict, onehot.astype(BF16), preferred_element_type=F32) + cnt_ref[0:1, :]
    rank1 = jnp.sum(jnp.where(sel1, cum, 0.0), axis=-1, keepdims=True).astype(I32)
    rank2 = jnp.sum(jnp.where(sel2, cum, 0.0), axis=-1, keepdims=True).astype(I32)
    cnt_ref[...] = cnt_ref[...] + jnp.sum(onehot, axis=0, keepdims=True)
    ri = jnp.where(lane == 0, i1 - N_GROUPS, jnp.where(lane == 1, i2 - N_GROUPS,
                   jnp.where(lane == 2, rank1, jnp.where(lane == 3, rank2, 0))))
    ri_ref[...] = ri
    rf_ref[...] = jnp.where(lane == 0, gate1, jnp.where(lane == 1, gate2, 0.0))


def _outproj_router(x, mixed, w_out_bf16, norm_g, rw_pad, rb_pad):
    n = x.shape[0]
    tm = _pick_tile(n, (320, 256, 128, 64, 16))
    full = lambda shape: pl.BlockSpec(shape, lambda i: (0,) * len(shape))
    rowb = lambda w: pl.BlockSpec((tm, w), lambda i: (i, 0))
    return pl.pallas_call(
        _outproj_router_kernel,
        grid=(n // tm,),
        in_specs=[rowb(D_MODEL), rowb(D_MODEL), full((D_MODEL, D_MODEL)), full((1, D_MODEL)),
                  full((D_MODEL, LANES)), full((1, LANES))],
        out_specs=[rowb(D_MODEL), rowb(D_MODEL), rowb(LANES), rowb(LANES), full((SUBLANES, LANES))],
        out_shape=[
            jax.ShapeDtypeStruct((n, D_MODEL), F32),
            jax.ShapeDtypeStruct((n, D_MODEL), F32),
            jax.ShapeDtypeStruct((n, LANES), I32),
            jax.ShapeDtypeStruct((n, LANES), F32),
            jax.ShapeDtypeStruct((SUBLANES, LANES), F32),
        ],
        compiler_params=_cparams(("arbitrary",)),
        name="outproj_router",
    )(x, mixed, w_out_bf16, norm_g.reshape(1, D_MODEL), rw_pad, rb_pad)


def _cast_kernel(x_ref, o_ref):
    o_ref[...] = x_ref[...].astype(o_ref.dtype)


def _cast_bf16(w_all, layer):
    depth, n_e, k, f = w_all.shape
    rows = n_e * k
    tr = _pick_tile(rows, tuple(c for c in (2048, 1024, 512, 256, 128, 16) if c * f <= CAST_BLOCK_ELEMS))
    steps = rows // tr
    out = pl.pallas_call(
        _cast_kernel,
        grid=(steps,),
        in_specs=[pl.BlockSpec((tr, f), lambda i: (layer * steps + i, 0))],
        out_specs=pl.BlockSpec((tr, f), lambda i: (i, 0)),
        out_shape=jax.ShapeDtypeStruct((rows, f), BF16),
        compiler_params=_cparams(("parallel",)),
        name="cast_bf16",
    )(w_all.reshape(depth * rows, f))
    return out.reshape(n_e, k, f)


def _dispatch_kernel(pos_ref, zt_ref, hn_ref, xs_ref, zbuf, sem, zsem):
    tm = hn_ref.shape[0]
    tile = zbuf.shape[0]
    base = pl.program_id(0) * tm

    @pl.when(pl.program_id(0) == 0)
    def _():
        zbuf[...] = jnp.zeros_like(zbuf)

        def zero_tile(k, carry):
            @pl.when(zt_ref[k] >= 0)
            def _():
                row = pl.multiple_of(zt_ref[k] * tile, tile)
                cp = pltpu.make_async_copy(zbuf, xs_ref.at[pl.ds(row, tile)], zsem.at[0])
                cp.start()
                cp.wait()
            return carry

        lax.fori_loop(0, zt_ref.shape[0], zero_tile, 0)

    def issue(r, carry):
        tok = base + r
        src = hn_ref.at[pl.ds(r, 1)]
        pltpu.make_async_copy(src, xs_ref.at[pl.ds(pos_ref[2 * tok], 1)], sem.at[0]).start()
        pltpu.make_async_copy(src, xs_ref.at[pl.ds(pos_ref[2 * tok + 1], 1)], sem.at[1]).start()
        return carry

    lax.fori_loop(0, tm, issue, 0, unroll=4)
    pltpu.make_async_copy(hn_ref, xs_ref.at[pl.ds(0, tm)], sem.at[0]).wait()
    pltpu.make_async_copy(hn_ref, xs_ref.at[pl.ds(0, tm)], sem.at[1]).wait()


def _dispatch(pos, zero_tiles, hn, n_rows, tile):
    n = hn.shape[0]
    tm = _pick_tile(n, (256, 128, 64, 8))
    return pl.pallas_call(
        _dispatch_kernel,
        grid_spec=pltpu.PrefetchScalarGridSpec(
            num_scalar_prefetch=2,
            grid=(n // tm,),
            in_specs=[pl.BlockSpec((tm, D_MODEL), lambda i, p, z: (i, 0))],
            out_specs=pl.BlockSpec(memory_space=pl.ANY),
            scratch_shapes=[pltpu.VMEM((tile, D_MODEL), F32), pltpu.SemaphoreType.DMA((2,)),
                            pltpu.SemaphoreType.DMA((1,))],
        ),
        out_shape=jax.ShapeDtypeStruct((n_rows, D_MODEL), F32),
        compiler_params=_cparams(("arbitrary",)),
        name="moe_dispatch",
    )(pos, zero_tiles, hn)


def _expert_kernel(te_ref, nt_ref, x_ref, wg_ref, wu_ref, wd_ref, y_ref):
    del te_ref
    used = pl.program_id(0) < nt_ref[0]

    @pl.when(used)
    def _():
        x = x_ref[...].astype(BF16)
        a = jnp.dot(x, wg_ref[0], preferred_element_type=F32)
        u = jnp.dot(x, wu_ref[0], preferred_element_type=F32)
        hmid = (_silu(a) * u).astype(BF16)
        y_ref[...] = jnp.dot(hmid, wd_ref[0], preferred_element_type=F32)

    @pl.when(jnp.logical_not(used))
    def _():
        y_ref[...] = jnp.zeros_like(y_ref)


def _expert_mlp(tile_expert, n_tiles_used, xs, wg, wu, wd, tile):
    n_tiles = xs.shape[0] // tile

    def row_map(t, te, nt):
        return (jnp.minimum(t, nt[0] - 1), 0)

    def w_map(t, te, nt):
        return (te[jnp.minimum(t, nt[0] - 1)], 0, 0)

    return pl.pallas_call(
        _expert_kernel,
        grid_spec=pltpu.PrefetchScalarGridSpec(
            num_scalar_prefetch=2,
            grid=(n_tiles,),
            in_specs=[
                pl.BlockSpec((tile, D_MODEL), row_map),
                pl.BlockSpec((1, D_MODEL, D_EXPERT), w_map),
                pl.BlockSpec((1, D_MODEL, D_EXPERT), w_map),
                pl.BlockSpec((1, D_EXPERT, D_MODEL), w_map),
            ],
            out_specs=pl.BlockSpec((tile, D_MODEL), lambda t, te, nt: (t, 0)),
        ),
        out_shape=jax.ShapeDtypeStruct(xs.shape, F32),
        compiler_params=_cparams(("arbitrary",)),
        name="expert_mlp",
    )(tile_expert, n_tiles_used, xs, wg, wu, wd)


def _combine_kernel(pos_ref, x1_ref, rf_ref, fg_ref, ys_ref, o_ref, buf_a, buf_b, sem, *, final_norm):
    tm = x1_ref.shape[0]
    base = pl.program_id(0) * tm

    def issue(r, carry):
        tok = base + r
        pltpu.make_async_copy(ys_ref.at[pl.ds(pos_ref[2 * tok], 1)], buf_a.at[pl.ds(r, 1)], sem.at[0]).start()
        pltpu.make_async_copy(ys_ref.at[pl.ds(pos_ref[2 * tok + 1], 1)], buf_b.at[pl.ds(r, 1)], sem.at[1]).start()
        return carry

    lax.fori_loop(0, tm, issue, 0, unroll=4)
    pltpu.make_async_copy(ys_ref.at[pl.ds(0, tm)], buf_a, sem.at[0]).wait()
    pltpu.make_async_copy(ys_ref.at[pl.ds(0, tm)], buf_b, sem.at[1]).wait()
    rf = rf_ref[...]
    x2 = x1_ref[...] + rf[:, 0:1] * buf_a[...] + rf[:, 1:2] * buf_b[...]
    if final_norm:
        ms = jnp.mean(x2 * x2, axis=-1, keepdims=True)
        x2 = x2 * lax.rsqrt(ms + NORM_EPS) * fg_ref[...]
    o_ref[...] = x2


def _combine(pos, x1, rf, ys, final_g, final_norm):
    n = x1.shape[0]
    tm = _pick_tile(n, (256, 128, 64, 8))
    return pl.pallas_call(
        functools.partial(_combine_kernel, final_norm=final_norm),
        grid_spec=pltpu.PrefetchScalarGridSpec(
            num_scalar_prefetch=1,
            grid=(n // tm,),
            in_specs=[
                pl.BlockSpec((tm, D_MODEL), lambda i, p: (i, 0)),
                pl.BlockSpec((tm, LANES), lambda i, p: (i, 0)),
                pl.BlockSpec((1, D_MODEL), lambda i, p: (0, 0)),
                pl.BlockSpec(memory_space=pl.ANY),
            ],
            out_specs=pl.BlockSpec((tm, D_MODEL), lambda i, p: (i, 0)),
            scratch_shapes=[pltpu.VMEM((tm, D_MODEL), F32), pltpu.VMEM((tm, D_MODEL), F32),
                            pltpu.SemaphoreType.DMA((2,))],
        ),
        out_shape=jax.ShapeDtypeStruct((n, D_MODEL), F32),
        compiler_params=_cparams(("arbitrary",)),
        name="moe_combine",
    )(pos, x1, rf, final_g.reshape(1, D_MODEL), ys)


def _pad_w_in(w_in):
    parts = []
    at = 0
    for lo, hi, dst in sorted(_W_IN_SEGMENTS, key=lambda s: s[2]):
        if dst > at:
            parts.append(jnp.zeros((D_MODEL, dst - at), BF16))
        parts.append(w_in[:, lo:hi].astype(BF16))
        at = dst + (hi - lo)
    if at < N_PROJ:
        parts.append(jnp.zeros((D_MODEL, N_PROJ - at), BF16))
    return jnp.concatenate(parts, axis=1)


def _moe(x1, hn, ri, rf, counts, prm, final_g, final_norm, tile):
    n = x1.shape[0]
    n_tiles = -(-2 * n // tile) + N_EXPERTS
    cnt = counts[0, N_GROUPS:N_GROUPS + N_EXPERTS].astype(I32)
    tiles_per = (cnt + tile - 1) // tile
    tile_end = jnp.cumsum(tiles_per)
    row_off = (tile_end - tiles_per) * tile
    pos = (row_off[ri[:, 0:2]] + ri[:, 2:4]).reshape(2 * n)
    tile_ids = jnp.arange(n_tiles, dtype=I32)
    tile_expert = jnp.minimum(jnp.sum((tile_ids[:, None] >= tile_end[None, :]).astype(I32), axis=1), N_EXPERTS - 1)
    n_used = tile_end[N_EXPERTS - 1:N_EXPERTS].astype(I32)
    last_tile = jnp.where(tiles_per > 0, tile_end - 1, -1).astype(I32)
    zero_tiles = jnp.concatenate([last_tile, jnp.where(tile_ids >= n_used[0], tile_ids, -1)])
    xs = _dispatch(pos, zero_tiles, hn, n_tiles * tile, tile)
    ys = _expert_mlp(tile_expert, n_used, xs, prm['expert_w_gate'], prm['expert_w_up'],
                     prm['expert_w_down'], tile)
    return _combine(pos, x1, rf, ys, final_g, final_norm)


def _layer(x, t_len, states, layer, prm, final_g, final_norm, moe_tile):
    proj = _in_projection(x, prm['norm_mix_g'], prm['w_in_pad'])
    ya, cn, m_p = _mlstm_prompt(proj, t_len, prm['gate_bias'], prm['mlstm_norm_g'])
    yb, conv_tail = _conv_prompt(proj, t_len, prm['conv_w'], prm['conv_b'], prm['conv_norm_g'], prm['conv_norm_b'])
    yc, sp = _gla_prompt(proj, t_len, prm['gla_w_gate_pad'], prm['gla_b_gate'], prm['gla_norm_g'])
    yd = _swa_prompt(proj, t_len, prm['swa_sinks'])
    y_s, new_s = _sample_mixers(proj[t_len:], states, layer, prm)
    mixed = jnp.concatenate([jnp.concatenate([ya, yb, yc, yd], axis=1), y_s], axis=0)
    x1, hn, ri, rf, counts = _outproj_router(x, mixed, prm['w_out'], prm['norm_ffn_g'], prm['router_w'], prm['router_b'])
    x2 = _moe(x1, hn, ri, rf, counts, prm, final_g, final_norm, moe_tile)
    p_c = cn[None, :, :, :HEAD_W]
    p_n = cn[None, :, :, HEAD_W]
    p_m = m_p[None, :MLSTM_HEADS, 0]
    p_s = jnp.stack([sp[0, :GLA_DK, :HEAD_W], sp[0, GLA_DK:, HEAD_W:],
                     sp[1, :GLA_DK, :HEAD_W], sp[1, GLA_DK:, HEAD_W:]])[None]
    p_conv = conv_tail[None, CONV_HALO - (CONV_WIDTH - 1):]
    p_k = proj[t_len - WINDOW:t_len, S_K:S_K + LANES].reshape(1, WINDOW, SWA_KV_HEADS, SWA_HEAD_DIM)
    p_v = proj[t_len - WINDOW:t_len, S_V:S_V + LANES].reshape(1, WINDOW, SWA_KV_HEADS, SWA_HEAD_DIM)
    return x2, (p_c, p_n, p_m, p_s, p_conv, p_k, p_v), new_s


def _forward(x_prompt, x_sample, states, layer_params, final_norm_g, moe_tile=MOE_TILE):
    t_len = x_prompt.shape[1]
    x = jnp.concatenate([x_prompt[0], x_sample[:, 0]], axis=0)
    new_p, new_s = [], []
    depth = len(layer_params)
    for l, prm in enumerate(layer_params):
        x, sp, ss = _layer(x, t_len, states, l, prm, final_norm_g, l == depth - 1, moe_tile)
        new_p.append(sp)
        new_s.append(ss)
    y_prompt = x[None, :t_len]
    y_sample = x[t_len:, None]
    p_states = [jnp.stack(parts) for parts in zip(*new_p)]
    s_states = [jnp.stack(parts) for parts in zip(*new_s)]
    return (y_prompt, y_sample, *p_states, *s_states)


def _prep_layer_params(l, norm_mix_g, w_in, mlstm_b_i, mlstm_b_f, mlstm_norm_g, conv_w, conv_b, conv_norm_g,
                       conv_norm_b, gla_w_gate, gla_b_gate, gla_norm_g, swa_sinks, w_out, norm_ffn_g,
                       router_group_w, router_group_b, router_expert_w, router_expert_b, expert_w_gate,
                       expert_w_up, expert_w_down):
    gate_bias = jnp.concatenate([mlstm_b_i[l], mlstm_b_f[l], jnp.zeros((LANES - 2 * MLSTM_HEADS,), F32)])
    rw = jnp.concatenate([router_group_w[l], router_expert_w[l],
                          jnp.zeros((D_MODEL, LANES - N_GROUPS - N_EXPERTS), F32)], axis=1)
    rb = jnp.concatenate([router_group_b[l], router_expert_b[l],
                          jnp.zeros((LANES - N_GROUPS - N_EXPERTS,), F32)])
    return {
        'norm_mix_g': norm_mix_g[l],
        'w_in_pad': _pad_w_in(w_in[l]),
        'gate_bias': gate_bias.reshape(1, LANES),
        'mlstm_norm_g': mlstm_norm_g[l],
        'conv_w': conv_w[l], 'conv_b': conv_b[l], 'conv_norm_g': conv_norm_g[l], 'conv_norm_b': conv_norm_b[l],
        'gla_w_gate_pad': jnp.concatenate(
            [gla_w_gate[l], jnp.zeros((LANES - GLA_LOWRANK, GLA_HEADS * GLA_DK), F32)], axis=0).astype(BF16),
        'gla_b_gate': gla_b_gate[l], 'gla_norm_g': gla_norm_g[l],
        'swa_sinks': swa_sinks[l],
        'w_out': w_out[l].astype(BF16),
        'norm_ffn_g': norm_ffn_g[l],
        'router_w': rw.astype(BF16), 'router_b': rb.reshape(1, LANES),
        'expert_w_gate': _cast_bf16(expert_w_gate, l),
        'expert_w_up': _cast_bf16(expert_w_up, l),
        'expert_w_down': _cast_bf16(expert_w_down, l),
    }


def kernel(x_prompt, x_sample, state_mlstm_C, state_mlstm_n, state_mlstm_m, state_gla_S, cache_conv, cache_swa_k, cache_swa_v, norm_mix_g, w_in, mlstm_b_i, mlstm_b_f, mlstm_norm_g, conv_w, conv_b, conv_norm_g, conv_norm_b, gla_w_gate, gla_b_gate, gla_norm_g, swa_sinks, w_out, norm_ffn_g, router_group_w, router_group_b, router_expert_w, router_expert_b, expert_w_gate, expert_w_up, expert_w_down, final_norm_g):
    depth = w_in.shape[0]
    weights = (norm_mix_g, w_in, mlstm_b_i, mlstm_b_f, mlstm_norm_g, conv_w, conv_b, conv_norm_g, conv_norm_b,
               gla_w_gate, gla_b_gate, gla_norm_g, swa_sinks, w_out, norm_ffn_g, router_group_w, router_group_b,
               router_expert_w, router_expert_b, expert_w_gate, expert_w_up, expert_w_down)
    layer_params = [_prep_layer_params(l, *weights) for l in range(depth)]
    states = (state_mlstm_C, state_mlstm_n, state_mlstm_m, state_gla_S, cache_conv, cache_swa_k, cache_swa_v)
    return _forward(x_prompt, x_sample, states, layer_params, final_norm_g)
```

```python
import functools

import jax
import jax.numpy as jnp
from jax import lax
from jax.experimental import pallas as pl
from jax.experimental.pallas import tpu as pltpu

F32 = jnp.float32
BF16 = jnp.bfloat16
I32 = jnp.int32
HIGHEST = lax.Precision.HIGHEST

D_MODEL = 2048
GROUP_WIDTH = 512
HEAD_W = 128
MLSTM_HEADS = 4
GLA_HEADS = 4
GLA_DK = 64
GLA_LOWRANK = 16
GLA_TAU = 16.0
CONV_WIDTH = 31
SWA_HEADS = 8
SWA_KV_HEADS = 2
SWA_HEAD_DIM = 64
WINDOW = 128
N_GROUPS = 4
EXPERTS_PER_GROUP = 4
N_EXPERTS = 16
D_EXPERT = 1024
NORM_EPS = 1e-6
LANES = 128
SUBLANES = 8
TOK_ROWS = D_MODEL // LANES
PACK_ROWS = TOK_ROWS // 2

A_Q, A_K, A_V, A_O = 0, 512, 1024, 1536
C_A, C_G = 2048, 2560
G_V, G_R = 3072, 3584
S_Q = 4096
G_Q, G_K = 4608, 4864
A_G, G_LR, S_K, S_V = 5120, 5248, 5376, 5504
N_PROJ = 5632
MIX_COL_MLSTM, MIX_COL_CONV, MIX_COL_GLA, MIX_COL_SWA = 0, 1, 2, 3
_W_IN_SEGMENTS = (
    (0, 512, A_Q), (512, 1024, A_K), (1024, 1536, A_V), (1536, 2048, A_O),
    (2048, 2056, A_G),
    (2056, 2568, C_A), (2568, 3080, C_G),
    (3080, 3336, G_Q), (3336, 3592, G_K), (3592, 4104, G_V), (4104, 4616, G_R),
    (4616, 4632, G_LR),
    (4632, 5144, S_Q), (5144, 5272, S_K), (5272, 5400, S_V),
)

MLSTM_CHUNK = 128
MLSTM_STEP_ROWS = 256
GLA_CHUNK = 64
GLA_STEP_ROWS = 256
GLA_SUB = 16
SWA_STEP_ROWS = 256
CONV_TILE = 256
CONV_ROWS = 64
CONV_HALO = 32
SAMPLE_BB = 8
MOE_TILE = 512
CAST_BLOCK_ELEMS = 2 * 1024 * 1024
VMEM_LIMIT = 56 * 1024 * 1024


def _cparams(sem, vmem=VMEM_LIMIT):
    return pltpu.CompilerParams(dimension_semantics=sem, vmem_limit_bytes=vmem)


def _log_sigmoid(x):
    return jnp.minimum(x, 0.0) - jnp.log1p(jnp.exp(-jnp.abs(x)))


def _sigmoid(x):
    return 1.0 / (1.0 + jnp.exp(-x))


def _silu(x):
    return x * _sigmoid(x)


def _masked_row_sums(mask, x):
    m = jnp.where(mask, 1.0, 0.0).astype(BF16)
    hi = x.astype(BF16)
    r1 = x - hi.astype(F32)
    mid = r1.astype(BF16)
    lo = (r1 - mid.astype(F32)).astype(BF16)
    out = jnp.dot(m, hi, preferred_element_type=F32)
    out = out + jnp.dot(m, mid, preferred_element_type=F32)
    return out + jnp.dot(m, lo, preferred_element_type=F32)


def _pick_tile(n, candidates):
    for c in candidates:
        if n % c == 0:
            return c
    raise ValueError(f"no tile for {n} in {candidates}")


def _proj_kernel(x_ref, g_ref, w_ref, o_ref, hn_ref):
    @pl.when(pl.program_id(1) == 0)
    def _():
        x = x_ref[...]
        ms = jnp.mean(x * x, axis=-1, keepdims=True)
        hn_ref[...] = (x * lax.rsqrt(ms + NORM_EPS) * g_ref[...]).astype(BF16)

    o_ref[...] = jnp.dot(hn_ref[...], w_ref[...], preferred_element_type=F32)


def _in_projection(x, g, w_bf16):
    n = x.shape[0]
    tm = _pick_tile(n, (832, 640, 512, 256, 128, 64, 8))
    tn = N_PROJ // 4
    return pl.pallas_call(
        _proj_kernel,
        grid=(n // tm, N_PROJ // tn),
        in_specs=[
            pl.BlockSpec((tm, D_MODEL), lambda i, j: (i, 0)),
            pl.BlockSpec((1, D_MODEL), lambda i, j: (0, 0)),
            pl.BlockSpec((D_MODEL, tn), lambda i, j: (0, j)),
        ],
        out_specs=pl.BlockSpec((tm, tn), lambda i, j: (i, j)),
        out_shape=jax.ShapeDtypeStruct((n, N_PROJ), F32),
        scratch_shapes=[pltpu.VMEM((tm, D_MODEL), BF16)],
        compiler_params=_cparams(("parallel", "arbitrary")),
        name="in_projection",
    )(x, g.reshape(1, D_MODEL), w_bf16)


def _mlstm_prompt_kernel(q_ref, k_ref, v_ref, o_ref, gt_ref, bias_ref, ng_ref, mixed_in_ref,
                         y_ref, cn_ref, m_ref):
    del mixed_in_ref
    rows_step = q_ref.shape[0]
    L = min(MLSTM_CHUNK, rows_step)
    n_chunks = rows_step // L

    @pl.when(pl.program_id(0) == 0)
    def _():
        cn_ref[...] = jnp.zeros_like(cn_ref)
        m_ref[...] = jnp.zeros_like(m_ref)

    pre = gt_ref[...] + bias_ref[...]
    lf = _log_sigmoid(pre)
    row = lax.broadcasted_iota(I32, (rows_step, rows_step), 0)
    col = lax.broadcasted_iota(I32, (rows_step, rows_step), 1)
    cum = ((row // L) == (col // L)) & (col <= row)
    b_all = _masked_row_sums(cum, lf)
    pre_t = pre.T
    b_t = b_all.T
    trow = lax.broadcasted_iota(I32, (L, L), 0)
    tcol = lax.broadcasted_iota(I32, (L, L), 1)
    tri = tcol <= trow
    lane = lax.broadcasted_iota(I32, (L, HEAD_W), 1)
    ones_col = (lane == 0).astype(BF16)
    for h in range(MLSTM_HEADS):
        sl = slice(HEAD_W * h, HEAD_W * (h + 1))
        m_prev = m_ref[h:h + 1, 0:1]
        cn = cn_ref[h]
        for c in range(n_chunks):
            cs = slice(L * c, L * (c + 1))
            q = q_ref[cs, sl]
            k = k_ref[cs, sl] * (HEAD_W ** -0.5)
            v = v_ref[cs, sl]
            b_col = b_all[cs, 4 + h:5 + h]
            i_col = pre[cs, h:h + 1]
            b_row = b_t[4 + h:5 + h, cs]
            i_row = pre_t[h:h + 1, cs]
            log_d = jnp.where(tri, b_col - b_row + i_row, -jnp.inf)
            log_inter = b_col + m_prev
            m_t = jnp.maximum(log_inter, jnp.max(log_d, axis=-1, keepdims=True))
            d_mat = jnp.exp(log_d - m_t)
            g_inter = jnp.exp(log_inter - m_t)
            qb = q.astype(BF16)
            kb = k.astype(BF16)
            s = lax.dot_general(qb, kb, (((1,), (1,)), ((), ())), preferred_element_type=F32)
            w = (s * d_mat).astype(BF16)
            v1 = jnp.concatenate([v.astype(BF16), ones_col], axis=1)
            nd = g_inter * jnp.dot(qb, cn.astype(BF16), preferred_element_type=F32)
            nd = nd + jnp.dot(w, v1, preferred_element_type=F32)
            num = nd[:, :HEAD_W]
            den = nd[:, HEAD_W:HEAD_W + 1]
            hh = num / jnp.maximum(jnp.abs(den), jnp.exp(-m_t))
            hh = _sigmoid(o_ref[cs, sl]) * hh
            hh = hh * lax.rsqrt(jnp.mean(hh * hh, axis=-1, keepdims=True) + NORM_EPS) * ng_ref[:, sl]
            y_ref[cs, sl] = hh.astype(y_ref.dtype)
            m_last = m_t[L - 1:L, :]
            b_last = b_col[L - 1:L, :]
            g_state = jnp.exp(b_last + m_prev - m_last)
            w_k = jnp.exp(b_last - b_col + i_col - m_last)
            kw = (k * w_k).astype(BF16)
            upd = lax.dot_general(kw, v1, (((0,), (0,)), ((), ())), preferred_element_type=F32)
            cn = g_state * cn + upd
            m_prev = m_last
        cn_ref[h] = cn
        m_ref[h:h + 1, :] = jnp.broadcast_to(m_prev, (1, LANES))


def _mlstm_prompt(proj, mixed, t_len, bias_row, norm_g):
    L = _pick_tile(t_len, (MLSTM_STEP_ROWS, MLSTM_CHUNK, 64, 32, 16, 8))

    def col(off):
        return pl.BlockSpec((L, GROUP_WIDTH), lambda i, o=off: (i, o // GROUP_WIDTH))

    return pl.pallas_call(
        _mlstm_prompt_kernel,
        grid=(t_len // L,),
        in_specs=[
            col(A_Q), col(A_K), col(A_V), col(A_O),
            pl.BlockSpec((L, LANES), lambda i: (i, A_G // LANES)),
            pl.BlockSpec((1, LANES), lambda i: (0, 0)),
            pl.BlockSpec((1, GROUP_WIDTH), lambda i: (0, 0)),
            pl.BlockSpec(memory_space=pl.ANY),
        ],
        out_specs=[
            pl.BlockSpec((L, GROUP_WIDTH), lambda i: (i, MIX_COL_MLSTM)),
            pl.BlockSpec((MLSTM_HEADS, HEAD_W, 2 * HEAD_W), lambda i: (0, 0, 0)),
            pl.BlockSpec((SUBLANES, LANES), lambda i: (0, 0)),
        ],
        out_shape=[
            jax.ShapeDtypeStruct(mixed.shape, mixed.dtype),
            jax.ShapeDtypeStruct((MLSTM_HEADS, HEAD_W, 2 * HEAD_W), F32),
            jax.ShapeDtypeStruct((SUBLANES, LANES), F32),
        ],
        input_output_aliases={7: 0},
        compiler_params=_cparams(("arbitrary",)),
        name="mlstm_prompt",
    )(proj, proj, proj, proj, proj, bias_row, norm_g.reshape(1, GROUP_WIDTH), mixed)


def _gla_prompt_kernel(q_ref, k_ref, v_ref, r_ref, lr_ref, wg_ref, bg_ref, ng_ref, mixed_in_ref, y_ref, sp_ref):
    del mixed_in_ref
    rows_step = q_ref.shape[0]
    L = min(GLA_CHUNK, rows_step)
    n_chunks = rows_step // L
    n_sub = L // GLA_SUB

    @pl.when(pl.program_id(0) == 0)
    def _():
        sp_ref[...] = jnp.zeros_like(sp_ref)

    gate_pre = jnp.dot(lr_ref[...].astype(BF16), wg_ref[...], preferred_element_type=F32) + bg_ref[...]
    log_a = _log_sigmoid(gate_pre) * (1.0 / GLA_TAU)
    row = lax.broadcasted_iota(I32, (rows_step, rows_step), 0)
    col = lax.broadcasted_iota(I32, (rows_step, rows_step), 1)
    tri = ((row // L) == (col // L)) & (col <= row)
    b = _masked_row_sums(tri, log_a)
    q = q_ref[...] * (GLA_DK ** -0.5)
    k = k_ref[...]
    b_last = jnp.concatenate(
        [jnp.broadcast_to(b[L * c + L - 1:L * c + L, :], (L, b.shape[1])) for c in range(n_chunks)], axis=0)
    q_in = q * jnp.exp(b)
    k_dec = k * jnp.exp(b_last - b)
    lane16 = lax.broadcasted_iota(I32, (GLA_SUB, LANES), 1)
    lo16 = lane16 < GLA_DK
    srow = lax.broadcasted_iota(I32, (LANES, 2 * HEAD_W), 0)
    scol = lax.broadcasted_iota(I32, (LANES, 2 * HEAD_W), 1)
    block_diag = (srow < GLA_DK) == (scol < HEAD_W)
    for p in range(2):
        pls = slice(LANES * p, LANES * (p + 1))
        vp = v_ref[:, 2 * HEAD_W * p:2 * HEAD_W * (p + 1)].astype(BF16)
        b_t = b[:, pls].T
        sp = sp_ref[p]
        states = []
        for c in range(n_chunks):
            cs = slice(L * c, L * (c + 1))
            states.append(sp)
            dec_col = jnp.exp(b_t[:, L * c + L - 1:L * c + L])
            upd = lax.dot_general(k_dec[cs, pls].astype(BF16), vp[cs], (((0,), (0,)), ((), ())),
                                  preferred_element_type=F32)
            sp = jnp.where(block_diag, dec_col * sp + upd, 0.0)
        sp_ref[p] = sp
        outs = []
        for c in range(n_chunks):
            c0 = L * c
            o_inter = jnp.dot(q_in[c0:c0 + L, pls].astype(BF16), states[c].astype(BF16),
                              preferred_element_type=F32)
            rows = []
            for blk in range(n_sub):
                r0 = GLA_SUB * blk
                n = GLA_SUB * (blk + 1)
                qrows = slice(c0 + r0, c0 + r0 + GLA_SUB)
                krows = slice(c0, c0 + n)
                if blk == 0:
                    qs = q[qrows, pls] * jnp.exp(b[qrows, pls])
                    ks = k[krows, pls] * jnp.exp(-b[krows, pls])
                else:
                    anchor = b[c0 + r0 - 1:c0 + r0, pls]
                    qs = q[qrows, pls] * jnp.exp(b[qrows, pls] - anchor)
                    ks = k[krows, pls] * jnp.exp(anchor - b[krows, pls])
                qs2 = jnp.concatenate([jnp.where(lo16, qs, 0.0), jnp.where(lo16, 0.0, qs)], axis=0)
                att = lax.dot_general(qs2.astype(BF16), ks.astype(BF16), (((1,), (1,)), ((), ())),
                                      preferred_element_type=F32)
                trow = lax.broadcasted_iota(I32, (2 * GLA_SUB, n), 0)
                tcol = lax.broadcasted_iota(I32, (2 * GLA_SUB, n), 1)
                t_idx = r0 + jnp.where(trow >= GLA_SUB, trow - GLA_SUB, trow)
                att = jnp.where(tcol <= t_idx, att, 0.0)
                o2 = jnp.dot(att.astype(BF16), vp[krows], preferred_element_type=F32)
                rows.append(jnp.concatenate([o2[:GLA_SUB, :HEAD_W], o2[GLA_SUB:, HEAD_W:]], axis=1))
            outs.append(o_inter + jnp.concatenate(rows, axis=0))
        o = jnp.concatenate(outs, axis=0)
        for hh in range(2):
            head = 2 * p + hh
            hs = slice(HEAD_W * head, HEAD_W * (head + 1))
            oh = o[:, HEAD_W * hh:HEAD_W * (hh + 1)]
            oh = oh * lax.rsqrt(jnp.mean(oh * oh, axis=-1, keepdims=True) + NORM_EPS) * ng_ref[:, hs]
            y_ref[:, hs] = (oh * _silu(r_ref[:, hs])).astype(y_ref.dtype)


def _gla_prompt(proj, mixed, t_len, w_gate_pad, b_gate, norm_g):
    L = _pick_tile(t_len, (GLA_STEP_ROWS, GLA_CHUNK))
    return pl.pallas_call(
        _gla_prompt_kernel,
        grid=(t_len // L,),
        in_specs=[
            pl.BlockSpec((L, 256), lambda i: (i, G_Q // 256)),
            pl.BlockSpec((L, 256), lambda i: (i, G_K // 256)),
            pl.BlockSpec((L, GROUP_WIDTH), lambda i: (i, G_V // GROUP_WIDTH)),
            pl.BlockSpec((L, GROUP_WIDTH), lambda i: (i, G_R // GROUP_WIDTH)),
            pl.BlockSpec((L, LANES), lambda i: (i, G_LR // LANES)),
            pl.BlockSpec((LANES, 256), lambda i: (0, 0)),
            pl.BlockSpec((1, 256), lambda i: (0, 0)),
            pl.BlockSpec((1, GROUP_WIDTH), lambda i: (0, 0)),
            pl.BlockSpec(memory_space=pl.ANY),
        ],
        out_specs=[
            pl.BlockSpec((L, GROUP_WIDTH), lambda i: (i, MIX_COL_GLA)),
            pl.BlockSpec((2, LANES, 2 * HEAD_W), lambda i: (0, 0, 0)),
        ],
        out_shape=[
            jax.ShapeDtypeStruct(mixed.shape, mixed.dtype),
            jax.ShapeDtypeStruct((2, LANES, 2 * HEAD_W), F32),
        ],
        input_output_aliases={8: 0},
        compiler_params=_cparams(("arbitrary",)),
        name="gla_prompt",
    )(proj, proj, proj, proj, proj, w_gate_pad, b_gate.reshape(1, 256), norm_g.reshape(1, GROUP_WIDTH), mixed)


def _conv_norm_act(y, g_ref, be_ref):
    mu = jnp.mean(y, axis=-1, keepdims=True)
    yc = y - mu
    var = jnp.mean(yc * yc, axis=-1, keepdims=True)
    return _silu(yc * lax.rsqrt(var + NORM_EPS) * g_ref[...] + be_ref[...])


def _conv_prompt_kernel(ua_ref, ug_ref, ha_ref, hg_ref, w_ref, b_ref, g_ref, be_ref, mixed_in_ref,
                        y_ref, tail_ref, buf_ref, sh_ref):
    del mixed_in_ref
    tt = ua_ref.shape[0]
    span = tt + CONV_HALO
    halo = ha_ref[...] * _sigmoid(hg_ref[...])
    buf_ref[0:CONV_HALO, :] = jnp.where(pl.program_id(0) > 0, halo, 0.0)
    buf_ref[CONV_HALO:span, :] = ua_ref[...] * _sigmoid(ug_ref[...])
    buf_ref[span:span + SUBLANES, :] = jnp.zeros((SUBLANES, GROUP_WIDTH), F32)
    for k in range(1, SUBLANES):
        sh_ref[k] = buf_ref[k:k + span, :]
    base = CONV_HALO - (CONV_WIDTH - 1)
    for r in range(tt // CONV_ROWS):
        acc = jnp.zeros((CONV_ROWS, GROUP_WIDTH), F32)
        for j in range(CONV_WIDTH):
            s0 = r * CONV_ROWS + base + j
            k = s0 % SUBLANES
            a0 = s0 - k
            win = buf_ref[a0:a0 + CONV_ROWS, :] if k == 0 else sh_ref[k, a0:a0 + CONV_ROWS, :]
            acc = acc + w_ref[j:j + 1, :] * win
        y = _conv_norm_act(acc + b_ref[...], g_ref, be_ref)
        y_ref[r * CONV_ROWS:(r + 1) * CONV_ROWS, :] = y.astype(y_ref.dtype)
    tail_ref[...] = buf_ref[tt:span, :]


def _conv_prompt(proj, mixed, t_len, w, b, g, beta):
    tt = _pick_tile(t_len, (CONV_TILE, 128, 64))
    ratio = tt // CONV_HALO
    vec = lambda: pl.BlockSpec((1, GROUP_WIDTH), lambda i: (0, 0))
    return pl.pallas_call(
        _conv_prompt_kernel,
        grid=(t_len // tt,),
        in_specs=[
            pl.BlockSpec((tt, GROUP_WIDTH), lambda i: (i, C_A // GROUP_WIDTH)),
            pl.BlockSpec((tt, GROUP_WIDTH), lambda i: (i, C_G // GROUP_WIDTH)),
            pl.BlockSpec((CONV_HALO, GROUP_WIDTH), lambda i: (jnp.maximum(i * ratio - 1, 0), C_A // GROUP_WIDTH)),
            pl.BlockSpec((CONV_HALO, GROUP_WIDTH), lambda i: (jnp.maximum(i * ratio - 1, 0), C_G // GROUP_WIDTH)),
            pl.BlockSpec((CONV_WIDTH, GROUP_WIDTH), lambda i: (0, 0)),
            vec(), vec(), vec(),
            pl.BlockSpec(memory_space=pl.ANY),
        ],
        out_specs=[
            pl.BlockSpec((tt, GROUP_WIDTH), lambda i: (i, MIX_COL_CONV)),
            pl.BlockSpec((CONV_HALO, GROUP_WIDTH), lambda i: (0, 0)),
        ],
        out_shape=[
            jax.ShapeDtypeStruct(mixed.shape, mixed.dtype),
            jax.ShapeDtypeStruct((CONV_HALO, GROUP_WIDTH), F32),
        ],
        scratch_shapes=[pltpu.VMEM((tt + CONV_HALO + SUBLANES, GROUP_WIDTH), F32),
                        pltpu.VMEM((SUBLANES, tt + CONV_HALO, GROUP_WIDTH), F32)],
        input_output_aliases={8: 0},
        compiler_params=_cparams(("arbitrary",)),
        name="conv_prompt",
    )(proj, proj, proj, proj, w, b.reshape(1, -1), g.reshape(1, -1), beta.reshape(1, -1), mixed)


def _swa_prompt_kernel(sink_ref, q_ref, kc_ref, vc_ref, kp_ref, vp_ref, mixed_in_ref, y_ref):
    del mixed_in_ref
    bq = WINDOW
    n_blk = q_ref.shape[0] // bq
    first = pl.program_id(0) == 0
    k_full = jnp.concatenate([kp_ref[...], kc_ref[...]], axis=0)
    v_full = jnp.concatenate([vp_ref[...], vc_ref[...]], axis=0)
    k_sw_full = pltpu.roll(k_full, SWA_HEAD_DIM, 1).astype(BF16)
    v_sw_full = pltpu.roll(v_full, SWA_HEAD_DIM, 1).astype(BF16)
    k_full = k_full.astype(BF16)
    v_full = v_full.astype(BF16)
    tq = lax.broadcasted_iota(I32, (bq, 2 * bq), 0)
    kj = lax.broadcasted_iota(I32, (bq, 2 * bq), 1)
    band = (kj > tq) & (kj <= tq + WINDOW)
    lane = lax.broadcasted_iota(I32, (bq, LANES), 1)
    lo = lane < SWA_HEAD_DIM
    rep = SWA_HEADS // SWA_KV_HEADS
    for blk in range(n_blk):
        qs = slice(bq * blk, bq * (blk + 1))
        ks = slice(bq * blk, bq * (blk + 2))
        valid = band & (kj >= jnp.where(first, bq, 0)) if blk == 0 else band
        k_all, v_all, k_sw, v_sw = k_full[ks], v_full[ks], k_sw_full[ks], v_sw_full[ks]
        for c in range(SWA_HEADS // 2):
            qc = q_ref[qs, LANES * c:LANES * (c + 1)] * (SWA_HEAD_DIM ** -0.5)
            outs = []
            for hh in range(2):
                h = 2 * c + hh
                g = h // rep
                qm = jnp.where(lo if hh == 0 else jnp.logical_not(lo), qc, 0.0).astype(BF16)
                k_use = k_all if g == hh else k_sw
                v_use = v_all if g == hh else v_sw
                s = lax.dot_general(qm, k_use, (((1,), (1,)), ((), ())), preferred_element_type=F32)
                s = jnp.where(valid, s, -jnp.inf)
                sink = sink_ref[h]
                mx = jnp.maximum(jnp.max(s, axis=-1, keepdims=True), sink)
                p = jnp.exp(s - mx)
                den = jnp.sum(p, axis=-1, keepdims=True) + jnp.exp(sink - mx)
                p = (p / den).astype(BF16)
                outs.append(jnp.dot(p, v_use, preferred_element_type=F32))
            y_ref[qs, LANES * c:LANES * (c + 1)] = jnp.where(lo, outs[0], outs[1]).astype(y_ref.dtype)


def _swa_prompt(proj, mixed, t_len, sinks):
    bq = WINDOW
    rows = _pick_tile(t_len, (SWA_STEP_ROWS, bq))
    ratio = rows // bq
    cur = lambda off: pl.BlockSpec((rows, LANES), lambda i: (i, off // LANES))
    prev = lambda off: pl.BlockSpec((bq, LANES), lambda i: (jnp.maximum(i * ratio - 1, 0), off // LANES))
    return pl.pallas_call(
        _swa_prompt_kernel,
        grid=(t_len // rows,),
        in_specs=[
            pl.BlockSpec(memory_space=pltpu.SMEM),
            pl.BlockSpec((rows, GROUP_WIDTH), lambda i: (i, S_Q // GROUP_WIDTH)),
            cur(S_K), cur(S_V), prev(S_K), prev(S_V),
            pl.BlockSpec(memory_space=pl.ANY),
        ],
        out_specs=pl.BlockSpec((rows, GROUP_WIDTH), lambda i: (i, MIX_COL_SWA)),
        out_shape=jax.ShapeDtypeStruct(mixed.shape, mixed.dtype),
        input_output_aliases={6: 0},
        compiler_params=_cparams(("parallel",)),
        name="swa_prompt",
    )(sinks, proj, proj, proj, proj, proj, mixed)


_T_MK, _T_MQ = 0, 512
_T_GA, _T_GK, _T_GQ = 1024, 1280, 1536
_T_ROWS = 1792


def _sample_kernel(ps_ref, mm_ref, bias_ref, n0_ref, c0_ref, s0_ref, cv0_ref, k0_ref, v0_ref,
                   qm_ref, sink_ref, wg_ref, bg_ref, mng_ref, gng_ref, cw_ref, cb_ref, cg_ref, cbe_ref,
                   mix_ref, c1_ref, n1_ref, m1_ref, s1_ref, cv1_ref, k1_ref, v1_ref,
                   tt_ref, bc_ref, num_ref, go_ref, yc_ref, yd_ref, ybuf, ybf, zbuf, zsem, ysem):
    i = pl.program_id(0)
    nb = ps_ref.shape[0]
    bb = c0_ref.shape[0]
    t_len = mix_ref.shape[0] - nb
    zr = zbuf.shape[0]

    def zero_copy(t):
        return pltpu.make_async_copy(zbuf, mix_ref.at[pl.ds(t * zr, zr)], zsem.at[0])

    def gla_gate(lr):
        gp = jnp.dot(lr.astype(BF16), wg_ref[...], preferred_element_type=F32) + bg_ref[...]
        return jnp.exp(_log_sigmoid(gp) * (1.0 / GLA_TAU))

    @pl.when(i == 0)
    def _():
        zbuf[...] = jnp.zeros_like(zbuf)
        for t in range(t_len // zr):
            zero_copy(t).start()
        for h in range(MLSTM_HEADS):
            kk = ps_ref[:, A_K + HEAD_W * h:A_K + HEAD_W * (h + 1)] * (HEAD_W ** -0.5)
            tt_ref[_T_MK + HEAD_W * h:_T_MK + HEAD_W * (h + 1), :] = kk.T.astype(BF16)
            qq = ps_ref[:, A_Q + HEAD_W * h:A_Q + HEAD_W * (h + 1)]
            tt_ref[_T_MQ + HEAD_W * h:_T_MQ + HEAD_W * (h + 1), :] = qq.T.astype(BF16)
        a_all = gla_gate(ps_ref[:, G_LR:G_LR + LANES])
        for p in range(2):
            pls = slice(LANES * p, LANES * (p + 1))
            tt_ref[_T_GA + LANES * p:_T_GA + LANES * (p + 1), :] = a_all[:, pls].T.astype(BF16)
            kk = ps_ref[:, G_K + LANES * p:G_K + LANES * (p + 1)]
            tt_ref[_T_GK + LANES * p:_T_GK + LANES * (p + 1), :] = kk.T.astype(BF16)
            qq = ps_ref[:, G_Q + LANES * p:G_Q + LANES * (p + 1)] * (GLA_DK ** -0.5)
            tt_ref[_T_GQ + LANES * p:_T_GQ + LANES * (p + 1), :] = qq.T.astype(BF16)

    r0 = pl.multiple_of(i * bb, bb)
    rows = pl.ds(r0, bb)

    pre = ps_ref[rows, A_G:A_G + LANES] + bias_ref[...]
    lfm = _log_sigmoid(pre) + mm_ref[...]
    f_al = pltpu.roll(lfm, LANES - MLSTM_HEADS, 1)
    m_t = jnp.maximum(f_al, pre)
    g_st = jnp.exp(f_al - m_t)
    w_k = jnp.exp(pre - m_t)
    m1_ref[...] = m_t
    n_new = []
    for h in range(MLSTM_HEADS):
        kk = ps_ref[rows, A_K + HEAD_W * h:A_K + HEAD_W * (h + 1)] * (HEAD_W ** -0.5)
        nn = g_st[:, h:h + 1] * n0_ref[:, HEAD_W * h:HEAD_W * (h + 1)] + w_k[:, h:h + 1] * kk
        n1_ref[:, HEAD_W * h:HEAD_W * (h + 1)] = nn
        n_new.append(nn)

    glu = ps_ref[rows, C_A:C_A + GROUP_WIDTH] * _sigmoid(ps_ref[rows, C_G:C_G + GROUP_WIDTH])
    yc_ref[...] = glu * cw_ref[CONV_WIDTH - 1:CONV_WIDTH, :]
    a_v = ps_ref[rows, A_V:A_V + GROUP_WIDTH]
    g_v = ps_ref[rows, G_V:G_V + GROUP_WIDTH]
    s_k = ps_ref[rows, S_K:S_K + LANES]
    s_v = ps_ref[rows, S_V:S_V + LANES]

    lane_b = lax.broadcasted_iota(I32, (nb, LANES), 0)
    key_row = lax.broadcasted_iota(I32, (SWA_HEADS, WINDOW), 1)
    lo_row = lax.broadcasted_iota(I32, (1, LANES), 1) < SWA_HEAD_DIM
    sink_col = sink_ref[:, 0:1]

    for j in range(bb):
        onehot = (lane_b == r0 + j).astype(BF16)
        bc_ref[...] = jnp.dot(tt_ref[...], onehot, preferred_element_type=F32)
        jrow = slice(j, j + 1)
        for h in range(MLSTM_HEADS):
            hs = slice(HEAD_W * h, HEAD_W * (h + 1))
            kbc = bc_ref[_T_MK + HEAD_W * h:_T_MK + HEAD_W * (h + 1), :]
            qbc = bc_ref[_T_MQ + HEAD_W * h:_T_MQ + HEAD_W * (h + 1), :]
            g1 = g_st[jrow, h:h + 1]
            w1 = w_k[jrow, h:h + 1]
            v_row = a_v[jrow, hs]
            c_new = g1 * c0_ref[j, h] + kbc * (w1 * v_row)
            c1_ref[j, h] = c_new
            num_ref[jrow, hs] = jnp.sum(qbc * c_new, axis=0, keepdims=True)
        for p in range(2):
            abc = bc_ref[_T_GA + LANES * p:_T_GA + LANES * (p + 1), :]
            kbc = bc_ref[_T_GK + LANES * p:_T_GK + LANES * (p + 1), :]
            qbc = bc_ref[_T_GQ + LANES * p:_T_GQ + LANES * (p + 1), :]
            for hh in range(2):
                head = 2 * p + hh
                hs = slice(HEAD_W * head, HEAD_W * (head + 1))
                ds_ = slice(GLA_DK * hh, GLA_DK * (hh + 1))
                v_row = g_v[jrow, hs]
                s_new = abc[ds_, :] * s0_ref[j, head] + kbc[ds_, :] * v_row
                s1_ref[j, head] = s_new
                go_ref[jrow, hs] = jnp.sum(qbc[ds_, :] * s_new, axis=0, keepdims=True)
        cache = cv0_ref[j]
        yc_ref[jrow, :] = yc_ref[jrow, :] + jnp.sum(cache * cw_ref[0:CONV_WIDTH - 1, :], axis=0, keepdims=True)
        cv1_ref[j, 0:CONV_WIDTH - 2, :] = cv0_ref[j, 1:CONV_WIDTH - 1, :]
        cv1_ref[j, CONV_WIDTH - 2:CONV_WIDTH - 1, :] = glu[jrow, :]
        k_new = s_k[jrow, :]
        v_new = s_v[jrow, :]
        k1_ref[j, 0:WINDOW - 1, :] = k0_ref[j, 1:WINDOW, :]
        k1_ref[j, WINDOW - 1:WINDOW, :] = k_new
        v1_ref[j, 0:WINDOW - 1, :] = v0_ref[j, 1:WINDOW, :]
        v1_ref[j, WINDOW - 1:WINDOW, :] = v_new
        qmat = qm_ref[j] * (SWA_HEAD_DIM ** -0.5)
        s_old = lax.dot_general(qmat.astype(BF16), k0_ref[j].astype(BF16), (((1,), (1,)), ((), ())),
                                preferred_element_type=F32)
        s_old = jnp.where(key_row >= 1, s_old, -jnp.inf)
        s_cur = jnp.sum(qmat * k_new, axis=-1, keepdims=True)
        mx = jnp.maximum(jnp.maximum(jnp.max(s_old, axis=-1, keepdims=True), s_cur), sink_col)
        p_old = jnp.exp(s_old - mx)
        p_cur = jnp.exp(s_cur - mx)
        den = jnp.sum(p_old, axis=-1, keepdims=True) + p_cur + jnp.exp(sink_col - mx)
        o = jnp.dot((p_old / den).astype(BF16), v0_ref[j].astype(BF16), preferred_element_type=F32)
        o = o + (p_cur / den) * v_new
        o_sw = pltpu.roll(o, SWA_HEAD_DIM, 1)
        for c in range(SWA_HEADS // 2):
            g = c // (SWA_HEADS // SWA_KV_HEADS // 2)
            left = (o if g == 0 else o_sw)[2 * c:2 * c + 1, :]
            right = (o_sw if g == 0 else o)[2 * c + 1:2 * c + 2, :]
            yd_ref[jrow, LANES * c:LANES * (c + 1)] = jnp.where(lo_row, left, right)

    for h in range(MLSTM_HEADS):
        hs = slice(HEAD_W * h, HEAD_W * (h + 1))
        qq = ps_ref[rows, A_Q + HEAD_W * h:A_Q + HEAD_W * (h + 1)]
        den = jnp.sum(qq * n_new[h], axis=-1, keepdims=True)
        hh = num_ref[:, hs] / jnp.maximum(jnp.abs(den), jnp.exp(-m_t[:, h:h + 1]))
        hh = _sigmoid(ps_ref[rows, A_O + HEAD_W * h:A_O + HEAD_W * (h + 1)]) * hh
        hh = hh * lax.rsqrt(jnp.mean(hh * hh, axis=-1, keepdims=True) + NORM_EPS) * mng_ref[:, hs]
        ybuf[rows, hs] = hh
    ybuf[rows, GROUP_WIDTH:2 * GROUP_WIDTH] = _conv_norm_act(yc_ref[...] + cb_ref[...], cg_ref, cbe_ref)
    for head in range(GLA_HEADS):
        hs = slice(HEAD_W * head, HEAD_W * (head + 1))
        oh = go_ref[:, hs]
        oh = oh * lax.rsqrt(jnp.mean(oh * oh, axis=-1, keepdims=True) + NORM_EPS) * gng_ref[:, hs]
        gr = ps_ref[rows, G_R + HEAD_W * head:G_R + HEAD_W * (head + 1)]
        ybuf[rows, 2 * GROUP_WIDTH + HEAD_W * head:2 * GROUP_WIDTH + HEAD_W * (head + 1)] = oh * _silu(gr)
    ybuf[rows, 3 * GROUP_WIDTH:4 * GROUP_WIDTH] = yd_ref[...]

    @pl.when(i == pl.num_programs(0) - 1)
    def _():
        ybf[...] = ybuf[...].astype(BF16)
        cp = pltpu.make_async_copy(ybf, mix_ref.at[pl.ds(t_len, nb)], ysem.at[0])
        cp.start()
        for t in range(t_len // zr):
            zero_copy(t).wait()
        cp.wait()


def _sample_mixers(proj_s, t_len, states, layer, prm):
    c_all, n_all, m_all, s_all, cv_all, k_all, v_all = states
    depth = c_all.shape[0]
    nb = proj_s.shape[0]
    bb = SAMPLE_BB
    assert nb == LANES and nb % bb == 0
    lb = layer * (nb // bb)
    n0, m0 = n_all[layer], m_all[layer]
    mm = jnp.concatenate([m0, m0, jnp.zeros((nb, LANES - 2 * MLSTM_HEADS), F32)], axis=1)
    n0f = n0.reshape(nb, GROUP_WIDTH)
    c0 = c_all.reshape((depth * nb,) + c_all.shape[2:])
    s0 = s_all.reshape((depth * nb,) + s_all.shape[2:])
    cv0 = cv_all.reshape((depth * nb,) + cv_all.shape[2:])
    k0f = k_all.reshape(depth * nb, WINDOW, LANES)
    v0f = v_all.reshape(depth * nb, WINDOW, LANES)
    sq = proj_s[:, S_Q:S_Q + GROUP_WIDTH].reshape(nb, SWA_KV_HEADS, SWA_HEADS // SWA_KV_HEADS, SWA_HEAD_DIM)
    zq = jnp.zeros_like(sq[:, 0])
    qm = jnp.concatenate([jnp.concatenate([sq[:, 0], zq], axis=-1), jnp.concatenate([zq, sq[:, 1]], axis=-1)], axis=1)
    sink_b = jnp.broadcast_to(prm['swa_sinks'].astype(F32)[:, None], (SWA_HEADS, LANES))

    full = lambda shape: pl.BlockSpec(shape, lambda i: (0,) * len(shape))
    rowb = lambda w: pl.BlockSpec((bb, w), lambda i: (i, 0))
    in_specs = [
        full((nb, N_PROJ)), rowb(LANES), full((1, LANES)), rowb(GROUP_WIDTH),
        pl.BlockSpec((bb, MLSTM_HEADS, HEAD_W, HEAD_W), lambda i: (lb + i, 0, 0, 0)),
        pl.BlockSpec((bb, GLA_HEADS, GLA_DK, HEAD_W), lambda i: (lb + i, 0, 0, 0)),
        pl.BlockSpec((bb, CONV_WIDTH - 1, GROUP_WIDTH), lambda i: (lb + i, 0, 0)),
        pl.BlockSpec((bb, WINDOW, LANES), lambda i: (lb + i, 0, 0)),
        pl.BlockSpec((bb, WINDOW, LANES), lambda i: (lb + i, 0, 0)),
        pl.BlockSpec((bb, SWA_HEADS, LANES), lambda i: (i, 0, 0)),
        full((SWA_HEADS, LANES)), full((LANES, 256)), full((1, 256)),
        full((1, GROUP_WIDTH)), full((1, GROUP_WIDTH)),
        full((CONV_WIDTH, GROUP_WIDTH)), full((1, GROUP_WIDTH)), full((1, GROUP_WIDTH)), full((1, GROUP_WIDTH)),
    ]
    out_specs = [
        pl.BlockSpec(memory_space=pl.ANY),
        pl.BlockSpec((bb, MLSTM_HEADS, HEAD_W, HEAD_W), lambda i: (i, 0, 0, 0)),
        rowb(GROUP_WIDTH), rowb(LANES),
        pl.BlockSpec((bb, GLA_HEADS, GLA_DK, HEAD_W), lambda i: (i, 0, 0, 0)),
        pl.BlockSpec((bb, CONV_WIDTH - 1, GROUP_WIDTH), lambda i: (i, 0, 0)),
        pl.BlockSpec((bb, WINDOW, LANES), lambda i: (i, 0, 0)),
        pl.BlockSpec((bb, WINDOW, LANES), lambda i: (i, 0, 0)),
    ]
    out_shape = [
        jax.ShapeDtypeStruct((t_len + nb, D_MODEL), BF16),
        jax.ShapeDtypeStruct(c_all.shape[1:], F32),
        jax.ShapeDtypeStruct((nb, GROUP_WIDTH), F32),
        jax.ShapeDtypeStruct((nb, LANES), F32),
        jax.ShapeDtypeStruct(s_all.shape[1:], F32),
        jax.ShapeDtypeStruct(cv_all.shape[1:], F32),
        jax.ShapeDtypeStruct((nb, WINDOW, LANES), F32),
        jax.ShapeDtypeStruct((nb, WINDOW, LANES), F32),
    ]
    scratch = [
        pltpu.VMEM((_T_ROWS, nb), BF16), pltpu.VMEM((_T_ROWS, LANES), F32),
        pltpu.VMEM((bb, GROUP_WIDTH), F32), pltpu.VMEM((bb, GROUP_WIDTH), F32), pltpu.VMEM((bb, GROUP_WIDTH), F32),
        pltpu.VMEM((bb, GROUP_WIDTH), F32),
        pltpu.VMEM((nb, D_MODEL), F32), pltpu.VMEM((nb, D_MODEL), BF16),
        pltpu.VMEM((_pick_tile(t_len, (512, 256, 128, 64, 16)), D_MODEL), BF16),
        pltpu.SemaphoreType.DMA((1,)), pltpu.SemaphoreType.DMA((1,)),
    ]
    mixed, c1, n1, m1, s1, cv1, k1, v1 = pl.pallas_call(
        _sample_kernel,
        grid=(nb // bb,),
        in_specs=in_specs, out_specs=out_specs, out_shape=out_shape, scratch_shapes=scratch,
        compiler_params=_cparams(("arbitrary",)),
        name="sample_mixers",
    )(proj_s, mm, prm['gate_bias'], n0f, c0, s0, cv0, k0f, v0f, qm, sink_b,
      prm['gla_w_gate_pad'], prm['gla_b_gate'].reshape(1, 256),
      prm['mlstm_norm_g'].reshape(1, -1), prm['gla_norm_g'].reshape(1, -1),
      prm['conv_w'], prm['conv_b'].reshape(1, -1), prm['conv_norm_g'].reshape(1, -1),
      prm['conv_norm_b'].reshape(1, -1))
    new_state = (c1, n1.reshape(n0.shape), m1[:, :MLSTM_HEADS], s1, cv1,
                 k1.reshape(k_all.shape[1:]), v1.reshape(v_all.shape[1:]))
    return mixed, new_state


def _outproj_router_kernel(x_ref, mix_ref, w_ref, g_ref, rw_ref, rb_ref,
                           x1_ref, hn_ref, ri_ref, rf_ref, cnt_ref):
    tm = x_ref.shape[0]

    @pl.when(pl.program_id(0) == 0)
    def _():
        cnt_ref[...] = jnp.zeros_like(cnt_ref)

    x1 = x_ref[...] + jnp.dot(mix_ref[...], w_ref[...], preferred_element_type=F32)
    x1_ref[...] = x1
    ms = jnp.mean(x1 * x1, axis=-1, keepdims=True)
    hn = x1 * lax.rsqrt(ms + NORM_EPS) * g_ref[...]
    hn_ref[...] = hn
    logits = jnp.dot(hn.astype(BF16), rw_ref[...], preferred_element_type=F32) + rb_ref[...]
    lane = lax.broadcasted_iota(I32, (tm, LANES), 1)
    big = jnp.int32(LANES)
    gl = jnp.where(lane < N_GROUPS, logits, -jnp.inf)
    gmax = jnp.max(gl, axis=-1, keepdims=True)
    g_sel = jnp.min(jnp.where(gl == gmax, lane, big), axis=-1, keepdims=True)
    g_w = 1.0 / jnp.sum(jnp.exp(gl - gmax), axis=-1, keepdims=True)
    e_lane = lane - N_GROUPS
    in_grp = (e_lane >= 0) & (e_lane < N_EXPERTS) & ((e_lane // EXPERTS_PER_GROUP) == g_sel)
    el = jnp.where(in_grp, logits, -jnp.inf)
    m1 = jnp.max(el, axis=-1, keepdims=True)
    i1 = jnp.min(jnp.where(el == m1, lane, big), axis=-1, keepdims=True)
    el2 = jnp.where(lane == i1, -jnp.inf, el)
    m2 = jnp.max(el2, axis=-1, keepdims=True)
    i2 = jnp.min(jnp.where(el2 == m2, lane, big), axis=-1, keepdims=True)
    r = jnp.exp(m2 - m1)
    p1 = 1.0 / (1.0 + r)
    gate1 = g_w * p1
    gate2 = g_w * (r * p1)
    sel1 = lane == i1
    sel2 = lane == i2
    onehot = jnp.where(sel1 | sel2, 1.0, 0.0)
    row = lax.broadcasted_iota(I32, (tm, tm), 0)
    col = lax.broadcasted_iota(I32, (tm, tm), 1)
    strict = jnp.where(col < row, 1.0, 0.0).astype(BF16)
    cum = jnp.dot(strict, onehot.astype(BF16), preferred_element_type=F32) + cnt_ref[0:1, :]
    rank1 = jnp.sum(jnp.where(sel1, cum, 0.0), axis=-1, keepdims=True).astype(I32)
    rank2 = jnp.sum(jnp.where(sel2, cum, 0.0), axis=-1, keepdims=True).astype(I32)
    cnt_ref[...] = cnt_ref[...] + jnp.sum(onehot, axis=0, keepdims=True)
    ri = jnp.where(lane == 0, i1 - N_GROUPS, jnp.where(lane == 1, i2 - N_GROUPS,
                   jnp.where(lane == 2, rank1, jnp.where(lane == 3, rank2, 0))))
    ri_ref[...] = ri
    rf_ref[...] = jnp.where(lane == 0, gate1, jnp.where(lane == 1, gate2, 0.0))


def _outproj_router(x, mixed, w_out_bf16, norm_g, rw_pad, rb_pad):
    n = x.shape[0]
    tm = _pick_tile(n, (320, 256, 128, 64, 16))
    full = lambda shape: pl.BlockSpec(shape, lambda i: (0,) * len(shape))
    rowb = lambda w: pl.BlockSpec((tm, w), lambda i: (i, 0))
    return pl.pallas_call(
        _outproj_router_kernel,
        grid=(n // tm,),
        in_specs=[rowb(D_MODEL), rowb(D_MODEL), full((D_MODEL, D_MODEL)), full((1, D_MODEL)),
                  full((D_MODEL, LANES)), full((1, LANES))],
        out_specs=[rowb(D_MODEL), rowb(D_MODEL), rowb(LANES), rowb(LANES), full((SUBLANES, LANES))],
        out_shape=[
            jax.ShapeDtypeStruct((n, D_MODEL), F32),
            jax.ShapeDtypeStruct((n, D_MODEL), F32),
            jax.ShapeDtypeStruct((n, LANES), I32),
            jax.ShapeDtypeStruct((n, LANES), F32),
            jax.ShapeDtypeStruct((SUBLANES, LANES), F32),
        ],
        compiler_params=_cparams(("arbitrary",)),
        name="outproj_router",
    )(x, mixed, w_out_bf16, norm_g.reshape(1, D_MODEL), rw_pad, rb_pad)


def _cast_kernel(x_ref, o_ref):
    o_ref[...] = x_ref[...].astype(o_ref.dtype)


def _cast_bf16(w_all, layer):
    depth, n_e, k, f = w_all.shape
    rows = n_e * k
    tr = _pick_tile(rows, tuple(c for c in (2048, 1024, 512, 256, 128, 16) if c * f <= CAST_BLOCK_ELEMS))
    steps = rows // tr
    out = pl.pallas_call(
        _cast_kernel,
        grid=(steps,),
        in_specs=[pl.BlockSpec((tr, f), lambda i: (layer * steps + i, 0))],
        out_specs=pl.BlockSpec((tr, f), lambda i: (i, 0)),
        out_shape=jax.ShapeDtypeStruct((rows, f), BF16),
        compiler_params=_cparams(("parallel",)),
        name="cast_bf16",
    )(w_all.reshape(depth * rows, f))
    return out.reshape(n_e, k, f)


def _dispatch_kernel(pos_ref, zt_ref, hn_ref, xs_ref, zbuf, sem, zsem):
    tm = hn_ref.shape[0]
    tile = zbuf.shape[0]
    base = pl.program_id(0) * tm

    @pl.when(pl.program_id(0) == 0)
    def _():
        zbuf[...] = jnp.zeros_like(zbuf)

        def zero_tile(k, carry):
            @pl.when(zt_ref[k] >= 0)
            def _():
                row = pl.multiple_of(zt_ref[k] * tile, tile)
                cp = pltpu.make_async_copy(zbuf, xs_ref.at[pl.ds(row, tile)], zsem.at[0])
                cp.start()
                cp.wait()
            return carry

        lax.fori_loop(0, zt_ref.shape[0], zero_tile, 0)

    def issue(r, carry):
        tok = base + r
        src = hn_ref.at[pl.ds(r, 1)]
        pltpu.make_async_copy(src, xs_ref.at[pl.ds(pos_ref[2 * tok], 1)], sem.at[0]).start(priority=0)
        pltpu.make_async_copy(src, xs_ref.at[pl.ds(pos_ref[2 * tok + 1], 1)], sem.at[1]).start(priority=1)
        return carry

    lax.fori_loop(0, tm, issue, 0, unroll=4)
    pltpu.make_async_copy(hn_ref, xs_ref.at[pl.ds(0, tm)], sem.at[0]).wait()
    pltpu.make_async_copy(hn_ref, xs_ref.at[pl.ds(0, tm)], sem.at[1]).wait()


def _dispatch(pos, zero_tiles, hn, n_rows, tile):
    n = hn.shape[0]
    tm = _pick_tile(n, (256, 128, 64, 8))
    return pl.pallas_call(
        _dispatch_kernel,
        grid_spec=pltpu.PrefetchScalarGridSpec(
            num_scalar_prefetch=2,
            grid=(n // tm,),
            in_specs=[pl.BlockSpec((tm, D_MODEL), lambda i, p, z: (i, 0))],
            out_specs=pl.BlockSpec(memory_space=pl.ANY),
            scratch_shapes=[pltpu.VMEM((tile, D_MODEL), F32), pltpu.SemaphoreType.DMA((2,)),
                            pltpu.SemaphoreType.DMA((1,))],
        ),
        out_shape=jax.ShapeDtypeStruct((n_rows, D_MODEL), F32),
        compiler_params=_cparams(("arbitrary",)),
        name="moe_dispatch",
    )(pos, zero_tiles, hn)


def _expert_kernel(te_ref, nt_ref, x_ref, wg_ref, wu_ref, wd_ref, y_ref):
    del te_ref
    used = pl.program_id(0) < nt_ref[0]

    @pl.when(used)
    def _():
        x = x_ref[...].astype(BF16)
        a = jnp.dot(x, wg_ref[0], preferred_element_type=F32)
        u = jnp.dot(x, wu_ref[0], preferred_element_type=F32)
        hmid = (_silu(a) * u).astype(BF16)
        y_ref[...] = jnp.dot(hmid, wd_ref[0], preferred_element_type=F32)

    @pl.when(jnp.logical_not(used))
    def _():
        y_ref[...] = jnp.zeros_like(y_ref)


def _expert_mlp(tile_expert, n_tiles_used, xs, wg, wu, wd, tile):
    n_tiles = xs.shape[0] // tile

    def row_map(t, te, nt):
        return (jnp.minimum(t, nt[0] - 1), 0)

    def w_map(t, te, nt):
        return (te[jnp.minimum(t, nt[0] - 1)], 0, 0)

    return pl.pallas_call(
        _expert_kernel,
        grid_spec=pltpu.PrefetchScalarGridSpec(
            num_scalar_prefetch=2,
            grid=(n_tiles,),
            in_specs=[
                pl.BlockSpec((tile, D_MODEL), row_map),
                pl.BlockSpec((1, D_MODEL, D_EXPERT), w_map),
                pl.BlockSpec((1, D_MODEL, D_EXPERT), w_map),
                pl.BlockSpec((1, D_EXPERT, D_MODEL), w_map),
            ],
            out_specs=pl.BlockSpec((tile, D_MODEL), lambda t, te, nt: (t, 0)),
        ),
        out_shape=jax.ShapeDtypeStruct(xs.shape, F32),
        compiler_params=_cparams(("arbitrary",)),
        name="expert_mlp",
    )(tile_expert, n_tiles_used, xs, wg, wu, wd)


def _combine_kernel(pos_ref, x1_ref, rf_ref, fg_ref, ys_ref, o_ref, buf_a, buf_b, sem, *, final_norm):
    tm = x1_ref.shape[0]
    base = pl.program_id(0) * tm

    def issue(r, carry):
        tok = base + r
        pltpu.make_async_copy(ys_ref.at[pl.ds(pos_ref[2 * tok], 1)], buf_a.at[pl.ds(r, 1)],
                              sem.at[0]).start(priority=0)
        pltpu.make_async_copy(ys_ref.at[pl.ds(pos_ref[2 * tok + 1], 1)], buf_b.at[pl.ds(r, 1)],
                              sem.at[1]).start(priority=1)
        return carry

    lax.fori_loop(0, tm, issue, 0, unroll=4)
    pltpu.make_async_copy(ys_ref.at[pl.ds(0, tm)], buf_a, sem.at[0]).wait()
    pltpu.make_async_copy(ys_ref.at[pl.ds(0, tm)], buf_b, sem.at[1]).wait()
    rf = rf_ref[...]
    x2 = x1_ref[...] + rf[:, 0:1] * buf_a[...] + rf[:, 1:2] * buf_b[...]
    if final_norm:
        ms = jnp.mean(x2 * x2, axis=-1, keepdims=True)
        x2 = x2 * lax.rsqrt(ms + NORM_EPS) * fg_ref[...]
    o_ref[...] = x2


def _combine(pos, x1, rf, ys, final_g, final_norm):
    n = x1.shape[0]
    tm = _pick_tile(n, (256, 128, 64, 8))
    return pl.pallas_call(
        functools.partial(_combine_kernel, final_norm=final_norm),
        grid_spec=pltpu.PrefetchScalarGridSpec(
            num_scalar_prefetch=1,
            grid=(n // tm,),
            in_specs=[
                pl.BlockSpec((tm, D_MODEL), lambda i, p: (i, 0)),
                pl.BlockSpec((tm, LANES), lambda i, p: (i, 0)),
                pl.BlockSpec((1, D_MODEL), lambda i, p: (0, 0)),
                pl.BlockSpec(memory_space=pl.ANY),
            ],
            out_specs=pl.BlockSpec((tm, D_MODEL), lambda i, p: (i, 0)),
            scratch_shapes=[pltpu.VMEM((tm, D_MODEL), F32), pltpu.VMEM((tm, D_MODEL), F32),
                            pltpu.SemaphoreType.DMA((2,))],
        ),
        out_shape=jax.ShapeDtypeStruct((n, D_MODEL), F32),
        compiler_params=_cparams(("arbitrary",)),
        name="moe_combine",
    )(pos, x1, rf, final_g.reshape(1, D_MODEL), ys)


def _pad_w_in(w_in):
    parts = []
    at = 0
    for lo, hi, dst in sorted(_W_IN_SEGMENTS, key=lambda s: s[2]):
        if dst > at:
            parts.append(jnp.zeros((D_MODEL, dst - at), BF16))
        parts.append(w_in[:, lo:hi].astype(BF16))
        at = dst + (hi - lo)
    if at < N_PROJ:
        parts.append(jnp.zeros((D_MODEL, N_PROJ - at), BF16))
    return jnp.concatenate(parts, axis=1)


def _moe(x1, hn, ri, rf, counts, prm, final_g, final_norm, tile):
    n = x1.shape[0]
    n_tiles = -(-2 * n // tile) + N_EXPERTS
    cnt = counts[0, N_GROUPS:N_GROUPS + N_EXPERTS].astype(I32)
    tiles_per = (cnt + tile - 1) // tile
    tile_end = jnp.cumsum(tiles_per)
    row_off = (tile_end - tiles_per) * tile
    pos = (row_off[ri[:, 0:2]] + ri[:, 2:4]).reshape(2 * n)
    tile_ids = jnp.arange(n_tiles, dtype=I32)
    tile_expert = jnp.minimum(jnp.sum((tile_ids[:, None] >= tile_end[None, :]).astype(I32), axis=1), N_EXPERTS - 1)
    n_used = tile_end[N_EXPERTS - 1:N_EXPERTS].astype(I32)
    last_tile = jnp.where(tiles_per > 0, tile_end - 1, -1).astype(I32)
    zero_tiles = jnp.concatenate([last_tile, jnp.where(tile_ids >= n_used[0], tile_ids, -1)])
    xs = _dispatch(pos, zero_tiles, hn, n_tiles * tile, tile)
    ys = _expert_mlp(tile_expert, n_used, xs, prm['expert_w_gate'], prm['expert_w_up'],
                     prm['expert_w_down'], tile)
    return _combine(pos, x1, rf, ys, final_g, final_norm)


def _layer(x, t_len, states, layer, prm, final_g, final_norm, moe_tile):
    proj = _in_projection(x, prm['norm_mix_g'], prm['w_in_pad'])
    mixed, new_s = _sample_mixers(proj[t_len:], t_len, states, layer, prm)
    mixed, cn, m_p = _mlstm_prompt(proj, mixed, t_len, prm['gate_bias'], prm['mlstm_norm_g'])
    mixed, conv_tail = _conv_prompt(proj, mixed, t_len, prm['conv_w'], prm['conv_b'], prm['conv_norm_g'],
                                    prm['conv_norm_b'])
    mixed, sp = _gla_prompt(proj, mixed, t_len, prm['gla_w_gate_pad'], prm['gla_b_gate'], prm['gla_norm_g'])
    mixed = _swa_prompt(proj, mixed, t_len, prm['swa_sinks'])
    x1, hn, ri, rf, counts = _outproj_router(x, mixed, prm['w_out'], prm['norm_ffn_g'], prm['router_w'], prm['router_b'])
    x2 = _moe(x1, hn, ri, rf, counts, prm, final_g, final_norm, moe_tile)
    p_c = cn[None, :, :, :HEAD_W]
    p_n = cn[None, :, :, HEAD_W]
    p_m = m_p[None, :MLSTM_HEADS, 0]
    p_s = jnp.stack([sp[0, :GLA_DK, :HEAD_W], sp[0, GLA_DK:, HEAD_W:],
                     sp[1, :GLA_DK, :HEAD_W], sp[1, GLA_DK:, HEAD_W:]])[None]
    p_conv = conv_tail[None, CONV_HALO - (CONV_WIDTH - 1):]
    p_k = proj[t_len - WINDOW:t_len, S_K:S_K + LANES].reshape(1, WINDOW, SWA_KV_HEADS, SWA_HEAD_DIM)
    p_v = proj[t_len - WINDOW:t_len, S_V:S_V + LANES].reshape(1, WINDOW, SWA_KV_HEADS, SWA_HEAD_DIM)
    return x2, (p_c, p_n, p_m, p_s, p_conv, p_k, p_v), new_s


def _forward(x_prompt, x_sample, states, layer_params, final_norm_g, moe_tile=MOE_TILE):
    t_len = x_prompt.shape[1]
    x = jnp.concatenate([x_prompt[0], x_sample[:, 0]], axis=0)
    new_p, new_s = [], []
    depth = len(layer_params)
    for l, prm in enumerate(layer_params):
        x, sp, ss = _layer(x, t_len, states, l, prm, final_norm_g, l == depth - 1, moe_tile)
        new_p.append(sp)
        new_s.append(ss)
    y_prompt = x[None, :t_len]
    y_sample = x[t_len:, None]
    p_states = [jnp.stack(parts) for parts in zip(*new_p)]
    s_states = [jnp.stack(parts) for parts in zip(*new_s)]
    return (y_prompt, y_sample, *p_states, *s_states)


def _prep_layer_params(l, norm_mix_g, w_in, mlstm_b_i, mlstm_b_f, mlstm_norm_g, conv_w, conv_b, conv_norm_g,
                       conv_norm_b, gla_w_gate, gla_b_gate, gla_norm_g, swa_sinks, w_out, norm_ffn_g,
                       router_group_w, router_group_b, router_expert_w, router_expert_b, expert_w_gate,
                       expert_w_up, expert_w_down):
    gate_bias = jnp.concatenate([mlstm_b_i[l], mlstm_b_f[l], jnp.zeros((LANES - 2 * MLSTM_HEADS,), F32)])
    rw = jnp.concatenate([router_group_w[l], router_expert_w[l],
                          jnp.zeros((D_MODEL, LANES - N_GROUPS - N_EXPERTS), F32)], axis=1)
    rb = jnp.concatenate([router_group_b[l], router_expert_b[l],
                          jnp.zeros((LANES - N_GROUPS - N_EXPERTS,), F32)])
    return {
        'norm_mix_g': norm_mix_g[l],
        'w_in_pad': _pad_w_in(w_in[l]),
        'gate_bias': gate_bias.reshape(1, LANES),
        'mlstm_norm_g': mlstm_norm_g[l],
        'conv_w': conv_w[l], 'conv_b': conv_b[l], 'conv_norm_g': conv_norm_g[l], 'conv_norm_b': conv_norm_b[l],
        'gla_w_gate_pad': jnp.concatenate(
            [gla_w_gate[l], jnp.zeros((LANES - GLA_LOWRANK, GLA_HEADS * GLA_DK), F32)], axis=0).astype(BF16),
        'gla_b_gate': gla_b_gate[l], 'gla_norm_g': gla_norm_g[l],
        'swa_sinks': swa_sinks[l],
        'w_out': w_out[l].astype(BF16),
        'norm_ffn_g': norm_ffn_g[l],
        'router_w': rw.astype(BF16), 'router_b': rb.reshape(1, LANES),
        'expert_w_gate': _cast_bf16(expert_w_gate, l),
        'expert_w_up': _cast_bf16(expert_w_up, l),
        'expert_w_down': _cast_bf16(expert_w_down, l),
    }


def kernel(x_prompt, x_sample, state_mlstm_C, state_mlstm_n, state_mlstm_m, state_gla_S, cache_conv, cache_swa_k, cache_swa_v, norm_mix_g, w_in, mlstm_b_i, mlstm_b_f, mlstm_norm_g, conv_w, conv_b, conv_norm_g, conv_norm_b, gla_w_gate, gla_b_gate, gla_norm_g, swa_sinks, w_out, norm_ffn_g, router_group_w, router_group_b, router_expert_w, router_expert_b, expert_w_gate, expert_w_up, expert_w_down, final_norm_g):
    depth = w_in.shape[0]
    weights = (norm_mix_g, w_in, mlstm_b_i, mlstm_b_f, mlstm_norm_g, conv_w, conv_b, conv_norm_g, conv_norm_b,
               gla_w_gate, gla_b_gate, gla_norm_g, swa_sinks, w_out, norm_ffn_g, router_group_w, router_group_b,
               router_expert_w, router_expert_b, expert_w_gate, expert_w_up, expert_w_down)
    layer_params = [_prep_layer_params(l, *weights) for l in range(depth)]
    states = (state_mlstm_C, state_mlstm_n, state_mlstm_m, state_gla_S, cache_conv, cache_swa_k, cache_swa_v)
    return _forward(x_prompt, x_sample, states, layer_params, final_norm_g)
```

```python
import functools
from typing import NamedTuple

import jax
import jax.numpy as jnp
from jax import lax
from jax.experimental import pallas as pl
from jax.experimental.pallas import tpu as pltpu

F32 = jnp.float32
BF16 = jnp.bfloat16
I32 = jnp.int32
HIGHEST = lax.Precision.HIGHEST

D_MODEL = 2048
GROUP_WIDTH = 512
HEAD_W = 128
MLSTM_HEADS = 4
GLA_HEADS = 4
GLA_DK = 64
GLA_LOWRANK = 16
GLA_TAU = 16.0
CONV_WIDTH = 31
SWA_HEADS = 8
SWA_KV_HEADS = 2
SWA_HEAD_DIM = 64
WINDOW = 128
N_GROUPS = 4
EXPERTS_PER_GROUP = 4
N_EXPERTS = 16
D_EXPERT = 1024
NORM_EPS = 1e-6
LANES = 128
SUBLANES = 8
TOK_ROWS = D_MODEL // LANES
PACK_ROWS = TOK_ROWS // 2

A_Q, A_K, A_V, A_O = 0, 512, 1024, 1536
C_A, C_G = 2048, 2560
G_V, G_R = 3072, 3584
S_Q = 4096
G_Q, G_K = 4608, 4864
A_G, G_LR, S_K, S_V = 5120, 5248, 5376, 5504
N_PROJ = 5632
MIX_COL_MLSTM, MIX_COL_CONV, MIX_COL_GLA, MIX_COL_SWA = 0, 1, 2, 3
_W_IN_SEGMENTS = (
    (0, 512, A_Q), (512, 1024, A_K), (1024, 1536, A_V), (1536, 2048, A_O),
    (2048, 2056, A_G),
    (2056, 2568, C_A), (2568, 3080, C_G),
    (3080, 3336, G_Q), (3336, 3592, G_K), (3592, 4104, G_V), (4104, 4616, G_R),
    (4616, 4632, G_LR),
    (4632, 5144, S_Q), (5144, 5272, S_K), (5272, 5400, S_V),
)

MLSTM_CHUNK = 128
MLSTM_STEP_ROWS = 256
GLA_CHUNK = 64
GLA_STEP_ROWS = 256
GLA_SUB = 16
SWA_STEP_ROWS = 256
CONV_TILE = 256
CONV_ROWS = 64
CONV_HALO = 32
SAMPLE_BB = 8
MOE_TILE = 512
VMEM_LIMIT = 56 * 1024 * 1024


def _cparams(sem, vmem=VMEM_LIMIT):
    return pltpu.CompilerParams(dimension_semantics=sem, vmem_limit_bytes=vmem)


def _log_sigmoid(x):
    return jnp.minimum(x, 0.0) - jnp.log1p(jnp.exp(-jnp.abs(x)))


def _sigmoid(x):
    return 1.0 / (1.0 + jnp.exp(-x))


def _silu(x):
    return x * _sigmoid(x)


def _masked_row_sums(mask, x):
    m = jnp.where(mask, 1.0, 0.0).astype(BF16)
    hi = x.astype(BF16)
    r1 = x - hi.astype(F32)
    mid = r1.astype(BF16)
    lo = (r1 - mid.astype(F32)).astype(BF16)
    out = jnp.dot(m, hi, preferred_element_type=F32)
    out = out + jnp.dot(m, mid, preferred_element_type=F32)
    return out + jnp.dot(m, lo, preferred_element_type=F32)


def _pick_tile(n, candidates):
    for c in candidates:
        if n % c == 0:
            return c
    raise ValueError(f"no tile for {n} in {candidates}")


class _SideCast(NamedTuple):
    operand: jax.Array
    in_spec: pl.BlockSpec
    out_spec: pl.BlockSpec
    out_shape: jax.ShapeDtypeStruct
    shape: tuple


def _side_cast(w_all, layer, n_steps):
    depth, n_e, k, f = w_all.shape
    rows = n_e * k
    assert rows % n_steps == 0, (rows, n_steps)
    tr = rows // n_steps
    return _SideCast(
        operand=w_all.reshape(depth * rows, f),
        in_spec=pl.BlockSpec((tr, f), lambda i: (layer * n_steps + i, 0)),
        out_spec=pl.BlockSpec((tr, f), lambda i: (i, 0)),
        out_shape=jax.ShapeDtypeStruct((rows, f), BF16),
        shape=(n_e, k, f),
    )


def _proj_kernel(x_ref, g_ref, w_ref, o_ref, hn_ref):
    @pl.when(pl.program_id(1) == 0)
    def _():
        x = x_ref[...]
        ms = jnp.mean(x * x, axis=-1, keepdims=True)
        hn_ref[...] = (x * lax.rsqrt(ms + NORM_EPS) * g_ref[...]).astype(BF16)

    o_ref[...] = jnp.dot(hn_ref[...], w_ref[...], preferred_element_type=F32)


def _in_projection(x, g, w_bf16):
    n = x.shape[0]
    tm = _pick_tile(n, (832, 640, 512, 256, 128, 64, 8))
    tn = N_PROJ // 4
    return pl.pallas_call(
        _proj_kernel,
        grid=(n // tm, N_PROJ // tn),
        in_specs=[
            pl.BlockSpec((tm, D_MODEL), lambda i, j: (i, 0)),
            pl.BlockSpec((1, D_MODEL), lambda i, j: (0, 0)),
            pl.BlockSpec((D_MODEL, tn), lambda i, j: (0, j)),
        ],
        out_specs=pl.BlockSpec((tm, tn), lambda i, j: (i, j)),
        out_shape=jax.ShapeDtypeStruct((n, N_PROJ), F32),
        scratch_shapes=[pltpu.VMEM((tm, D_MODEL), BF16)],
        compiler_params=_cparams(("parallel", "arbitrary")),
        name="in_projection",
    )(x, g.reshape(1, D_MODEL), w_bf16)


def _mlstm_prompt_kernel(q_ref, k_ref, v_ref, o_ref, gt_ref, bias_ref, ng_ref, mixed_in_ref, wc_in_ref,
                         y_ref, cn_ref, m_ref, wc_out_ref):
    del mixed_in_ref
    wc_out_ref[...] = wc_in_ref[...].astype(wc_out_ref.dtype)
    rows_step = q_ref.shape[0]
    L = min(MLSTM_CHUNK, rows_step)
    n_chunks = rows_step // L

    @pl.when(pl.program_id(0) == 0)
    def _():
        cn_ref[...] = jnp.zeros_like(cn_ref)
        m_ref[...] = jnp.zeros_like(m_ref)

    pre = gt_ref[...] + bias_ref[...]
    lf = _log_sigmoid(pre)
    row = lax.broadcasted_iota(I32, (rows_step, rows_step), 0)
    col = lax.broadcasted_iota(I32, (rows_step, rows_step), 1)
    cum = ((row // L) == (col // L)) & (col <= row)
    b_all = _masked_row_sums(cum, lf)
    pre_t = pre.T
    b_t = b_all.T
    trow = lax.broadcasted_iota(I32, (L, L), 0)
    tcol = lax.broadcasted_iota(I32, (L, L), 1)
    tri = tcol <= trow
    lane = lax.broadcasted_iota(I32, (L, HEAD_W), 1)
    ones_col = (lane == 0).astype(BF16)
    for h in range(MLSTM_HEADS):
        sl = slice(HEAD_W * h, HEAD_W * (h + 1))
        m_prev = m_ref[h:h + 1, 0:1]
        cn = cn_ref[h]
        for c in range(n_chunks):
            cs = slice(L * c, L * (c + 1))
            q = q_ref[cs, sl]
            k = k_ref[cs, sl] * (HEAD_W ** -0.5)
            v = v_ref[cs, sl]
            b_col = b_all[cs, 4 + h:5 + h]
            i_col = pre[cs, h:h + 1]
            b_row = b_t[4 + h:5 + h, cs]
            i_row = pre_t[h:h + 1, cs]
            log_d = jnp.where(tri, b_col - b_row + i_row, -jnp.inf)
            log_inter = b_col + m_prev
            m_t = jnp.maximum(log_inter, jnp.max(log_d, axis=-1, keepdims=True))
            d_mat = jnp.exp(log_d - m_t)
            g_inter = jnp.exp(log_inter - m_t)
            qb = q.astype(BF16)
            kb = k.astype(BF16)
            s = lax.dot_general(qb, kb, (((1,), (1,)), ((), ())), preferred_element_type=F32)
            w = (s * d_mat).astype(BF16)
            v1 = jnp.concatenate([v.astype(BF16), ones_col], axis=1)
            nd = g_inter * jnp.dot(qb, cn.astype(BF16), preferred_element_type=F32)
            nd = nd + jnp.dot(w, v1, preferred_element_type=F32)
            num = nd[:, :HEAD_W]
            den = nd[:, HEAD_W:HEAD_W + 1]
            hh = num / jnp.maximum(jnp.abs(den), jnp.exp(-m_t))
            hh = _sigmoid(o_ref[cs, sl]) * hh
            hh = hh * lax.rsqrt(jnp.mean(hh * hh, axis=-1, keepdims=True) + NORM_EPS) * ng_ref[:, sl]
            y_ref[cs, sl] = hh.astype(y_ref.dtype)
            m_last = m_t[L - 1:L, :]
            b_last = b_col[L - 1:L, :]
            g_state = jnp.exp(b_last + m_prev - m_last)
            w_k = jnp.exp(b_last - b_col + i_col - m_last)
            kw = (k * w_k).astype(BF16)
            upd = lax.dot_general(kw, v1, (((0,), (0,)), ((), ())), preferred_element_type=F32)
            cn = g_state * cn + upd
            m_prev = m_last
        cn_ref[h] = cn
        m_ref[h:h + 1, :] = jnp.broadcast_to(m_prev, (1, LANES))


def _mlstm_prompt(proj, mixed, t_len, bias_row, norm_g, w_cast, layer):
    L = _pick_tile(t_len, (MLSTM_STEP_ROWS, MLSTM_CHUNK, 64, 32, 16, 8))
    side = _side_cast(w_cast, layer, t_len // L)

    def col(off):
        return pl.BlockSpec((L, GROUP_WIDTH), lambda i, o=off: (i, o // GROUP_WIDTH))

    return pl.pallas_call(
        _mlstm_prompt_kernel,
        grid=(t_len // L,),
        in_specs=[
            col(A_Q), col(A_K), col(A_V), col(A_O),
            pl.BlockSpec((L, LANES), lambda i: (i, A_G // LANES)),
            pl.BlockSpec((1, LANES), lambda i: (0, 0)),
            pl.BlockSpec((1, GROUP_WIDTH), lambda i: (0, 0)),
            pl.BlockSpec(memory_space=pl.ANY),
            side.in_spec,
        ],
        out_specs=[
            pl.BlockSpec((L, GROUP_WIDTH), lambda i: (i, MIX_COL_MLSTM)),
            pl.BlockSpec((MLSTM_HEADS, HEAD_W, 2 * HEAD_W), lambda i: (0, 0, 0)),
            pl.BlockSpec((SUBLANES, LANES), lambda i: (0, 0)),
            side.out_spec,
        ],
        out_shape=[
            jax.ShapeDtypeStruct(mixed.shape, mixed.dtype),
            jax.ShapeDtypeStruct((MLSTM_HEADS, HEAD_W, 2 * HEAD_W), F32),
            jax.ShapeDtypeStruct((SUBLANES, LANES), F32),
            side.out_shape,
        ],
        input_output_aliases={7: 0},
        compiler_params=_cparams(("arbitrary",)),
        name="mlstm_prompt",
    )(proj, proj, proj, proj, proj, bias_row, norm_g.reshape(1, GROUP_WIDTH), mixed, side.operand)


def _gla_prompt_kernel(q_ref, k_ref, v_ref, r_ref, lr_ref, wg_ref, bg_ref, ng_ref, mixed_in_ref, wc_in_ref,
                       y_ref, sp_ref, wc_out_ref):
    del mixed_in_ref
    wc_out_ref[...] = wc_in_ref[...].astype(wc_out_ref.dtype)
    rows_step = q_ref.shape[0]
    L = min(GLA_CHUNK, rows_step)
    n_chunks = rows_step // L
    n_sub = L // GLA_SUB

    @pl.when(pl.program_id(0) == 0)
    def _():
        sp_ref[...] = jnp.zeros_like(sp_ref)

    gate_pre = jnp.dot(lr_ref[...].astype(BF16), wg_ref[...], preferred_element_type=F32) + bg_ref[...]
    log_a = _log_sigmoid(gate_pre) * (1.0 / GLA_TAU)
    row = lax.broadcasted_iota(I32, (rows_step, rows_step), 0)
    col = lax.broadcasted_iota(I32, (rows_step, rows_step), 1)
    tri = ((row // L) == (col // L)) & (col <= row)
    b = _masked_row_sums(tri, log_a)
    q = q_ref[...] * (GLA_DK ** -0.5)
    k = k_ref[...]
    b_last = jnp.concatenate(
        [jnp.broadcast_to(b[L * c + L - 1:L * c + L, :], (L, b.shape[1])) for c in range(n_chunks)], axis=0)
    q_in = q * jnp.exp(b)
    k_dec = k * jnp.exp(b_last - b)
    lane16 = lax.broadcasted_iota(I32, (GLA_SUB, LANES), 1)
    lo16 = lane16 < GLA_DK
    srow = lax.broadcasted_iota(I32, (LANES, 2 * HEAD_W), 0)
    scol = lax.broadcasted_iota(I32, (LANES, 2 * HEAD_W), 1)
    block_diag = (srow < GLA_DK) == (scol < HEAD_W)
    for p in range(2):
        pls = slice(LANES * p, LANES * (p + 1))
        vp = v_ref[:, 2 * HEAD_W * p:2 * HEAD_W * (p + 1)].astype(BF16)
        b_t = b[:, pls].T
        sp = sp_ref[p]
        states = []
        for c in range(n_chunks):
            cs = slice(L * c, L * (c + 1))
            states.append(sp)
            dec_col = jnp.exp(b_t[:, L * c + L - 1:L * c + L])
            upd = lax.dot_general(k_dec[cs, pls].astype(BF16), vp[cs], (((0,), (0,)), ((), ())),
                                  preferred_element_type=F32)
            sp = jnp.where(block_diag, dec_col * sp + upd, 0.0)
        sp_ref[p] = sp
        outs = []
        for c in range(n_chunks):
            c0 = L * c
            o_inter = jnp.dot(q_in[c0:c0 + L, pls].astype(BF16), states[c].astype(BF16),
                              preferred_element_type=F32)
            rows = []
            for blk in range(n_sub):
                r0 = GLA_SUB * blk
                n = GLA_SUB * (blk + 1)
                qrows = slice(c0 + r0, c0 + r0 + GLA_SUB)
                krows = slice(c0, c0 + n)
                if blk == 0:
                    qs = q[qrows, pls] * jnp.exp(b[qrows, pls])
                    ks = k[krows, pls] * jnp.exp(-b[krows, pls])
                else:
                    anchor = b[c0 + r0 - 1:c0 + r0, pls]
                    qs = q[qrows, pls] * jnp.exp(b[qrows, pls] - anchor)
                    ks = k[krows, pls] * jnp.exp(anchor - b[krows, pls])
                qs2 = jnp.concatenate([jnp.where(lo16, qs, 0.0), jnp.where(lo16, 0.0, qs)], axis=0)
                att = lax.dot_general(qs2.astype(BF16), ks.astype(BF16), (((1,), (1,)), ((), ())),
                                      preferred_element_type=F32)
                trow = lax.broadcasted_iota(I32, (2 * GLA_SUB, n), 0)
                tcol = lax.broadcasted_iota(I32, (2 * GLA_SUB, n), 1)
                t_idx = r0 + jnp.where(trow >= GLA_SUB, trow - GLA_SUB, trow)
                att = jnp.where(tcol <= t_idx, att, 0.0)
                o2 = jnp.dot(att.astype(BF16), vp[krows], preferred_element_type=F32)
                rows.append(jnp.concatenate([o2[:GLA_SUB, :HEAD_W], o2[GLA_SUB:, HEAD_W:]], axis=1))
            outs.append(o_inter + jnp.concatenate(rows, axis=0))
        o = jnp.concatenate(outs, axis=0)
        for hh in range(2):
            head = 2 * p + hh
            hs = slice(HEAD_W * head, HEAD_W * (head + 1))
            oh = o[:, HEAD_W * hh:HEAD_W * (hh + 1)]
            oh = oh * lax.rsqrt(jnp.mean(oh * oh, axis=-1, keepdims=True) + NORM_EPS) * ng_ref[:, hs]
            y_ref[:, hs] = (oh * _silu(r_ref[:, hs])).astype(y_ref.dtype)


def _gla_prompt(proj, mixed, t_len, w_gate_pad, b_gate, norm_g, w_cast, layer):
    L = _pick_tile(t_len, (GLA_STEP_ROWS, GLA_CHUNK))
    side = _side_cast(w_cast, layer, t_len // L)
    return pl.pallas_call(
        _gla_prompt_kernel,
        grid=(t_len // L,),
        in_specs=[
            pl.BlockSpec((L, 256), lambda i: (i, G_Q // 256)),
            pl.BlockSpec((L, 256), lambda i: (i, G_K // 256)),
            pl.BlockSpec((L, GROUP_WIDTH), lambda i: (i, G_V // GROUP_WIDTH)),
            pl.BlockSpec((L, GROUP_WIDTH), lambda i: (i, G_R // GROUP_WIDTH)),
            pl.BlockSpec((L, LANES), lambda i: (i, G_LR // LANES)),
            pl.BlockSpec((LANES, 256), lambda i: (0, 0)),
            pl.BlockSpec((1, 256), lambda i: (0, 0)),
            pl.BlockSpec((1, GROUP_WIDTH), lambda i: (0, 0)),
            pl.BlockSpec(memory_space=pl.ANY),
            side.in_spec,
        ],
        out_specs=[
            pl.BlockSpec((L, GROUP_WIDTH), lambda i: (i, MIX_COL_GLA)),
            pl.BlockSpec((2, LANES, 2 * HEAD_W), lambda i: (0, 0, 0)),
            side.out_spec,
        ],
        out_shape=[
            jax.ShapeDtypeStruct(mixed.shape, mixed.dtype),
            jax.ShapeDtypeStruct((2, LANES, 2 * HEAD_W), F32),
            side.out_shape,
        ],
        input_output_aliases={8: 0},
        compiler_params=_cparams(("arbitrary",)),
        name="gla_prompt",
    )(proj, proj, proj, proj, proj, w_gate_pad, b_gate.reshape(1, 256), norm_g.reshape(1, GROUP_WIDTH), mixed,
      side.operand)


def _conv_norm_act(y, g_ref, be_ref):
    mu = jnp.mean(y, axis=-1, keepdims=True)
    yc = y - mu
    var = jnp.mean(yc * yc, axis=-1, keepdims=True)
    return _silu(yc * lax.rsqrt(var + NORM_EPS) * g_ref[...] + be_ref[...])


def _conv_prompt_kernel(ua_ref, ug_ref, ha_ref, hg_ref, w_ref, b_ref, g_ref, be_ref, mixed_in_ref, wc_in_ref,
                        y_ref, tail_ref, wc_out_ref, buf_ref, sh_ref):
    del mixed_in_ref
    wc_out_ref[...] = wc_in_ref[...].astype(wc_out_ref.dtype)
    tt = ua_ref.shape[0]
    span = tt + CONV_HALO
    halo = ha_ref[...] * _sigmoid(hg_ref[...])
    buf_ref[0:CONV_HALO, :] = jnp.where(pl.program_id(0) > 0, halo, 0.0)
    buf_ref[CONV_HALO:span, :] = ua_ref[...] * _sigmoid(ug_ref[...])
    buf_ref[span:span + SUBLANES, :] = jnp.zeros((SUBLANES, GROUP_WIDTH), F32)
    for k in range(1, SUBLANES):
        sh_ref[k] = buf_ref[k:k + span, :]
    base = CONV_HALO - (CONV_WIDTH - 1)
    for r in range(tt // CONV_ROWS):
        acc = jnp.zeros((CONV_ROWS, GROUP_WIDTH), F32)
        for j in range(CONV_WIDTH):
            s0 = r * CONV_ROWS + base + j
            k = s0 % SUBLANES
            a0 = s0 - k
            win = buf_ref[a0:a0 + CONV_ROWS, :] if k == 0 else sh_ref[k, a0:a0 + CONV_ROWS, :]
            acc = acc + w_ref[j:j + 1, :] * win
        y = _conv_norm_act(acc + b_ref[...], g_ref, be_ref)
        y_ref[r * CONV_ROWS:(r + 1) * CONV_ROWS, :] = y.astype(y_ref.dtype)
    tail_ref[...] = buf_ref[tt:span, :]


def _conv_prompt(proj, mixed, t_len, w, b, g, beta, w_cast, layer):
    tt = _pick_tile(t_len, (CONV_TILE, 128, 64))
    ratio = tt // CONV_HALO
    side = _side_cast(w_cast, layer, t_len // tt)
    vec = lambda: pl.BlockSpec((1, GROUP_WIDTH), lambda i: (0, 0))
    return pl.pallas_call(
        _conv_prompt_kernel,
        grid=(t_len // tt,),
        in_specs=[
            pl.BlockSpec((tt, GROUP_WIDTH), lambda i: (i, C_A // GROUP_WIDTH)),
            pl.BlockSpec((tt, GROUP_WIDTH), lambda i: (i, C_G // GROUP_WIDTH)),
            pl.BlockSpec((CONV_HALO, GROUP_WIDTH), lambda i: (jnp.maximum(i * ratio - 1, 0), C_A // GROUP_WIDTH)),
            pl.BlockSpec((CONV_HALO, GROUP_WIDTH), lambda i: (jnp.maximum(i * ratio - 1, 0), C_G // GROUP_WIDTH)),
            pl.BlockSpec((CONV_WIDTH, GROUP_WIDTH), lambda i: (0, 0)),
            vec(), vec(), vec(),
            pl.BlockSpec(memory_space=pl.ANY),
            side.in_spec,
        ],
        out_specs=[
            pl.BlockSpec((tt, GROUP_WIDTH), lambda i: (i, MIX_COL_CONV)),
            pl.BlockSpec((CONV_HALO, GROUP_WIDTH), lambda i: (0, 0)),
            side.out_spec,
        ],
        out_shape=[
            jax.ShapeDtypeStruct(mixed.shape, mixed.dtype),
            jax.ShapeDtypeStruct((CONV_HALO, GROUP_WIDTH), F32),
            side.out_shape,
        ],
        scratch_shapes=[pltpu.VMEM((tt + CONV_HALO + SUBLANES, GROUP_WIDTH), F32),
                        pltpu.VMEM((SUBLANES, tt + CONV_HALO, GROUP_WIDTH), F32)],
        input_output_aliases={8: 0},
        compiler_params=_cparams(("arbitrary",)),
        name="conv_prompt",
    )(proj, proj, proj, proj, w, b.reshape(1, -1), g.reshape(1, -1), beta.reshape(1, -1), mixed, side.operand)


def _swa_prompt_kernel(sink_ref, q_ref, kc_ref, vc_ref, kp_ref, vp_ref, mixed_in_ref, y_ref):
    del mixed_in_ref
    bq = WINDOW
    n_blk = q_ref.shape[0] // bq
    first = pl.program_id(0) == 0
    k_full = jnp.concatenate([kp_ref[...], kc_ref[...]], axis=0)
    v_full = jnp.concatenate([vp_ref[...], vc_ref[...]], axis=0)
    k_sw_full = pltpu.roll(k_full, SWA_HEAD_DIM, 1).astype(BF16)
    v_sw_full = pltpu.roll(v_full, SWA_HEAD_DIM, 1).astype(BF16)
    k_full = k_full.astype(BF16)
    v_full = v_full.astype(BF16)
    tq = lax.broadcasted_iota(I32, (bq, 2 * bq), 0)
    kj = lax.broadcasted_iota(I32, (bq, 2 * bq), 1)
    band = (kj > tq) & (kj <= tq + WINDOW)
    lane = lax.broadcasted_iota(I32, (bq, LANES), 1)
    lo = lane < SWA_HEAD_DIM
    rep = SWA_HEADS // SWA_KV_HEADS
    for blk in range(n_blk):
        qs = slice(bq * blk, bq * (blk + 1))
        ks = slice(bq * blk, bq * (blk + 2))
        valid = band & (kj >= jnp.where(first, bq, 0)) if blk == 0 else band
        k_all, v_all, k_sw, v_sw = k_full[ks], v_full[ks], k_sw_full[ks], v_sw_full[ks]
        for c in range(SWA_HEADS // 2):
            qc = q_ref[qs, LANES * c:LANES * (c + 1)] * (SWA_HEAD_DIM ** -0.5)
            outs = []
            for hh in range(2):
                h = 2 * c + hh
                g = h // rep
                qm = jnp.where(lo if hh == 0 else jnp.logical_not(lo), qc, 0.0).astype(BF16)
                k_use = k_all if g == hh else k_sw
                v_use = v_all if g == hh else v_sw
                s = lax.dot_general(qm, k_use, (((1,), (1,)), ((), ())), preferred_element_type=F32)
                s = jnp.where(valid, s, -jnp.inf)
                sink = sink_ref[h]
                mx = jnp.maximum(jnp.max(s, axis=-1, keepdims=True), sink)
                p = jnp.exp(s - mx)
                den = jnp.sum(p, axis=-1, keepdims=True) + jnp.exp(sink - mx)
                p = (p / den).astype(BF16)
                outs.append(jnp.dot(p, v_use, preferred_element_type=F32))
            y_ref[qs, LANES * c:LANES * (c + 1)] = jnp.where(lo, outs[0], outs[1]).astype(y_ref.dtype)


def _swa_prompt(proj, mixed, t_len, sinks):
    bq = WINDOW
    rows = _pick_tile(t_len, (SWA_STEP_ROWS, bq))
    ratio = rows // bq
    cur = lambda off: pl.BlockSpec((rows, LANES), lambda i: (i, off // LANES))
    prev = lambda off: pl.BlockSpec((bq, LANES), lambda i: (jnp.maximum(i * ratio - 1, 0), off // LANES))
    return pl.pallas_call(
        _swa_prompt_kernel,
        grid=(t_len // rows,),
        in_specs=[
            pl.BlockSpec(memory_space=pltpu.SMEM),
            pl.BlockSpec((rows, GROUP_WIDTH), lambda i: (i, S_Q // GROUP_WIDTH)),
            cur(S_K), cur(S_V), prev(S_K), prev(S_V),
            pl.BlockSpec(memory_space=pl.ANY),
        ],
        out_specs=pl.BlockSpec((rows, GROUP_WIDTH), lambda i: (i, MIX_COL_SWA)),
        out_shape=jax.ShapeDtypeStruct(mixed.shape, mixed.dtype),
        input_output_aliases={6: 0},
        compiler_params=_cparams(("parallel",)),
        name="swa_prompt",
    )(sinks, proj, proj, proj, proj, proj, mixed)


_T_MK, _T_MQ = 0, 512
_T_GA, _T_GK, _T_GQ = 1024, 1280, 1536
_T_ROWS = 1792


def _sample_kernel(ps_ref, mm_ref, bias_ref, n0_ref, c0_ref, s0_ref, cv0_ref, k0_ref, v0_ref,
                   qm_ref, sink_ref, wg_ref, bg_ref, mng_ref, gng_ref, cw_ref, cb_ref, cg_ref, cbe_ref,
                   mix_ref, c1_ref, n1_ref, m1_ref, s1_ref, cv1_ref, k1_ref, v1_ref,
                   tt_ref, bc_ref, num_ref, go_ref, yc_ref, yd_ref, ybuf, ybf, zbuf, zsem, ysem):
    i = pl.program_id(0)
    nb = ps_ref.shape[0]
    bb = c0_ref.shape[0]
    t_len = mix_ref.shape[0] - nb
    zr = zbuf.shape[0]

    def zero_copy(t):
        return pltpu.make_async_copy(zbuf, mix_ref.at[pl.ds(t * zr, zr)], zsem.at[0])

    def gla_gate(lr):
        gp = jnp.dot(lr.astype(BF16), wg_ref[...], preferred_element_type=F32) + bg_ref[...]
        return jnp.exp(_log_sigmoid(gp) * (1.0 / GLA_TAU))

    @pl.when(i == 0)
    def _():
        zbuf[...] = jnp.zeros_like(zbuf)
        for t in range(t_len // zr):
            zero_copy(t).start()
        for h in range(MLSTM_HEADS):
            kk = ps_ref[:, A_K + HEAD_W * h:A_K + HEAD_W * (h + 1)] * (HEAD_W ** -0.5)
            tt_ref[_T_MK + HEAD_W * h:_T_MK + HEAD_W * (h + 1), :] = kk.T.astype(BF16)
            qq = ps_ref[:, A_Q + HEAD_W * h:A_Q + HEAD_W * (h + 1)]
            tt_ref[_T_MQ + HEAD_W * h:_T_MQ + HEAD_W * (h + 1), :] = qq.T.astype(BF16)
        a_all = gla_gate(ps_ref[:, G_LR:G_LR + LANES])
        for p in range(2):
            pls = slice(LANES * p, LANES * (p + 1))
            tt_ref[_T_GA + LANES * p:_T_GA + LANES * (p + 1), :] = a_all[:, pls].T.astype(BF16)
            kk = ps_ref[:, G_K + LANES * p:G_K + LANES * (p + 1)]
            tt_ref[_T_GK + LANES * p:_T_GK + LANES * (p + 1), :] = kk.T.astype(BF16)
            qq = ps_ref[:, G_Q + LANES * p:G_Q + LANES * (p + 1)] * (GLA_DK ** -0.5)
            tt_ref[_T_GQ + LANES * p:_T_GQ + LANES * (p + 1), :] = qq.T.astype(BF16)

    r0 = pl.multiple_of(i * bb, bb)
    rows = pl.ds(r0, bb)

    pre = ps_ref[rows, A_G:A_G + LANES] + bias_ref[...]
    lfm = _log_sigmoid(pre) + mm_ref[...]
    f_al = pltpu.roll(lfm, LANES - MLSTM_HEADS, 1)
    m_t = jnp.maximum(f_al, pre)
    g_st = jnp.exp(f_al - m_t)
    w_k = jnp.exp(pre - m_t)
    m1_ref[...] = m_t
    n_new = []
    for h in range(MLSTM_HEADS):
        kk = ps_ref[rows, A_K + HEAD_W * h:A_K + HEAD_W * (h + 1)] * (HEAD_W ** -0.5)
        nn = g_st[:, h:h + 1] * n0_ref[:, HEAD_W * h:HEAD_W * (h + 1)] + w_k[:, h:h + 1] * kk
        n1_ref[:, HEAD_W * h:HEAD_W * (h + 1)] = nn
        n_new.append(nn)

    glu = ps_ref[rows, C_A:C_A + GROUP_WIDTH] * _sigmoid(ps_ref[rows, C_G:C_G + GROUP_WIDTH])
    yc_ref[...] = glu * cw_ref[CONV_WIDTH - 1:CONV_WIDTH, :]
    a_v = ps_ref[rows, A_V:A_V + GROUP_WIDTH]
    g_v = ps_ref[rows, G_V:G_V + GROUP_WIDTH]
    s_k = ps_ref[rows, S_K:S_K + LANES]
    s_v = ps_ref[rows, S_V:S_V + LANES]

    lane_b = lax.broadcasted_iota(I32, (nb, LANES), 0)
    key_row = lax.broadcasted_iota(I32, (SWA_HEADS, WINDOW), 1)
    lo_row = lax.broadcasted_iota(I32, (1, LANES), 1) < SWA_HEAD_DIM
    sink_col = sink_ref[:, 0:1]

    for j in range(bb):
        onehot = (lane_b == r0 + j).astype(BF16)
        bc_ref[...] = jnp.dot(tt_ref[...], onehot, preferred_element_type=F32)
        jrow = slice(j, j + 1)
        for h in range(MLSTM_HEADS):
            hs = slice(HEAD_W * h, HEAD_W * (h + 1))
            kbc = bc_ref[_T_MK + HEAD_W * h:_T_MK + HEAD_W * (h + 1), :]
            qbc = bc_ref[_T_MQ + HEAD_W * h:_T_MQ + HEAD_W * (h + 1), :]
            g1 = g_st[jrow, h:h + 1]
            w1 = w_k[jrow, h:h + 1]
            v_row = a_v[jrow, hs]
            c_new = g1 * c0_ref[j, h] + kbc * (w1 * v_row)
            c1_ref[j, h] = c_new
            num_ref[jrow, hs] = jnp.sum(qbc * c_new, axis=0, keepdims=True)
        for p in range(2):
            abc = bc_ref[_T_GA + LANES * p:_T_GA + LANES * (p + 1), :]
            kbc = bc_ref[_T_GK + LANES * p:_T_GK + LANES * (p + 1), :]
            qbc = bc_ref[_T_GQ + LANES * p:_T_GQ + LANES * (p + 1), :]
            for hh in range(2):
                head = 2 * p + hh
                hs = slice(HEAD_W * head, HEAD_W * (head + 1))
                ds_ = slice(GLA_DK * hh, GLA_DK * (hh + 1))
                v_row = g_v[jrow, hs]
                s_new = abc[ds_, :] * s0_ref[j, head] + kbc[ds_, :] * v_row
                s1_ref[j, head] = s_new
                go_ref[jrow, hs] = jnp.sum(qbc[ds_, :] * s_new, axis=0, keepdims=True)
        cache = cv0_ref[j]
        yc_ref[jrow, :] = yc_ref[jrow, :] + jnp.sum(cache * cw_ref[0:CONV_WIDTH - 1, :], axis=0, keepdims=True)
        cv1_ref[j, 0:CONV_WIDTH - 2, :] = cv0_ref[j, 1:CONV_WIDTH - 1, :]
        cv1_ref[j, CONV_WIDTH - 2:CONV_WIDTH - 1, :] = glu[jrow, :]
        k_new = s_k[jrow, :]
        v_new = s_v[jrow, :]
        k1_ref[j, 0:WINDOW - 1, :] = k0_ref[j, 1:WINDOW, :]
        k1_ref[j, WINDOW - 1:WINDOW, :] = k_new
        v1_ref[j, 0:WINDOW - 1, :] = v0_ref[j, 1:WINDOW, :]
        v1_ref[j, WINDOW - 1:WINDOW, :] = v_new
        qmat = qm_ref[j] * (SWA_HEAD_DIM ** -0.5)
        s_old = lax.dot_general(qmat.astype(BF16), k0_ref[j].astype(BF16), (((1,), (1,)), ((), ())),
                                preferred_element_type=F32)
        s_old = jnp.where(key_row >= 1, s_old, -jnp.inf)
        s_cur = jnp.sum(qmat * k_new, axis=-1, keepdims=True)
        mx = jnp.maximum(jnp.maximum(jnp.max(s_old, axis=-1, keepdims=True), s_cur), sink_col)
        p_old = jnp.exp(s_old - mx)
        p_cur = jnp.exp(s_cur - mx)
        den = jnp.sum(p_old, axis=-1, keepdims=True) + p_cur + jnp.exp(sink_col - mx)
        o = jnp.dot((p_old / den).astype(BF16), v0_ref[j].astype(BF16), preferred_element_type=F32)
        o = o + (p_cur / den) * v_new
        o_sw = pltpu.roll(o, SWA_HEAD_DIM, 1)
        for c in range(SWA_HEADS // 2):
            g = c // (SWA_HEADS // SWA_KV_HEADS // 2)
            left = (o if g == 0 else o_sw)[2 * c:2 * c + 1, :]
            right = (o_sw if g == 0 else o)[2 * c + 1:2 * c + 2, :]
            yd_ref[jrow, LANES * c:LANES * (c + 1)] = jnp.where(lo_row, left, right)

    for h in range(MLSTM_HEADS):
        hs = slice(HEAD_W * h, HEAD_W * (h + 1))
        qq = ps_ref[rows, A_Q + HEAD_W * h:A_Q + HEAD_W * (h + 1)]
        den = jnp.sum(qq * n_new[h], axis=-1, keepdims=True)
        hh = num_ref[:, hs] / jnp.maximum(jnp.abs(den), jnp.exp(-m_t[:, h:h + 1]))
        hh = _sigmoid(ps_ref[rows, A_O + HEAD_W * h:A_O + HEAD_W * (h + 1)]) * hh
        hh = hh * lax.rsqrt(jnp.mean(hh * hh, axis=-1, keepdims=True) + NORM_EPS) * mng_ref[:, hs]
        ybuf[rows, hs] = hh
    ybuf[rows, GROUP_WIDTH:2 * GROUP_WIDTH] = _conv_norm_act(yc_ref[...] + cb_ref[...], cg_ref, cbe_ref)
    for head in range(GLA_HEADS):
        hs = slice(HEAD_W * head, HEAD_W * (head + 1))
        oh = go_ref[:, hs]
        oh = oh * lax.rsqrt(jnp.mean(oh * oh, axis=-1, keepdims=True) + NORM_EPS) * gng_ref[:, hs]
        gr = ps_ref[rows, G_R + HEAD_W * head:G_R + HEAD_W * (head + 1)]
        ybuf[rows, 2 * GROUP_WIDTH + HEAD_W * head:2 * GROUP_WIDTH + HEAD_W * (head + 1)] = oh * _silu(gr)
    ybuf[rows, 3 * GROUP_WIDTH:4 * GROUP_WIDTH] = yd_ref[...]

    @pl.when(i == pl.num_programs(0) - 1)
    def _():
        ybf[...] = ybuf[...].astype(BF16)
        cp = pltpu.make_async_copy(ybf, mix_ref.at[pl.ds(t_len, nb)], ysem.at[0])
        cp.start()
        for t in range(t_len // zr):
            zero_copy(t).wait()
        cp.wait()


def _sample_mixers(proj_s, t_len, states, layer, prm):
    c_all, n_all, m_all, s_all, cv_all, k_all, v_all = states
    depth = c_all.shape[0]
    nb = proj_s.shape[0]
    bb = SAMPLE_BB
    assert nb == LANES and nb % bb == 0
    lb = layer * (nb // bb)
    n0, m0 = n_all[layer], m_all[layer]
    mm = jnp.concatenate([m0, m0, jnp.zeros((nb, LANES - 2 * MLSTM_HEADS), F32)], axis=1)
    n0f = n0.reshape(nb, GROUP_WIDTH)
    c0 = c_all.reshape((depth * nb,) + c_all.shape[2:])
    s0 = s_all.reshape((depth * nb,) + s_all.shape[2:])
    cv0 = cv_all.reshape((depth * nb,) + cv_all.shape[2:])
    k0f = k_all.reshape(depth * nb, WINDOW, LANES)
    v0f = v_all.reshape(depth * nb, WINDOW, LANES)
    sq = proj_s[:, S_Q:S_Q + GROUP_WIDTH].reshape(nb, SWA_KV_HEADS, SWA_HEADS // SWA_KV_HEADS, SWA_HEAD_DIM)
    zq = jnp.zeros_like(sq[:, 0])
    qm = jnp.concatenate([jnp.concatenate([sq[:, 0], zq], axis=-1), jnp.concatenate([zq, sq[:, 1]], axis=-1)], axis=1)
    sink_b = jnp.broadcast_to(prm['swa_sinks'].astype(F32)[:, None], (SWA_HEADS, LANES))

    full = lambda shape: pl.BlockSpec(shape, lambda i: (0,) * len(shape))
    rowb = lambda w: pl.BlockSpec((bb, w), lambda i: (i, 0))
    in_specs = [
        full((nb, N_PROJ)), rowb(LANES), full((1, LANES)), rowb(GROUP_WIDTH),
        pl.BlockSpec((bb, MLSTM_HEADS, HEAD_W, HEAD_W), lambda i: (lb + i, 0, 0, 0)),
        pl.BlockSpec((bb, GLA_HEADS, GLA_DK, HEAD_W), lambda i: (lb + i, 0, 0, 0)),
        pl.BlockSpec((bb, CONV_WIDTH - 1, GROUP_WIDTH), lambda i: (lb + i, 0, 0)),
        pl.BlockSpec((bb, WINDOW, LANES), lambda i: (lb + i, 0, 0)),
        pl.BlockSpec((bb, WINDOW, LANES), lambda i: (lb + i, 0, 0)),
        pl.BlockSpec((bb, SWA_HEADS, LANES), lambda i: (i, 0, 0)),
        full((SWA_HEADS, LANES)), full((LANES, 256)), full((1, 256)),
        full((1, GROUP_WIDTH)), full((1, GROUP_WIDTH)),
        full((CONV_WIDTH, GROUP_WIDTH)), full((1, GROUP_WIDTH)), full((1, GROUP_WIDTH)), full((1, GROUP_WIDTH)),
    ]
    out_specs = [
        pl.BlockSpec(memory_space=pl.ANY),
        pl.BlockSpec((bb, MLSTM_HEADS, HEAD_W, HEAD_W), lambda i: (i, 0, 0, 0)),
        rowb(GROUP_WIDTH), rowb(LANES),
        pl.BlockSpec((bb, GLA_HEADS, GLA_DK, HEAD_W), lambda i: (i, 0, 0, 0)),
        pl.BlockSpec((bb, CONV_WIDTH - 1, GROUP_WIDTH), lambda i: (i, 0, 0)),
        pl.BlockSpec((bb, WINDOW, LANES), lambda i: (i, 0, 0)),
        pl.BlockSpec((bb, WINDOW, LANES), lambda i: (i, 0, 0)),
    ]
    out_shape = [
        jax.ShapeDtypeStruct((t_len + nb, D_MODEL), BF16),
        jax.ShapeDtypeStruct(c_all.shape[1:], F32),
        jax.ShapeDtypeStruct((nb, GROUP_WIDTH), F32),
        jax.ShapeDtypeStruct((nb, LANES), F32),
        jax.ShapeDtypeStruct(s_all.shape[1:], F32),
        jax.ShapeDtypeStruct(cv_all.shape[1:], F32),
        jax.ShapeDtypeStruct((nb, WINDOW, LANES), F32),
        jax.ShapeDtypeStruct((nb, WINDOW, LANES), F32),
    ]
    scratch = [
        pltpu.VMEM((_T_ROWS, nb), BF16), pltpu.VMEM((_T_ROWS, LANES), F32),
        pltpu.VMEM((bb, GROUP_WIDTH), F32), pltpu.VMEM((bb, GROUP_WIDTH), F32), pltpu.VMEM((bb, GROUP_WIDTH), F32),
        pltpu.VMEM((bb, GROUP_WIDTH), F32),
        pltpu.VMEM((nb, D_MODEL), F32), pltpu.VMEM((nb, D_MODEL), BF16),
        pltpu.VMEM((_pick_tile(t_len, (512, 256, 128, 64, 16)), D_MODEL), BF16),
        pltpu.SemaphoreType.DMA((1,)), pltpu.SemaphoreType.DMA((1,)),
    ]
    mixed, c1, n1, m1, s1, cv1, k1, v1 = pl.pallas_call(
        _sample_kernel,
        grid=(nb // bb,),
        in_specs=in_specs, out_specs=out_specs, out_shape=out_shape, scratch_shapes=scratch,
        compiler_params=_cparams(("arbitrary",)),
        name="sample_mixers",
    )(proj_s, mm, prm['gate_bias'], n0f, c0, s0, cv0, k0f, v0f, qm, sink_b,
      prm['gla_w_gate_pad'], prm['gla_b_gate'].reshape(1, 256),
      prm['mlstm_norm_g'].reshape(1, -1), prm['gla_norm_g'].reshape(1, -1),
      prm['conv_w'], prm['conv_b'].reshape(1, -1), prm['conv_norm_g'].reshape(1, -1),
      prm['conv_norm_b'].reshape(1, -1))
    new_state = (c1, n1.reshape(n0.shape), m1[:, :MLSTM_HEADS], s1, cv1,
                 k1.reshape(k_all.shape[1:]), v1.reshape(v_all.shape[1:]))
    return mixed, new_state


def _outproj_router_kernel(x_ref, mix_ref, w_ref, g_ref, rw_ref, rb_ref,
                           x1_ref, hn_ref, ri_ref, rf_ref, cnt_ref):
    tm = x_ref.shape[0]

    @pl.when(pl.program_id(0) == 0)
    def _():
        cnt_ref[...] = jnp.zeros_like(cnt_ref)

    x1 = x_ref[...] + jnp.dot(mix_ref[...], w_ref[...], preferred_element_type=F32)
    x1_ref[...] = x1
    ms = jnp.mean(x1 * x1, axis=-1, keepdims=True)
    hn = x1 * lax.rsqrt(ms + NORM_EPS) * g_ref[...]
    hn_ref[...] = hn
    logits = jnp.dot(hn.astype(BF16), rw_ref[...], preferred_element_type=F32) + rb_ref[...]
    lane = lax.broadcasted_iota(I32, (tm, LANES), 1)
    big = jnp.int32(LANES)
    gl = jnp.where(lane < N_GROUPS, logits, -jnp.inf)
    gmax = jnp.max(gl, axis=-1, keepdims=True)
    g_sel = jnp.min(jnp.where(gl == gmax, lane, big), axis=-1, keepdims=True)
    g_w = 1.0 / jnp.sum(jnp.exp(gl - gmax), axis=-1, keepdims=True)
    e_lane = lane - N_GROUPS
    in_grp = (e_lane >= 0) & (e_lane < N_EXPERTS) & ((e_lane // EXPERTS_PER_GROUP) == g_sel)
    el = jnp.where(in_grp, logits, -jnp.inf)
    m1 = jnp.max(el, axis=-1, keepdims=True)
    i1 = jnp.min(jnp.where(el == m1, lane, big), axis=-1, keepdims=True)
    el2 = jnp.where(lane == i1, -jnp.inf, el)
    m2 = jnp.max(el2, axis=-1, keepdims=True)
    i2 = jnp.min(jnp.where(el2 == m2, lane, big), axis=-1, keepdims=True)
    r = jnp.exp(m2 - m1)
    p1 = 1.0 / (1.0 + r)
    gate1 = g_w * p1
    gate2 = g_w * (r * p1)
    sel1 = lane == i1
    sel2 = lane == i2
    onehot = jnp.where(sel1 | sel2, 1.0, 0.0)
    row = lax.broadcasted_iota(I32, (tm, tm), 0)
    col = lax.broadcasted_iota(I32, (tm, tm), 1)
    strict = jnp.where(col < row, 1.0, 0.0).astype(BF16)
    cum = jnp.dot(strict, onehot.astype(BF16), preferred_element_type=F32) + cnt_ref[0:1, :]
    rank1 = jnp.sum(jnp.where(sel1, cum, 0.0), axis=-1, keepdims=True).astype(I32)
    rank2 = jnp.sum(jnp.where(sel2, cum, 0.0), axis=-1, keepdims=True).astype(I32)
    cnt_ref[...] = cnt_ref[...] + jnp.sum(onehot, axis=0, keepdims=True)
    ri = jnp.where(lane == 0, i1 - N_GROUPS, jnp.where(lane == 1, i2 - N_GROUPS,
                   jnp.where(lane == 2, rank1, jnp.where(lane == 3, rank2, 0))))
    ri_ref[...] = ri
    rf_ref[...] = jnp.where(lane == 0, gate1, jnp.where(lane == 1, gate2, 0.0))


def _outproj_router(x, mixed, w_out_bf16, norm_g, rw_pad, rb_pad):
    n = x.shape[0]
    tm = _pick_tile(n, (320, 256, 128, 64, 16))
    full = lambda shape: pl.BlockSpec(shape, lambda i: (0,) * len(shape))
    rowb = lambda w: pl.BlockSpec((tm, w), lambda i: (i, 0))
    return pl.pallas_call(
        _outproj_router_kernel,
        grid=(n // tm,),
        in_specs=[rowb(D_MODEL), rowb(D_MODEL), full((D_MODEL, D_MODEL)), full((1, D_MODEL)),
                  full((D_MODEL, LANES)), full((1, LANES))],
        out_specs=[rowb(D_MODEL), rowb(D_MODEL), rowb(LANES), rowb(LANES), full((SUBLANES, LANES))],
        out_shape=[
            jax.ShapeDtypeStruct((n, D_MODEL), F32),
            jax.ShapeDtypeStruct((n, D_MODEL), F32),
            jax.ShapeDtypeStruct((n, LANES), I32),
            jax.ShapeDtypeStruct((n, LANES), F32),
            jax.ShapeDtypeStruct((SUBLANES, LANES), F32),
        ],
        compiler_params=_cparams(("arbitrary",)),
        name="outproj_router",
    )(x, mixed, w_out_bf16, norm_g.reshape(1, D_MODEL), rw_pad, rb_pad)


def _dispatch_kernel(pos_ref, zt_ref, hn_ref, xs_ref, zbuf, sem, zsem):
    tm = hn_ref.shape[0]
    tile = zbuf.shape[0]
    base = pl.program_id(0) * tm

    @pl.when(pl.program_id(0) == 0)
    def _():
        zbuf[...] = jnp.zeros_like(zbuf)

        def zero_tile(k, carry):
            @pl.when(zt_ref[k] >= 0)
            def _():
                row = pl.multiple_of(zt_ref[k] * tile, tile)
                cp = pltpu.make_async_copy(zbuf, xs_ref.at[pl.ds(row, tile)], zsem.at[0])
                cp.start()
                cp.wait()
            return carry

        lax.fori_loop(0, zt_ref.shape[0], zero_tile, 0)

    def issue(r, carry):
        tok = base + r
        src = hn_ref.at[pl.ds(r, 1)]
        pltpu.make_async_copy(src, xs_ref.at[pl.ds(pos_ref[2 * tok], 1)], sem.at[0]).start(priority=0)
        pltpu.make_async_copy(src, xs_ref.at[pl.ds(pos_ref[2 * tok + 1], 1)], sem.at[1]).start(priority=1)
        return carry

    lax.fori_loop(0, tm, issue, 0, unroll=4)
    pltpu.make_async_copy(hn_ref, xs_ref.at[pl.ds(0, tm)], sem.at[0]).wait()
    pltpu.make_async_copy(hn_ref, xs_ref.at[pl.ds(0, tm)], sem.at[1]).wait()


def _dispatch(pos, zero_tiles, hn, n_rows, tile):
    n = hn.shape[0]
    tm = _pick_tile(n, (256, 128, 64, 8))
    return pl.pallas_call(
        _dispatch_kernel,
        grid_spec=pltpu.PrefetchScalarGridSpec(
            num_scalar_prefetch=2,
            grid=(n // tm,),
            in_specs=[pl.BlockSpec((tm, D_MODEL), lambda i, p, z: (i, 0))],
            out_specs=pl.BlockSpec(memory_space=pl.ANY),
            scratch_shapes=[pltpu.VMEM((tile, D_MODEL), F32), pltpu.SemaphoreType.DMA((2,)),
                            pltpu.SemaphoreType.DMA((1,))],
        ),
        out_shape=jax.ShapeDtypeStruct((n_rows, D_MODEL), F32),
        compiler_params=_cparams(("arbitrary",)),
        name="moe_dispatch",
    )(pos, zero_tiles, hn)


def _expert_kernel(te_ref, nt_ref, x_ref, wg_ref, wu_ref, wd_ref, y_ref):
    del te_ref
    used = pl.program_id(0) < nt_ref[0]

    @pl.when(used)
    def _():
        x = x_ref[...].astype(BF16)
        a = jnp.dot(x, wg_ref[0], preferred_element_type=F32)
        u = jnp.dot(x, wu_ref[0], preferred_element_type=F32)
        hmid = (_silu(a) * u).astype(BF16)
        y_ref[...] = jnp.dot(hmid, wd_ref[0], preferred_element_type=F32)

    @pl.when(jnp.logical_not(used))
    def _():
        y_ref[...] = jnp.zeros_like(y_ref)


def _expert_mlp(tile_expert, n_tiles_used, xs, wg, wu, wd, tile):
    n_tiles = xs.shape[0] // tile

    def row_map(t, te, nt):
        return (jnp.minimum(t, nt[0] - 1), 0)

    def w_map(t, te, nt):
        return (te[jnp.minimum(t, nt[0] - 1)], 0, 0)

    return pl.pallas_call(
        _expert_kernel,
        grid_spec=pltpu.PrefetchScalarGridSpec(
            num_scalar_prefetch=2,
            grid=(n_tiles,),
            in_specs=[
                pl.BlockSpec((tile, D_MODEL), row_map),
                pl.BlockSpec((1, D_MODEL, D_EXPERT), w_map),
                pl.BlockSpec((1, D_MODEL, D_EXPERT), w_map),
                pl.BlockSpec((1, D_EXPERT, D_MODEL), w_map),
            ],
            out_specs=pl.BlockSpec((tile, D_MODEL), lambda t, te, nt: (t, 0)),
        ),
        out_shape=jax.ShapeDtypeStruct(xs.shape, F32),
        compiler_params=_cparams(("arbitrary",)),
        name="expert_mlp",
    )(tile_expert, n_tiles_used, xs, wg, wu, wd)


def _combine_kernel(pos_ref, x1_ref, rf_ref, fg_ref, ys_ref, o_ref, buf_a, buf_b, sem, *, final_norm):
    tm = x1_ref.shape[0]
    base = pl.program_id(0) * tm

    def issue(r, carry):
        tok = base + r
        pltpu.make_async_copy(ys_ref.at[pl.ds(pos_ref[2 * tok], 1)], buf_a.at[pl.ds(r, 1)],
                              sem.at[0]).start(priority=0)
        pltpu.make_async_copy(ys_ref.at[pl.ds(pos_ref[2 * tok + 1], 1)], buf_b.at[pl.ds(r, 1)],
                              sem.at[1]).start(priority=1)
        return carry

    lax.fori_loop(0, tm, issue, 0, unroll=4)
    pltpu.make_async_copy(ys_ref.at[pl.ds(0, tm)], buf_a, sem.at[0]).wait()
    pltpu.make_async_copy(ys_ref.at[pl.ds(0, tm)], buf_b, sem.at[1]).wait()
    rf = rf_ref[...]
    x2 = x1_ref[...] + rf[:, 0:1] * buf_a[...] + rf[:, 1:2] * buf_b[...]
    if final_norm:
        ms = jnp.mean(x2 * x2, axis=-1, keepdims=True)
        x2 = x2 * lax.rsqrt(ms + NORM_EPS) * fg_ref[...]
    o_ref[...] = x2


def _combine(pos, x1, rf, ys, final_g, final_norm):
    n = x1.shape[0]
    tm = _pick_tile(n, (256, 128, 64, 8))
    return pl.pallas_call(
        functools.partial(_combine_kernel, final_norm=final_norm),
        grid_spec=pltpu.PrefetchScalarGridSpec(
            num_scalar_prefetch=1,
            grid=(n // tm,),
            in_specs=[
                pl.BlockSpec((tm, D_MODEL), lambda i, p: (i, 0)),
                pl.BlockSpec((tm, LANES), lambda i, p: (i, 0)),
                pl.BlockSpec((1, D_MODEL), lambda i, p: (0, 0)),
                pl.BlockSpec(memory_space=pl.ANY),
            ],
            out_specs=pl.BlockSpec((tm, D_MODEL), lambda i, p: (i, 0)),
            scratch_shapes=[pltpu.VMEM((tm, D_MODEL), F32), pltpu.VMEM((tm, D_MODEL), F32),
                            pltpu.SemaphoreType.DMA((2,))],
        ),
        out_shape=jax.ShapeDtypeStruct((n, D_MODEL), F32),
        compiler_params=_cparams(("arbitrary",)),
        name="moe_combine",
    )(pos, x1, rf, final_g.reshape(1, D_MODEL), ys)


def _pad_w_in(w_in):
    parts = []
    at = 0
    for lo, hi, dst in sorted(_W_IN_SEGMENTS, key=lambda s: s[2]):
        if dst > at:
            parts.append(jnp.zeros((D_MODEL, dst - at), BF16))
        parts.append(w_in[:, lo:hi].astype(BF16))
        at = dst + (hi - lo)
    if at < N_PROJ:
        parts.append(jnp.zeros((D_MODEL, N_PROJ - at), BF16))
    return jnp.concatenate(parts, axis=1)


def _moe(x1, hn, ri, rf, counts, experts, final_g, final_norm, tile):
    n = x1.shape[0]
    n_tiles = -(-2 * n // tile) + N_EXPERTS
    cnt = counts[0, N_GROUPS:N_GROUPS + N_EXPERTS].astype(I32)
    tiles_per = (cnt + tile - 1) // tile
    tile_end = jnp.cumsum(tiles_per)
    row_off = (tile_end - tiles_per) * tile
    pos = (row_off[ri[:, 0:2]] + ri[:, 2:4]).reshape(2 * n)
    tile_ids = jnp.arange(n_tiles, dtype=I32)
    tile_expert = jnp.minimum(jnp.sum((tile_ids[:, None] >= tile_end[None, :]).astype(I32), axis=1), N_EXPERTS - 1)
    n_used = tile_end[N_EXPERTS - 1:N_EXPERTS].astype(I32)
    last_tile = jnp.where(tiles_per > 0, tile_end - 1, -1).astype(I32)
    zero_tiles = jnp.concatenate([last_tile, jnp.where(tile_ids >= n_used[0], tile_ids, -1)])
    xs = _dispatch(pos, zero_tiles, hn, n_tiles * tile, tile)
    ys = _expert_mlp(tile_expert, n_used, xs, *experts, tile)
    return _combine(pos, x1, rf, ys, final_g, final_norm)


def _layer(x, t_len, states, layer, prm, final_g, final_norm, moe_tile):
    proj = _in_projection(x, prm['norm_mix_g'], prm['w_in_pad'])
    mixed, new_s = _sample_mixers(proj[t_len:], t_len, states, layer, prm)
    e_shape = prm['expert_w_gate_all'].shape[1:]
    mixed, cn, m_p, wg = _mlstm_prompt(proj, mixed, t_len, prm['gate_bias'], prm['mlstm_norm_g'],
                                       prm['expert_w_gate_all'], layer)
    mixed, conv_tail, wu = _conv_prompt(proj, mixed, t_len, prm['conv_w'], prm['conv_b'], prm['conv_norm_g'],
                                        prm['conv_norm_b'], prm['expert_w_up_all'], layer)
    mixed, sp, wd = _gla_prompt(proj, mixed, t_len, prm['gla_w_gate_pad'], prm['gla_b_gate'], prm['gla_norm_g'],
                                prm['expert_w_down_all'], layer)
    mixed = _swa_prompt(proj, mixed, t_len, prm['swa_sinks'])
    experts = (wg.reshape(e_shape), wu.reshape(e_shape), wd.reshape(prm['expert_w_down_all'].shape[1:]))
    x1, hn, ri, rf, counts = _outproj_router(x, mixed, prm['w_out'], prm['norm_ffn_g'], prm['router_w'], prm['router_b'])
    x2 = _moe(x1, hn, ri, rf, counts, experts, final_g, final_norm, moe_tile)
    p_c = cn[None, :, :, :HEAD_W]
    p_n = cn[None, :, :, HEAD_W]
    p_m = m_p[None, :MLSTM_HEADS, 0]
    p_s = jnp.stack([sp[0, :GLA_DK, :HEAD_W], sp[0, GLA_DK:, HEAD_W:],
                     sp[1, :GLA_DK, :HEAD_W], sp[1, GLA_DK:, HEAD_W:]])[None]
    p_conv = conv_tail[None, CONV_HALO - (CONV_WIDTH - 1):]
    p_k = proj[t_len - WINDOW:t_len, S_K:S_K + LANES].reshape(1, WINDOW, SWA_KV_HEADS, SWA_HEAD_DIM)
    p_v = proj[t_len - WINDOW:t_len, S_V:S_V + LANES].reshape(1, WINDOW, SWA_KV_HEADS, SWA_HEAD_DIM)
    return x2, (p_c, p_n, p_m, p_s, p_conv, p_k, p_v), new_s


def _forward(x_prompt, x_sample, states, layer_params, final_norm_g, moe_tile=MOE_TILE):
    t_len = x_prompt.shape[1]
    x = jnp.concatenate([x_prompt[0], x_sample[:, 0]], axis=0)
    new_p, new_s = [], []
    depth = len(layer_params)
    for l, prm in enumerate(layer_params):
        x, sp, ss = _layer(x, t_len, states, l, prm, final_norm_g, l == depth - 1, moe_tile)
        new_p.append(sp)
        new_s.append(ss)
    y_prompt = x[None, :t_len]
    y_sample = x[t_len:, None]
    p_states = [jnp.stack(parts) for parts in zip(*new_p)]
    s_states = [jnp.stack(parts) for parts in zip(*new_s)]
    return (y_prompt, y_sample, *p_states, *s_states)


def _prep_layer_params(l, norm_mix_g, w_in, mlstm_b_i, mlstm_b_f, mlstm_norm_g, conv_w, conv_b, conv_norm_g,
                       conv_norm_b, gla_w_gate, gla_b_gate, gla_norm_g, swa_sinks, w_out, norm_ffn_g,
                       router_group_w, router_group_b, router_expert_w, router_expert_b, expert_w_gate,
                       expert_w_up, expert_w_down):
    gate_bias = jnp.concatenate([mlstm_b_i[l], mlstm_b_f[l], jnp.zeros((LANES - 2 * MLSTM_HEADS,), F32)])
    rw = jnp.concatenate([router_group_w[l], router_expert_w[l],
                          jnp.zeros((D_MODEL, LANES - N_GROUPS - N_EXPERTS), F32)], axis=1)
    rb = jnp.concatenate([router_group_b[l], router_expert_b[l],
                          jnp.zeros((LANES - N_GROUPS - N_EXPERTS,), F32)])
    return {
        'norm_mix_g': norm_mix_g[l],
        'w_in_pad': _pad_w_in(w_in[l]),
        'gate_bias': gate_bias.reshape(1, LANES),
        'mlstm_norm_g': mlstm_norm_g[l],
        'conv_w': conv_w[l], 'conv_b': conv_b[l], 'conv_norm_g': conv_norm_g[l], 'conv_norm_b': conv_norm_b[l],
        'gla_w_gate_pad': jnp.concatenate(
            [gla_w_gate[l], jnp.zeros((LANES - GLA_LOWRANK, GLA_HEADS * GLA_DK), F32)], axis=0).astype(BF16),
        'gla_b_gate': gla_b_gate[l], 'gla_norm_g': gla_norm_g[l],
        'swa_sinks': swa_sinks[l],
        'w_out': w_out[l].astype(BF16),
        'norm_ffn_g': norm_ffn_g[l],
        'router_w': rw.astype(BF16), 'router_b': rb.reshape(1, LANES),
        'expert_w_gate_all': expert_w_gate, 'expert_w_up_all': expert_w_up, 'expert_w_down_all': expert_w_down,
    }


def kernel(x_prompt, x_sample, state_mlstm_C, state_mlstm_n, state_mlstm_m, state_gla_S, cache_conv, cache_swa_k, cache_swa_v, norm_mix_g, w_in, mlstm_b_i, mlstm_b_f, mlstm_norm_g, conv_w, conv_b, conv_norm_g, conv_norm_b, gla_w_gate, gla_b_gate, gla_norm_g, swa_sinks, w_out, norm_ffn_g, router_group_w, router_group_b, router_expert_w, router_expert_b, expert_w_gate, expert_w_up, expert_w_down, final_norm_g):
    depth = w_in.shape[0]
    weights = (norm_mix_g, w_in, mlstm_b_i, mlstm_b_f, mlstm_norm_g, conv_w, conv_b, conv_norm_g, conv_norm_b,
               gla_w_gate, gla_b_gate, gla_norm_g, swa_sinks, w_out, norm_ffn_g, router_group_w, router_group_b,
               router_expert_w, router_expert_b, expert_w_gate, expert_w_up, expert_w_down)
    layer_params = [_prep_layer_params(l, *weights) for l in range(depth)]
    states = (state_mlstm_C, state_mlstm_n, state_mlstm_m, state_gla_S, cache_conv, cache_swa_k, cache_swa_v)
    return _forward(x_prompt, x_sample, states, layer_params, final_norm_g)
```

```python
import functools
from typing import NamedTuple

import jax
import jax.numpy as jnp
from jax import lax
from jax.experimental import pallas as pl
from jax.experimental.pallas import tpu as pltpu

F32 = jnp.float32
BF16 = jnp.bfloat16
I32 = jnp.int32
HIGHEST = lax.Precision.HIGHEST

D_MODEL = 2048
GROUP_WIDTH = 512
HEAD_W = 128
MLSTM_HEADS = 4
GLA_HEADS = 4
GLA_DK = 64
GLA_LOWRANK = 16
GLA_TAU = 16.0
CONV_WIDTH = 31
SWA_HEADS = 8
SWA_KV_HEADS = 2
SWA_HEAD_DIM = 64
WINDOW = 128
N_GROUPS = 4
EXPERTS_PER_GROUP = 4
N_EXPERTS = 16
D_EXPERT = 1024
NORM_EPS = 1e-6
LANES = 128
SUBLANES = 8
TOK_ROWS = D_MODEL // LANES
PACK_ROWS = TOK_ROWS // 2

A_Q, A_K, A_V, A_O = 0, 512, 1024, 1536
C_A, C_G = 2048, 2560
G_V, G_R = 3072, 3584
S_Q = 4096
G_Q, G_K = 4608, 4864
A_G, G_LR, S_K, S_V = 5120, 5248, 5376, 5504
N_PROJ = 5632
MIX_COL_MLSTM, MIX_COL_CONV, MIX_COL_GLA, MIX_COL_SWA = 0, 1, 2, 3
_W_IN_SEGMENTS = (
    (0, 512, A_Q), (512, 1024, A_K), (1024, 1536, A_V), (1536, 2048, A_O),
    (2048, 2056, A_G),
    (2056, 2568, C_A), (2568, 3080, C_G),
    (3080, 3336, G_Q), (3336, 3592, G_K), (3592, 4104, G_V), (4104, 4616, G_R),
    (4616, 4632, G_LR),
    (4632, 5144, S_Q), (5144, 5272, S_K), (5272, 5400, S_V),
)

MLSTM_CHUNK = 128
MLSTM_STEP_ROWS = 256
GLA_CHUNK = 64
GLA_STEP_ROWS = 256
GLA_SUB = 16
SWA_STEP_ROWS = 256
CONV_TILE = 256
CONV_ROWS = 64
CONV_HALO = 32
SAMPLE_BB = 8
MOE_TILE = 512
VMEM_LIMIT = 56 * 1024 * 1024


def _cparams(sem, vmem=VMEM_LIMIT):
    return pltpu.CompilerParams(dimension_semantics=sem, vmem_limit_bytes=vmem)


def _log_sigmoid(x):
    return jnp.minimum(x, 0.0) - jnp.log1p(jnp.exp(-jnp.abs(x)))


def _sigmoid(x):
    return 1.0 / (1.0 + jnp.exp(-x))


def _silu(x):
    return x * _sigmoid(x)


def _masked_row_sums(mask, x):
    m = jnp.where(mask, 1.0, 0.0).astype(BF16)
    hi = x.astype(BF16)
    r1 = x - hi.astype(F32)
    mid = r1.astype(BF16)
    lo = (r1 - mid.astype(F32)).astype(BF16)
    out = jnp.dot(m, hi, preferred_element_type=F32)
    out = out + jnp.dot(m, mid, preferred_element_type=F32)
    return out + jnp.dot(m, lo, preferred_element_type=F32)


def _pick_tile(n, candidates):
    for c in candidates:
        if n % c == 0:
            return c
    raise ValueError(f"no tile for {n} in {candidates}")


class _SideCast(NamedTuple):
    operand: jax.Array
    in_spec: pl.BlockSpec
    out_spec: pl.BlockSpec
    out_shape: jax.ShapeDtypeStruct
    shape: tuple


def _side_cast(w_all, layer, n_steps):
    depth, n_e, k, f = w_all.shape
    rows = n_e * k
    assert rows % n_steps == 0, (rows, n_steps)
    tr = rows // n_steps
    return _SideCast(
        operand=w_all.reshape(depth * rows, f),
        in_spec=pl.BlockSpec((tr, f), lambda i: (layer * n_steps + i, 0)),
        out_spec=pl.BlockSpec((tr, f), lambda i: (i, 0)),
        out_shape=jax.ShapeDtypeStruct((rows, f), BF16),
        shape=(n_e, k, f),
    )


def _proj_kernel(x_ref, g_ref, w_ref, o_ref, hn_ref):
    @pl.when(pl.program_id(1) == 0)
    def _():
        x = x_ref[...]
        ms = jnp.mean(x * x, axis=-1, keepdims=True)
        hn_ref[...] = (x * lax.rsqrt(ms + NORM_EPS) * g_ref[...]).astype(BF16)

    o_ref[...] = jnp.dot(hn_ref[...], w_ref[...], preferred_element_type=F32)


def _in_projection(x, g, w_bf16):
    n = x.shape[0]
    tm = _pick_tile(n, (832, 640, 512, 256, 128, 64, 8))
    tn = N_PROJ // 4
    return pl.pallas_call(
        _proj_kernel,
        grid=(n // tm, N_PROJ // tn),
        in_specs=[
            pl.BlockSpec((tm, D_MODEL), lambda i, j: (i, 0)),
            pl.BlockSpec((1, D_MODEL), lambda i, j: (0, 0)),
            pl.BlockSpec((D_MODEL, tn), lambda i, j: (0, j)),
        ],
        out_specs=pl.BlockSpec((tm, tn), lambda i, j: (i, j)),
        out_shape=jax.ShapeDtypeStruct((n, N_PROJ), F32),
        scratch_shapes=[pltpu.VMEM((tm, D_MODEL), BF16)],
        compiler_params=_cparams(("parallel", "arbitrary")),
        name="in_projection",
    )(x, g.reshape(1, D_MODEL), w_bf16)


def _mlstm_prompt_kernel(q_ref, k_ref, v_ref, o_ref, gt_ref, bias_ref, ng_ref, mixed_in_ref, wc_in_ref,
                         y_ref, cn_ref, m_ref, wc_out_ref):
    del mixed_in_ref
    wc_out_ref[...] = wc_in_ref[...].astype(wc_out_ref.dtype)
    rows_step = q_ref.shape[0]
    L = min(MLSTM_CHUNK, rows_step)
    n_chunks = rows_step // L

    @pl.when(pl.program_id(0) == 0)
    def _():
        cn_ref[...] = jnp.zeros_like(cn_ref)
        m_ref[...] = jnp.zeros_like(m_ref)

    pre = gt_ref[...] + bias_ref[...]
    lf = _log_sigmoid(pre)
    row = lax.broadcasted_iota(I32, (rows_step, rows_step), 0)
    col = lax.broadcasted_iota(I32, (rows_step, rows_step), 1)
    cum = ((row // L) == (col // L)) & (col <= row)
    b_all = _masked_row_sums(cum, lf)
    pre_t = pre.T
    b_t = b_all.T
    trow = lax.broadcasted_iota(I32, (L, L), 0)
    tcol = lax.broadcasted_iota(I32, (L, L), 1)
    tri = tcol <= trow
    lane = lax.broadcasted_iota(I32, (L, HEAD_W), 1)
    ones_col = (lane == 0).astype(BF16)
    for h in range(MLSTM_HEADS):
        sl = slice(HEAD_W * h, HEAD_W * (h + 1))
        m_prev = m_ref[h:h + 1, 0:1]
        cn = cn_ref[h]
        for c in range(n_chunks):
            cs = slice(L * c, L * (c + 1))
            q = q_ref[cs, sl]
            k = k_ref[cs, sl] * (HEAD_W ** -0.5)
            v = v_ref[cs, sl]
            b_col = b_all[cs, 4 + h:5 + h]
            i_col = pre[cs, h:h + 1]
            b_row = b_t[4 + h:5 + h, cs]
            i_row = pre_t[h:h + 1, cs]
            log_d = jnp.where(tri, b_col - b_row + i_row, -jnp.inf)
            log_inter = b_col + m_prev
            m_t = jnp.maximum(log_inter, jnp.max(log_d, axis=-1, keepdims=True))
            d_mat = jnp.exp(log_d - m_t)
            g_inter = jnp.exp(log_inter - m_t)
            qb = q.astype(BF16)
            kb = k.astype(BF16)
            s = lax.dot_general(qb, kb, (((1,), (1,)), ((), ())), preferred_element_type=F32)
            w = (s * d_mat).astype(BF16)
            v1 = jnp.concatenate([v.astype(BF16), ones_col], axis=1)
            nd = g_inter * jnp.dot(qb, cn.astype(BF16), preferred_element_type=F32)
            nd = nd + jnp.dot(w, v1, preferred_element_type=F32)
            num = nd[:, :HEAD_W]
            den = nd[:, HEAD_W:HEAD_W + 1]
            hh = num / jnp.maximum(jnp.abs(den), jnp.exp(-m_t))
            hh = _sigmoid(o_ref[cs, sl]) * hh
            hh = hh * lax.rsqrt(jnp.mean(hh * hh, axis=-1, keepdims=True) + NORM_EPS) * ng_ref[:, sl]
            y_ref[cs, sl] = hh.astype(y_ref.dtype)
            m_last = m_t[L - 1:L, :]
            b_last = b_col[L - 1:L, :]
            g_state = jnp.exp(b_last + m_prev - m_last)
            w_k = jnp.exp(b_last - b_col + i_col - m_last)
            kw = (k * w_k).astype(BF16)
            upd = lax.dot_general(kw, v1, (((0,), (0,)), ((), ())), preferred_element_type=F32)
            cn = g_state * cn + upd
            m_prev = m_last
        cn_ref[h] = cn
        m_ref[h:h + 1, :] = jnp.broadcast_to(m_prev, (1, LANES))


def _mlstm_prompt(proj, mixed, t_len, bias_row, norm_g, w_cast, layer):
    L = _pick_tile(t_len, (MLSTM_STEP_ROWS, MLSTM_CHUNK, 64, 32, 16, 8))
    side = _side_cast(w_cast, layer, t_len // L)

    def col(off):
        return pl.BlockSpec((L, GROUP_WIDTH), lambda i, o=off: (i, o // GROUP_WIDTH))

    return pl.pallas_call(
        _mlstm_prompt_kernel,
        grid=(t_len // L,),
        in_specs=[
            col(A_Q), col(A_K), col(A_V), col(A_O),
            pl.BlockSpec((L, LANES), lambda i: (i, A_G // LANES)),
            pl.BlockSpec((1, LANES), lambda i: (0, 0)),
            pl.BlockSpec((1, GROUP_WIDTH), lambda i: (0, 0)),
            pl.BlockSpec(memory_space=pl.ANY),
            side.in_spec,
        ],
        out_specs=[
            pl.BlockSpec((L, GROUP_WIDTH), lambda i: (i, MIX_COL_MLSTM)),
            pl.BlockSpec((MLSTM_HEADS, HEAD_W, 2 * HEAD_W), lambda i: (0, 0, 0)),
            pl.BlockSpec((SUBLANES, LANES), lambda i: (0, 0)),
            side.out_spec,
        ],
        out_shape=[
            jax.ShapeDtypeStruct(mixed.shape, mixed.dtype),
            jax.ShapeDtypeStruct((MLSTM_HEADS, HEAD_W, 2 * HEAD_W), F32),
            jax.ShapeDtypeStruct((SUBLANES, LANES), F32),
            side.out_shape,
        ],
        input_output_aliases={7: 0},
        compiler_params=_cparams(("arbitrary",)),
        name="mlstm_prompt",
    )(proj, proj, proj, proj, proj, bias_row, norm_g.reshape(1, GROUP_WIDTH), mixed, side.operand)


def _gla_prompt_kernel(q_ref, k_ref, v_ref, r_ref, lr_ref, wg_ref, bg_ref, ng_ref, mixed_in_ref, wc_in_ref,
                       y_ref, sp_ref, wc_out_ref):
    del mixed_in_ref
    wc_out_ref[...] = wc_in_ref[...].astype(wc_out_ref.dtype)
    rows_step = q_ref.shape[0]
    L = min(GLA_CHUNK, rows_step)
    n_chunks = rows_step // L
    n_sub = L // GLA_SUB

    @pl.when(pl.program_id(0) == 0)
    def _():
        sp_ref[...] = jnp.zeros_like(sp_ref)

    gate_pre = jnp.dot(lr_ref[...].astype(BF16), wg_ref[...], preferred_element_type=F32) + bg_ref[...]
    log_a = _log_sigmoid(gate_pre) * (1.0 / GLA_TAU)
    row = lax.broadcasted_iota(I32, (rows_step, rows_step), 0)
    col = lax.broadcasted_iota(I32, (rows_step, rows_step), 1)
    tri = ((row // L) == (col // L)) & (col <= row)
    b = _masked_row_sums(tri, log_a)
    q = q_ref[...] * (GLA_DK ** -0.5)
    k = k_ref[...]
    b_last = jnp.concatenate(
        [jnp.broadcast_to(b[L * c + L - 1:L * c + L, :], (L, b.shape[1])) for c in range(n_chunks)], axis=0)
    q_in = q * jnp.exp(b)
    k_dec = k * jnp.exp(b_last - b)
    lane16 = lax.broadcasted_iota(I32, (GLA_SUB, LANES), 1)
    lo16 = lane16 < GLA_DK
    srow = lax.broadcasted_iota(I32, (LANES, 2 * HEAD_W), 0)
    scol = lax.broadcasted_iota(I32, (LANES, 2 * HEAD_W), 1)
    block_diag = (srow < GLA_DK) == (scol < HEAD_W)
    for p in range(2):
        pls = slice(LANES * p, LANES * (p + 1))
        vp = v_ref[:, 2 * HEAD_W * p:2 * HEAD_W * (p + 1)].astype(BF16)
        b_t = b[:, pls].T
        sp = sp_ref[p]
        states = []
        for c in range(n_chunks):
            cs = slice(L * c, L * (c + 1))
            states.append(sp)
            dec_col = jnp.exp(b_t[:, L * c + L - 1:L * c + L])
            upd = lax.dot_general(k_dec[cs, pls].astype(BF16), vp[cs], (((0,), (0,)), ((), ())),
                                  preferred_element_type=F32)
            sp = jnp.where(block_diag, dec_col * sp + upd, 0.0)
        sp_ref[p] = sp
        outs = []
        for c in range(n_chunks):
            c0 = L * c
            o_inter = jnp.dot(q_in[c0:c0 + L, pls].astype(BF16), states[c].astype(BF16),
                              preferred_element_type=F32)
            rows = []
            for blk in range(n_sub):
                r0 = GLA_SUB * blk
                n = GLA_SUB * (blk + 1)
                qrows = slice(c0 + r0, c0 + r0 + GLA_SUB)
                krows = slice(c0, c0 + n)
                if blk == 0:
                    qs = q[qrows, pls] * jnp.exp(b[qrows, pls])
                    ks = k[krows, pls] * jnp.exp(-b[krows, pls])
                else:
                    anchor = b[c0 + r0 - 1:c0 + r0, pls]
                    qs = q[qrows, pls] * jnp.exp(b[qrows, pls] - anchor)
                    ks = k[krows, pls] * jnp.exp(anchor - b[krows, pls])
                qs2 = jnp.concatenate([jnp.where(lo16, qs, 0.0), jnp.where(lo16, 0.0, qs)], axis=0)
                att = lax.dot_general(qs2.astype(BF16), ks.astype(BF16), (((1,), (1,)), ((), ())),
                                      preferred_element_type=F32)
                trow = lax.broadcasted_iota(I32, (2 * GLA_SUB, n), 0)
                tcol = lax.broadcasted_iota(I32, (2 * GLA_SUB, n), 1)
                t_idx = r0 + jnp.where(trow >= GLA_SUB, trow - GLA_SUB, trow)
                att = jnp.where(tcol <= t_idx, att, 0.0)
                o2 = jnp.dot(att.astype(BF16), vp[krows], preferred_element_type=F32)
                rows.append(jnp.concatenate([o2[:GLA_SUB, :HEAD_W], o2[GLA_SUB:, HEAD_W:]], axis=1))
            outs.append(o_inter + jnp.concatenate(rows, axis=0))
        o = jnp.concatenate(outs, axis=0)
        for hh in range(2):
            head = 2 * p + hh
            hs = slice(HEAD_W * head, HEAD_W * (head + 1))
            oh = o[:, HEAD_W * hh:HEAD_W * (hh + 1)]
            oh = oh * lax.rsqrt(jnp.mean(oh * oh, axis=-1, keepdims=True) + NORM_EPS) * ng_ref[:, hs]
            y_ref[:, hs] = (oh * _silu(r_ref[:, hs])).astype(y_ref.dtype)


def _gla_prompt(proj, mixed, t_len, w_gate_pad, b_gate, norm_g, w_cast, layer):
    L = _pick_tile(t_len, (GLA_STEP_ROWS, GLA_CHUNK))
    side = _side_cast(w_cast, layer, t_len // L)
    return pl.pallas_call(
        _gla_prompt_kernel,
        grid=(t_len // L,),
        in_specs=[
            pl.BlockSpec((L, 256), lambda i: (i, G_Q // 256)),
            pl.BlockSpec((L, 256), lambda i: (i, G_K // 256)),
            pl.BlockSpec((L, GROUP_WIDTH), lambda i: (i, G_V // GROUP_WIDTH)),
            pl.BlockSpec((L, GROUP_WIDTH), lambda i: (i, G_R // GROUP_WIDTH)),
            pl.BlockSpec((L, LANES), lambda i: (i, G_LR // LANES)),
            pl.BlockSpec((LANES, 256), lambda i: (0, 0)),
            pl.BlockSpec((1, 256), lambda i: (0, 0)),
            pl.BlockSpec((1, GROUP_WIDTH), lambda i: (0, 0)),
            pl.BlockSpec(memory_space=pl.ANY),
            side.in_spec,
        ],
        out_specs=[
            pl.BlockSpec((L, GROUP_WIDTH), lambda i: (i, MIX_COL_GLA)),
            pl.BlockSpec((2, LANES, 2 * HEAD_W), lambda i: (0, 0, 0)),
            side.out_spec,
        ],
        out_shape=[
            jax.ShapeDtypeStruct(mixed.shape, mixed.dtype),
            jax.ShapeDtypeStruct((2, LANES, 2 * HEAD_W), F32),
            side.out_shape,
        ],
        input_output_aliases={8: 0},
        compiler_params=_cparams(("arbitrary",)),
        name="gla_prompt",
    )(proj, proj, proj, proj, proj, w_gate_pad, b_gate.reshape(1, 256), norm_g.reshape(1, GROUP_WIDTH), mixed,
      side.operand)


def _conv_norm_act(y, g_ref, be_ref):
    mu = jnp.mean(y, axis=-1, keepdims=True)
    yc = y - mu
    var = jnp.mean(yc * yc, axis=-1, keepdims=True)
    return _silu(yc * lax.rsqrt(var + NORM_EPS) * g_ref[...] + be_ref[...])


def _conv_prompt_kernel(ua_ref, ug_ref, ha_ref, hg_ref, w_ref, b_ref, g_ref, be_ref, mixed_in_ref, wc_in_ref,
                        y_ref, tail_ref, wc_out_ref, buf_ref, sh_ref):
    del mixed_in_ref
    wc_out_ref[...] = wc_in_ref[...].astype(wc_out_ref.dtype)
    tt = ua_ref.shape[0]
    span = tt + CONV_HALO
    halo = ha_ref[...] * _sigmoid(hg_ref[...])
    buf_ref[0:CONV_HALO, :] = jnp.where(pl.program_id(0) > 0, halo, 0.0)
    buf_ref[CONV_HALO:span, :] = ua_ref[...] * _sigmoid(ug_ref[...])
    buf_ref[span:span + SUBLANES, :] = jnp.zeros((SUBLANES, GROUP_WIDTH), F32)
    for k in range(1, SUBLANES):
        sh_ref[k] = buf_ref[k:k + span, :]
    base = CONV_HALO - (CONV_WIDTH - 1)
    for r in range(tt // CONV_ROWS):
        acc = jnp.zeros((CONV_ROWS, GROUP_WIDTH), F32)
        for j in range(CONV_WIDTH):
            s0 = r * CONV_ROWS + base + j
            k = s0 % SUBLANES
            a0 = s0 - k
            win = buf_ref[a0:a0 + CONV_ROWS, :] if k == 0 else sh_ref[k, a0:a0 + CONV_ROWS, :]
            acc = acc + w_ref[j:j + 1, :] * win
        y = _conv_norm_act(acc + b_ref[...], g_ref, be_ref)
        y_ref[r * CONV_ROWS:(r + 1) * CONV_ROWS, :] = y.astype(y_ref.dtype)
    tail_ref[...] = buf_ref[tt:span, :]


def _conv_prompt(proj, mixed, t_len, w, b, g, beta, w_cast, layer):
    tt = _pick_tile(t_len, (CONV_TILE, 128, 64))
    ratio = tt // CONV_HALO
    side = _side_cast(w_cast, layer, t_len // tt)
    vec = lambda: pl.BlockSpec((1, GROUP_WIDTH), lambda i: (0, 0))
    return pl.pallas_call(
        _conv_prompt_kernel,
        grid=(t_len // tt,),
        in_specs=[
            pl.BlockSpec((tt, GROUP_WIDTH), lambda i: (i, C_A // GROUP_WIDTH)),
            pl.BlockSpec((tt, GROUP_WIDTH), lambda i: (i, C_G // GROUP_WIDTH)),
            pl.BlockSpec((CONV_HALO, GROUP_WIDTH), lambda i: (jnp.maximum(i * ratio - 1, 0), C_A // GROUP_WIDTH)),
            pl.BlockSpec((CONV_HALO, GROUP_WIDTH), lambda i: (jnp.maximum(i * ratio - 1, 0), C_G // GROUP_WIDTH)),
            pl.BlockSpec((CONV_WIDTH, GROUP_WIDTH), lambda i: (0, 0)),
            vec(), vec(), vec(),
            pl.BlockSpec(memory_space=pl.ANY),
            side.in_spec,
        ],
        out_specs=[
            pl.BlockSpec((tt, GROUP_WIDTH), lambda i: (i, MIX_COL_CONV)),
            pl.BlockSpec((CONV_HALO, GROUP_WIDTH), lambda i: (0, 0)),
            side.out_spec,
        ],
        out_shape=[
            jax.ShapeDtypeStruct(mixed.shape, mixed.dtype),
            jax.ShapeDtypeStruct((CONV_HALO, GROUP_WIDTH), F32),
            side.out_shape,
        ],
        scratch_shapes=[pltpu.VMEM((tt + CONV_HALO + SUBLANES, GROUP_WIDTH), F32),
                        pltpu.VMEM((SUBLANES, tt + CONV_HALO, GROUP_WIDTH), F32)],
        input_output_aliases={8: 0},
        compiler_params=_cparams(("arbitrary",)),
        name="conv_prompt",
    )(proj, proj, proj, proj, w, b.reshape(1, -1), g.reshape(1, -1), beta.reshape(1, -1), mixed, side.operand)


def _swa_prompt_kernel(sink_ref, q_ref, kc_ref, vc_ref, kp_ref, vp_ref, mixed_in_ref, *rest):
    del mixed_in_ref
    if len(rest) == 3:
        w_ref, y_ref, wpad_ref = rest
        _pad_w_in_rows(w_ref, wpad_ref)
    else:
        (y_ref,) = rest
    bq = WINDOW
    n_blk = q_ref.shape[0] // bq
    first = pl.program_id(0) == 0
    k_full = jnp.concatenate([kp_ref[...], kc_ref[...]], axis=0)
    v_full = jnp.concatenate([vp_ref[...], vc_ref[...]], axis=0)
    k_sw_full = pltpu.roll(k_full, SWA_HEAD_DIM, 1).astype(BF16)
    v_sw_full = pltpu.roll(v_full, SWA_HEAD_DIM, 1).astype(BF16)
    k_full = k_full.astype(BF16)
    v_full = v_full.astype(BF16)
    tq = lax.broadcasted_iota(I32, (bq, 2 * bq), 0)
    kj = lax.broadcasted_iota(I32, (bq, 2 * bq), 1)
    band = (kj > tq) & (kj <= tq + WINDOW)
    lane = lax.broadcasted_iota(I32, (bq, LANES), 1)
    lo = lane < SWA_HEAD_DIM
    rep = SWA_HEADS // SWA_KV_HEADS
    for blk in range(n_blk):
        qs = slice(bq * blk, bq * (blk + 1))
        ks = slice(bq * blk, bq * (blk + 2))
        valid = band & (kj >= jnp.where(first, bq, 0)) if blk == 0 else band
        k_all, v_all, k_sw, v_sw = k_full[ks], v_full[ks], k_sw_full[ks], v_sw_full[ks]
        for c in range(SWA_HEADS // 2):
            qc = q_ref[qs, LANES * c:LANES * (c + 1)] * (SWA_HEAD_DIM ** -0.5)
            outs = []
            for hh in range(2):
                h = 2 * c + hh
                g = h // rep
                qm = jnp.where(lo if hh == 0 else jnp.logical_not(lo), qc, 0.0).astype(BF16)
                k_use = k_all if g == hh else k_sw
                v_use = v_all if g == hh else v_sw
                s = lax.dot_general(qm, k_use, (((1,), (1,)), ((), ())), preferred_element_type=F32)
                s = jnp.where(valid, s, -jnp.inf)
                sink = sink_ref[h]
                mx = jnp.maximum(jnp.max(s, axis=-1, keepdims=True), sink)
                p = jnp.exp(s - mx)
                den = jnp.sum(p, axis=-1, keepdims=True) + jnp.exp(sink - mx)
                p = (p / den).astype(BF16)
                outs.append(jnp.dot(p, v_use, preferred_element_type=F32))
            y_ref[qs, LANES * c:LANES * (c + 1)] = jnp.where(lo, outs[0], outs[1]).astype(y_ref.dtype)


def _swa_prompt(proj, mixed, t_len, sinks, w_in_all=None, next_layer=None):
    bq = WINDOW
    rows = _pick_tile(t_len, (SWA_STEP_ROWS, bq))
    ratio = rows // bq
    cur = lambda off: pl.BlockSpec((rows, LANES), lambda i: (i, off // LANES))
    prev = lambda off: pl.BlockSpec((bq, LANES), lambda i: (jnp.maximum(i * ratio - 1, 0), off // LANES))
    in_specs = [
        pl.BlockSpec(memory_space=pltpu.SMEM),
        pl.BlockSpec((rows, GROUP_WIDTH), lambda i: (i, S_Q // GROUP_WIDTH)),
        cur(S_K), cur(S_V), prev(S_K), prev(S_V),
        pl.BlockSpec(memory_space=pl.ANY),
    ]
    out_specs = [pl.BlockSpec((rows, GROUP_WIDTH), lambda i: (i, MIX_COL_SWA))]
    out_shape = [jax.ShapeDtypeStruct(mixed.shape, mixed.dtype)]
    operands = [sinks, proj, proj, proj, proj, proj, mixed]
    if next_layer is not None:
        side = _w_in_side(w_in_all, next_layer, t_len // rows)
        in_specs.append(side.in_spec)
        out_specs.append(side.out_spec)
        out_shape.append(side.out_shape)
        operands.append(side.operand)
    outs = pl.pallas_call(
        _swa_prompt_kernel,
        grid=(t_len // rows,),
        in_specs=in_specs, out_specs=out_specs, out_shape=out_shape,
        input_output_aliases={6: 0},
        compiler_params=_cparams(("arbitrary",)),
        name="swa_prompt",
    )(*operands)
    return outs[0], (outs[1] if next_layer is not None else None)


_T_MK, _T_MQ = 0, 512
_T_GA, _T_GK, _T_GQ = 1024, 1280, 1536
_T_ROWS = 1792


def _sample_kernel(ps_ref, mm_ref, bias_ref, n0_ref, c0_ref, s0_ref, cv0_ref, k0_ref, v0_ref,
                   qm_ref, sink_ref, wg_ref, bg_ref, mng_ref, gng_ref, cw_ref, cb_ref, cg_ref, cbe_ref,
                   mix_ref, c1_ref, n1_ref, m1_ref, s1_ref, cv1_ref, k1_ref, v1_ref,
                   tt_ref, bc_ref, num_ref, go_ref, yc_ref, yd_ref, ybuf, ybf, zbuf, zsem, ysem):
    i = pl.program_id(0)
    nb = ps_ref.shape[0]
    bb = c0_ref.shape[0]
    t_len = mix_ref.shape[0] - nb
    zr = zbuf.shape[0]

    def zero_copy(t):
        return pltpu.make_async_copy(zbuf, mix_ref.at[pl.ds(t * zr, zr)], zsem.at[0])

    def gla_gate(lr):
        gp = jnp.dot(lr.astype(BF16), wg_ref[...], preferred_element_type=F32) + bg_ref[...]
        return jnp.exp(_log_sigmoid(gp) * (1.0 / GLA_TAU))

    @pl.when(i == 0)
    def _():
        zbuf[...] = jnp.zeros_like(zbuf)
        for t in range(t_len // zr):
            zero_copy(t).start()
        for h in range(MLSTM_HEADS):
            kk = ps_ref[:, A_K + HEAD_W * h:A_K + HEAD_W * (h + 1)] * (HEAD_W ** -0.5)
            tt_ref[_T_MK + HEAD_W * h:_T_MK + HEAD_W * (h + 1), :] = kk.T.astype(BF16)
            qq = ps_ref[:, A_Q + HEAD_W * h:A_Q + HEAD_W * (h + 1)]
            tt_ref[_T_MQ + HEAD_W * h:_T_MQ + HEAD_W * (h + 1), :] = qq.T.astype(BF16)
        a_all = gla_gate(ps_ref[:, G_LR:G_LR + LANES])
        for p in range(2):
            pls = slice(LANES * p, LANES * (p + 1))
            tt_ref[_T_GA + LANES * p:_T_GA + LANES * (p + 1), :] = a_all[:, pls].T.astype(BF16)
            kk = ps_ref[:, G_K + LANES * p:G_K + LANES * (p + 1)]
            tt_ref[_T_GK + LANES * p:_T_GK + LANES * (p + 1), :] = kk.T.astype(BF16)
            qq = ps_ref[:, G_Q + LANES * p:G_Q + LANES * (p + 1)] * (GLA_DK ** -0.5)
            tt_ref[_T_GQ + LANES * p:_T_GQ + LANES * (p + 1), :] = qq.T.astype(BF16)

    r0 = pl.multiple_of(i * bb, bb)
    rows = pl.ds(r0, bb)

    pre = ps_ref[rows, A_G:A_G + LANES] + bias_ref[...]
    lfm = _log_sigmoid(pre) + mm_ref[...]
    f_al = pltpu.roll(lfm, LANES - MLSTM_HEADS, 1)
    m_t = jnp.maximum(f_al, pre)
    g_st = jnp.exp(f_al - m_t)
    w_k = jnp.exp(pre - m_t)
    m1_ref[...] = m_t
    n_new = []
    for h in range(MLSTM_HEADS):
        kk = ps_ref[rows, A_K + HEAD_W * h:A_K + HEAD_W * (h + 1)] * (HEAD_W ** -0.5)
        nn = g_st[:, h:h + 1] * n0_ref[:, HEAD_W * h:HEAD_W * (h + 1)] + w_k[:, h:h + 1] * kk
        n1_ref[:, HEAD_W * h:HEAD_W * (h + 1)] = nn
        n_new.append(nn)

    glu = ps_ref[rows, C_A:C_A + GROUP_WIDTH] * _sigmoid(ps_ref[rows, C_G:C_G + GROUP_WIDTH])
    yc_ref[...] = glu * cw_ref[CONV_WIDTH - 1:CONV_WIDTH, :]
    a_v = ps_ref[rows, A_V:A_V + GROUP_WIDTH]
    g_v = ps_ref[rows, G_V:G_V + GROUP_WIDTH]
    s_k = ps_ref[rows, S_K:S_K + LANES]
    s_v = ps_ref[rows, S_V:S_V + LANES]

    lane_b = lax.broadcasted_iota(I32, (nb, LANES), 0)
    key_row = lax.broadcasted_iota(I32, (SWA_HEADS, WINDOW), 1)
    lo_row = lax.broadcasted_iota(I32, (1, LANES), 1) < SWA_HEAD_DIM
    sink_col = sink_ref[:, 0:1]

    for j in range(bb):
        onehot = (lane_b == r0 + j).astype(BF16)
        bc_ref[...] = jnp.dot(tt_ref[...], onehot, preferred_element_type=F32)
        jrow = slice(j, j + 1)
        for h in range(MLSTM_HEADS):
            hs = slice(HEAD_W * h, HEAD_W * (h + 1))
            kbc = bc_ref[_T_MK + HEAD_W * h:_T_MK + HEAD_W * (h + 1), :]
            qbc = bc_ref[_T_MQ + HEAD_W * h:_T_MQ + HEAD_W * (h + 1), :]
            g1 = g_st[jrow, h:h + 1]
            w1 = w_k[jrow, h:h + 1]
            v_row = a_v[jrow, hs]
            c_new = g1 * c0_ref[j, h] + kbc * (w1 * v_row)
            c1_ref[j, h] = c_new
            num_ref[jrow, hs] = jnp.sum(qbc * c_new, axis=0, keepdims=True)
        for p in range(2):
            abc = bc_ref[_T_GA + LANES * p:_T_GA + LANES * (p + 1), :]
            kbc = bc_ref[_T_GK + LANES * p:_T_GK + LANES * (p + 1), :]
            qbc = bc_ref[_T_GQ + LANES * p:_T_GQ + LANES * (p + 1), :]
            for hh in range(2):
                head = 2 * p + hh
                hs = slice(HEAD_W * head, HEAD_W * (head + 1))
                ds_ = slice(GLA_DK * hh, GLA_DK * (hh + 1))
                v_row = g_v[jrow, hs]
                s_new = abc[ds_, :] * s0_ref[j, head] + kbc[ds_, :] * v_row
                s1_ref[j, head] = s_new
                go_ref[jrow, hs] = jnp.sum(qbc[ds_, :] * s_new, axis=0, keepdims=True)
        cache = cv0_ref[j]
        yc_ref[jrow, :] = yc_ref[jrow, :] + jnp.sum(cache * cw_ref[0:CONV_WIDTH - 1, :], axis=0, keepdims=True)
        cv1_ref[j, 0:CONV_WIDTH - 2, :] = cv0_ref[j, 1:CONV_WIDTH - 1, :]
        cv1_ref[j, CONV_WIDTH - 2:CONV_WIDTH - 1, :] = glu[jrow, :]
        k_new = s_k[jrow, :]
        v_new = s_v[jrow, :]
        k1_ref[j, 0:WINDOW - 1, :] = k0_ref[j, 1:WINDOW, :]
        k1_ref[j, WINDOW - 1:WINDOW, :] = k_new
        v1_ref[j, 0:WINDOW - 1, :] = v0_ref[j, 1:WINDOW, :]
        v1_ref[j, WINDOW - 1:WINDOW, :] = v_new
        qmat = qm_ref[j] * (SWA_HEAD_DIM ** -0.5)
        s_old = lax.dot_general(qmat.astype(BF16), k0_ref[j].astype(BF16), (((1,), (1,)), ((), ())),
                                preferred_element_type=F32)
        s_old = jnp.where(key_row >= 1, s_old, -jnp.inf)
        s_cur = jnp.sum(qmat * k_new, axis=-1, keepdims=True)
        mx = jnp.maximum(jnp.maximum(jnp.max(s_old, axis=-1, keepdims=True), s_cur), sink_col)
        p_old = jnp.exp(s_old - mx)
        p_cur = jnp.exp(s_cur - mx)
        den = jnp.sum(p_old, axis=-1, keepdims=True) + p_cur + jnp.exp(sink_col - mx)
        o = jnp.dot((p_old / den).astype(BF16), v0_ref[j].astype(BF16), preferred_element_type=F32)
        o = o + (p_cur / den) * v_new
        o_sw = pltpu.roll(o, SWA_HEAD_DIM, 1)
        for c in range(SWA_HEADS // 2):
            g = c // (SWA_HEADS // SWA_KV_HEADS // 2)
            left = (o if g == 0 else o_sw)[2 * c:2 * c + 1, :]
            right = (o_sw if g == 0 else o)[2 * c + 1:2 * c + 2, :]
            yd_ref[jrow, LANES * c:LANES * (c + 1)] = jnp.where(lo_row, left, right)

    for h in range(MLSTM_HEADS):
        hs = slice(HEAD_W * h, HEAD_W * (h + 1))
        qq = ps_ref[rows, A_Q + HEAD_W * h:A_Q + HEAD_W * (h + 1)]
        den = jnp.sum(qq * n_new[h], axis=-1, keepdims=True)
        hh = num_ref[:, hs] / jnp.maximum(jnp.abs(den), jnp.exp(-m_t[:, h:h + 1]))
        hh = _sigmoid(ps_ref[rows, A_O + HEAD_W * h:A_O + HEAD_W * (h + 1)]) * hh
        hh = hh * lax.rsqrt(jnp.mean(hh * hh, axis=-1, keepdims=True) + NORM_EPS) * mng_ref[:, hs]
        ybuf[rows, hs] = hh
    ybuf[rows, GROUP_WIDTH:2 * GROUP_WIDTH] = _conv_norm_act(yc_ref[...] + cb_ref[...], cg_ref, cbe_ref)
    for head in range(GLA_HEADS):
        hs = slice(HEAD_W * head, HEAD_W * (head + 1))
        oh = go_ref[:, hs]
        oh = oh * lax.rsqrt(jnp.mean(oh * oh, axis=-1, keepdims=True) + NORM_EPS) * gng_ref[:, hs]
        gr = ps_ref[rows, G_R + HEAD_W * head:G_R + HEAD_W * (head + 1)]
        ybuf[rows, 2 * GROUP_WIDTH + HEAD_W * head:2 * GROUP_WIDTH + HEAD_W * (head + 1)] = oh * _silu(gr)
    ybuf[rows, 3 * GROUP_WIDTH:4 * GROUP_WIDTH] = yd_ref[...]

    @pl.when(i == pl.num_programs(0) - 1)
    def _():
        ybf[...] = ybuf[...].astype(BF16)
        cp = pltpu.make_async_copy(ybf, mix_ref.at[pl.ds(t_len, nb)], ysem.at[0])
        cp.start()
        for t in range(t_len // zr):
            zero_copy(t).wait()
        cp.wait()


def _sample_mixers(proj_s, t_len, states, layer, prm):
    c_all, n_all, m_all, s_all, cv_all, k_all, v_all = states
    depth = c_all.shape[0]
    nb = proj_s.shape[0]
    bb = SAMPLE_BB
    assert nb == LANES and nb % bb == 0
    lb = layer * (nb // bb)
    n0, m0 = n_all[layer], m_all[layer]
    mm = jnp.concatenate([m0, m0, jnp.zeros((nb, LANES - 2 * MLSTM_HEADS), F32)], axis=1)
    n0f = n0.reshape(nb, GROUP_WIDTH)
    c0 = c_all.reshape((depth * nb,) + c_all.shape[2:])
    s0 = s_all.reshape((depth * nb,) + s_all.shape[2:])
    cv0 = cv_all.reshape((depth * nb,) + cv_all.shape[2:])
    k0f = k_all.reshape(depth * nb, WINDOW, LANES)
    v0f = v_all.reshape(depth * nb, WINDOW, LANES)
    sq = proj_s[:, S_Q:S_Q + GROUP_WIDTH].reshape(nb, SWA_KV_HEADS, SWA_HEADS // SWA_KV_HEADS, SWA_HEAD_DIM)
    zq = jnp.zeros_like(sq[:, 0])
    qm = jnp.concatenate([jnp.concatenate([sq[:, 0], zq], axis=-1), jnp.concatenate([zq, sq[:, 1]], axis=-1)], axis=1)
    sink_b = jnp.broadcast_to(prm['swa_sinks'].astype(F32)[:, None], (SWA_HEADS, LANES))

    full = lambda shape: pl.BlockSpec(shape, lambda i: (0,) * len(shape))
    rowb = lambda w: pl.BlockSpec((bb, w), lambda i: (i, 0))
    in_specs = [
        full((nb, N_PROJ)), rowb(LANES), full((1, LANES)), rowb(GROUP_WIDTH),
        pl.BlockSpec((bb, MLSTM_HEADS, HEAD_W, HEAD_W), lambda i: (lb + i, 0, 0, 0)),
        pl.BlockSpec((bb, GLA_HEADS, GLA_DK, HEAD_W), lambda i: (lb + i, 0, 0, 0)),
        pl.BlockSpec((bb, CONV_WIDTH - 1, GROUP_WIDTH), lambda i: (lb + i, 0, 0)),
        pl.BlockSpec((bb, WINDOW, LANES), lambda i: (lb + i, 0, 0)),
        pl.BlockSpec((bb, WINDOW, LANES), lambda i: (lb + i, 0, 0)),
        pl.BlockSpec((bb, SWA_HEADS, LANES), lambda i: (i, 0, 0)),
        full((SWA_HEADS, LANES)), full((LANES, 256)), full((1, 256)),
        full((1, GROUP_WIDTH)), full((1, GROUP_WIDTH)),
        full((CONV_WIDTH, GROUP_WIDTH)), full((1, GROUP_WIDTH)), full((1, GROUP_WIDTH)), full((1, GROUP_WIDTH)),
    ]
    out_specs = [
        pl.BlockSpec(memory_space=pl.ANY),
        pl.BlockSpec((bb, MLSTM_HEADS, HEAD_W, HEAD_W), lambda i: (i, 0, 0, 0)),
        rowb(GROUP_WIDTH), rowb(LANES),
        pl.BlockSpec((bb, GLA_HEADS, GLA_DK, HEAD_W), lambda i: (i, 0, 0, 0)),
        pl.BlockSpec((bb, CONV_WIDTH - 1, GROUP_WIDTH), lambda i: (i, 0, 0)),
        pl.BlockSpec((bb, WINDOW, LANES), lambda i: (i, 0, 0)),
        pl.BlockSpec((bb, WINDOW, LANES), lambda i: (i, 0, 0)),
    ]
    out_shape = [
        jax.ShapeDtypeStruct((t_len + nb, D_MODEL), BF16),
        jax.ShapeDtypeStruct(c_all.shape[1:], F32),
        jax.ShapeDtypeStruct((nb, GROUP_WIDTH), F32),
        jax.ShapeDtypeStruct((nb, LANES), F32),
        jax.ShapeDtypeStruct(s_all.shape[1:], F32),
        jax.ShapeDtypeStruct(cv_all.shape[1:], F32),
        jax.ShapeDtypeStruct((nb, WINDOW, LANES), F32),
        jax.ShapeDtypeStruct((nb, WINDOW, LANES), F32),
    ]
    scratch = [
        pltpu.VMEM((_T_ROWS, nb), BF16), pltpu.VMEM((_T_ROWS, LANES), F32),
        pltpu.VMEM((bb, GROUP_WIDTH), F32), pltpu.VMEM((bb, GROUP_WIDTH), F32), pltpu.VMEM((bb, GROUP_WIDTH), F32),
        pltpu.VMEM((bb, GROUP_WIDTH), F32),
        pltpu.VMEM((nb, D_MODEL), F32), pltpu.VMEM((nb, D_MODEL), BF16),
        pltpu.VMEM((_pick_tile(t_len, (512, 256, 128, 64, 16)), D_MODEL), BF16),
        pltpu.SemaphoreType.DMA((1,)), pltpu.SemaphoreType.DMA((1,)),
    ]
    mixed, c1, n1, m1, s1, cv1, k1, v1 = pl.pallas_call(
        _sample_kernel,
        grid=(nb // bb,),
        in_specs=in_specs, out_specs=out_specs, out_shape=out_shape, scratch_shapes=scratch,
        compiler_params=_cparams(("arbitrary",)),
        name="sample_mixers",
    )(proj_s, mm, prm['gate_bias'], n0f, c0, s0, cv0, k0f, v0f, qm, sink_b,
      prm['gla_w_gate_pad'], prm['gla_b_gate'].reshape(1, 256),
      prm['mlstm_norm_g'].reshape(1, -1), prm['gla_norm_g'].reshape(1, -1),
      prm['conv_w'], prm['conv_b'].reshape(1, -1), prm['conv_norm_g'].reshape(1, -1),
      prm['conv_norm_b'].reshape(1, -1))
    new_state = (c1, n1.reshape(n0.shape), m1[:, :MLSTM_HEADS], s1, cv1,
                 k1.reshape(k_all.shape[1:]), v1.reshape(v_all.shape[1:]))
    return mixed, new_state


def _outproj_router_kernel(x_ref, mix_ref, w_ref, g_ref, rw_ref, rb_ref,
                           x1_ref, hn_ref, ri_ref, rf_ref, cnt_ref):
    tm = x_ref.shape[0]

    @pl.when(pl.program_id(0) == 0)
    def _():
        cnt_ref[...] = jnp.zeros_like(cnt_ref)

    x1 = x_ref[...] + jnp.dot(mix_ref[...], w_ref[...], preferred_element_type=F32)
    x1_ref[...] = x1
    ms = jnp.mean(x1 * x1, axis=-1, keepdims=True)
    hn = x1 * lax.rsqrt(ms + NORM_EPS) * g_ref[...]
    hn_ref[...] = hn
    logits = jnp.dot(hn.astype(BF16), rw_ref[...], preferred_element_type=F32) + rb_ref[...]
    lane = lax.broadcasted_iota(I32, (tm, LANES), 1)
    big = jnp.int32(LANES)
    gl = jnp.where(lane < N_GROUPS, logits, -jnp.inf)
    gmax = jnp.max(gl, axis=-1, keepdims=True)
    g_sel = jnp.min(jnp.where(gl == gmax, lane, big), axis=-1, keepdims=True)
    g_w = 1.0 / jnp.sum(jnp.exp(gl - gmax), axis=-1, keepdims=True)
    e_lane = lane - N_GROUPS
    in_grp = (e_lane >= 0) & (e_lane < N_EXPERTS) & ((e_lane // EXPERTS_PER_GROUP) == g_sel)
    el = jnp.where(in_grp, logits, -jnp.inf)
    m1 = jnp.max(el, axis=-1, keepdims=True)
    i1 = jnp.min(jnp.where(el == m1, lane, big), axis=-1, keepdims=True)
    el2 = jnp.where(lane == i1, -jnp.inf, el)
    m2 = jnp.max(el2, axis=-1, keepdims=True)
    i2 = jnp.min(jnp.where(el2 == m2, lane, big), axis=-1, keepdims=True)
    r = jnp.exp(m2 - m1)
    p1 = 1.0 / (1.0 + r)
    gate1 = g_w * p1
    gate2 = g_w * (r * p1)
    sel1 = lane == i1
    sel2 = lane == i2
    onehot = jnp.where(sel1 | sel2, 1.0, 0.0)
    row = lax.broadcasted_iota(I32, (tm, tm), 0)
    col = lax.broadcasted_iota(I32, (tm, tm), 1)
    strict = jnp.where(col < row, 1.0, 0.0).astype(BF16)
    cum = jnp.dot(strict, onehot.astype(BF16), preferred_element_type=F32) + cnt_ref[0:1, :]
    rank1 = jnp.sum(jnp.where(sel1, cum, 0.0), axis=-1, keepdims=True).astype(I32)
    rank2 = jnp.sum(jnp.where(sel2, cum, 0.0), axis=-1, keepdims=True).astype(I32)
    cnt_ref[...] = cnt_ref[...] + jnp.sum(onehot, axis=0, keepdims=True)
    ri = jnp.where(lane == 0, i1 - N_GROUPS, jnp.where(lane == 1, i2 - N_GROUPS,
                   jnp.where(lane == 2, rank1, jnp.where(lane == 3, rank2, 0))))
    ri_ref[...] = ri
    rf_ref[...] = jnp.where(lane == 0, gate1, jnp.where(lane == 1, gate2, 0.0))


def _outproj_router(x, mixed, w_out_bf16, norm_g, rw_pad, rb_pad):
    n = x.shape[0]
    tm = _pick_tile(n, (320, 256, 128, 64, 16))
    full = lambda shape: pl.BlockSpec(shape, lambda i: (0,) * len(shape))
    rowb = lambda w: pl.BlockSpec((tm, w), lambda i: (i, 0))
    return pl.pallas_call(
        _outproj_router_kernel,
        grid=(n // tm,),
        in_specs=[rowb(D_MODEL), rowb(D_MODEL), full((D_MODEL, D_MODEL)), full((1, D_MODEL)),
                  full((D_MODEL, LANES)), full((1, LANES))],
        out_specs=[rowb(D_MODEL), rowb(D_MODEL), rowb(LANES), rowb(LANES), full((SUBLANES, LANES))],
        out_shape=[
            jax.ShapeDtypeStruct((n, D_MODEL), F32),
            jax.ShapeDtypeStruct((n, D_MODEL), F32),
            jax.ShapeDtypeStruct((n, LANES), I32),
            jax.ShapeDtypeStruct((n, LANES), F32),
            jax.ShapeDtypeStruct((SUBLANES, LANES), F32),
        ],
        compiler_params=_cparams(("arbitrary",)),
        name="outproj_router",
    )(x, mixed, w_out_bf16, norm_g.reshape(1, D_MODEL), rw_pad, rb_pad)


def _dispatch_kernel(pos_ref, zt_ref, hn_ref, xs_ref, zbuf, sem, zsem):
    tm = hn_ref.shape[0]
    tile = zbuf.shape[0]
    base = pl.program_id(0) * tm

    @pl.when(pl.program_id(0) == 0)
    def _():
        zbuf[...] = jnp.zeros_like(zbuf)

        def zero_tile(k, carry):
            @pl.when(zt_ref[k] >= 0)
            def _():
                row = pl.multiple_of(zt_ref[k] * tile, tile)
                cp = pltpu.make_async_copy(zbuf, xs_ref.at[pl.ds(row, tile)], zsem.at[0])
                cp.start()
                cp.wait()
            return carry

        lax.fori_loop(0, zt_ref.shape[0], zero_tile, 0)

    def issue(r, carry):
        tok = base + r
        src = hn_ref.at[pl.ds(r, 1)]
        pltpu.make_async_copy(src, xs_ref.at[pl.ds(pos_ref[2 * tok], 1)], sem.at[0]).start(priority=0)
        pltpu.make_async_copy(src, xs_ref.at[pl.ds(pos_ref[2 * tok + 1], 1)], sem.at[1]).start(priority=1)
        return carry

    lax.fori_loop(0, tm, issue, 0, unroll=4)
    pltpu.make_async_copy(hn_ref, xs_ref.at[pl.ds(0, tm)], sem.at[0]).wait()
    pltpu.make_async_copy(hn_ref, xs_ref.at[pl.ds(0, tm)], sem.at[1]).wait()


def _dispatch(pos, zero_tiles, hn, n_rows, tile):
    n = hn.shape[0]
    tm = _pick_tile(n, (256, 128, 64, 8))
    return pl.pallas_call(
        _dispatch_kernel,
        grid_spec=pltpu.PrefetchScalarGridSpec(
            num_scalar_prefetch=2,
            grid=(n // tm,),
            in_specs=[pl.BlockSpec((tm, D_MODEL), lambda i, p, z: (i, 0))],
            out_specs=pl.BlockSpec(memory_space=pl.ANY),
            scratch_shapes=[pltpu.VMEM((tile, D_MODEL), F32), pltpu.SemaphoreType.DMA((2,)),
                            pltpu.SemaphoreType.DMA((1,))],
        ),
        out_shape=jax.ShapeDtypeStruct((n_rows, D_MODEL), F32),
        compiler_params=_cparams(("arbitrary",)),
        name="moe_dispatch",
    )(pos, zero_tiles, hn)


def _expert_kernel(te_ref, nt_ref, x_ref, wg_ref, wu_ref, wd_ref, y_ref):
    del te_ref
    used = pl.program_id(0) < nt_ref[0]

    @pl.when(used)
    def _():
        x = x_ref[...].astype(BF16)
        a = jnp.dot(x, wg_ref[0], preferred_element_type=F32)
        u = jnp.dot(x, wu_ref[0], preferred_element_type=F32)
        hmid = (_silu(a) * u).astype(BF16)
        y_ref[...] = jnp.dot(hmid, wd_ref[0], preferred_element_type=F32)

    @pl.when(jnp.logical_not(used))
    def _():
        y_ref[...] = jnp.zeros_like(y_ref)


def _expert_mlp(tile_expert, n_tiles_used, xs, wg, wu, wd, tile):
    n_tiles = xs.shape[0] // tile

    def row_map(t, te, nt):
        return (jnp.minimum(t, nt[0] - 1), 0)

    def w_map(t, te, nt):
        return (te[jnp.minimum(t, nt[0] - 1)], 0, 0)

    return pl.pallas_call(
        _expert_kernel,
        grid_spec=pltpu.PrefetchScalarGridSpec(
            num_scalar_prefetch=2,
            grid=(n_tiles,),
            in_specs=[
                pl.BlockSpec((tile, D_MODEL), row_map),
                pl.BlockSpec((1, D_MODEL, D_EXPERT), w_map),
                pl.BlockSpec((1, D_MODEL, D_EXPERT), w_map),
                pl.BlockSpec((1, D_EXPERT, D_MODEL), w_map),
            ],
            out_specs=pl.BlockSpec((tile, D_MODEL), lambda t, te, nt: (t, 0)),
        ),
        out_shape=jax.ShapeDtypeStruct(xs.shape, F32),
        compiler_params=_cparams(("arbitrary",)),
        name="expert_mlp",
    )(tile_expert, n_tiles_used, xs, wg, wu, wd)


def _combine_kernel(pos_ref, x1_ref, rf_ref, fg_ref, ys_ref, o_ref, buf_a, buf_b, sem, *, final_norm, row0):
    tm = x1_ref.shape[0]
    base = row0 + pl.program_id(0) * tm

    def issue(r, carry):
        tok = base + r
        pltpu.make_async_copy(ys_ref.at[pl.ds(pos_ref[2 * tok], 1)], buf_a.at[pl.ds(r, 1)],
                              sem.at[0]).start(priority=0)
        pltpu.make_async_copy(ys_ref.at[pl.ds(pos_ref[2 * tok + 1], 1)], buf_b.at[pl.ds(r, 1)],
                              sem.at[1]).start(priority=1)
        return carry

    lax.fori_loop(0, tm, issue, 0, unroll=4)
    pltpu.make_async_copy(ys_ref.at[pl.ds(0, tm)], buf_a, sem.at[0]).wait()
    pltpu.make_async_copy(ys_ref.at[pl.ds(0, tm)], buf_b, sem.at[1]).wait()
    rf = rf_ref[...]
    x2 = x1_ref[...] + rf[:, 0:1] * buf_a[...] + rf[:, 1:2] * buf_b[...]
    if final_norm:
        ms = jnp.mean(x2 * x2, axis=-1, keepdims=True)
        x2 = x2 * lax.rsqrt(ms + NORM_EPS) * fg_ref[...]
    o_ref[...] = x2


def _combine(pos, x1, rf, ys, final_g, final_norm, row0=0, n_rows=None):
    n = x1.shape[0]
    n_rows = n if n_rows is None else n_rows
    tm = _pick_tile(n_rows, (256, 128, 64, 8))
    assert row0 % tm == 0
    b0 = row0 // tm
    return pl.pallas_call(
        functools.partial(_combine_kernel, final_norm=final_norm, row0=row0),
        grid_spec=pltpu.PrefetchScalarGridSpec(
            num_scalar_prefetch=1,
            grid=(n_rows // tm,),
            in_specs=[
                pl.BlockSpec((tm, D_MODEL), lambda i, p: (b0 + i, 0)),
                pl.BlockSpec((tm, LANES), lambda i, p: (b0 + i, 0)),
                pl.BlockSpec((1, D_MODEL), lambda i, p: (0, 0)),
                pl.BlockSpec(memory_space=pl.ANY),
            ],
            out_specs=pl.BlockSpec((tm, D_MODEL), lambda i, p: (i, 0)),
            scratch_shapes=[pltpu.VMEM((tm, D_MODEL), F32), pltpu.VMEM((tm, D_MODEL), F32),
                            pltpu.SemaphoreType.DMA((2,))],
        ),
        out_shape=jax.ShapeDtypeStruct((n_rows, D_MODEL), F32),
        compiler_params=_cparams(("arbitrary",)),
        name="moe_combine",
    )(pos, x1, rf, final_g.reshape(1, D_MODEL), ys)


def _pad_w_in_rows(w_ref, o_ref):
    rows = o_ref.shape[0]
    at = 0
    for lo, hi, dst in sorted(_W_IN_SEGMENTS, key=lambda s: s[2]):
        if dst > at:
            o_ref[:, at:dst] = jnp.zeros((rows, dst - at), o_ref.dtype)
        o_ref[:, dst:dst + (hi - lo)] = w_ref[:, lo:hi].astype(o_ref.dtype)
        at = dst + (hi - lo)
    if at < N_PROJ:
        o_ref[:, at:N_PROJ] = jnp.zeros((rows, N_PROJ - at), o_ref.dtype)


def _w_in_side(w_in_all, layer, n_steps):
    depth, d, n_in = w_in_all.shape
    assert d % n_steps == 0
    tr = d // n_steps
    return _SideCast(
        operand=w_in_all.reshape(depth * d, n_in),
        in_spec=pl.BlockSpec((tr, n_in), lambda i: (layer * n_steps + i, 0)),
        out_spec=pl.BlockSpec((tr, N_PROJ), lambda i: (i, 0)),
        out_shape=jax.ShapeDtypeStruct((d, N_PROJ), BF16),
        shape=(d, N_PROJ),
    )


def _pad_w_in(w_in_all, layer):
    side = _w_in_side(w_in_all, layer, 32)
    return pl.pallas_call(
        _pad_w_in_rows,
        grid=(32,),
        in_specs=[side.in_spec], out_specs=side.out_spec, out_shape=side.out_shape,
        compiler_params=_cparams(("parallel",)),
        name="pad_w_in",
    )(side.operand)


def _moe(x1, hn, ri, rf, counts, experts, final_g, final_norm, tile, t_len):
    n = x1.shape[0]
    n_tiles = -(-2 * n // tile) + N_EXPERTS
    cnt = counts[0, N_GROUPS:N_GROUPS + N_EXPERTS].astype(I32)
    tiles_per = (cnt + tile - 1) // tile
    tile_end = jnp.cumsum(tiles_per)
    row_off = (tile_end - tiles_per) * tile
    pos = (row_off[ri[:, 0:2]] + ri[:, 2:4]).reshape(2 * n)
    tile_ids = jnp.arange(n_tiles, dtype=I32)
    tile_expert = jnp.minimum(jnp.sum((tile_ids[:, None] >= tile_end[None, :]).astype(I32), axis=1), N_EXPERTS - 1)
    n_used = tile_end[N_EXPERTS - 1:N_EXPERTS].astype(I32)
    last_tile = jnp.where(tiles_per > 0, tile_end - 1, -1).astype(I32)
    zero_tiles = jnp.concatenate([last_tile, jnp.where(tile_ids >= n_used[0], tile_ids, -1)])
    xs = _dispatch(pos, zero_tiles, hn, n_tiles * tile, tile)
    ys = _expert_mlp(tile_expert, n_used, xs, *experts, tile)
    if final_norm:
        return (_combine(pos, x1, rf, ys, final_g, True, 0, t_len),
                _combine(pos, x1, rf, ys, final_g, True, t_len, n - t_len))
    return _combine(pos, x1, rf, ys, final_g, False)


def _layer(x, t_len, states, layer, prm, w_in_pad, final_g, final_norm, moe_tile):
    proj = _in_projection(x, prm['norm_mix_g'], w_in_pad)
    mixed, new_s = _sample_mixers(proj[t_len:], t_len, states, layer, prm)
    e_shape = prm['expert_w_gate_all'].shape[1:]
    mixed, cn, m_p, wg = _mlstm_prompt(proj, mixed, t_len, prm['gate_bias'], prm['mlstm_norm_g'],
                                       prm['expert_w_gate_all'], layer)
    mixed, conv_tail, wu = _conv_prompt(proj, mixed, t_len, prm['conv_w'], prm['conv_b'], prm['conv_norm_g'],
                                        prm['conv_norm_b'], prm['expert_w_up_all'], layer)
    mixed, sp, wd = _gla_prompt(proj, mixed, t_len, prm['gla_w_gate_pad'], prm['gla_b_gate'], prm['gla_norm_g'],
                                prm['expert_w_down_all'], layer)
    mixed, next_w_in_pad = _swa_prompt(proj, mixed, t_len, prm['swa_sinks'], prm['w_in_all'],
                                       None if final_norm else layer + 1)
    experts = (wg.reshape(e_shape), wu.reshape(e_shape), wd.reshape(prm['expert_w_down_all'].shape[1:]))
    x1, hn, ri, rf, counts = _outproj_router(x, mixed, prm['w_out'], prm['norm_ffn_g'], prm['router_w'], prm['router_b'])
    x2 = _moe(x1, hn, ri, rf, counts, experts, final_g, final_norm, moe_tile, t_len)
    p_c = cn[None, :, :, :HEAD_W]
    p_n = cn[None, :, :, HEAD_W]
    p_m = m_p[None, :MLSTM_HEADS, 0]
    p_s = jnp.stack([sp[0, :GLA_DK, :HEAD_W], sp[0, GLA_DK:, HEAD_W:],
                     sp[1, :GLA_DK, :HEAD_W], sp[1, GLA_DK:, HEAD_W:]])[None]
    p_conv = conv_tail[None, CONV_HALO - (CONV_WIDTH - 1):]
    p_k = proj[t_len - WINDOW:t_len, S_K:S_K + LANES].reshape(1, WINDOW, SWA_KV_HEADS, SWA_HEAD_DIM)
    p_v = proj[t_len - WINDOW:t_len, S_V:S_V + LANES].reshape(1, WINDOW, SWA_KV_HEADS, SWA_HEAD_DIM)
    return x2, (p_c, p_n, p_m, p_s, p_conv, p_k, p_v), new_s, next_w_in_pad


def _forward(x_prompt, x_sample, states, layer_params, final_norm_g, moe_tile=MOE_TILE):
    t_len = x_prompt.shape[1]
    x = jnp.concatenate([x_prompt[0], x_sample[:, 0]], axis=0)
    new_p, new_s = [], []
    depth = len(layer_params)
    w_in_pad = _pad_w_in(layer_params[0]['w_in_all'], 0)
    for l, prm in enumerate(layer_params):
        x, sp, ss, w_in_pad = _layer(x, t_len, states, l, prm, w_in_pad, final_norm_g, l == depth - 1, moe_tile)
        new_p.append(sp)
        new_s.append(ss)
    y_prompt, y_sample = x
    y_prompt = y_prompt[None]
    y_sample = y_sample[:, None]
    p_states = [jnp.stack(parts) for parts in zip(*new_p)]
    s_states = [jnp.stack(parts) for parts in zip(*new_s)]
    return (y_prompt, y_sample, *p_states, *s_states)


def _prep_layer_params(l, norm_mix_g, w_in, mlstm_b_i, mlstm_b_f, mlstm_norm_g, conv_w, conv_b, conv_norm_g,
                       conv_norm_b, gla_w_gate, gla_b_gate, gla_norm_g, swa_sinks, w_out, norm_ffn_g,
                       router_group_w, router_group_b, router_expert_w, router_expert_b, expert_w_gate,
                       expert_w_up, expert_w_down):
    gate_bias = jnp.concatenate([mlstm_b_i[l], mlstm_b_f[l], jnp.zeros((LANES - 2 * MLSTM_HEADS,), F32)])
    rw = jnp.concatenate([router_group_w[l], router_expert_w[l],
                          jnp.zeros((D_MODEL, LANES - N_GROUPS - N_EXPERTS), F32)], axis=1)
    rb = jnp.concatenate([router_group_b[l], router_expert_b[l],
                          jnp.zeros((LANES - N_GROUPS - N_EXPERTS,), F32)])
    return {
        'norm_mix_g': norm_mix_g[l],
        'w_in_all': w_in,
        'gate_bias': gate_bias.reshape(1, LANES),
        'mlstm_norm_g': mlstm_norm_g[l],
        'conv_w': conv_w[l], 'conv_b': conv_b[l], 'conv_norm_g': conv_norm_g[l], 'conv_norm_b': conv_norm_b[l],
        'gla_w_gate_pad': jnp.concatenate(
            [gla_w_gate[l], jnp.zeros((LANES - GLA_LOWRANK, GLA_HEADS * GLA_DK), F32)], axis=0).astype(BF16),
        'gla_b_gate': gla_b_gate[l], 'gla_norm_g': gla_norm_g[l],
        'swa_sinks': swa_sinks[l],
        'w_out': w_out[l].astype(BF16),
        'norm_ffn_g': norm_ffn_g[l],
        'router_w': rw.astype(BF16), 'router_b': rb.reshape(1, LANES),
        'expert_w_gate_all': expert_w_gate, 'expert_w_up_all': expert_w_up, 'expert_w_down_all': expert_w_down,
    }


def kernel(x_prompt, x_sample, state_mlstm_C, state_mlstm_n, state_mlstm_m, state_gla_S, cache_conv, cache_swa_k, cache_swa_v, norm_mix_g, w_in, mlstm_b_i, mlstm_b_f, mlstm_norm_g, conv_w, conv_b, conv_norm_g, conv_norm_b, gla_w_gate, gla_b_gate, gla_norm_g, swa_sinks, w_out, norm_ffn_g, router_group_w, router_group_b, router_expert_w, router_expert_b, expert_w_gate, expert_w_up, expert_w_down, final_norm_g):
    depth = w_in.shape[0]
    weights = (norm_mix_g, w_in, mlstm_b_i, mlstm_b_f, mlstm_norm_g, conv_w, conv_b, conv_norm_g, conv_norm_b,
               gla_w_gate, gla_b_gate, gla_norm_g, swa_sinks, w_out, norm_ffn_g, router_group_w, router_group_b,
               router_expert_w, router_expert_b, expert_w_gate, expert_w_up, expert_w_down)
    layer_params = [_prep_layer_params(l, *weights) for l in range(depth)]
    states = (state_mlstm_C, state_mlstm_n, state_mlstm_m, state_gla_S, cache_conv, cache_swa_k, cache_swa_v)
    return _forward(x_prompt, x_sample, states, layer_params, final_norm_g)
```

```python
import functools
from typing import NamedTuple

import jax
import jax.numpy as jnp
from jax import lax
from jax.experimental import pallas as pl
from jax.experimental.pallas import tpu as pltpu

F32 = jnp.float32
BF16 = jnp.bfloat16
I32 = jnp.int32
HIGHEST = lax.Precision.HIGHEST

D_MODEL = 2048
GROUP_WIDTH = 512
HEAD_W = 128
MLSTM_HEADS = 4
GLA_HEADS = 4
GLA_DK = 64
GLA_LOWRANK = 16
GLA_TAU = 16.0
CONV_WIDTH = 31
SWA_HEADS = 8
SWA_KV_HEADS = 2
SWA_HEAD_DIM = 64
WINDOW = 128
N_GROUPS = 4
EXPERTS_PER_GROUP = 4
N_EXPERTS = 16
D_EXPERT = 1024
NORM_EPS = 1e-6
LANES = 128
SUBLANES = 8
TOK_ROWS = D_MODEL // LANES
PACK_ROWS = TOK_ROWS // 2

A_Q, A_K, A_V, A_O = 0, 512, 1024, 1536
C_A, C_G = 2048, 2560
G_V, G_R = 3072, 3584
S_Q = 4096
G_Q, G_K = 4608, 4864
A_G, G_LR, S_K, S_V = 5120, 5248, 5376, 5504
N_PROJ = 5632
MIX_COL_MLSTM, MIX_COL_CONV, MIX_COL_GLA, MIX_COL_SWA = 0, 1, 2, 3
_W_IN_SEGMENTS = (
    (0, 512, A_Q), (512, 1024, A_K), (1024, 1536, A_V), (1536, 2048, A_O),
    (2048, 2056, A_G),
    (2056, 2568, C_A), (2568, 3080, C_G),
    (3080, 3336, G_Q), (3336, 3592, G_K), (3592, 4104, G_V), (4104, 4616, G_R),
    (4616, 4632, G_LR),
    (4632, 5144, S_Q), (5144, 5272, S_K), (5272, 5400, S_V),
)

MLSTM_CHUNK = 128
MLSTM_STEP_ROWS = 256
GLA_CHUNK = 64
GLA_STEP_ROWS = 256
GLA_SUB = 16
SWA_STEP_ROWS = 256
CONV_TILE = 256
CONV_ROWS = 64
CONV_HALO = 32
SAMPLE_BB = 8
MOE_TILE = 256
VMEM_LIMIT = 56 * 1024 * 1024


def _cparams(sem, vmem=VMEM_LIMIT):
    return pltpu.CompilerParams(dimension_semantics=sem, vmem_limit_bytes=vmem)


def _log_sigmoid(x):
    return jnp.minimum(x, 0.0) - jnp.log1p(jnp.exp(-jnp.abs(x)))


def _sigmoid(x):
    return 1.0 / (1.0 + jnp.exp(-x))


def _silu(x):
    return x * _sigmoid(x)


def _masked_row_sums(mask, x):
    m = jnp.where(mask, 1.0, 0.0).astype(BF16)
    hi = x.astype(BF16)
    r1 = x - hi.astype(F32)
    mid = r1.astype(BF16)
    lo = (r1 - mid.astype(F32)).astype(BF16)
    out = jnp.dot(m, hi, preferred_element_type=F32)
    out = out + jnp.dot(m, mid, preferred_element_type=F32)
    return out + jnp.dot(m, lo, preferred_element_type=F32)


def _pick_tile(n, candidates):
    for c in candidates:
        if n % c == 0:
            return c
    raise ValueError(f"no tile for {n} in {candidates}")


class _SideCast(NamedTuple):
    operand: jax.Array
    in_spec: pl.BlockSpec
    out_spec: pl.BlockSpec
    out_shape: jax.ShapeDtypeStruct
    shape: tuple


def _side_cast(w_all, layer, n_steps):
    depth, n_e, k, f = w_all.shape
    rows = n_e * k
    assert rows % n_steps == 0, (rows, n_steps)
    tr = rows // n_steps
    return _SideCast(
        operand=w_all.reshape(depth * rows, f),
        in_spec=pl.BlockSpec((tr, f), lambda i: (layer * n_steps + i, 0)),
        out_spec=pl.BlockSpec((tr, f), lambda i: (i, 0)),
        out_shape=jax.ShapeDtypeStruct((rows, f), BF16),
        shape=(n_e, k, f),
    )


def _proj_kernel(x_ref, g_ref, w_ref, o_ref, hn_ref):
    @pl.when(pl.program_id(1) == 0)
    def _():
        x = x_ref[...]
        ms = jnp.mean(x * x, axis=-1, keepdims=True)
        hn_ref[...] = (x * lax.rsqrt(ms + NORM_EPS) * g_ref[...]).astype(BF16)

    o_ref[...] = jnp.dot(hn_ref[...], w_ref[...], preferred_element_type=F32)


def _in_projection(x, g, w_bf16):
    n = x.shape[0]
    tm = _pick_tile(n, (832, 640, 512, 256, 128, 64, 8))
    tn = N_PROJ // 4
    return pl.pallas_call(
        _proj_kernel,
        grid=(n // tm, N_PROJ // tn),
        in_specs=[
            pl.BlockSpec((tm, D_MODEL), lambda i, j: (i, 0)),
            pl.BlockSpec((1, D_MODEL), lambda i, j: (0, 0)),
            pl.BlockSpec((D_MODEL, tn), lambda i, j: (0, j)),
        ],
        out_specs=pl.BlockSpec((tm, tn), lambda i, j: (i, j)),
        out_shape=jax.ShapeDtypeStruct((n, N_PROJ), F32),
        scratch_shapes=[pltpu.VMEM((tm, D_MODEL), BF16)],
        compiler_params=_cparams(("parallel", "arbitrary")),
        name="in_projection",
    )(x, g.reshape(1, D_MODEL), w_bf16)


def _mlstm_prompt_kernel(q_ref, k_ref, v_ref, o_ref, gt_ref, bias_ref, ng_ref, mixed_in_ref, wc_in_ref,
                         y_ref, cn_ref, m_ref, wc_out_ref):
    del mixed_in_ref
    wc_out_ref[...] = wc_in_ref[...].astype(wc_out_ref.dtype)
    rows_step = q_ref.shape[0]
    L = min(MLSTM_CHUNK, rows_step)
    n_chunks = rows_step // L

    @pl.when(pl.program_id(0) == 0)
    def _():
        cn_ref[...] = jnp.zeros_like(cn_ref)
        m_ref[...] = jnp.zeros_like(m_ref)

    pre = gt_ref[...] + bias_ref[...]
    lf = _log_sigmoid(pre)
    row = lax.broadcasted_iota(I32, (rows_step, rows_step), 0)
    col = lax.broadcasted_iota(I32, (rows_step, rows_step), 1)
    cum = ((row // L) == (col // L)) & (col <= row)
    b_all = _masked_row_sums(cum, lf)
    pre_t = pre.T
    b_t = b_all.T
    trow = lax.broadcasted_iota(I32, (L, L), 0)
    tcol = lax.broadcasted_iota(I32, (L, L), 1)
    tri = tcol <= trow
    lane = lax.broadcasted_iota(I32, (L, HEAD_W), 1)
    ones_col = (lane == 0).astype(BF16)
    for h in range(MLSTM_HEADS):
        sl = slice(HEAD_W * h, HEAD_W * (h + 1))
        m_prev = m_ref[h:h + 1, 0:1]
        cn = cn_ref[h]
        for c in range(n_chunks):
            cs = slice(L * c, L * (c + 1))
            q = q_ref[cs, sl]
            k = k_ref[cs, sl] * (HEAD_W ** -0.5)
            v = v_ref[cs, sl]
            b_col = b_all[cs, 4 + h:5 + h]
            i_col = pre[cs, h:h + 1]
            b_row = b_t[4 + h:5 + h, cs]
            i_row = pre_t[h:h + 1, cs]
            log_d = jnp.where(tri, b_col - b_row + i_row, -jnp.inf)
            log_inter = b_col + m_prev
            m_t = jnp.maximum(log_inter, jnp.max(log_d, axis=-1, keepdims=True))
            d_mat = jnp.exp(log_d - m_t)
            g_inter = jnp.exp(log_inter - m_t)
            qb = q.astype(BF16)
            kb = k.astype(BF16)
            s = lax.dot_general(qb, kb, (((1,), (1,)), ((), ())), preferred_element_type=F32)
            w = (s * d_mat).astype(BF16)
            v1 = jnp.concatenate([v.astype(BF16), ones_col], axis=1)
            nd = g_inter * jnp.dot(qb, cn.astype(BF16), preferred_element_type=F32)
            nd = nd + jnp.dot(w, v1, preferred_element_type=F32)
            num = nd[:, :HEAD_W]
            den = nd[:, HEAD_W:HEAD_W + 1]
            hh = num / jnp.maximum(jnp.abs(den), jnp.exp(-m_t))
            hh = _sigmoid(o_ref[cs, sl]) * hh
            hh = hh * lax.rsqrt(jnp.mean(hh * hh, axis=-1, keepdims=True) + NORM_EPS) * ng_ref[:, sl]
            y_ref[cs, sl] = hh.astype(y_ref.dtype)
            m_last = m_t[L - 1:L, :]
            b_last = b_col[L - 1:L, :]
            g_state = jnp.exp(b_last + m_prev - m_last)
            w_k = jnp.exp(b_last - b_col + i_col - m_last)
            kw = (k * w_k).astype(BF16)
            upd = lax.dot_general(kw, v1, (((0,), (0,)), ((), ())), preferred_element_type=F32)
            cn = g_state * cn + upd
            m_prev = m_last
        cn_ref[h] = cn
        m_ref[h:h + 1, :] = jnp.broadcast_to(m_prev, (1, LANES))


def _mlstm_prompt(proj, mixed, t_len, bias_row, norm_g, w_cast, layer):
    L = _pick_tile(t_len, (MLSTM_STEP_ROWS, MLSTM_CHUNK, 64, 32, 16, 8))
    side = _side_cast(w_cast, layer, t_len // L)

    def col(off):
        return pl.BlockSpec((L, GROUP_WIDTH), lambda i, o=off: (i, o // GROUP_WIDTH))

    return pl.pallas_call(
        _mlstm_prompt_kernel,
        grid=(t_len // L,),
        in_specs=[
            col(A_Q), col(A_K), col(A_V), col(A_O),
            pl.BlockSpec((L, LANES), lambda i: (i, A_G // LANES)),
            pl.BlockSpec((1, LANES), lambda i: (0, 0)),
            pl.BlockSpec((1, GROUP_WIDTH), lambda i: (0, 0)),
            pl.BlockSpec(memory_space=pl.ANY),
            side.in_spec,
        ],
        out_specs=[
            pl.BlockSpec((L, GROUP_WIDTH), lambda i: (i, MIX_COL_MLSTM)),
            pl.BlockSpec((MLSTM_HEADS, HEAD_W, 2 * HEAD_W), lambda i: (0, 0, 0)),
            pl.BlockSpec((SUBLANES, LANES), lambda i: (0, 0)),
            side.out_spec,
        ],
        out_shape=[
            jax.ShapeDtypeStruct(mixed.shape, mixed.dtype),
            jax.ShapeDtypeStruct((MLSTM_HEADS, HEAD_W, 2 * HEAD_W), F32),
            jax.ShapeDtypeStruct((SUBLANES, LANES), F32),
            side.out_shape,
        ],
        input_output_aliases={7: 0},
        compiler_params=_cparams(("arbitrary",)),
        name="mlstm_prompt",
    )(proj, proj, proj, proj, proj, bias_row, norm_g.reshape(1, GROUP_WIDTH), mixed, side.operand)


def _gla_prompt_kernel(q_ref, k_ref, v_ref, r_ref, lr_ref, wg_ref, bg_ref, ng_ref, mixed_in_ref, wc_in_ref,
                       y_ref, sp_ref, wc_out_ref):
    del mixed_in_ref
    wc_out_ref[...] = wc_in_ref[...].astype(wc_out_ref.dtype)
    rows_step = q_ref.shape[0]
    L = min(GLA_CHUNK, rows_step)
    n_chunks = rows_step // L
    n_sub = L // GLA_SUB

    @pl.when(pl.program_id(0) == 0)
    def _():
        sp_ref[...] = jnp.zeros_like(sp_ref)

    gate_pre = jnp.dot(lr_ref[...].astype(BF16), wg_ref[...], preferred_element_type=F32) + bg_ref[...]
    log_a = _log_sigmoid(gate_pre) * (1.0 / GLA_TAU)
    row = lax.broadcasted_iota(I32, (rows_step, rows_step), 0)
    col = lax.broadcasted_iota(I32, (rows_step, rows_step), 1)
    tri = ((row // L) == (col // L)) & (col <= row)
    b = _masked_row_sums(tri, log_a)
    q = q_ref[...] * (GLA_DK ** -0.5)
    k = k_ref[...]
    b_last = jnp.concatenate(
        [jnp.broadcast_to(b[L * c + L - 1:L * c + L, :], (L, b.shape[1])) for c in range(n_chunks)], axis=0)
    q_in = q * jnp.exp(b)
    k_dec = k * jnp.exp(b_last - b)
    lane16 = lax.broadcasted_iota(I32, (GLA_SUB, LANES), 1)
    lo16 = lane16 < GLA_DK
    srow = lax.broadcasted_iota(I32, (LANES, 2 * HEAD_W), 0)
    scol = lax.broadcasted_iota(I32, (LANES, 2 * HEAD_W), 1)
    block_diag = (srow < GLA_DK) == (scol < HEAD_W)
    for p in range(2):
        pls = slice(LANES * p, LANES * (p + 1))
        vp = v_ref[:, 2 * HEAD_W * p:2 * HEAD_W * (p + 1)].astype(BF16)
        b_t = b[:, pls].T
        sp = sp_ref[p]
        states = []
        for c in range(n_chunks):
            cs = slice(L * c, L * (c + 1))
            states.append(sp)
            dec_col = jnp.exp(b_t[:, L * c + L - 1:L * c + L])
            upd = lax.dot_general(k_dec[cs, pls].astype(BF16), vp[cs], (((0,), (0,)), ((), ())),
                                  preferred_element_type=F32)
            sp = jnp.where(block_diag, dec_col * sp + upd, 0.0)
        sp_ref[p] = sp
        outs = []
        for c in range(n_chunks):
            c0 = L * c
            o_inter = jnp.dot(q_in[c0:c0 + L, pls].astype(BF16), states[c].astype(BF16),
                              preferred_element_type=F32)
            rows = []
            for blk in range(n_sub):
                r0 = GLA_SUB * blk
                n = GLA_SUB * (blk + 1)
                qrows = slice(c0 + r0, c0 + r0 + GLA_SUB)
                krows = slice(c0, c0 + n)
                if blk == 0:
                    qs = q[qrows, pls] * jnp.exp(b[qrows, pls])
                    ks = k[krows, pls] * jnp.exp(-b[krows, pls])
                else:
                    anchor = b[c0 + r0 - 1:c0 + r0, pls]
                    qs = q[qrows, pls] * jnp.exp(b[qrows, pls] - anchor)
                    ks = k[krows, pls] * jnp.exp(anchor - b[krows, pls])
                qs2 = jnp.concatenate([jnp.where(lo16, qs, 0.0), jnp.where(lo16, 0.0, qs)], axis=0)
                att = lax.dot_general(qs2.astype(BF16), ks.astype(BF16), (((1,), (1,)), ((), ())),
                                      preferred_element_type=F32)
                trow = lax.broadcasted_iota(I32, (2 * GLA_SUB, n), 0)
                tcol = lax.broadcasted_iota(I32, (2 * GLA_SUB, n), 1)
                t_idx = r0 + jnp.where(trow >= GLA_SUB, trow - GLA_SUB, trow)
                att = jnp.where(tcol <= t_idx, att, 0.0)
                o2 = jnp.dot(att.astype(BF16), vp[krows], preferred_element_type=F32)
                rows.append(jnp.concatenate([o2[:GLA_SUB, :HEAD_W], o2[GLA_SUB:, HEAD_W:]], axis=1))
            outs.append(o_inter + jnp.concatenate(rows, axis=0))
        o = jnp.concatenate(outs, axis=0)
        for hh in range(2):
            head = 2 * p + hh
            hs = slice(HEAD_W * head, HEAD_W * (head + 1))
            oh = o[:, HEAD_W * hh:HEAD_W * (hh + 1)]
            oh = oh * lax.rsqrt(jnp.mean(oh * oh, axis=-1, keepdims=True) + NORM_EPS) * ng_ref[:, hs]
            y_ref[:, hs] = (oh * _silu(r_ref[:, hs])).astype(y_ref.dtype)


def _gla_prompt(proj, mixed, t_len, w_gate_pad, b_gate, norm_g, w_cast, layer):
    L = _pick_tile(t_len, (GLA_STEP_ROWS, GLA_CHUNK))
    side = _side_cast(w_cast, layer, t_len // L)
    return pl.pallas_call(
        _gla_prompt_kernel,
        grid=(t_len // L,),
        in_specs=[
            pl.BlockSpec((L, 256), lambda i: (i, G_Q // 256)),
            pl.BlockSpec((L, 256), lambda i: (i, G_K // 256)),
            pl.BlockSpec((L, GROUP_WIDTH), lambda i: (i, G_V // GROUP_WIDTH)),
            pl.BlockSpec((L, GROUP_WIDTH), lambda i: (i, G_R // GROUP_WIDTH)),
            pl.BlockSpec((L, LANES), lambda i: (i, G_LR // LANES)),
            pl.BlockSpec((LANES, 256), lambda i: (0, 0)),
            pl.BlockSpec((1, 256), lambda i: (0, 0)),
            pl.BlockSpec((1, GROUP_WIDTH), lambda i: (0, 0)),
            pl.BlockSpec(memory_space=pl.ANY),
            side.in_spec,
        ],
        out_specs=[
            pl.BlockSpec((L, GROUP_WIDTH), lambda i: (i, MIX_COL_GLA)),
            pl.BlockSpec((2, LANES, 2 * HEAD_W), lambda i: (0, 0, 0)),
            side.out_spec,
        ],
        out_shape=[
            jax.ShapeDtypeStruct(mixed.shape, mixed.dtype),
            jax.ShapeDtypeStruct((2, LANES, 2 * HEAD_W), F32),
            side.out_shape,
        ],
        input_output_aliases={8: 0},
        compiler_params=_cparams(("arbitrary",)),
        name="gla_prompt",
    )(proj, proj, proj, proj, proj, w_gate_pad, b_gate.reshape(1, 256), norm_g.reshape(1, GROUP_WIDTH), mixed,
      side.operand)


def _conv_norm_act(y, g_ref, be_ref):
    mu = jnp.mean(y, axis=-1, keepdims=True)
    yc = y - mu
    var = jnp.mean(yc * yc, axis=-1, keepdims=True)
    return _silu(yc * lax.rsqrt(var + NORM_EPS) * g_ref[...] + be_ref[...])


def _conv_prompt_kernel(ua_ref, ug_ref, ha_ref, hg_ref, w_ref, b_ref, g_ref, be_ref, mixed_in_ref, wc_in_ref,
                        y_ref, tail_ref, wc_out_ref, buf_ref, sh_ref):
    del mixed_in_ref
    wc_out_ref[...] = wc_in_ref[...].astype(wc_out_ref.dtype)
    tt = ua_ref.shape[0]
    span = tt + CONV_HALO
    halo = ha_ref[...] * _sigmoid(hg_ref[...])
    buf_ref[0:CONV_HALO, :] = jnp.where(pl.program_id(0) > 0, halo, 0.0)
    buf_ref[CONV_HALO:span, :] = ua_ref[...] * _sigmoid(ug_ref[...])
    buf_ref[span:span + SUBLANES, :] = jnp.zeros((SUBLANES, GROUP_WIDTH), F32)
    for k in range(1, SUBLANES):
        sh_ref[k] = buf_ref[k:k + span, :]
    base = CONV_HALO - (CONV_WIDTH - 1)
    for r in range(tt // CONV_ROWS):
        acc = jnp.zeros((CONV_ROWS, GROUP_WIDTH), F32)
        for j in range(CONV_WIDTH):
            s0 = r * CONV_ROWS + base + j
            k = s0 % SUBLANES
            a0 = s0 - k
            win = buf_ref[a0:a0 + CONV_ROWS, :] if k == 0 else sh_ref[k, a0:a0 + CONV_ROWS, :]
            acc = acc + w_ref[j:j + 1, :] * win
        y = _conv_norm_act(acc + b_ref[...], g_ref, be_ref)
        y_ref[r * CONV_ROWS:(r + 1) * CONV_ROWS, :] = y.astype(y_ref.dtype)
    tail_ref[...] = buf_ref[tt:span, :]


def _conv_prompt(proj, mixed, t_len, w, b, g, beta, w_cast, layer):
    tt = _pick_tile(t_len, (CONV_TILE, 128, 64))
    ratio = tt // CONV_HALO
    side = _side_cast(w_cast, layer, t_len // tt)
    vec = lambda: pl.BlockSpec((1, GROUP_WIDTH), lambda i: (0, 0))
    return pl.pallas_call(
        _conv_prompt_kernel,
        grid=(t_len // tt,),
        in_specs=[
            pl.BlockSpec((tt, GROUP_WIDTH), lambda i: (i, C_A // GROUP_WIDTH)),
            pl.BlockSpec((tt, GROUP_WIDTH), lambda i: (i, C_G // GROUP_WIDTH)),
            pl.BlockSpec((CONV_HALO, GROUP_WIDTH), lambda i: (jnp.maximum(i * ratio - 1, 0), C_A // GROUP_WIDTH)),
            pl.BlockSpec((CONV_HALO, GROUP_WIDTH), lambda i: (jnp.maximum(i * ratio - 1, 0), C_G // GROUP_WIDTH)),
            pl.BlockSpec((CONV_WIDTH, GROUP_WIDTH), lambda i: (0, 0)),
            vec(), vec(), vec(),
            pl.BlockSpec(memory_space=pl.ANY),
            side.in_spec,
        ],
        out_specs=[
            pl.BlockSpec((tt, GROUP_WIDTH), lambda i: (i, MIX_COL_CONV)),
            pl.BlockSpec((CONV_HALO, GROUP_WIDTH), lambda i: (0, 0)),
            side.out_spec,
        ],
        out_shape=[
            jax.ShapeDtypeStruct(mixed.shape, mixed.dtype),
            jax.ShapeDtypeStruct((CONV_HALO, GROUP_WIDTH), F32),
            side.out_shape,
        ],
        scratch_shapes=[pltpu.VMEM((tt + CONV_HALO + SUBLANES, GROUP_WIDTH), F32),
                        pltpu.VMEM((SUBLANES, tt + CONV_HALO, GROUP_WIDTH), F32)],
        input_output_aliases={8: 0},
        compiler_params=_cparams(("arbitrary",)),
        name="conv_prompt",
    )(proj, proj, proj, proj, w, b.reshape(1, -1), g.reshape(1, -1), beta.reshape(1, -1), mixed, side.operand)


def _swa_prompt_kernel(sink_ref, q_ref, kc_ref, vc_ref, kp_ref, vp_ref, mixed_in_ref, *rest):
    del mixed_in_ref
    if len(rest) == 3:
        w_ref, y_ref, wpad_ref = rest
        _pad_w_in_rows(w_ref, wpad_ref)
    else:
        (y_ref,) = rest
    bq = WINDOW
    n_blk = q_ref.shape[0] // bq
    first = pl.program_id(0) == 0
    k_full = jnp.concatenate([kp_ref[...], kc_ref[...]], axis=0)
    v_full = jnp.concatenate([vp_ref[...], vc_ref[...]], axis=0)
    k_sw_full = pltpu.roll(k_full, SWA_HEAD_DIM, 1).astype(BF16)
    v_sw_full = pltpu.roll(v_full, SWA_HEAD_DIM, 1).astype(BF16)
    k_full = k_full.astype(BF16)
    v_full = v_full.astype(BF16)
    tq = lax.broadcasted_iota(I32, (bq, 2 * bq), 0)
    kj = lax.broadcasted_iota(I32, (bq, 2 * bq), 1)
    band = (kj > tq) & (kj <= tq + WINDOW)
    lane = lax.broadcasted_iota(I32, (bq, LANES), 1)
    lo = lane < SWA_HEAD_DIM
    rep = SWA_HEADS // SWA_KV_HEADS
    for blk in range(n_blk):
        qs = slice(bq * blk, bq * (blk + 1))
        ks = slice(bq * blk, bq * (blk + 2))
        valid = band & (kj >= jnp.where(first, bq, 0)) if blk == 0 else band
        k_all, v_all, k_sw, v_sw = k_full[ks], v_full[ks], k_sw_full[ks], v_sw_full[ks]
        for c in range(SWA_HEADS // 2):
            qc = q_ref[qs, LANES * c:LANES * (c + 1)] * (SWA_HEAD_DIM ** -0.5)
            outs = []
            for hh in range(2):
                h = 2 * c + hh
                g = h // rep
                qm = jnp.where(lo if hh == 0 else jnp.logical_not(lo), qc, 0.0).astype(BF16)
                k_use = k_all if g == hh else k_sw
                v_use = v_all if g == hh else v_sw
                s = lax.dot_general(qm, k_use, (((1,), (1,)), ((), ())), preferred_element_type=F32)
                s = jnp.where(valid, s, -jnp.inf)
                sink = sink_ref[h]
                mx = jnp.maximum(jnp.max(s, axis=-1, keepdims=True), sink)
                p = jnp.exp(s - mx)
                den = jnp.sum(p, axis=-1, keepdims=True) + jnp.exp(sink - mx)
                p = (p / den).astype(BF16)
                outs.append(jnp.dot(p, v_use, preferred_element_type=F32))
            y_ref[qs, LANES * c:LANES * (c + 1)] = jnp.where(lo, outs[0], outs[1]).astype(y_ref.dtype)


def _swa_prompt(proj, mixed, t_len, sinks, w_in_all=None, next_layer=None):
    bq = WINDOW
    rows = _pick_tile(t_len, (SWA_STEP_ROWS, bq))
    ratio = rows // bq
    cur = lambda off: pl.BlockSpec((rows, LANES), lambda i: (i, off // LANES))
    prev = lambda off: pl.BlockSpec((bq, LANES), lambda i: (jnp.maximum(i * ratio - 1, 0), off // LANES))
    in_specs = [
        pl.BlockSpec(memory_space=pltpu.SMEM),
        pl.BlockSpec((rows, GROUP_WIDTH), lambda i: (i, S_Q // GROUP_WIDTH)),
        cur(S_K), cur(S_V), prev(S_K), prev(S_V),
        pl.BlockSpec(memory_space=pl.ANY),
    ]
    out_specs = [pl.BlockSpec((rows, GROUP_WIDTH), lambda i: (i, MIX_COL_SWA))]
    out_shape = [jax.ShapeDtypeStruct(mixed.shape, mixed.dtype)]
    operands = [sinks, proj, proj, proj, proj, proj, mixed]
    if next_layer is not None:
        side = _w_in_side(w_in_all, next_layer, t_len // rows)
        in_specs.append(side.in_spec)
        out_specs.append(side.out_spec)
        out_shape.append(side.out_shape)
        operands.append(side.operand)
    outs = pl.pallas_call(
        _swa_prompt_kernel,
        grid=(t_len // rows,),
        in_specs=in_specs, out_specs=out_specs, out_shape=out_shape,
        input_output_aliases={6: 0},
        compiler_params=_cparams(("arbitrary",)),
        name="swa_prompt",
    )(*operands)
    return outs[0], (outs[1] if next_layer is not None else None)


_T_MK, _T_MQ = 0, 512
_T_GA, _T_GK, _T_GQ = 1024, 1280, 1536
_T_ROWS = 1792


def _sample_kernel(ps_ref, mm_ref, bias_ref, n0_ref, c0_ref, s0_ref, cv0_ref, k0_ref, v0_ref,
                   qm_ref, sink_ref, wg_ref, bg_ref, mng_ref, gng_ref, cw_ref, cb_ref, cg_ref, cbe_ref,
                   mix_ref, c1_ref, n1_ref, m1_ref, s1_ref, cv1_ref, k1_ref, v1_ref,
                   tt_ref, bc_ref, num_ref, go_ref, yc_ref, yd_ref, ybuf, ybf, zbuf, zsem, ysem):
    i = pl.program_id(0)
    nb = ps_ref.shape[0]
    bb = c0_ref.shape[0]
    t_len = mix_ref.shape[0] - nb
    zr = zbuf.shape[0]

    def zero_copy(t):
        return pltpu.make_async_copy(zbuf, mix_ref.at[pl.ds(t * zr, zr)], zsem.at[0])

    def gla_gate(lr):
        gp = jnp.dot(lr.astype(BF16), wg_ref[...], preferred_element_type=F32) + bg_ref[...]
        return jnp.exp(_log_sigmoid(gp) * (1.0 / GLA_TAU))

    @pl.when(i == 0)
    def _():
        zbuf[...] = jnp.zeros_like(zbuf)
        for t in range(t_len // zr):
            zero_copy(t).start()
        for h in range(MLSTM_HEADS):
            kk = ps_ref[:, A_K + HEAD_W * h:A_K + HEAD_W * (h + 1)] * (HEAD_W ** -0.5)
            tt_ref[_T_MK + HEAD_W * h:_T_MK + HEAD_W * (h + 1), :] = kk.T.astype(BF16)
            qq = ps_ref[:, A_Q + HEAD_W * h:A_Q + HEAD_W * (h + 1)]
            tt_ref[_T_MQ + HEAD_W * h:_T_MQ + HEAD_W * (h + 1), :] = qq.T.astype(BF16)
        a_all = gla_gate(ps_ref[:, G_LR:G_LR + LANES])
        for p in range(2):
            pls = slice(LANES * p, LANES * (p + 1))
            tt_ref[_T_GA + LANES * p:_T_GA + LANES * (p + 1), :] = a_all[:, pls].T.astype(BF16)
            kk = ps_ref[:, G_K + LANES * p:G_K + LANES * (p + 1)]
            tt_ref[_T_GK + LANES * p:_T_GK + LANES * (p + 1), :] = kk.T.astype(BF16)
            qq = ps_ref[:, G_Q + LANES * p:G_Q + LANES * (p + 1)] * (GLA_DK ** -0.5)
            tt_ref[_T_GQ + LANES * p:_T_GQ + LANES * (p + 1), :] = qq.T.astype(BF16)

    r0 = pl.multiple_of(i * bb, bb)
    rows = pl.ds(r0, bb)

    pre = ps_ref[rows, A_G:A_G + LANES] + bias_ref[...]
    lfm = _log_sigmoid(pre) + mm_ref[...]
    f_al = pltpu.roll(lfm, LANES - MLSTM_HEADS, 1)
    m_t = jnp.maximum(f_al, pre)
    g_st = jnp.exp(f_al - m_t)
    w_k = jnp.exp(pre - m_t)
    m1_ref[...] = m_t
    n_new = []
    for h in range(MLSTM_HEADS):
        kk = ps_ref[rows, A_K + HEAD_W * h:A_K + HEAD_W * (h + 1)] * (HEAD_W ** -0.5)
        nn = g_st[:, h:h + 1] * n0_ref[:, HEAD_W * h:HEAD_W * (h + 1)] + w_k[:, h:h + 1] * kk
        n1_ref[:, HEAD_W * h:HEAD_W * (h + 1)] = nn
        n_new.append(nn)

    glu = ps_ref[rows, C_A:C_A + GROUP_WIDTH] * _sigmoid(ps_ref[rows, C_G:C_G + GROUP_WIDTH])
    yc_ref[...] = glu * cw_ref[CONV_WIDTH - 1:CONV_WIDTH, :]
    a_v = ps_ref[rows, A_V:A_V + GROUP_WIDTH]
    g_v = ps_ref[rows, G_V:G_V + GROUP_WIDTH]
    s_k = ps_ref[rows, S_K:S_K + LANES]
    s_v = ps_ref[rows, S_V:S_V + LANES]

    lane_b = lax.broadcasted_iota(I32, (nb, LANES), 0)
    key_row = lax.broadcasted_iota(I32, (SWA_HEADS, WINDOW), 1)
    lo_row = lax.broadcasted_iota(I32, (1, LANES), 1) < SWA_HEAD_DIM
    sink_col = sink_ref[:, 0:1]

    for j in range(bb):
        onehot = (lane_b == r0 + j).astype(BF16)
        bc_ref[...] = jnp.dot(tt_ref[...], onehot, preferred_element_type=F32)
        jrow = slice(j, j + 1)
        for h in range(MLSTM_HEADS):
            hs = slice(HEAD_W * h, HEAD_W * (h + 1))
            kbc = bc_ref[_T_MK + HEAD_W * h:_T_MK + HEAD_W * (h + 1), :]
            qbc = bc_ref[_T_MQ + HEAD_W * h:_T_MQ + HEAD_W * (h + 1), :]
            g1 = g_st[jrow, h:h + 1]
            w1 = w_k[jrow, h:h + 1]
            v_row = a_v[jrow, hs]
            c_new = g1 * c0_ref[j, h] + kbc * (w1 * v_row)
            c1_ref[j, h] = c_new
            num_ref[jrow, hs] = jnp.sum(qbc * c_new, axis=0, keepdims=True)
        for p in range(2):
            abc = bc_ref[_T_GA + LANES * p:_T_GA + LANES * (p + 1), :]
            kbc = bc_ref[_T_GK + LANES * p:_T_GK + LANES * (p + 1), :]
            qbc = bc_ref[_T_GQ + LANES * p:_T_GQ + LANES * (p + 1), :]
            for hh in range(2):
                head = 2 * p + hh
                hs = slice(HEAD_W * head, HEAD_W * (head + 1))
                ds_ = slice(GLA_DK * hh, GLA_DK * (hh + 1))
                v_row = g_v[jrow, hs]
                s_new = abc[ds_, :] * s0_ref[j, head] + kbc[ds_, :] * v_row
                s1_ref[j, head] = s_new
                go_ref[jrow, hs] = jnp.sum(qbc[ds_, :] * s_new, axis=0, keepdims=True)
        cache = cv0_ref[j]
        yc_ref[jrow, :] = yc_ref[jrow, :] + jnp.sum(cache * cw_ref[0:CONV_WIDTH - 1, :], axis=0, keepdims=True)
        cv1_ref[j, 0:CONV_WIDTH - 2, :] = cv0_ref[j, 1:CONV_WIDTH - 1, :]
        cv1_ref[j, CONV_WIDTH - 2:CONV_WIDTH - 1, :] = glu[jrow, :]
        k_new = s_k[jrow, :]
        v_new = s_v[jrow, :]
        k1_ref[j, 0:WINDOW - 1, :] = k0_ref[j, 1:WINDOW, :]
        k1_ref[j, WINDOW - 1:WINDOW, :] = k_new
        v1_ref[j, 0:WINDOW - 1, :] = v0_ref[j, 1:WINDOW, :]
        v1_ref[j, WINDOW - 1:WINDOW, :] = v_new
        qmat = qm_ref[j] * (SWA_HEAD_DIM ** -0.5)
        s_old = lax.dot_general(qmat.astype(BF16), k0_ref[j].astype(BF16), (((1,), (1,)), ((), ())),
                                preferred_element_type=F32)
        s_old = jnp.where(key_row >= 1, s_old, -jnp.inf)
        s_cur = jnp.sum(qmat * k_new, axis=-1, keepdims=True)
        mx = jnp.maximum(jnp.maximum(jnp.max(s_old, axis=-1, keepdims=True), s_cur), sink_col)
        p_old = jnp.exp(s_old - mx)
        p_cur = jnp.exp(s_cur - mx)
        den = jnp.sum(p_old, axis=-1, keepdims=True) + p_cur + jnp.exp(sink_col - mx)
        o = jnp.dot((p_old / den).astype(BF16), v0_ref[j].astype(BF16), preferred_element_type=F32)
        o = o + (p_cur / den) * v_new
        o_sw = pltpu.roll(o, SWA_HEAD_DIM, 1)
        for c in range(SWA_HEADS // 2):
            g = c // (SWA_HEADS // SWA_KV_HEADS // 2)
            left = (o if g == 0 else o_sw)[2 * c:2 * c + 1, :]
            right = (o_sw if g == 0 else o)[2 * c + 1:2 * c + 2, :]
            yd_ref[jrow, LANES * c:LANES * (c + 1)] = jnp.where(lo_row, left, right)

    for h in range(MLSTM_HEADS):
        hs = slice(HEAD_W * h, HEAD_W * (h + 1))
        qq = ps_ref[rows, A_Q + HEAD_W * h:A_Q + HEAD_W * (h + 1)]
        den = jnp.sum(qq * n_new[h], axis=-1, keepdims=True)
        hh = num_ref[:, hs] / jnp.maximum(jnp.abs(den), jnp.exp(-m_t[:, h:h + 1]))
        hh = _sigmoid(ps_ref[rows, A_O + HEAD_W * h:A_O + HEAD_W * (h + 1)]) * hh
        hh = hh * lax.rsqrt(jnp.mean(hh * hh, axis=-1, keepdims=True) + NORM_EPS) * mng_ref[:, hs]
        ybuf[rows, hs] = hh
    ybuf[rows, GROUP_WIDTH:2 * GROUP_WIDTH] = _conv_norm_act(yc_ref[...] + cb_ref[...], cg_ref, cbe_ref)
    for head in range(GLA_HEADS):
        hs = slice(HEAD_W * head, HEAD_W * (head + 1))
        oh = go_ref[:, hs]
        oh = oh * lax.rsqrt(jnp.mean(oh * oh, axis=-1, keepdims=True) + NORM_EPS) * gng_ref[:, hs]
        gr = ps_ref[rows, G_R + HEAD_W * head:G_R + HEAD_W * (head + 1)]
        ybuf[rows, 2 * GROUP_WIDTH + HEAD_W * head:2 * GROUP_WIDTH + HEAD_W * (head + 1)] = oh * _silu(gr)
    ybuf[rows, 3 * GROUP_WIDTH:4 * GROUP_WIDTH] = yd_ref[...]

    @pl.when(i == pl.num_programs(0) - 1)
    def _():
        ybf[...] = ybuf[...].astype(BF16)
        cp = pltpu.make_async_copy(ybf, mix_ref.at[pl.ds(t_len, nb)], ysem.at[0])
        cp.start()
        for t in range(t_len // zr):
            zero_copy(t).wait()
        cp.wait()


def _sample_mixers(proj_s, t_len, states, layer, prm):
    c_all, n_all, m_all, s_all, cv_all, k_all, v_all = states
    depth = c_all.shape[0]
    nb = proj_s.shape[0]
    bb = SAMPLE_BB
    assert nb == LANES and nb % bb == 0
    lb = layer * (nb // bb)
    n0, m0 = n_all[layer], m_all[layer]
    mm = jnp.concatenate([m0, m0, jnp.zeros((nb, LANES - 2 * MLSTM_HEADS), F32)], axis=1)
    n0f = n0.reshape(nb, GROUP_WIDTH)
    c0 = c_all.reshape((depth * nb,) + c_all.shape[2:])
    s0 = s_all.reshape((depth * nb,) + s_all.shape[2:])
    cv0 = cv_all.reshape((depth * nb,) + cv_all.shape[2:])
    k0f = k_all.reshape(depth * nb, WINDOW, LANES)
    v0f = v_all.reshape(depth * nb, WINDOW, LANES)
    sq = proj_s[:, S_Q:S_Q + GROUP_WIDTH].reshape(nb, SWA_KV_HEADS, SWA_HEADS // SWA_KV_HEADS, SWA_HEAD_DIM)
    zq = jnp.zeros_like(sq[:, 0])
    qm = jnp.concatenate([jnp.concatenate([sq[:, 0], zq], axis=-1), jnp.concatenate([zq, sq[:, 1]], axis=-1)], axis=1)
    sink_b = jnp.broadcast_to(prm['swa_sinks'].astype(F32)[:, None], (SWA_HEADS, LANES))

    full = lambda shape: pl.BlockSpec(shape, lambda i: (0,) * len(shape))
    rowb = lambda w: pl.BlockSpec((bb, w), lambda i: (i, 0))
    in_specs = [
        full((nb, N_PROJ)), rowb(LANES), full((1, LANES)), rowb(GROUP_WIDTH),
        pl.BlockSpec((bb, MLSTM_HEADS, HEAD_W, HEAD_W), lambda i: (lb + i, 0, 0, 0)),
        pl.BlockSpec((bb, GLA_HEADS, GLA_DK, HEAD_W), lambda i: (lb + i, 0, 0, 0)),
        pl.BlockSpec((bb, CONV_WIDTH - 1, GROUP_WIDTH), lambda i: (lb + i, 0, 0)),
        pl.BlockSpec((bb, WINDOW, LANES), lambda i: (lb + i, 0, 0)),
        pl.BlockSpec((bb, WINDOW, LANES), lambda i: (lb + i, 0, 0)),
        pl.BlockSpec((bb, SWA_HEADS, LANES), lambda i: (i, 0, 0)),
        full((SWA_HEADS, LANES)), full((LANES, 256)), full((1, 256)),
        full((1, GROUP_WIDTH)), full((1, GROUP_WIDTH)),
        full((CONV_WIDTH, GROUP_WIDTH)), full((1, GROUP_WIDTH)), full((1, GROUP_WIDTH)), full((1, GROUP_WIDTH)),
    ]
    out_specs = [
        pl.BlockSpec(memory_space=pl.ANY),
        pl.BlockSpec((bb, MLSTM_HEADS, HEAD_W, HEAD_W), lambda i: (i, 0, 0, 0)),
        rowb(GROUP_WIDTH), rowb(LANES),
        pl.BlockSpec((bb, GLA_HEADS, GLA_DK, HEAD_W), lambda i: (i, 0, 0, 0)),
        pl.BlockSpec((bb, CONV_WIDTH - 1, GROUP_WIDTH), lambda i: (i, 0, 0)),
        pl.BlockSpec((bb, WINDOW, LANES), lambda i: (i, 0, 0)),
        pl.BlockSpec((bb, WINDOW, LANES), lambda i: (i, 0, 0)),
    ]
    out_shape = [
        jax.ShapeDtypeStruct((t_len + nb, D_MODEL), BF16),
        jax.ShapeDtypeStruct(c_all.shape[1:], F32),
        jax.ShapeDtypeStruct((nb, GROUP_WIDTH), F32),
        jax.ShapeDtypeStruct((nb, LANES), F32),
        jax.ShapeDtypeStruct(s_all.shape[1:], F32),
        jax.ShapeDtypeStruct(cv_all.shape[1:], F32),
        jax.ShapeDtypeStruct((nb, WINDOW, LANES), F32),
        jax.ShapeDtypeStruct((nb, WINDOW, LANES), F32),
    ]
    scratch = [
        pltpu.VMEM((_T_ROWS, nb), BF16), pltpu.VMEM((_T_ROWS, LANES), F32),
        pltpu.VMEM((bb, GROUP_WIDTH), F32), pltpu.VMEM((bb, GROUP_WIDTH), F32), pltpu.VMEM((bb, GROUP_WIDTH), F32),
        pltpu.VMEM((bb, GROUP_WIDTH), F32),
        pltpu.VMEM((nb, D_MODEL), F32), pltpu.VMEM((nb, D_MODEL), BF16),
        pltpu.VMEM((_pick_tile(t_len, (512, 256, 128, 64, 16)), D_MODEL), BF16),
        pltpu.SemaphoreType.DMA((1,)), pltpu.SemaphoreType.DMA((1,)),
    ]
    mixed, c1, n1, m1, s1, cv1, k1, v1 = pl.pallas_call(
        _sample_kernel,
        grid=(nb // bb,),
        in_specs=in_specs, out_specs=out_specs, out_shape=out_shape, scratch_shapes=scratch,
        compiler_params=_cparams(("arbitrary",)),
        name="sample_mixers",
    )(proj_s, mm, prm['gate_bias'], n0f, c0, s0, cv0, k0f, v0f, qm, sink_b,
      prm['gla_w_gate_pad'], prm['gla_b_gate'].reshape(1, 256),
      prm['mlstm_norm_g'].reshape(1, -1), prm['gla_norm_g'].reshape(1, -1),
      prm['conv_w'], prm['conv_b'].reshape(1, -1), prm['conv_norm_g'].reshape(1, -1),
      prm['conv_norm_b'].reshape(1, -1))
    new_state = (c1, n1.reshape(n0.shape), m1[:, :MLSTM_HEADS], s1, cv1,
                 k1.reshape(k_all.shape[1:]), v1.reshape(v_all.shape[1:]))
    return mixed, new_state


def _outproj_router_kernel(x_ref, mix_ref, w_ref, g_ref, rw_ref, rb_ref,
                           x1_ref, hn_ref, ri_ref, rf_ref, cnt_ref):
    @pl.when(pl.program_id(0) == 0)
    def _():
        cnt_ref[...] = jnp.zeros_like(cnt_ref)

    counts = _route_rows(x_ref[...], mix_ref[...], w_ref, g_ref, rw_ref, rb_ref, cnt_ref[0:1, :],
                         x1_ref, hn_ref, ri_ref, rf_ref)
    cnt_ref[...] = jnp.broadcast_to(counts, cnt_ref.shape)


def _route_rows(x, mix, w_ref, g_ref, rw_ref, rb_ref, counts, x1_ref, hn_ref, ri_ref, rf_ref):
    tm = x.shape[0]
    x1 = x + jnp.dot(mix, w_ref[...], preferred_element_type=F32)
    x1_ref[...] = x1
    ms = jnp.mean(x1 * x1, axis=-1, keepdims=True)
    hn = x1 * lax.rsqrt(ms + NORM_EPS) * g_ref[...]
    hn_ref[...] = hn
    logits = jnp.dot(hn.astype(BF16), rw_ref[...], preferred_element_type=F32) + rb_ref[...]
    lane = lax.broadcasted_iota(I32, (tm, LANES), 1)
    big = jnp.int32(LANES)
    gl = jnp.where(lane < N_GROUPS, logits, -jnp.inf)
    gmax = jnp.max(gl, axis=-1, keepdims=True)
    g_sel = jnp.min(jnp.where(gl == gmax, lane, big), axis=-1, keepdims=True)
    g_w = 1.0 / jnp.sum(jnp.exp(gl - gmax), axis=-1, keepdims=True)
    e_lane = lane - N_GROUPS
    in_grp = (e_lane >= 0) & (e_lane < N_EXPERTS) & ((e_lane // EXPERTS_PER_GROUP) == g_sel)
    el = jnp.where(in_grp, logits, -jnp.inf)
    m1 = jnp.max(el, axis=-1, keepdims=True)
    i1 = jnp.min(jnp.where(el == m1, lane, big), axis=-1, keepdims=True)
    el2 = jnp.where(lane == i1, -jnp.inf, el)
    m2 = jnp.max(el2, axis=-1, keepdims=True)
    i2 = jnp.min(jnp.where(el2 == m2, lane, big), axis=-1, keepdims=True)
    r = jnp.exp(m2 - m1)
    p1 = 1.0 / (1.0 + r)
    gate1 = g_w * p1
    gate2 = g_w * (r * p1)
    sel1 = lane == i1
    sel2 = lane == i2
    onehot = jnp.where(sel1 | sel2, 1.0, 0.0)
    row = lax.broadcasted_iota(I32, (tm, tm), 0)
    col = lax.broadcasted_iota(I32, (tm, tm), 1)
    strict = jnp.where(col < row, 1.0, 0.0).astype(BF16)
    cum = jnp.dot(strict, onehot.astype(BF16), preferred_element_type=F32) + counts
    rank1 = jnp.sum(jnp.where(sel1, cum, 0.0), axis=-1, keepdims=True).astype(I32)
    rank2 = jnp.sum(jnp.where(sel2, cum, 0.0), axis=-1, keepdims=True).astype(I32)
    ri = jnp.where(lane == 0, i1 - N_GROUPS, jnp.where(lane == 1, i2 - N_GROUPS,
                   jnp.where(lane == 2, rank1, jnp.where(lane == 3, rank2, 0))))
    ri_ref[...] = ri
    rf_ref[...] = jnp.where(lane == 0, gate1, jnp.where(lane == 1, gate2, 0.0))
    return counts + jnp.sum(onehot, axis=0, keepdims=True)


def _outproj_router(x, mixed, w_out_bf16, norm_g, rw_pad, rb_pad):
    n = x.shape[0]
    tm = _pick_tile(n, (320, 256, 128, 64, 16))
    full = lambda shape: pl.BlockSpec(shape, lambda i: (0,) * len(shape))
    rowb = lambda w: pl.BlockSpec((tm, w), lambda i: (i, 0))
    return pl.pallas_call(
        _outproj_router_kernel,
        grid=(n // tm,),
        in_specs=[rowb(D_MODEL), rowb(D_MODEL), full((D_MODEL, D_MODEL)), full((1, D_MODEL)),
                  full((D_MODEL, LANES)), full((1, LANES))],
        out_specs=[rowb(D_MODEL), rowb(D_MODEL), rowb(LANES), rowb(LANES), full((SUBLANES, LANES))],
        out_shape=[
            jax.ShapeDtypeStruct((n, D_MODEL), F32),
            jax.ShapeDtypeStruct((n, D_MODEL), F32),
            jax.ShapeDtypeStruct((n, LANES), I32),
            jax.ShapeDtypeStruct((n, LANES), F32),
            jax.ShapeDtypeStruct((SUBLANES, LANES), F32),
        ],
        compiler_params=_cparams(("arbitrary",)),
        name="outproj_router",
    )(x, mixed, w_out_bf16, norm_g.reshape(1, D_MODEL), rw_pad, rb_pad)


def _dispatch_kernel(pos_ref, zt_ref, hn_ref, xs_ref, zbuf, sem, zsem):
    tm = hn_ref.shape[0]
    tile = zbuf.shape[0]
    base = pl.program_id(0) * tm

    @pl.when(pl.program_id(0) == 0)
    def _():
        zbuf[...] = jnp.zeros_like(zbuf)

        def zero_tile(k, carry):
            @pl.when(zt_ref[k] >= 0)
            def _():
                row = pl.multiple_of(zt_ref[k] * tile, tile)
                cp = pltpu.make_async_copy(zbuf, xs_ref.at[pl.ds(row, tile)], zsem.at[0])
                cp.start()
                cp.wait()
            return carry

        lax.fori_loop(0, zt_ref.shape[0], zero_tile, 0)

    def issue(r, carry):
        tok = base + r
        src = hn_ref.at[pl.ds(r, 1)]
        pltpu.make_async_copy(src, xs_ref.at[pl.ds(pos_ref[2 * tok], 1)], sem.at[0]).start(priority=0)
        pltpu.make_async_copy(src, xs_ref.at[pl.ds(pos_ref[2 * tok + 1], 1)], sem.at[1]).start(priority=1)
        return carry

    lax.fori_loop(0, tm, issue, 0, unroll=4)
    pltpu.make_async_copy(hn_ref, xs_ref.at[pl.ds(0, tm)], sem.at[0]).wait()
    pltpu.make_async_copy(hn_ref, xs_ref.at[pl.ds(0, tm)], sem.at[1]).wait()


def _dispatch(pos, zero_tiles, hn, n_rows, tile):
    n = hn.shape[0]
    tm = _pick_tile(n, (256, 128, 64, 8))
    return pl.pallas_call(
        _dispatch_kernel,
        grid_spec=pltpu.PrefetchScalarGridSpec(
            num_scalar_prefetch=2,
            grid=(n // tm,),
            in_specs=[pl.BlockSpec((tm, D_MODEL), lambda i, p, z: (i, 0))],
            out_specs=pl.BlockSpec(memory_space=pl.ANY),
            scratch_shapes=[pltpu.VMEM((tile, D_MODEL), F32), pltpu.SemaphoreType.DMA((2,)),
                            pltpu.SemaphoreType.DMA((1,))],
        ),
        out_shape=jax.ShapeDtypeStruct((n_rows, D_MODEL), F32),
        compiler_params=_cparams(("arbitrary",)),
        name="moe_dispatch",
    )(pos, zero_tiles, hn)


def _expert_kernel(te_ref, nt_ref, x_ref, wg_ref, wu_ref, wd_ref, y_ref):
    del te_ref
    used = pl.program_id(0) < nt_ref[0]

    @pl.when(used)
    def _():
        x = x_ref[...].astype(BF16)
        a = jnp.dot(x, wg_ref[0], preferred_element_type=F32)
        u = jnp.dot(x, wu_ref[0], preferred_element_type=F32)
        hmid = (_silu(a) * u).astype(BF16)
        y_ref[...] = jnp.dot(hmid, wd_ref[0], preferred_element_type=F32)

    @pl.when(jnp.logical_not(used))
    def _():
        y_ref[...] = jnp.zeros_like(y_ref)


def _expert_mlp(tile_expert, n_tiles_used, xs, wg, wu, wd, tile):
    n_tiles = xs.shape[0] // tile

    def row_map(t, te, nt):
        return (jnp.minimum(t, nt[0] - 1), 0)

    def w_map(t, te, nt):
        return (te[jnp.minimum(t, nt[0] - 1)], 0, 0)

    return pl.pallas_call(
        _expert_kernel,
        grid_spec=pltpu.PrefetchScalarGridSpec(
            num_scalar_prefetch=2,
            grid=(n_tiles,),
            in_specs=[
                pl.BlockSpec((tile, D_MODEL), row_map),
                pl.BlockSpec((1, D_MODEL, D_EXPERT), w_map),
                pl.BlockSpec((1, D_MODEL, D_EXPERT), w_map),
                pl.BlockSpec((1, D_EXPERT, D_MODEL), w_map),
            ],
            out_specs=pl.BlockSpec((tile, D_MODEL), lambda t, te, nt: (t, 0)),
        ),
        out_shape=jax.ShapeDtypeStruct(xs.shape, F32),
        compiler_params=_cparams(("arbitrary",)),
        name="expert_mlp",
    )(tile_expert, n_tiles_used, xs, wg, wu, wd)


def _combine_kernel(pos_ref, x1_ref, rf_ref, fg_ref, ys_ref, o_ref, buf_a, buf_b, sem, *, final_norm, row0):
    tm = x1_ref.shape[0]
    base = row0 + pl.program_id(0) * tm

    def issue(r, carry):
        tok = base + r
        pltpu.make_async_copy(ys_ref.at[pl.ds(pos_ref[2 * tok], 1)], buf_a.at[pl.ds(r, 1)],
                              sem.at[0]).start(priority=0)
        pltpu.make_async_copy(ys_ref.at[pl.ds(pos_ref[2 * tok + 1], 1)], buf_b.at[pl.ds(r, 1)],
                              sem.at[1]).start(priority=1)
        return carry

    lax.fori_loop(0, tm, issue, 0, unroll=4)
    pltpu.make_async_copy(ys_ref.at[pl.ds(0, tm)], buf_a, sem.at[0]).wait()
    pltpu.make_async_copy(ys_ref.at[pl.ds(0, tm)], buf_b, sem.at[1]).wait()
    rf = rf_ref[...]
    x2 = x1_ref[...] + rf[:, 0:1] * buf_a[...] + rf[:, 1:2] * buf_b[...]
    if final_norm:
        ms = jnp.mean(x2 * x2, axis=-1, keepdims=True)
        x2 = x2 * lax.rsqrt(ms + NORM_EPS) * fg_ref[...]
    o_ref[...] = x2


def _combine(pos, x1, rf, ys, final_g, final_norm, row0=0, n_rows=None):
    n = x1.shape[0]
    n_rows = n if n_rows is None else n_rows
    tm = _pick_tile(n_rows, (256, 128, 64, 8))
    assert row0 % tm == 0
    b0 = row0 // tm
    return pl.pallas_call(
        functools.partial(_combine_kernel, final_norm=final_norm, row0=row0),
        grid_spec=pltpu.PrefetchScalarGridSpec(
            num_scalar_prefetch=1,
            grid=(n_rows // tm,),
            in_specs=[
                pl.BlockSpec((tm, D_MODEL), lambda i, p: (b0 + i, 0)),
                pl.BlockSpec((tm, LANES), lambda i, p: (b0 + i, 0)),
                pl.BlockSpec((1, D_MODEL), lambda i, p: (0, 0)),
                pl.BlockSpec(memory_space=pl.ANY),
            ],
            out_specs=pl.BlockSpec((tm, D_MODEL), lambda i, p: (i, 0)),
            scratch_shapes=[pltpu.VMEM((tm, D_MODEL), F32), pltpu.VMEM((tm, D_MODEL), F32),
                            pltpu.SemaphoreType.DMA((2,))],
        ),
        out_shape=jax.ShapeDtypeStruct((n_rows, D_MODEL), F32),
        compiler_params=_cparams(("arbitrary",)),
        name="moe_combine",
    )(pos, x1, rf, final_g.reshape(1, D_MODEL), ys)


def _pad_w_in_rows(w_ref, o_ref):
    rows = o_ref.shape[0]
    at = 0
    for lo, hi, dst in sorted(_W_IN_SEGMENTS, key=lambda s: s[2]):
        if dst > at:
            o_ref[:, at:dst] = jnp.zeros((rows, dst - at), o_ref.dtype)
        o_ref[:, dst:dst + (hi - lo)] = w_ref[0, :, lo:hi].astype(o_ref.dtype)
        at = dst + (hi - lo)
    if at < N_PROJ:
        o_ref[:, at:N_PROJ] = jnp.zeros((rows, N_PROJ - at), o_ref.dtype)


def _w_in_side(w_in_all, layer, n_steps):
    depth, d, n_in = w_in_all.shape
    assert d % n_steps == 0
    tr = d // n_steps
    return _SideCast(
        operand=w_in_all,
        in_spec=pl.BlockSpec((1, tr, n_in), lambda i: (layer, i, 0)),
        out_spec=pl.BlockSpec((tr, N_PROJ), lambda i: (i, 0)),
        out_shape=jax.ShapeDtypeStruct((d, N_PROJ), BF16),
        shape=(d, N_PROJ),
    )


def _pad_w_in(w_in_all, layer):
    side = _w_in_side(w_in_all, layer, 32)
    return pl.pallas_call(
        _pad_w_in_rows,
        grid=(32,),
        in_specs=[side.in_spec], out_specs=side.out_spec, out_shape=side.out_shape,
        compiler_params=_cparams(("parallel",)),
        name="pad_w_in",
    )(side.operand)


def _moe(x1, hn, ri, rf, counts, experts, final_g, final_norm, tile, t_len):
    n = x1.shape[0]
    n_tiles = -(-2 * n // tile) + N_EXPERTS
    cnt = counts[0, N_GROUPS:N_GROUPS + N_EXPERTS].astype(I32)
    tiles_per = (cnt + tile - 1) // tile
    tile_end = jnp.cumsum(tiles_per)
    row_off = (tile_end - tiles_per) * tile
    pos = (row_off[ri[:, 0:2]] + ri[:, 2:4]).reshape(2 * n)
    tile_ids = jnp.arange(n_tiles, dtype=I32)
    tile_expert = jnp.minimum(jnp.sum((tile_ids[:, None] >= tile_end[None, :]).astype(I32), axis=1), N_EXPERTS - 1)
    n_used = tile_end[N_EXPERTS - 1:N_EXPERTS].astype(I32)
    last_tile = jnp.where(tiles_per > 0, tile_end - 1, -1).astype(I32)
    zero_tiles = jnp.concatenate([last_tile, jnp.where(tile_ids >= n_used[0], tile_ids, -1)])
    xs = _dispatch(pos, zero_tiles, hn, n_tiles * tile, tile)
    ys = _expert_mlp(tile_expert, n_used, xs, *experts, tile)
    if final_norm:
        return (_combine(pos, x1, rf, ys, final_g, True, 0, t_len),
                _combine(pos, x1, rf, ys, final_g, True, t_len, n - t_len))
    return _combine(pos, x1, rf, ys, final_g, False)


def _layer(x, t_len, states, layer, prm, w_in_pad, final_g, final_norm, moe_tile):
    proj = _in_projection(x, prm['norm_mix_g'], w_in_pad)
    mixed, new_s = _sample_mixers(proj[t_len:], t_len, states, layer, prm)
    e_shape = prm['expert_w_gate_all'].shape[1:]
    mixed, cn, m_p, wg = _mlstm_prompt(proj, mixed, t_len, prm['gate_bias'], prm['mlstm_norm_g'],
                                       prm['expert_w_gate_all'], layer)
    mixed, conv_tail, wu = _conv_prompt(proj, mixed, t_len, prm['conv_w'], prm['conv_b'], prm['conv_norm_g'],
                                        prm['conv_norm_b'], prm['expert_w_up_all'], layer)
    mixed, sp, wd = _gla_prompt(proj, mixed, t_len, prm['gla_w_gate_pad'], prm['gla_b_gate'], prm['gla_norm_g'],
                                prm['expert_w_down_all'], layer)
    mixed, next_w_in_pad = _swa_prompt(proj, mixed, t_len, prm['swa_sinks'], prm['w_in_all'],
                                       None if final_norm else layer + 1)
    experts = (wg.reshape(e_shape), wu.reshape(e_shape), wd.reshape(prm['expert_w_down_all'].shape[1:]))
    x1, hn, ri, rf, counts = _outproj_router(x, mixed, prm['w_out'], prm['norm_ffn_g'], prm['router_w'], prm['router_b'])
    x2 = _moe(x1, hn, ri, rf, counts, experts, final_g, final_norm, moe_tile, t_len)
    p_c = cn[None, :, :, :HEAD_W]
    p_n = cn[None, :, :, HEAD_W]
    p_m = m_p[None, :MLSTM_HEADS, 0]
    p_s = jnp.stack([sp[0, :GLA_DK, :HEAD_W], sp[0, GLA_DK:, HEAD_W:],
                     sp[1, :GLA_DK, :HEAD_W], sp[1, GLA_DK:, HEAD_W:]])[None]
    p_conv = conv_tail[None, CONV_HALO - (CONV_WIDTH - 1):]
    p_k = proj[t_len - WINDOW:t_len, S_K:S_K + LANES].reshape(1, WINDOW, SWA_KV_HEADS, SWA_HEAD_DIM)
    p_v = proj[t_len - WINDOW:t_len, S_V:S_V + LANES].reshape(1, WINDOW, SWA_KV_HEADS, SWA_HEAD_DIM)
    return x2, (p_c, p_n, p_m, p_s, p_conv, p_k, p_v), new_s, next_w_in_pad


def _forward(x_prompt, x_sample, states, layer_params, final_norm_g, moe_tile=MOE_TILE):
    t_len = x_prompt.shape[1]
    x = jnp.concatenate([x_prompt[0], x_sample[:, 0]], axis=0)
    new_p, new_s = [], []
    depth = len(layer_params)
    w_in_pad = _pad_w_in(layer_params[0]['w_in_all'], 0)
    for l, prm in enumerate(layer_params):
        x, sp, ss, w_in_pad = _layer(x, t_len, states, l, prm, w_in_pad, final_norm_g, l == depth - 1, moe_tile)
        new_p.append(sp)
        new_s.append(ss)
    y_prompt, y_sample = x
    y_prompt = y_prompt[None]
    y_sample = y_sample[:, None]
    p_states = [jnp.stack(parts) for parts in zip(*new_p)]
    s_states = [jnp.stack(parts) for parts in zip(*new_s)]
    return (y_prompt, y_sample, *p_states, *s_states)


def _prep_layer_params(l, norm_mix_g, w_in, mlstm_b_i, mlstm_b_f, mlstm_norm_g, conv_w, conv_b, conv_norm_g,
                       conv_norm_b, gla_w_gate, gla_b_gate, gla_norm_g, swa_sinks, w_out, norm_ffn_g,
                       router_group_w, router_group_b, router_expert_w, router_expert_b, expert_w_gate,
                       expert_w_up, expert_w_down):
    gate_bias = jnp.concatenate([mlstm_b_i[l], mlstm_b_f[l], jnp.zeros((LANES - 2 * MLSTM_HEADS,), F32)])
    rw = jnp.concatenate([router_group_w[l], router_expert_w[l],
                          jnp.zeros((D_MODEL, LANES - N_GROUPS - N_EXPERTS), F32)], axis=1)
    rb = jnp.concatenate([router_group_b[l], router_expert_b[l],
                          jnp.zeros((LANES - N_GROUPS - N_EXPERTS,), F32)])
    return {
        'norm_mix_g': norm_mix_g[l],
        'w_in_all': w_in,
        'gate_bias': gate_bias.reshape(1, LANES),
        'mlstm_norm_g': mlstm_norm_g[l],
        'conv_w': conv_w[l], 'conv_b': conv_b[l], 'conv_norm_g': conv_norm_g[l], 'conv_norm_b': conv_norm_b[l],
        'gla_w_gate_pad': jnp.concatenate(
            [gla_w_gate[l], jnp.zeros((LANES - GLA_LOWRANK, GLA_HEADS * GLA_DK), F32)], axis=0).astype(BF16),
        'gla_b_gate': gla_b_gate[l], 'gla_norm_g': gla_norm_g[l],
        'swa_sinks': swa_sinks[l],
        'w_out': w_out[l].astype(BF16),
        'norm_ffn_g': norm_ffn_g[l],
        'router_w': rw.astype(BF16), 'router_b': rb.reshape(1, LANES),
        'expert_w_gate_all': expert_w_gate, 'expert_w_up_all': expert_w_up, 'expert_w_down_all': expert_w_down,
    }


def kernel(x_prompt, x_sample, state_mlstm_C, state_mlstm_n, state_mlstm_m, state_gla_S, cache_conv, cache_swa_k, cache_swa_v, norm_mix_g, w_in, mlstm_b_i, mlstm_b_f, mlstm_norm_g, conv_w, conv_b, conv_norm_g, conv_norm_b, gla_w_gate, gla_b_gate, gla_norm_g, swa_sinks, w_out, norm_ffn_g, router_group_w, router_group_b, router_expert_w, router_expert_b, expert_w_gate, expert_w_up, expert_w_down, final_norm_g):
    depth = w_in.shape[0]
    weights = (norm_mix_g, w_in, mlstm_b_i, mlstm_b_f, mlstm_norm_g, conv_w, conv_b, conv_norm_g, conv_norm_b,
               gla_w_gate, gla_b_gate, gla_norm_g, swa_sinks, w_out, norm_ffn_g, router_group_w, router_group_b,
               router_expert_w, router_expert_b, expert_w_gate, expert_w_up, expert_w_down)
    layer_params = [_prep_layer_params(l, *weights) for l in range(depth)]
    states = (state_mlstm_C, state_mlstm_n, state_mlstm_m, state_gla_S, cache_conv, cache_swa_k, cache_swa_v)
    return _forward(x_prompt, x_sample, states, layer_params, final_norm_g)
```

```python
import functools
from typing import NamedTuple

import jax
import jax.numpy as jnp
from jax import lax
from jax.experimental import pallas as pl
from jax.experimental.pallas import tpu as pltpu

F32 = jnp.float32
BF16 = jnp.bfloat16
I32 = jnp.int32
HIGHEST = lax.Precision.HIGHEST

D_MODEL = 2048
GROUP_WIDTH = 512
HEAD_W = 128
MLSTM_HEADS = 4
GLA_HEADS = 4
GLA_DK = 64
GLA_LOWRANK = 16
GLA_TAU = 16.0
CONV_WIDTH = 31
SWA_HEADS = 8
SWA_KV_HEADS = 2
SWA_HEAD_DIM = 64
WINDOW = 128
N_GROUPS = 4
EXPERTS_PER_GROUP = 4
N_EXPERTS = 16
D_EXPERT = 1024
NORM_EPS = 1e-6
LANES = 128
SUBLANES = 8
TOK_ROWS = D_MODEL // LANES
PACK_ROWS = TOK_ROWS // 2

A_Q, A_K, A_V, A_O = 0, 512, 1024, 1536
C_A, C_G = 2048, 2560
G_V, G_R = 3072, 3584
S_Q = 4096
G_Q, G_K = 4608, 4864
A_G, G_LR, S_K, S_V = 5120, 5248, 5376, 5504
N_PROJ = 5632
MIX_COL_MLSTM, MIX_COL_CONV, MIX_COL_GLA, MIX_COL_SWA = 0, 1, 2, 3
_W_IN_SEGMENTS = (
    (0, 512, A_Q), (512, 1024, A_K), (1024, 1536, A_V), (1536, 2048, A_O),
    (2048, 2056, A_G),
    (2056, 2568, C_A), (2568, 3080, C_G),
    (3080, 3336, G_Q), (3336, 3592, G_K), (3592, 4104, G_V), (4104, 4616, G_R),
    (4616, 4632, G_LR),
    (4632, 5144, S_Q), (5144, 5272, S_K), (5272, 5400, S_V),
)

MLSTM_CHUNK = 128
MLSTM_STEP_ROWS = 256
GLA_CHUNK = 64
GLA_STEP_ROWS = 256
GLA_SUB = 16
SWA_STEP_ROWS = 256
CONV_TILE = 256
CONV_ROWS = 64
CONV_HALO = 32
SAMPLE_BB = 8
MOE_TILE = 256
VMEM_LIMIT = 56 * 1024 * 1024


def _cparams(sem, vmem=VMEM_LIMIT):
    return pltpu.CompilerParams(dimension_semantics=sem, vmem_limit_bytes=vmem)


def _log_sigmoid(x):
    return jnp.minimum(x, 0.0) - jnp.log1p(jnp.exp(-jnp.abs(x)))


def _sigmoid(x):
    return 1.0 / (1.0 + jnp.exp(-x))


def _silu(x):
    return x * _sigmoid(x)


def _masked_row_sums(mask, x):
    m = jnp.where(mask, 1.0, 0.0).astype(BF16)
    hi = x.astype(BF16)
    r1 = x - hi.astype(F32)
    mid = r1.astype(BF16)
    lo = (r1 - mid.astype(F32)).astype(BF16)
    out = jnp.dot(m, hi, preferred_element_type=F32)
    out = out + jnp.dot(m, mid, preferred_element_type=F32)
    return out + jnp.dot(m, lo, preferred_element_type=F32)


def _pick_tile(n, candidates):
    for c in candidates:
        if n % c == 0:
            return c
    raise ValueError(f"no tile for {n} in {candidates}")


class _SideCast(NamedTuple):
    operand: jax.Array
    in_spec: pl.BlockSpec
    out_spec: pl.BlockSpec
    out_shape: jax.ShapeDtypeStruct
    shape: tuple


def _side_cast(w_all, layer, n_steps):
    depth, n_e, k, f = w_all.shape
    rows = n_e * k
    assert rows % n_steps == 0, (rows, n_steps)
    tr = rows // n_steps
    return _SideCast(
        operand=w_all.reshape(depth * rows, f),
        in_spec=pl.BlockSpec((tr, f), lambda i: (layer * n_steps + i, 0)),
        out_spec=pl.BlockSpec((tr, f), lambda i: (i, 0)),
        out_shape=jax.ShapeDtypeStruct((rows, f), BF16),
        shape=(n_e, k, f),
    )


def _proj_kernel(x_ref, g_ref, w_ref, o_ref, hn_ref):
    @pl.when(pl.program_id(1) == 0)
    def _():
        x = x_ref[...]
        ms = jnp.mean(x * x, axis=-1, keepdims=True)
        hn_ref[...] = (x * lax.rsqrt(ms + NORM_EPS) * g_ref[...]).astype(BF16)

    o_ref[...] = lax.dot_general(hn_ref[...], w_ref[...], (((1,), (1,)), ((), ())), preferred_element_type=F32)


def _in_projection(x, g, w_bf16):
    n = x.shape[0]
    tm = _pick_tile(n, (832, 640, 512, 256, 128, 64, 8))
    tn = N_PROJ // 4
    return pl.pallas_call(
        _proj_kernel,
        grid=(n // tm, N_PROJ // tn),
        in_specs=[
            pl.BlockSpec((tm, D_MODEL), lambda i, j: (i, 0)),
            pl.BlockSpec((1, D_MODEL), lambda i, j: (0, 0)),
            pl.BlockSpec((tn, D_MODEL), lambda i, j: (j, 0)),
        ],
        out_specs=pl.BlockSpec((tm, tn), lambda i, j: (i, j)),
        out_shape=jax.ShapeDtypeStruct((n, N_PROJ), F32),
        scratch_shapes=[pltpu.VMEM((tm, D_MODEL), BF16)],
        compiler_params=_cparams(("parallel", "arbitrary")),
        name="in_projection",
    )(x, g.reshape(1, D_MODEL), w_bf16)


def _mlstm_prompt_kernel(q_ref, k_ref, v_ref, o_ref, gt_ref, bias_ref, ng_ref, mixed_in_ref, wc_in_ref,
                         y_ref, cn_ref, m_ref, wc_out_ref):
    del mixed_in_ref
    wc_out_ref[...] = wc_in_ref[...].astype(wc_out_ref.dtype)
    rows_step = q_ref.shape[0]
    L = min(MLSTM_CHUNK, rows_step)
    n_chunks = rows_step // L

    @pl.when(pl.program_id(0) == 0)
    def _():
        cn_ref[...] = jnp.zeros_like(cn_ref)
        m_ref[...] = jnp.zeros_like(m_ref)

    pre = gt_ref[...] + bias_ref[...]
    lf = _log_sigmoid(pre)
    row = lax.broadcasted_iota(I32, (rows_step, rows_step), 0)
    col = lax.broadcasted_iota(I32, (rows_step, rows_step), 1)
    cum = ((row // L) == (col // L)) & (col <= row)
    b_all = _masked_row_sums(cum, lf)
    pre_t = pre.T
    b_t = b_all.T
    trow = lax.broadcasted_iota(I32, (L, L), 0)
    tcol = lax.broadcasted_iota(I32, (L, L), 1)
    tri = tcol <= trow
    lane = lax.broadcasted_iota(I32, (L, HEAD_W), 1)
    ones_col = (lane == 0).astype(BF16)
    for h in range(MLSTM_HEADS):
        sl = slice(HEAD_W * h, HEAD_W * (h + 1))
        m_prev = m_ref[h:h + 1, 0:1]
        cn = cn_ref[h]
        for c in range(n_chunks):
            cs = slice(L * c, L * (c + 1))
            q = q_ref[cs, sl]
            k = k_ref[cs, sl] * (HEAD_W ** -0.5)
            v = v_ref[cs, sl]
            b_col = b_all[cs, 4 + h:5 + h]
            i_col = pre[cs, h:h + 1]
            b_row = b_t[4 + h:5 + h, cs]
            i_row = pre_t[h:h + 1, cs]
            log_d = jnp.where(tri, b_col - b_row + i_row, -jnp.inf)
            log_inter = b_col + m_prev
            m_t = jnp.maximum(log_inter, jnp.max(log_d, axis=-1, keepdims=True))
            d_mat = jnp.exp(log_d - m_t)
            g_inter = jnp.exp(log_inter - m_t)
            qb = q.astype(BF16)
            kb = k.astype(BF16)
            s = lax.dot_general(qb, kb, (((1,), (1,)), ((), ())), preferred_element_type=F32)
            w = (s * d_mat).astype(BF16)
            v1 = jnp.concatenate([v.astype(BF16), ones_col], axis=1)
            nd = g_inter * jnp.dot(qb, cn.astype(BF16), preferred_element_type=F32)
            nd = nd + jnp.dot(w, v1, preferred_element_type=F32)
            num = nd[:, :HEAD_W]
            den = nd[:, HEAD_W:HEAD_W + 1]
            hh = num / jnp.maximum(jnp.abs(den), jnp.exp(-m_t))
            hh = _sigmoid(o_ref[cs, sl]) * hh
            hh = hh * lax.rsqrt(jnp.mean(hh * hh, axis=-1, keepdims=True) + NORM_EPS) * ng_ref[:, sl]
            y_ref[cs, sl] = hh.astype(y_ref.dtype)
            m_last = m_t[L - 1:L, :]
            b_last = b_col[L - 1:L, :]
            g_state = jnp.exp(b_last + m_prev - m_last)
            w_k = jnp.exp(b_last - b_col + i_col - m_last)
            kw = (k * w_k).astype(BF16)
            upd = lax.dot_general(kw, v1, (((0,), (0,)), ((), ())), preferred_element_type=F32)
            cn = g_state * cn + upd
            m_prev = m_last
        cn_ref[h] = cn
        m_ref[h:h + 1, :] = jnp.broadcast_to(m_prev, (1, LANES))


def _mlstm_prompt(proj, mixed, t_len, bias_row, norm_g, w_cast, layer):
    L = _pick_tile(t_len, (MLSTM_STEP_ROWS, MLSTM_CHUNK, 64, 32, 16, 8))
    side = _side_cast(w_cast, layer, t_len // L)

    def col(off):
        return pl.BlockSpec((L, GROUP_WIDTH), lambda i, o=off: (i, o // GROUP_WIDTH))

    return pl.pallas_call(
        _mlstm_prompt_kernel,
        grid=(t_len // L,),
        in_specs=[
            col(A_Q), col(A_K), col(A_V), col(A_O),
            pl.BlockSpec((L, LANES), lambda i: (i, A_G // LANES)),
            pl.BlockSpec((1, LANES), lambda i: (0, 0)),
            pl.BlockSpec((1, GROUP_WIDTH), lambda i: (0, 0)),
            pl.BlockSpec(memory_space=pl.ANY),
            side.in_spec,
        ],
        out_specs=[
            pl.BlockSpec((L, GROUP_WIDTH), lambda i: (i, MIX_COL_MLSTM)),
            pl.BlockSpec((MLSTM_HEADS, HEAD_W, 2 * HEAD_W), lambda i: (0, 0, 0)),
            pl.BlockSpec((SUBLANES, LANES), lambda i: (0, 0)),
            side.out_spec,
        ],
        out_shape=[
            jax.ShapeDtypeStruct(mixed.shape, mixed.dtype),
            jax.ShapeDtypeStruct((MLSTM_HEADS, HEAD_W, 2 * HEAD_W), F32),
            jax.ShapeDtypeStruct((SUBLANES, LANES), F32),
            side.out_shape,
        ],
        input_output_aliases={7: 0},
        compiler_params=_cparams(("arbitrary",)),
        name="mlstm_prompt",
    )(proj, proj, proj, proj, proj, bias_row, norm_g.reshape(1, GROUP_WIDTH), mixed, side.operand)


def _gla_prompt_kernel(q_ref, k_ref, v_ref, r_ref, lr_ref, wg_ref, bg_ref, ng_ref, mixed_in_ref, wc_in_ref,
                       y_ref, sp_ref, wc_out_ref):
    del mixed_in_ref
    wc_out_ref[...] = wc_in_ref[...].astype(wc_out_ref.dtype)
    rows_step = q_ref.shape[0]
    L = min(GLA_CHUNK, rows_step)
    n_chunks = rows_step // L
    n_sub = L // GLA_SUB

    @pl.when(pl.program_id(0) == 0)
    def _():
        sp_ref[...] = jnp.zeros_like(sp_ref)

    gate_pre = jnp.dot(lr_ref[...].astype(BF16), wg_ref[...], preferred_element_type=F32) + bg_ref[...]
    log_a = _log_sigmoid(gate_pre) * (1.0 / GLA_TAU)
    row = lax.broadcasted_iota(I32, (rows_step, rows_step), 0)
    col = lax.broadcasted_iota(I32, (rows_step, rows_step), 1)
    tri = ((row // L) == (col // L)) & (col <= row)
    b = _masked_row_sums(tri, log_a)
    q = q_ref[...] * (GLA_DK ** -0.5)
    k = k_ref[...]
    b_last = jnp.concatenate(
        [jnp.broadcast_to(b[L * c + L - 1:L * c + L, :], (L, b.shape[1])) for c in range(n_chunks)], axis=0)
    q_in = q * jnp.exp(b)
    k_dec = k * jnp.exp(b_last - b)
    lane16 = lax.broadcasted_iota(I32, (GLA_SUB, LANES), 1)
    lo16 = lane16 < GLA_DK
    srow = lax.broadcasted_iota(I32, (LANES, 2 * HEAD_W), 0)
    scol = lax.broadcasted_iota(I32, (LANES, 2 * HEAD_W), 1)
    block_diag = (srow < GLA_DK) == (scol < HEAD_W)
    for p in range(2):
        pls = slice(LANES * p, LANES * (p + 1))
        vp = v_ref[:, 2 * HEAD_W * p:2 * HEAD_W * (p + 1)].astype(BF16)
        b_t = b[:, pls].T
        sp = sp_ref[p]
        states = []
        for c in range(n_chunks):
            cs = slice(L * c, L * (c + 1))
            states.append(sp)
            dec_col = jnp.exp(b_t[:, L * c + L - 1:L * c + L])
            upd = lax.dot_general(k_dec[cs, pls].astype(BF16), vp[cs], (((0,), (0,)), ((), ())),
                                  preferred_element_type=F32)
            sp = jnp.where(block_diag, dec_col * sp + upd, 0.0)
        sp_ref[p] = sp
        outs = []
        for c in range(n_chunks):
            c0 = L * c
            o_inter = jnp.dot(q_in[c0:c0 + L, pls].astype(BF16), states[c].astype(BF16),
                              preferred_element_type=F32)
            rows = []
            for blk in range(n_sub):
                r0 = GLA_SUB * blk
                n = GLA_SUB * (blk + 1)
                qrows = slice(c0 + r0, c0 + r0 + GLA_SUB)
                krows = slice(c0, c0 + n)
                if blk == 0:
                    qs = q[qrows, pls] * jnp.exp(b[qrows, pls])
                    ks = k[krows, pls] * jnp.exp(-b[krows, pls])
                else:
                    anchor = b[c0 + r0 - 1:c0 + r0, pls]
                    qs = q[qrows, pls] * jnp.exp(b[qrows, pls] - anchor)
                    ks = k[krows, pls] * jnp.exp(anchor - b[krows, pls])
                qs2 = jnp.concatenate([jnp.where(lo16, qs, 0.0), jnp.where(lo16, 0.0, qs)], axis=0)
                att = lax.dot_general(qs2.astype(BF16), ks.astype(BF16), (((1,), (1,)), ((), ())),
                                      preferred_element_type=F32)
                trow = lax.broadcasted_iota(I32, (2 * GLA_SUB, n), 0)
                tcol = lax.broadcasted_iota(I32, (2 * GLA_SUB, n), 1)
                t_idx = r0 + jnp.where(trow >= GLA_SUB, trow - GLA_SUB, trow)
                att = jnp.where(tcol <= t_idx, att, 0.0)
                o2 = jnp.dot(att.astype(BF16), vp[krows], preferred_element_type=F32)
                rows.append(jnp.concatenate([o2[:GLA_SUB, :HEAD_W], o2[GLA_SUB:, HEAD_W:]], axis=1))
            outs.append(o_inter + jnp.concatenate(rows, axis=0))
        o = jnp.concatenate(outs, axis=0)
        for hh in range(2):
            head = 2 * p + hh
            hs = slice(HEAD_W * head, HEAD_W * (head + 1))
            oh = o[:, HEAD_W * hh:HEAD_W * (hh + 1)]
            oh = oh * lax.rsqrt(jnp.mean(oh * oh, axis=-1, keepdims=True) + NORM_EPS) * ng_ref[:, hs]
            y_ref[:, hs] = (oh * _silu(r_ref[:, hs])).astype(y_ref.dtype)


def _gla_prompt(proj, mixed, t_len, w_gate_pad, b_gate, norm_g, w_cast, layer):
    L = _pick_tile(t_len, (GLA_STEP_ROWS, GLA_CHUNK))
    side = _side_cast(w_cast, layer, t_len // L)
    return pl.pallas_call(
        _gla_prompt_kernel,
        grid=(t_len // L,),
        in_specs=[
            pl.BlockSpec((L, 256), lambda i: (i, G_Q // 256)),
            pl.BlockSpec((L, 256), lambda i: (i, G_K // 256)),
            pl.BlockSpec((L, GROUP_WIDTH), lambda i: (i, G_V // GROUP_WIDTH)),
            pl.BlockSpec((L, GROUP_WIDTH), lambda i: (i, G_R // GROUP_WIDTH)),
            pl.BlockSpec((L, LANES), lambda i: (i, G_LR // LANES)),
            pl.BlockSpec((LANES, 256), lambda i: (0, 0)),
            pl.BlockSpec((1, 256), lambda i: (0, 0)),
            pl.BlockSpec((1, GROUP_WIDTH), lambda i: (0, 0)),
            pl.BlockSpec(memory_space=pl.ANY),
            side.in_spec,
        ],
        out_specs=[
            pl.BlockSpec((L, GROUP_WIDTH), lambda i: (i, MIX_COL_GLA)),
            pl.BlockSpec((2, LANES, 2 * HEAD_W), lambda i: (0, 0, 0)),
            side.out_spec,
        ],
        out_shape=[
            jax.ShapeDtypeStruct(mixed.shape, mixed.dtype),
            jax.ShapeDtypeStruct((2, LANES, 2 * HEAD_W), F32),
            side.out_shape,
        ],
        input_output_aliases={8: 0},
        compiler_params=_cparams(("arbitrary",)),
        name="gla_prompt",
    )(proj, proj, proj, proj, proj, w_gate_pad, b_gate.reshape(1, 256), norm_g.reshape(1, GROUP_WIDTH), mixed,
      side.operand)


def _conv_norm_act(y, g_ref, be_ref):
    mu = jnp.mean(y, axis=-1, keepdims=True)
    yc = y - mu
    var = jnp.mean(yc * yc, axis=-1, keepdims=True)
    return _silu(yc * lax.rsqrt(var + NORM_EPS) * g_ref[...] + be_ref[...])


def _conv_prompt_kernel(ua_ref, ug_ref, ha_ref, hg_ref, w_ref, b_ref, g_ref, be_ref, mixed_in_ref, wc_in_ref,
                        y_ref, tail_ref, wc_out_ref, buf_ref, sh_ref):
    del mixed_in_ref
    wc_out_ref[...] = wc_in_ref[...].astype(wc_out_ref.dtype)
    tt = ua_ref.shape[0]
    span = tt + CONV_HALO
    halo = ha_ref[...] * _sigmoid(hg_ref[...])
    buf_ref[0:CONV_HALO, :] = jnp.where(pl.program_id(0) > 0, halo, 0.0)
    buf_ref[CONV_HALO:span, :] = ua_ref[...] * _sigmoid(ug_ref[...])
    buf_ref[span:span + SUBLANES, :] = jnp.zeros((SUBLANES, GROUP_WIDTH), F32)
    for k in range(1, SUBLANES):
        sh_ref[k] = buf_ref[k:k + span, :]
    base = CONV_HALO - (CONV_WIDTH - 1)
    for r in range(tt // CONV_ROWS):
        acc = jnp.zeros((CONV_ROWS, GROUP_WIDTH), F32)
        for j in range(CONV_WIDTH):
            s0 = r * CONV_ROWS + base + j
            k = s0 % SUBLANES
            a0 = s0 - k
            win = buf_ref[a0:a0 + CONV_ROWS, :] if k == 0 else sh_ref[k, a0:a0 + CONV_ROWS, :]
            acc = acc + w_ref[j:j + 1, :] * win
        y = _conv_norm_act(acc + b_ref[...], g_ref, be_ref)
        y_ref[r * CONV_ROWS:(r + 1) * CONV_ROWS, :] = y.astype(y_ref.dtype)
    tail_ref[...] = buf_ref[tt:span, :]


def _conv_prompt(proj, mixed, t_len, w, b, g, beta, w_cast, layer):
    tt = _pick_tile(t_len, (CONV_TILE, 128, 64))
    ratio = tt // CONV_HALO
    side = _side_cast(w_cast, layer, t_len // tt)
    vec = lambda: pl.BlockSpec((1, GROUP_WIDTH), lambda i: (0, 0))
    return pl.pallas_call(
        _conv_prompt_kernel,
        grid=(t_len // tt,),
        in_specs=[
            pl.BlockSpec((tt, GROUP_WIDTH), lambda i: (i, C_A // GROUP_WIDTH)),
            pl.BlockSpec((tt, GROUP_WIDTH), lambda i: (i, C_G // GROUP_WIDTH)),
            pl.BlockSpec((CONV_HALO, GROUP_WIDTH), lambda i: (jnp.maximum(i * ratio - 1, 0), C_A // GROUP_WIDTH)),
            pl.BlockSpec((CONV_HALO, GROUP_WIDTH), lambda i: (jnp.maximum(i * ratio - 1, 0), C_G // GROUP_WIDTH)),
            pl.BlockSpec((CONV_WIDTH, GROUP_WIDTH), lambda i: (0, 0)),
            vec(), vec(), vec(),
            pl.BlockSpec(memory_space=pl.ANY),
            side.in_spec,
        ],
        out_specs=[
            pl.BlockSpec((tt, GROUP_WIDTH), lambda i: (i, MIX_COL_CONV)),
            pl.BlockSpec((CONV_HALO, GROUP_WIDTH), lambda i: (0, 0)),
            side.out_spec,
        ],
        out_shape=[
            jax.ShapeDtypeStruct(mixed.shape, mixed.dtype),
            jax.ShapeDtypeStruct((CONV_HALO, GROUP_WIDTH), F32),
            side.out_shape,
        ],
        scratch_shapes=[pltpu.VMEM((tt + CONV_HALO + SUBLANES, GROUP_WIDTH), F32),
                        pltpu.VMEM((SUBLANES, tt + CONV_HALO, GROUP_WIDTH), F32)],
        input_output_aliases={8: 0},
        compiler_params=_cparams(("arbitrary",)),
        name="conv_prompt",
    )(proj, proj, proj, proj, w, b.reshape(1, -1), g.reshape(1, -1), beta.reshape(1, -1), mixed, side.operand)


def _swa_prompt_kernel(sink_ref, q_ref, kc_ref, vc_ref, kp_ref, vp_ref, mixed_in_ref, *rest):
    del mixed_in_ref
    if len(rest) == 3:
        w_ref, y_ref, wpad_ref = rest
        _pad_w_in_rows(w_ref, wpad_ref)
    else:
        (y_ref,) = rest
    bq = WINDOW
    n_blk = q_ref.shape[0] // bq
    first = pl.program_id(0) == 0
    k_full = jnp.concatenate([kp_ref[...], kc_ref[...]], axis=0)
    v_full = jnp.concatenate([vp_ref[...], vc_ref[...]], axis=0)
    k_sw_full = pltpu.roll(k_full, SWA_HEAD_DIM, 1).astype(BF16)
    v_sw_full = pltpu.roll(v_full, SWA_HEAD_DIM, 1).astype(BF16)
    k_full = k_full.astype(BF16)
    v_full = v_full.astype(BF16)
    tq = lax.broadcasted_iota(I32, (bq, 2 * bq), 0)
    kj = lax.broadcasted_iota(I32, (bq, 2 * bq), 1)
    band = (kj > tq) & (kj <= tq + WINDOW)
    lane = lax.broadcasted_iota(I32, (bq, LANES), 1)
    lo = lane < SWA_HEAD_DIM
    rep = SWA_HEADS // SWA_KV_HEADS
    for blk in range(n_blk):
        qs = slice(bq * blk, bq * (blk + 1))
        ks = slice(bq * blk, bq * (blk + 2))
        valid = band & (kj >= jnp.where(first, bq, 0)) if blk == 0 else band
        k_all, v_all, k_sw, v_sw = k_full[ks], v_full[ks], k_sw_full[ks], v_sw_full[ks]
        for c in range(SWA_HEADS // 2):
            qc = q_ref[qs, LANES * c:LANES * (c + 1)] * (SWA_HEAD_DIM ** -0.5)
            outs = []
            for hh in range(2):
                h = 2 * c + hh
                g = h // rep
                qm = jnp.where(lo if hh == 0 else jnp.logical_not(lo), qc, 0.0).astype(BF16)
                k_use = k_all if g == hh else k_sw
                v_use = v_all if g == hh else v_sw
                s = lax.dot_general(qm, k_use, (((1,), (1,)), ((), ())), preferred_element_type=F32)
                s = jnp.where(valid, s, -jnp.inf)
                sink = sink_ref[h]
                mx = jnp.maximum(jnp.max(s, axis=-1, keepdims=True), sink)
                p = jnp.exp(s - mx)
                den = jnp.sum(p, axis=-1, keepdims=True) + jnp.exp(sink - mx)
                p = (p / den).astype(BF16)
                outs.append(jnp.dot(p, v_use, preferred_element_type=F32))
            y_ref[qs, LANES * c:LANES * (c + 1)] = jnp.where(lo, outs[0], outs[1]).astype(y_ref.dtype)


def _swa_prompt(proj, mixed, t_len, sinks, w_in_all=None, next_layer=None):
    bq = WINDOW
    rows = _pick_tile(t_len, (SWA_STEP_ROWS, bq))
    ratio = rows // bq
    cur = lambda off: pl.BlockSpec((rows, LANES), lambda i: (i, off // LANES))
    prev = lambda off: pl.BlockSpec((bq, LANES), lambda i: (jnp.maximum(i * ratio - 1, 0), off // LANES))
    in_specs = [
        pl.BlockSpec(memory_space=pltpu.SMEM),
        pl.BlockSpec((rows, GROUP_WIDTH), lambda i: (i, S_Q // GROUP_WIDTH)),
        cur(S_K), cur(S_V), prev(S_K), prev(S_V),
        pl.BlockSpec(memory_space=pl.ANY),
    ]
    out_specs = [pl.BlockSpec((rows, GROUP_WIDTH), lambda i: (i, MIX_COL_SWA))]
    out_shape = [jax.ShapeDtypeStruct(mixed.shape, mixed.dtype)]
    operands = [sinks, proj, proj, proj, proj, proj, mixed]
    if next_layer is not None:
        side = _w_in_side(w_in_all, next_layer, t_len // rows)
        in_specs.append(side.in_spec)
        out_specs.append(side.out_spec)
        out_shape.append(side.out_shape)
        operands.append(side.operand)
    outs = pl.pallas_call(
        _swa_prompt_kernel,
        grid=(t_len // rows,),
        in_specs=in_specs, out_specs=out_specs, out_shape=out_shape,
        input_output_aliases={6: 0},
        compiler_params=_cparams(("arbitrary",)),
        name="swa_prompt",
    )(*operands)
    return outs[0], (outs[1] if next_layer is not None else None)


_T_MK, _T_MQ = 0, 512
_T_GA, _T_GK, _T_GQ = 1024, 1280, 1536
_T_ROWS = 1792


def _sample_kernel(ps_ref, mm_ref, bias_ref, n0_ref, c0_ref, s0_ref, cv0_ref, k0_ref, v0_ref,
                   qm_ref, sink_ref, wg_ref, bg_ref, mng_ref, gng_ref, cw_ref, cb_ref, cg_ref, cbe_ref,
                   mix_ref, c1_ref, n1_ref, m1_ref, s1_ref, cv1_ref, k1_ref, v1_ref,
                   tt_ref, bc_ref, num_ref, go_ref, yc_ref, yd_ref, ybuf, ybf, zbuf, zsem, ysem):
    i = pl.program_id(0)
    nb = ps_ref.shape[0]
    bb = c0_ref.shape[0]
    t_len = mix_ref.shape[0] - nb
    zr = zbuf.shape[0]

    def zero_copy(t):
        return pltpu.make_async_copy(zbuf, mix_ref.at[pl.ds(t * zr, zr)], zsem.at[0])

    def gla_gate(lr):
        gp = jnp.dot(lr.astype(BF16), wg_ref[...], preferred_element_type=F32) + bg_ref[...]
        return jnp.exp(_log_sigmoid(gp) * (1.0 / GLA_TAU))

    @pl.when(i == 0)
    def _():
        zbuf[...] = jnp.zeros_like(zbuf)
        for t in range(t_len // zr):
            zero_copy(t).start()
        for h in range(MLSTM_HEADS):
            kk = ps_ref[:, A_K + HEAD_W * h:A_K + HEAD_W * (h + 1)] * (HEAD_W ** -0.5)
            tt_ref[_T_MK + HEAD_W * h:_T_MK + HEAD_W * (h + 1), :] = kk.T.astype(BF16)
            qq = ps_ref[:, A_Q + HEAD_W * h:A_Q + HEAD_W * (h + 1)]
            tt_ref[_T_MQ + HEAD_W * h:_T_MQ + HEAD_W * (h + 1), :] = qq.T.astype(BF16)
        a_all = gla_gate(ps_ref[:, G_LR:G_LR + LANES])
        for p in range(2):
            pls = slice(LANES * p, LANES * (p + 1))
            tt_ref[_T_GA + LANES * p:_T_GA + LANES * (p + 1), :] = a_all[:, pls].T.astype(BF16)
            kk = ps_ref[:, G_K + LANES * p:G_K + LANES * (p + 1)]
            tt_ref[_T_GK + LANES * p:_T_GK + LANES * (p + 1), :] = kk.T.astype(BF16)
            qq = ps_ref[:, G_Q + LANES * p:G_Q + LANES * (p + 1)] * (GLA_DK ** -0.5)
            tt_ref[_T_GQ + LANES * p:_T_GQ + LANES * (p + 1), :] = qq.T.astype(BF16)

    r0 = pl.multiple_of(i * bb, bb)
    rows = pl.ds(r0, bb)

    pre = ps_ref[rows, A_G:A_G + LANES] + bias_ref[...]
    lfm = _log_sigmoid(pre) + mm_ref[...]
    f_al = pltpu.roll(lfm, LANES - MLSTM_HEADS, 1)
    m_t = jnp.maximum(f_al, pre)
    g_st = jnp.exp(f_al - m_t)
    w_k = jnp.exp(pre - m_t)
    m1_ref[...] = m_t
    n_new = []
    for h in range(MLSTM_HEADS):
        kk = ps_ref[rows, A_K + HEAD_W * h:A_K + HEAD_W * (h + 1)] * (HEAD_W ** -0.5)
        nn = g_st[:, h:h + 1] * n0_ref[:, HEAD_W * h:HEAD_W * (h + 1)] + w_k[:, h:h + 1] * kk
        n1_ref[:, HEAD_W * h:HEAD_W * (h + 1)] = nn
        n_new.append(nn)

    glu = ps_ref[rows, C_A:C_A + GROUP_WIDTH] * _sigmoid(ps_ref[rows, C_G:C_G + GROUP_WIDTH])
    yc_ref[...] = glu * cw_ref[CONV_WIDTH - 1:CONV_WIDTH, :]
    a_v = ps_ref[rows, A_V:A_V + GROUP_WIDTH]
    g_v = ps_ref[rows, G_V:G_V + GROUP_WIDTH]
    s_k = ps_ref[rows, S_K:S_K + LANES]
    s_v = ps_ref[rows, S_V:S_V + LANES]

    lane_b = lax.broadcasted_iota(I32, (nb, LANES), 0)
    key_row = lax.broadcasted_iota(I32, (SWA_HEADS, WINDOW), 1)
    lo_row = lax.broadcasted_iota(I32, (1, LANES), 1) < SWA_HEAD_DIM
    sink_col = sink_ref[:, 0:1]

    for j in range(bb):
        onehot = (lane_b == r0 + j).astype(BF16)
        bc_ref[...] = jnp.dot(tt_ref[...], onehot, preferred_element_type=F32)
        jrow = slice(j, j + 1)
        for h in range(MLSTM_HEADS):
            hs = slice(HEAD_W * h, HEAD_W * (h + 1))
            kbc = bc_ref[_T_MK + HEAD_W * h:_T_MK + HEAD_W * (h + 1), :]
            qbc = bc_ref[_T_MQ + HEAD_W * h:_T_MQ + HEAD_W * (h + 1), :]
            g1 = g_st[jrow, h:h + 1]
            w1 = w_k[jrow, h:h + 1]
            v_row = a_v[jrow, hs]
            c_new = g1 * c0_ref[j, h] + kbc * (w1 * v_row)
            c1_ref[j, h] = c_new
            num_ref[jrow, hs] = jnp.sum(qbc * c_new, axis=0, keepdims=True)
        for p in range(2):
            abc = bc_ref[_T_GA + LANES * p:_T_GA + LANES * (p + 1), :]
            kbc = bc_ref[_T_GK + LANES * p:_T_GK + LANES * (p + 1), :]
            qbc = bc_ref[_T_GQ + LANES * p:_T_GQ + LANES * (p + 1), :]
            for hh in range(2):
                head = 2 * p + hh
                hs = slice(HEAD_W * head, HEAD_W * (head + 1))
                ds_ = slice(GLA_DK * hh, GLA_DK * (hh + 1))
                v_row = g_v[jrow, hs]
                s_new = abc[ds_, :] * s0_ref[j, head] + kbc[ds_, :] * v_row
                s1_ref[j, head] = s_new
                go_ref[jrow, hs] = jnp.sum(qbc[ds_, :] * s_new, axis=0, keepdims=True)
        cache = cv0_ref[j]
        yc_ref[jrow, :] = yc_ref[jrow, :] + jnp.sum(cache * cw_ref[0:CONV_WIDTH - 1, :], axis=0, keepdims=True)
        cv1_ref[j, 0:CONV_WIDTH - 2, :] = cv0_ref[j, 1:CONV_WIDTH - 1, :]
        cv1_ref[j, CONV_WIDTH - 2:CONV_WIDTH - 1, :] = glu[jrow, :]
        k_new = s_k[jrow, :]
        v_new = s_v[jrow, :]
        k1_ref[j, 0:WINDOW - 1, :] = k0_ref[j, 1:WINDOW, :]
        k1_ref[j, WINDOW - 1:WINDOW, :] = k_new
        v1_ref[j, 0:WINDOW - 1, :] = v0_ref[j, 1:WINDOW, :]
        v1_ref[j, WINDOW - 1:WINDOW, :] = v_new
        qmat = qm_ref[j] * (SWA_HEAD_DIM ** -0.5)
        s_old = lax.dot_general(qmat.astype(BF16), k0_ref[j].astype(BF16), (((1,), (1,)), ((), ())),
                                preferred_element_type=F32)
        s_old = jnp.where(key_row >= 1, s_old, -jnp.inf)
        s_cur = jnp.sum(qmat * k_new, axis=-1, keepdims=True)
        mx = jnp.maximum(jnp.maximum(jnp.max(s_old, axis=-1, keepdims=True), s_cur), sink_col)
        p_old = jnp.exp(s_old - mx)
        p_cur = jnp.exp(s_cur - mx)
        den = jnp.sum(p_old, axis=-1, keepdims=True) + p_cur + jnp.exp(sink_col - mx)
        o = jnp.dot((p_old / den).astype(BF16), v0_ref[j].astype(BF16), preferred_element_type=F32)
        o = o + (p_cur / den) * v_new
        o_sw = pltpu.roll(o, SWA_HEAD_DIM, 1)
        for c in range(SWA_HEADS // 2):
            g = c // (SWA_HEADS // SWA_KV_HEADS // 2)
            left = (o if g == 0 else o_sw)[2 * c:2 * c + 1, :]
            right = (o_sw if g == 0 else o)[2 * c + 1:2 * c + 2, :]
            yd_ref[jrow, LANES * c:LANES * (c + 1)] = jnp.where(lo_row, left, right)

    for h in range(MLSTM_HEADS):
        hs = slice(HEAD_W * h, HEAD_W * (h + 1))
        qq = ps_ref[rows, A_Q + HEAD_W * h:A_Q + HEAD_W * (h + 1)]
        den = jnp.sum(qq * n_new[h], axis=-1, keepdims=True)
        hh = num_ref[:, hs] / jnp.maximum(jnp.abs(den), jnp.exp(-m_t[:, h:h + 1]))
        hh = _sigmoid(ps_ref[rows, A_O + HEAD_W * h:A_O + HEAD_W * (h + 1)]) * hh
        hh = hh * lax.rsqrt(jnp.mean(hh * hh, axis=-1, keepdims=True) + NORM_EPS) * mng_ref[:, hs]
        ybuf[rows, hs] = hh
    ybuf[rows, GROUP_WIDTH:2 * GROUP_WIDTH] = _conv_norm_act(yc_ref[...] + cb_ref[...], cg_ref, cbe_ref)
    for head in range(GLA_HEADS):
        hs = slice(HEAD_W * head, HEAD_W * (head + 1))
        oh = go_ref[:, hs]
        oh = oh * lax.rsqrt(jnp.mean(oh * oh, axis=-1, keepdims=True) + NORM_EPS) * gng_ref[:, hs]
        gr = ps_ref[rows, G_R + HEAD_W * head:G_R + HEAD_W * (head + 1)]
        ybuf[rows, 2 * GROUP_WIDTH + HEAD_W * head:2 * GROUP_WIDTH + HEAD_W * (head + 1)] = oh * _silu(gr)
    ybuf[rows, 3 * GROUP_WIDTH:4 * GROUP_WIDTH] = yd_ref[...]

    @pl.when(i == pl.num_programs(0) - 1)
    def _():
        ybf[...] = ybuf[...].astype(BF16)
        cp = pltpu.make_async_copy(ybf, mix_ref.at[pl.ds(t_len, nb)], ysem.at[0])
        cp.start()
        for t in range(t_len // zr):
            zero_copy(t).wait()
        cp.wait()


def _sample_mixers(proj_s, t_len, states, layer, prm):
    c_all, n_all, m_all, s_all, cv_all, k_all, v_all = states
    depth = c_all.shape[0]
    nb = proj_s.shape[0]
    bb = SAMPLE_BB
    assert nb == LANES and nb % bb == 0
    lb = layer * (nb // bb)
    n0, m0 = n_all[layer], m_all[layer]
    mm = jnp.concatenate([m0, m0, jnp.zeros((nb, LANES - 2 * MLSTM_HEADS), F32)], axis=1)
    n0f = n0.reshape(nb, GROUP_WIDTH)
    c0 = c_all.reshape((depth * nb,) + c_all.shape[2:])
    s0 = s_all.reshape((depth * nb,) + s_all.shape[2:])
    cv0 = cv_all.reshape((depth * nb,) + cv_all.shape[2:])
    k0f = k_all.reshape(depth * nb, WINDOW, LANES)
    v0f = v_all.reshape(depth * nb, WINDOW, LANES)
    sq = proj_s[:, S_Q:S_Q + GROUP_WIDTH].reshape(nb, SWA_KV_HEADS, SWA_HEADS // SWA_KV_HEADS, SWA_HEAD_DIM)
    zq = jnp.zeros_like(sq[:, 0])
    qm = jnp.concatenate([jnp.concatenate([sq[:, 0], zq], axis=-1), jnp.concatenate([zq, sq[:, 1]], axis=-1)], axis=1)
    sink_b = jnp.broadcast_to(prm['swa_sinks'].astype(F32)[:, None], (SWA_HEADS, LANES))

    full = lambda shape: pl.BlockSpec(shape, lambda i: (0,) * len(shape))
    rowb = lambda w: pl.BlockSpec((bb, w), lambda i: (i, 0))
    in_specs = [
        full((nb, N_PROJ)), rowb(LANES), full((1, LANES)), rowb(GROUP_WIDTH),
        pl.BlockSpec((bb, MLSTM_HEADS, HEAD_W, HEAD_W), lambda i: (lb + i, 0, 0, 0)),
        pl.BlockSpec((bb, GLA_HEADS, GLA_DK, HEAD_W), lambda i: (lb + i, 0, 0, 0)),
        pl.BlockSpec((bb, CONV_WIDTH - 1, GROUP_WIDTH), lambda i: (lb + i, 0, 0)),
        pl.BlockSpec((bb, WINDOW, LANES), lambda i: (lb + i, 0, 0)),
        pl.BlockSpec((bb, WINDOW, LANES), lambda i: (lb + i, 0, 0)),
        pl.BlockSpec((bb, SWA_HEADS, LANES), lambda i: (i, 0, 0)),
        full((SWA_HEADS, LANES)), full((LANES, 256)), full((1, 256)),
        full((1, GROUP_WIDTH)), full((1, GROUP_WIDTH)),
        full((CONV_WIDTH, GROUP_WIDTH)), full((1, GROUP_WIDTH)), full((1, GROUP_WIDTH)), full((1, GROUP_WIDTH)),
    ]
    out_specs = [
        pl.BlockSpec(memory_space=pl.ANY),
        pl.BlockSpec((bb, MLSTM_HEADS, HEAD_W, HEAD_W), lambda i: (i, 0, 0, 0)),
        rowb(GROUP_WIDTH), rowb(LANES),
        pl.BlockSpec((bb, GLA_HEADS, GLA_DK, HEAD_W), lambda i: (i, 0, 0, 0)),
        pl.BlockSpec((bb, CONV_WIDTH - 1, GROUP_WIDTH), lambda i: (i, 0, 0)),
        pl.BlockSpec((bb, WINDOW, LANES), lambda i: (i, 0, 0)),
        pl.BlockSpec((bb, WINDOW, LANES), lambda i: (i, 0, 0)),
    ]
    out_shape = [
        jax.ShapeDtypeStruct((t_len + nb, D_MODEL), BF16),
        jax.ShapeDtypeStruct(c_all.shape[1:], F32),
        jax.ShapeDtypeStruct((nb, GROUP_WIDTH), F32),
        jax.ShapeDtypeStruct((nb, LANES), F32),
        jax.ShapeDtypeStruct(s_all.shape[1:], F32),
        jax.ShapeDtypeStruct(cv_all.shape[1:], F32),
        jax.ShapeDtypeStruct((nb, WINDOW, LANES), F32),
        jax.ShapeDtypeStruct((nb, WINDOW, LANES), F32),
    ]
    scratch = [
        pltpu.VMEM((_T_ROWS, nb), BF16), pltpu.VMEM((_T_ROWS, LANES), F32),
        pltpu.VMEM((bb, GROUP_WIDTH), F32), pltpu.VMEM((bb, GROUP_WIDTH), F32), pltpu.VMEM((bb, GROUP_WIDTH), F32),
        pltpu.VMEM((bb, GROUP_WIDTH), F32),
        pltpu.VMEM((nb, D_MODEL), F32), pltpu.VMEM((nb, D_MODEL), BF16),
        pltpu.VMEM((_pick_tile(t_len, (512, 256, 128, 64, 16)), D_MODEL), BF16),
        pltpu.SemaphoreType.DMA((1,)), pltpu.SemaphoreType.DMA((1,)),
    ]
    mixed, c1, n1, m1, s1, cv1, k1, v1 = pl.pallas_call(
        _sample_kernel,
        grid=(nb // bb,),
        in_specs=in_specs, out_specs=out_specs, out_shape=out_shape, scratch_shapes=scratch,
        compiler_params=_cparams(("arbitrary",)),
        name="sample_mixers",
    )(proj_s, mm, prm['gate_bias'], n0f, c0, s0, cv0, k0f, v0f, qm, sink_b,
      prm['gla_w_gate_pad'], prm['gla_b_gate'].reshape(1, 256),
      prm['mlstm_norm_g'].reshape(1, -1), prm['gla_norm_g'].reshape(1, -1),
      prm['conv_w'], prm['conv_b'].reshape(1, -1), prm['conv_norm_g'].reshape(1, -1),
      prm['conv_norm_b'].reshape(1, -1))
    new_state = (c1, n1.reshape(n0.shape), m1[:, :MLSTM_HEADS], s1, cv1,
                 k1.reshape(k_all.shape[1:]), v1.reshape(v_all.shape[1:]))
    return mixed, new_state


def _outproj_router_kernel(x_ref, mix_ref, w_ref, g_ref, rw_ref, rb_ref,
                           x1_ref, hn_ref, ri_ref, rf_ref, cnt_ref):
    @pl.when(pl.program_id(0) == 0)
    def _():
        cnt_ref[...] = jnp.zeros_like(cnt_ref)

    counts = _route_rows(x_ref[...], mix_ref[...], w_ref, g_ref, rw_ref, rb_ref, cnt_ref[0:1, :],
                         x1_ref, hn_ref, ri_ref, rf_ref)
    cnt_ref[...] = jnp.broadcast_to(counts, cnt_ref.shape)


def _route_rows(x, mix, w_ref, g_ref, rw_ref, rb_ref, counts, x1_ref, hn_ref, ri_ref, rf_ref):
    tm = x.shape[0]
    x1 = x + jnp.dot(mix, w_ref[...], preferred_element_type=F32)
    x1_ref[...] = x1
    ms = jnp.mean(x1 * x1, axis=-1, keepdims=True)
    hn = x1 * lax.rsqrt(ms + NORM_EPS) * g_ref[...]
    hn_ref[...] = hn
    logits = jnp.dot(hn.astype(BF16), rw_ref[...], preferred_element_type=F32) + rb_ref[...]
    lane = lax.broadcasted_iota(I32, (tm, LANES), 1)
    big = jnp.int32(LANES)
    gl = jnp.where(lane < N_GROUPS, logits, -jnp.inf)
    gmax = jnp.max(gl, axis=-1, keepdims=True)
    g_sel = jnp.min(jnp.where(gl == gmax, lane, big), axis=-1, keepdims=True)
    g_w = 1.0 / jnp.sum(jnp.exp(gl - gmax), axis=-1, keepdims=True)
    e_lane = lane - N_GROUPS
    in_grp = (e_lane >= 0) & (e_lane < N_EXPERTS) & ((e_lane // EXPERTS_PER_GROUP) == g_sel)
    el = jnp.where(in_grp, logits, -jnp.inf)
    m1 = jnp.max(el, axis=-1, keepdims=True)
    i1 = jnp.min(jnp.where(el == m1, lane, big), axis=-1, keepdims=True)
    el2 = jnp.where(lane == i1, -jnp.inf, el)
    m2 = jnp.max(el2, axis=-1, keepdims=True)
    i2 = jnp.min(jnp.where(el2 == m2, lane, big), axis=-1, keepdims=True)
    r = jnp.exp(m2 - m1)
    p1 = 1.0 / (1.0 + r)
    gate1 = g_w * p1
    gate2 = g_w * (r * p1)
    sel1 = lane == i1
    sel2 = lane == i2
    onehot = jnp.where(sel1 | sel2, 1.0, 0.0)
    row = lax.broadcasted_iota(I32, (tm, tm), 0)
    col = lax.broadcasted_iota(I32, (tm, tm), 1)
    strict = jnp.where(col < row, 1.0, 0.0).astype(BF16)
    cum = jnp.dot(strict, onehot.astype(BF16), preferred_element_type=F32) + counts
    rank1 = jnp.sum(jnp.where(sel1, cum, 0.0), axis=-1, keepdims=True).astype(I32)
    rank2 = jnp.sum(jnp.where(sel2, cum, 0.0), axis=-1, keepdims=True).astype(I32)
    ri = jnp.where(lane == 0, i1 - N_GROUPS, jnp.where(lane == 1, i2 - N_GROUPS,
                   jnp.where(lane == 2, rank1, jnp.where(lane == 3, rank2, 0))))
    ri_ref[...] = ri
    rf_ref[...] = jnp.where(lane == 0, gate1, jnp.where(lane == 1, gate2, 0.0))
    return counts + jnp.sum(onehot, axis=0, keepdims=True)


def _outproj_router(x, mixed, w_out_bf16, norm_g, rw_pad, rb_pad):
    n = x.shape[0]
    tm = _pick_tile(n, (320, 256, 128, 64, 16))
    full = lambda shape: pl.BlockSpec(shape, lambda i: (0,) * len(shape))
    rowb = lambda w: pl.BlockSpec((tm, w), lambda i: (i, 0))
    return pl.pallas_call(
        _outproj_router_kernel,
        grid=(n // tm,),
        in_specs=[rowb(D_MODEL), rowb(D_MODEL), full((D_MODEL, D_MODEL)), full((1, D_MODEL)),
                  full((D_MODEL, LANES)), full((1, LANES))],
        out_specs=[rowb(D_MODEL), rowb(D_MODEL), rowb(LANES), rowb(LANES), full((SUBLANES, LANES))],
        out_shape=[
            jax.ShapeDtypeStruct((n, D_MODEL), F32),
            jax.ShapeDtypeStruct((n, D_MODEL), F32),
            jax.ShapeDtypeStruct((n, LANES), I32),
            jax.ShapeDtypeStruct((n, LANES), F32),
            jax.ShapeDtypeStruct((SUBLANES, LANES), F32),
        ],
        compiler_params=_cparams(("arbitrary",)),
        name="outproj_router",
    )(x, mixed, w_out_bf16, norm_g.reshape(1, D_MODEL), rw_pad, rb_pad)


def _dispatch_kernel(pos_ref, zt_ref, hn_ref, xs_ref, zbuf, sem, zsem):
    tm = hn_ref.shape[0]
    tile = zbuf.shape[0]
    base = pl.program_id(0) * tm

    @pl.when(pl.program_id(0) == 0)
    def _():
        zbuf[...] = jnp.zeros_like(zbuf)

        def zero_tile(k, carry):
            @pl.when(zt_ref[k] >= 0)
            def _():
                row = pl.multiple_of(zt_ref[k] * tile, tile)
                cp = pltpu.make_async_copy(zbuf, xs_ref.at[pl.ds(row, tile)], zsem.at[0])
                cp.start()
                cp.wait()
            return carry

        lax.fori_loop(0, zt_ref.shape[0], zero_tile, 0)

    def issue(r, carry):
        tok = base + r
        src = hn_ref.at[pl.ds(r, 1)]
        pltpu.make_async_copy(src, xs_ref.at[pl.ds(pos_ref[2 * tok], 1)], sem.at[0]).start(priority=0)
        pltpu.make_async_copy(src, xs_ref.at[pl.ds(pos_ref[2 * tok + 1], 1)], sem.at[1]).start(priority=1)
        return carry

    lax.fori_loop(0, tm, issue, 0, unroll=4)
    pltpu.make_async_copy(hn_ref, xs_ref.at[pl.ds(0, tm)], sem.at[0]).wait()
    pltpu.make_async_copy(hn_ref, xs_ref.at[pl.ds(0, tm)], sem.at[1]).wait()


def _dispatch(pos, zero_tiles, hn, n_rows, tile):
    n = hn.shape[0]
    tm = _pick_tile(n, (256, 128, 64, 8))
    return pl.pallas_call(
        _dispatch_kernel,
        grid_spec=pltpu.PrefetchScalarGridSpec(
            num_scalar_prefetch=2,
            grid=(n // tm,),
            in_specs=[pl.BlockSpec((tm, D_MODEL), lambda i, p, z: (i, 0))],
            out_specs=pl.BlockSpec(memory_space=pl.ANY),
            scratch_shapes=[pltpu.VMEM((tile, D_MODEL), F32), pltpu.SemaphoreType.DMA((2,)),
                            pltpu.SemaphoreType.DMA((1,))],
        ),
        out_shape=jax.ShapeDtypeStruct((n_rows, D_MODEL), F32),
        compiler_params=_cparams(("arbitrary",)),
        name="moe_dispatch",
    )(pos, zero_tiles, hn)


def _expert_kernel(te_ref, nt_ref, x_ref, wg_ref, wu_ref, wd_ref, y_ref):
    del te_ref
    used = pl.program_id(0) < nt_ref[0]

    @pl.when(used)
    def _():
        x = x_ref[...].astype(BF16)
        a = jnp.dot(x, wg_ref[0], preferred_element_type=F32)
        u = jnp.dot(x, wu_ref[0], preferred_element_type=F32)
        hmid = (_silu(a) * u).astype(BF16)
        y_ref[...] = jnp.dot(hmid, wd_ref[0], preferred_element_type=F32)

    @pl.when(jnp.logical_not(used))
    def _():
        y_ref[...] = jnp.zeros_like(y_ref)


def _expert_mlp(tile_expert, n_tiles_used, xs, wg, wu, wd, tile):
    n_tiles = xs.shape[0] // tile

    def row_map(t, te, nt):
        return (jnp.minimum(t, nt[0] - 1), 0)

    def w_map(t, te, nt):
        return (te[jnp.minimum(t, nt[0] - 1)], 0, 0)

    return pl.pallas_call(
        _expert_kernel,
        grid_spec=pltpu.PrefetchScalarGridSpec(
            num_scalar_prefetch=2,
            grid=(n_tiles,),
            in_specs=[
                pl.BlockSpec((tile, D_MODEL), row_map),
                pl.BlockSpec((1, D_MODEL, D_EXPERT), w_map),
                pl.BlockSpec((1, D_MODEL, D_EXPERT), w_map),
                pl.BlockSpec((1, D_EXPERT, D_MODEL), w_map),
            ],
            out_specs=pl.BlockSpec((tile, D_MODEL), lambda t, te, nt: (t, 0)),
        ),
        out_shape=jax.ShapeDtypeStruct(xs.shape, F32),
        compiler_params=_cparams(("arbitrary",)),
        name="expert_mlp",
    )(tile_expert, n_tiles_used, xs, wg, wu, wd)


def _combine_kernel(pos_ref, x1_ref, rf_ref, fg_ref, ys_ref, o_ref, buf_a, buf_b, sem, *, final_norm, row0):
    tm = x1_ref.shape[0]
    base = row0 + pl.program_id(0) * tm

    def issue(r, carry):
        tok = base + r
        pltpu.make_async_copy(ys_ref.at[pl.ds(pos_ref[2 * tok], 1)], buf_a.at[pl.ds(r, 1)],
                              sem.at[0]).start(priority=0)
        pltpu.make_async_copy(ys_ref.at[pl.ds(pos_ref[2 * tok + 1], 1)], buf_b.at[pl.ds(r, 1)],
                              sem.at[1]).start(priority=1)
        return carry

    lax.fori_loop(0, tm, issue, 0, unroll=4)
    pltpu.make_async_copy(ys_ref.at[pl.ds(0, tm)], buf_a, sem.at[0]).wait()
    pltpu.make_async_copy(ys_ref.at[pl.ds(0, tm)], buf_b, sem.at[1]).wait()
    rf = rf_ref[...]
    x2 = x1_ref[...] + rf[:, 0:1] * buf_a[...] + rf[:, 1:2] * buf_b[...]
    if final_norm:
        ms = jnp.mean(x2 * x2, axis=-1, keepdims=True)
        x2 = x2 * lax.rsqrt(ms + NORM_EPS) * fg_ref[...]
    o_ref[...] = x2


def _combine(pos, x1, rf, ys, final_g, final_norm, row0=0, n_rows=None):
    n = x1.shape[0]
    n_rows = n if n_rows is None else n_rows
    tm = _pick_tile(n_rows, (256, 128, 64, 8))
    assert row0 % tm == 0
    b0 = row0 // tm
    return pl.pallas_call(
        functools.partial(_combine_kernel, final_norm=final_norm, row0=row0),
        grid_spec=pltpu.PrefetchScalarGridSpec(
            num_scalar_prefetch=1,
            grid=(n_rows // tm,),
            in_specs=[
                pl.BlockSpec((tm, D_MODEL), lambda i, p: (b0 + i, 0)),
                pl.BlockSpec((tm, LANES), lambda i, p: (b0 + i, 0)),
                pl.BlockSpec((1, D_MODEL), lambda i, p: (0, 0)),
                pl.BlockSpec(memory_space=pl.ANY),
            ],
            out_specs=pl.BlockSpec((tm, D_MODEL), lambda i, p: (i, 0)),
            scratch_shapes=[pltpu.VMEM((tm, D_MODEL), F32), pltpu.VMEM((tm, D_MODEL), F32),
                            pltpu.SemaphoreType.DMA((2,))],
        ),
        out_shape=jax.ShapeDtypeStruct((n_rows, D_MODEL), F32),
        compiler_params=_cparams(("arbitrary",)),
        name="moe_combine",
    )(pos, x1, rf, final_g.reshape(1, D_MODEL), ys)


W_IN_COL_BLOCK = LANES


def _pad_w_in_rows(w_ref, o_ref):
    cols = o_ref.shape[1]
    at = 0
    for lo, hi, dst in sorted(_W_IN_SEGMENTS, key=lambda s: s[2]):
        if dst > at:
            o_ref[at:dst, :] = jnp.zeros((dst - at, cols), o_ref.dtype)
        o_ref[dst:dst + (hi - lo), :] = w_ref[0, lo:hi, :].astype(o_ref.dtype)
        at = dst + (hi - lo)
    if at < N_PROJ:
        o_ref[at:N_PROJ, :] = jnp.zeros((N_PROJ - at, cols), o_ref.dtype)


def _w_in_side(w_in_t, layer, n_steps):
    depth, n_in, d = w_in_t.shape
    n_blocks = min(n_steps, d // W_IN_COL_BLOCK)
    assert d % (n_blocks * LANES) == 0
    cols = d // n_blocks
    blk = lambda i: jnp.minimum(i, n_blocks - 1)
    return _SideCast(
        operand=w_in_t,
        in_spec=pl.BlockSpec((1, n_in, cols), lambda i: (layer, 0, blk(i))),
        out_spec=pl.BlockSpec((N_PROJ, cols), lambda i: (0, blk(i))),
        out_shape=jax.ShapeDtypeStruct((N_PROJ, d), BF16),
        shape=(N_PROJ, d),
    )


def _pad_w_in(w_in_t, layer):
    n_steps = w_in_t.shape[2] // W_IN_COL_BLOCK
    side = _w_in_side(w_in_t, layer, n_steps)
    return pl.pallas_call(
        _pad_w_in_rows,
        grid=(n_steps,),
        in_specs=[side.in_spec], out_specs=side.out_spec, out_shape=side.out_shape,
        compiler_params=_cparams(("arbitrary",)),
        name="pad_w_in",
    )(side.operand)


def _moe(x1, hn, ri, rf, counts, experts, final_g, final_norm, tile, t_len):
    n = x1.shape[0]
    n_tiles = -(-2 * n // tile) + N_EXPERTS
    cnt = counts[0, N_GROUPS:N_GROUPS + N_EXPERTS].astype(I32)
    tiles_per = (cnt + tile - 1) // tile
    tile_end = jnp.cumsum(tiles_per)
    row_off = (tile_end - tiles_per) * tile
    pos = (row_off[ri[:, 0:2]] + ri[:, 2:4]).reshape(2 * n)
    tile_ids = jnp.arange(n_tiles, dtype=I32)
    tile_expert = jnp.minimum(jnp.sum((tile_ids[:, None] >= tile_end[None, :]).astype(I32), axis=1), N_EXPERTS - 1)
    n_used = tile_end[N_EXPERTS - 1:N_EXPERTS].astype(I32)
    last_tile = jnp.where(tiles_per > 0, tile_end - 1, -1).astype(I32)
    zero_tiles = jnp.concatenate([last_tile, jnp.where(tile_ids >= n_used[0], tile_ids, -1)])
    xs = _dispatch(pos, zero_tiles, hn, n_tiles * tile, tile)
    ys = _expert_mlp(tile_expert, n_used, xs, *experts, tile)
    if final_norm:
        return (_combine(pos, x1, rf, ys, final_g, True, 0, t_len),
                _combine(pos, x1, rf, ys, final_g, True, t_len, n - t_len))
    return _combine(pos, x1, rf, ys, final_g, False)


def _layer(x, t_len, states, layer, prm, w_in_pad, final_g, final_norm, moe_tile):
    proj = _in_projection(x, prm['norm_mix_g'], w_in_pad)
    mixed, new_s = _sample_mixers(proj[t_len:], t_len, states, layer, prm)
    e_shape = prm['expert_w_gate_all'].shape[1:]
    mixed, cn, m_p, wg = _mlstm_prompt(proj, mixed, t_len, prm['gate_bias'], prm['mlstm_norm_g'],
                                       prm['expert_w_gate_all'], layer)
    mixed, conv_tail, wu = _conv_prompt(proj, mixed, t_len, prm['conv_w'], prm['conv_b'], prm['conv_norm_g'],
                                        prm['conv_norm_b'], prm['expert_w_up_all'], layer)
    mixed, sp, wd = _gla_prompt(proj, mixed, t_len, prm['gla_w_gate_pad'], prm['gla_b_gate'], prm['gla_norm_g'],
                                prm['expert_w_down_all'], layer)
    mixed, next_w_in_pad = _swa_prompt(proj, mixed, t_len, prm['swa_sinks'], prm['w_in_all'],
                                       None if final_norm else layer + 1)
    experts = (wg.reshape(e_shape), wu.reshape(e_shape), wd.reshape(prm['expert_w_down_all'].shape[1:]))
    x1, hn, ri, rf, counts = _outproj_router(x, mixed, prm['w_out'], prm['norm_ffn_g'], prm['router_w'], prm['router_b'])
    x2 = _moe(x1, hn, ri, rf, counts, experts, final_g, final_norm, moe_tile, t_len)
    p_c = cn[None, :, :, :HEAD_W]
    p_n = cn[None, :, :, HEAD_W]
    p_m = m_p[None, :MLSTM_HEADS, 0]
    p_s = jnp.stack([sp[0, :GLA_DK, :HEAD_W], sp[0, GLA_DK:, HEAD_W:],
                     sp[1, :GLA_DK, :HEAD_W], sp[1, GLA_DK:, HEAD_W:]])[None]
    p_conv = conv_tail[None, CONV_HALO - (CONV_WIDTH - 1):]
    p_k = proj[t_len - WINDOW:t_len, S_K:S_K + LANES].reshape(1, WINDOW, SWA_KV_HEADS, SWA_HEAD_DIM)
    p_v = proj[t_len - WINDOW:t_len, S_V:S_V + LANES].reshape(1, WINDOW, SWA_KV_HEADS, SWA_HEAD_DIM)
    return x2, (p_c, p_n, p_m, p_s, p_conv, p_k, p_v), new_s, next_w_in_pad


def _forward(x_prompt, x_sample, states, layer_params, final_norm_g, moe_tile=MOE_TILE):
    t_len = x_prompt.shape[1]
    x = jnp.concatenate([x_prompt[0], x_sample[:, 0]], axis=0)
    new_p, new_s = [], []
    depth = len(layer_params)
    w_in_pad = _pad_w_in(layer_params[0]['w_in_all'], 0)
    for l, prm in enumerate(layer_params):
        x, sp, ss, w_in_pad = _layer(x, t_len, states, l, prm, w_in_pad, final_norm_g, l == depth - 1, moe_tile)
        new_p.append(sp)
        new_s.append(ss)
    y_prompt, y_sample = x
    y_prompt = y_prompt[None]
    y_sample = y_sample[:, None]
    p_states = [jnp.stack(parts) for parts in zip(*new_p)]
    s_states = [jnp.stack(parts) for parts in zip(*new_s)]
    return (y_prompt, y_sample, *p_states, *s_states)


def _prep_layer_params(l, norm_mix_g, w_in, mlstm_b_i, mlstm_b_f, mlstm_norm_g, conv_w, conv_b, conv_norm_g,
                       conv_norm_b, gla_w_gate, gla_b_gate, gla_norm_g, swa_sinks, w_out, norm_ffn_g,
                       router_group_w, router_group_b, router_expert_w, router_expert_b, expert_w_gate,
                       expert_w_up, expert_w_down):
    gate_bias = jnp.concatenate([mlstm_b_i[l], mlstm_b_f[l], jnp.zeros((LANES - 2 * MLSTM_HEADS,), F32)])
    rw = jnp.concatenate([router_group_w[l], router_expert_w[l],
                          jnp.zeros((D_MODEL, LANES - N_GROUPS - N_EXPERTS), F32)], axis=1)
    rb = jnp.concatenate([router_group_b[l], router_expert_b[l],
                          jnp.zeros((LANES - N_GROUPS - N_EXPERTS,), F32)])
    return {
        'norm_mix_g': norm_mix_g[l],
        'w_in_all': jnp.swapaxes(w_in, 1, 2),
        'gate_bias': gate_bias.reshape(1, LANES),
        'mlstm_norm_g': mlstm_norm_g[l],
        'conv_w': conv_w[l], 'conv_b': conv_b[l], 'conv_norm_g': conv_norm_g[l], 'conv_norm_b': conv_norm_b[l],
        'gla_w_gate_pad': jnp.concatenate(
            [gla_w_gate[l], jnp.zeros((LANES - GLA_LOWRANK, GLA_HEADS * GLA_DK), F32)], axis=0).astype(BF16),
        'gla_b_gate': gla_b_gate[l], 'gla_norm_g': gla_norm_g[l],
        'swa_sinks': swa_sinks[l],
        'w_out': w_out[l].astype(BF16),
        'norm_ffn_g': norm_ffn_g[l],
        'router_w': rw.astype(BF16), 'router_b': rb.reshape(1, LANES),
        'expert_w_gate_all': expert_w_gate, 'expert_w_up_all': expert_w_up, 'expert_w_down_all': expert_w_down,
    }


def kernel(x_prompt, x_sample, state_mlstm_C, state_mlstm_n, state_mlstm_m, state_gla_S, cache_conv, cache_swa_k, cache_swa_v, norm_mix_g, w_in, mlstm_b_i, mlstm_b_f, mlstm_norm_g, conv_w, conv_b, conv_norm_g, conv_norm_b, gla_w_gate, gla_b_gate, gla_norm_g, swa_sinks, w_out, norm_ffn_g, router_group_w, router_group_b, router_expert_w, router_expert_b, expert_w_gate, expert_w_up, expert_w_down, final_norm_g):
    depth = w_in.shape[0]
    weights = (norm_mix_g, w_in, mlstm_b_i, mlstm_b_f, mlstm_norm_g, conv_w, conv_b, conv_norm_g, conv_norm_b,
               gla_w_gate, gla_b_gate, gla_norm_g, swa_sinks, w_out, norm_ffn_g, router_group_w, router_group_b,
               router_expert_w, router_expert_b, expert_w_gate, expert_w_up, expert_w_down)
    layer_params = [_prep_layer_params(l, *weights) for l in range(depth)]
    states = (state_mlstm_C, state_mlstm_n, state_mlstm_m, state_gla_S, cache_conv, cache_swa_k, cache_swa_v)
    return _forward(x_prompt, x_sample, states, layer_params, final_norm_g)
```

```python
import functools
from typing import NamedTuple

import jax
import jax.numpy as jnp
from jax import lax
from jax.experimental import pallas as pl
from jax.experimental.pallas import tpu as pltpu

F32 = jnp.float32
BF16 = jnp.bfloat16
I32 = jnp.int32

D_MODEL = 2048
GROUP_WIDTH = 512
HEAD_W = 128
MLSTM_HEADS = 4
GLA_HEADS = 4
GLA_DK = 64
GLA_LOWRANK = 16
GLA_TAU = 16.0
CONV_WIDTH = 31
SWA_HEADS = 8
SWA_KV_HEADS = 2
SWA_HEAD_DIM = 64
WINDOW = 128
N_GROUPS = 4
EXPERTS_PER_GROUP = 4
N_EXPERTS = 16
D_EXPERT = 1024
NORM_EPS = 1e-6
LANES = 128
SUBLANES = 8

A_Q, A_K, A_V, A_O = 0, 512, 1024, 1536
C_A, C_G = 2048, 2560
G_V, G_R = 3072, 3584
S_Q = 4096
G_Q, G_K = 4608, 4864
A_G, G_LR, S_K, S_V = 5120, 5248, 5376, 5504
N_PROJ = 5632
MIX_COL_MLSTM, MIX_COL_CONV, MIX_COL_GLA, MIX_COL_SWA = 0, 1, 2, 3
_W_IN_SEGMENTS = (
    (0, 512, A_Q), (512, 1024, A_K), (1024, 1536, A_V), (1536, 2048, A_O),
    (2048, 2056, A_G),
    (2056, 2568, C_A), (2568, 3080, C_G),
    (3080, 3336, G_Q), (3336, 3592, G_K), (3592, 4104, G_V), (4104, 4616, G_R),
    (4616, 4632, G_LR),
    (4632, 5144, S_Q), (5144, 5272, S_K), (5272, 5400, S_V),
)

MLSTM_CHUNK = 128
MLSTM_STEP_ROWS = 256
GLA_CHUNK = 64
GLA_STEP_ROWS = 256
GLA_SUB = 16
SWA_STEP_ROWS = 256
CONV_TILE = 256
CONV_ROWS = 64
CONV_HALO = 32
SAMPLE_BB = 8
MOE_TILE = 256
ROW_DMA_TILES = (640, 512, 256, 128, 64, 8)
VMEM_LIMIT = 56 * 1024 * 1024


def _cparams(sem, vmem=VMEM_LIMIT):
    return pltpu.CompilerParams(dimension_semantics=sem, vmem_limit_bytes=vmem)


def _log_sigmoid(x):
    return jnp.minimum(x, 0.0) - jnp.log1p(jnp.exp(-jnp.abs(x)))


def _sigmoid(x):
    return 1.0 / (1.0 + jnp.exp(-x))


def _silu(x):
    return x * _sigmoid(x)


def _masked_row_sums(mask, x):
    m = jnp.where(mask, 1.0, 0.0).astype(BF16)
    hi = x.astype(BF16)
    r1 = x - hi.astype(F32)
    mid = r1.astype(BF16)
    lo = (r1 - mid.astype(F32)).astype(BF16)
    out = jnp.dot(m, hi, preferred_element_type=F32)
    out = out + jnp.dot(m, mid, preferred_element_type=F32)
    return out + jnp.dot(m, lo, preferred_element_type=F32)


def _pick_tile(n, candidates):
    for c in candidates:
        if n % c == 0:
            return c
    raise ValueError(f"no tile for {n} in {candidates}")


class _SideCast(NamedTuple):
    operand: jax.Array
    in_spec: pl.BlockSpec
    out_spec: pl.BlockSpec
    out_shape: jax.ShapeDtypeStruct
    shape: tuple


def _side_cast(w_all, layer, n_steps):
    depth, n_e, k, f = w_all.shape
    rows = n_e * k
    assert rows % n_steps == 0, (rows, n_steps)
    tr = rows // n_steps
    return _SideCast(
        operand=w_all.reshape(depth * rows, f),
        in_spec=pl.BlockSpec((tr, f), lambda i: (layer * n_steps + i, 0)),
        out_spec=pl.BlockSpec((tr, f), lambda i: (i, 0)),
        out_shape=jax.ShapeDtypeStruct((rows, f), BF16),
        shape=(n_e, k, f),
    )


def _proj_kernel(x_ref, g_ref, w_ref, o_ref, hn_ref):
    @pl.when(pl.program_id(1) == 0)
    def _():
        x = x_ref[...]
        ms = jnp.mean(x * x, axis=-1, keepdims=True)
        hn_ref[...] = (x * lax.rsqrt(ms + NORM_EPS) * g_ref[...]).astype(BF16)

    o_ref[...] = lax.dot_general(hn_ref[...], w_ref[...], (((1,), (1,)), ((), ())), preferred_element_type=F32)


def _in_projection(x, g, w_bf16):
    n = x.shape[0]
    tm = _pick_tile(n, (832, 640, 512, 256, 128, 64, 8))
    tn = N_PROJ // 4
    return pl.pallas_call(
        _proj_kernel,
        grid=(n // tm, N_PROJ // tn),
        in_specs=[
            pl.BlockSpec((tm, D_MODEL), lambda i, j: (i, 0)),
            pl.BlockSpec((1, D_MODEL), lambda i, j: (0, 0)),
            pl.BlockSpec((tn, D_MODEL), lambda i, j: (j, 0)),
        ],
        out_specs=pl.BlockSpec((tm, tn), lambda i, j: (i, j)),
        out_shape=jax.ShapeDtypeStruct((n, N_PROJ), F32),
        scratch_shapes=[pltpu.VMEM((tm, D_MODEL), BF16)],
        compiler_params=_cparams(("parallel", "arbitrary")),
        name="in_projection",
    )(x, g.reshape(1, D_MODEL), w_bf16)


def _mlstm_prompt_kernel(q_ref, k_ref, v_ref, o_ref, gt_ref, bias_ref, ng_ref, mixed_in_ref, wc_in_ref,
                         y_ref, cn_ref, m_ref, wc_out_ref):
    del mixed_in_ref
    wc_out_ref[...] = wc_in_ref[...].astype(wc_out_ref.dtype)
    rows_step = q_ref.shape[0]
    L = min(MLSTM_CHUNK, rows_step)
    n_chunks = rows_step // L

    @pl.when(pl.program_id(0) == 0)
    def _():
        cn_ref[...] = jnp.zeros_like(cn_ref)
        m_ref[...] = jnp.zeros_like(m_ref)

    pre = gt_ref[...] + bias_ref[...]
    lf = _log_sigmoid(pre)
    row = lax.broadcasted_iota(I32, (rows_step, rows_step), 0)
    col = lax.broadcasted_iota(I32, (rows_step, rows_step), 1)
    cum = ((row // L) == (col // L)) & (col <= row)
    b_all = _masked_row_sums(cum, lf)
    pre_t = pre.T
    b_t = b_all.T
    trow = lax.broadcasted_iota(I32, (L, L), 0)
    tcol = lax.broadcasted_iota(I32, (L, L), 1)
    tri = tcol <= trow
    lane = lax.broadcasted_iota(I32, (L, HEAD_W), 1)
    ones_col = (lane == 0).astype(BF16)
    for h in range(MLSTM_HEADS):
        sl = slice(HEAD_W * h, HEAD_W * (h + 1))
        m_prev = m_ref[h:h + 1, 0:1]
        cn = cn_ref[h]
        for c in range(n_chunks):
            cs = slice(L * c, L * (c + 1))
            q = q_ref[cs, sl]
            k = k_ref[cs, sl] * (HEAD_W ** -0.5)
            v = v_ref[cs, sl]
            b_col = b_all[cs, 4 + h:5 + h]
            i_col = pre[cs, h:h + 1]
            b_row = b_t[4 + h:5 + h, cs]
            i_row = pre_t[h:h + 1, cs]
            log_d = jnp.where(tri, b_col - b_row + i_row, -jnp.inf)
            log_inter = b_col + m_prev
            m_t = jnp.maximum(log_inter, jnp.max(log_d, axis=-1, keepdims=True))
            d_mat = jnp.exp(log_d - m_t)
            g_inter = jnp.exp(log_inter - m_t)
            qb = q.astype(BF16)
            kb = k.astype(BF16)
            s = lax.dot_general(qb, kb, (((1,), (1,)), ((), ())), preferred_element_type=F32)
            w = (s * d_mat).astype(BF16)
            v1 = jnp.concatenate([v.astype(BF16), ones_col], axis=1)
            nd = g_inter * jnp.dot(qb, cn.astype(BF16), preferred_element_type=F32)
            nd = nd + jnp.dot(w, v1, preferred_element_type=F32)
            num = nd[:, :HEAD_W]
            den = nd[:, HEAD_W:HEAD_W + 1]
            hh = num / jnp.maximum(jnp.abs(den), jnp.exp(-m_t))
            hh = _sigmoid(o_ref[cs, sl]) * hh
            hh = hh * lax.rsqrt(jnp.mean(hh * hh, axis=-1, keepdims=True) + NORM_EPS) * ng_ref[:, sl]
            y_ref[cs, sl] = hh.astype(y_ref.dtype)
            m_last = m_t[L - 1:L, :]
            b_last = b_col[L - 1:L, :]
            g_state = jnp.exp(b_last + m_prev - m_last)
            w_k = jnp.exp(b_last - b_col + i_col - m_last)
            kw = (k * w_k).astype(BF16)
            upd = lax.dot_general(kw, v1, (((0,), (0,)), ((), ())), preferred_element_type=F32)
            cn = g_state * cn + upd
            m_prev = m_last
        cn_ref[h] = cn
        m_ref[h:h + 1, :] = jnp.broadcast_to(m_prev, (1, LANES))


def _mlstm_prompt(proj, mixed, t_len, bias_row, norm_g, w_cast, layer):
    L = _pick_tile(t_len, (MLSTM_STEP_ROWS, MLSTM_CHUNK, 64, 32, 16, 8))
    side = _side_cast(w_cast, layer, t_len // L)

    def col(off):
        return pl.BlockSpec((L, GROUP_WIDTH), lambda i, o=off: (i, o // GROUP_WIDTH))

    return pl.pallas_call(
        _mlstm_prompt_kernel,
        grid=(t_len // L,),
        in_specs=[
            col(A_Q), col(A_K), col(A_V), col(A_O),
            pl.BlockSpec((L, LANES), lambda i: (i, A_G // LANES)),
            pl.BlockSpec((1, LANES), lambda i: (0, 0)),
            pl.BlockSpec((1, GROUP_WIDTH), lambda i: (0, 0)),
            pl.BlockSpec(memory_space=pl.ANY),
            side.in_spec,
        ],
        out_specs=[
            pl.BlockSpec((L, GROUP_WIDTH), lambda i: (i, MIX_COL_MLSTM)),
            pl.BlockSpec((MLSTM_HEADS, HEAD_W, 2 * HEAD_W), lambda i: (0, 0, 0)),
            pl.BlockSpec((SUBLANES, LANES), lambda i: (0, 0)),
            side.out_spec,
        ],
        out_shape=[
            jax.ShapeDtypeStruct(mixed.shape, mixed.dtype),
            jax.ShapeDtypeStruct((MLSTM_HEADS, HEAD_W, 2 * HEAD_W), F32),
            jax.ShapeDtypeStruct((SUBLANES, LANES), F32),
            side.out_shape,
        ],
        input_output_aliases={7: 0},
        compiler_params=_cparams(("arbitrary",)),
        name="mlstm_prompt",
    )(proj, proj, proj, proj, proj, bias_row, norm_g.reshape(1, GROUP_WIDTH), mixed, side.operand)


def _gla_prompt_kernel(q_ref, k_ref, v_ref, r_ref, lr_ref, wg_ref, bg_ref, ng_ref, mixed_in_ref, wc_in_ref,
                       y_ref, sp_ref, wc_out_ref):
    del mixed_in_ref
    wc_out_ref[...] = wc_in_ref[...].astype(wc_out_ref.dtype)
    rows_step = q_ref.shape[0]
    L = min(GLA_CHUNK, rows_step)
    n_chunks = rows_step // L
    n_sub = L // GLA_SUB

    @pl.when(pl.program_id(0) == 0)
    def _():
        sp_ref[...] = jnp.zeros_like(sp_ref)

    gate_pre = jnp.dot(lr_ref[...].astype(BF16), wg_ref[...], preferred_element_type=F32) + bg_ref[...]
    log_a = _log_sigmoid(gate_pre) * (1.0 / GLA_TAU)
    row = lax.broadcasted_iota(I32, (rows_step, rows_step), 0)
    col = lax.broadcasted_iota(I32, (rows_step, rows_step), 1)
    tri = ((row // L) == (col // L)) & (col <= row)
    b = _masked_row_sums(tri, log_a)
    q = q_ref[...] * (GLA_DK ** -0.5)
    k = k_ref[...]
    b_last = jnp.concatenate(
        [jnp.broadcast_to(b[L * c + L - 1:L * c + L, :], (L, b.shape[1])) for c in range(n_chunks)], axis=0)
    q_in = q * jnp.exp(b)
    k_dec = k * jnp.exp(b_last - b)
    lane16 = lax.broadcasted_iota(I32, (GLA_SUB, LANES), 1)
    lo16 = lane16 < GLA_DK
    srow = lax.broadcasted_iota(I32, (LANES, 2 * HEAD_W), 0)
    scol = lax.broadcasted_iota(I32, (LANES, 2 * HEAD_W), 1)
    block_diag = (srow < GLA_DK) == (scol < HEAD_W)
    for p in range(2):
        pls = slice(LANES * p, LANES * (p + 1))
        vp = v_ref[:, 2 * HEAD_W * p:2 * HEAD_W * (p + 1)].astype(BF16)
        b_t = b[:, pls].T
        sp = sp_ref[p]
        states = []
        for c in range(n_chunks):
            cs = slice(L * c, L * (c + 1))
            states.append(sp)
            dec_col = jnp.exp(b_t[:, L * c + L - 1:L * c + L])
            upd = lax.dot_general(k_dec[cs, pls].astype(BF16), vp[cs], (((0,), (0,)), ((), ())),
                                  preferred_element_type=F32)
            sp = jnp.where(block_diag, dec_col * sp + upd, 0.0)
        sp_ref[p] = sp
        outs = []
        for c in range(n_chunks):
            c0 = L * c
            o_inter = jnp.dot(q_in[c0:c0 + L, pls].astype(BF16), states[c].astype(BF16),
                              preferred_element_type=F32)
            rows = []
            for blk in range(n_sub):
                r0 = GLA_SUB * blk
                n = GLA_SUB * (blk + 1)
                qrows = slice(c0 + r0, c0 + r0 + GLA_SUB)
                krows = slice(c0, c0 + n)
                if blk == 0:
                    qs = q[qrows, pls] * jnp.exp(b[qrows, pls])
                    ks = k[krows, pls] * jnp.exp(-b[krows, pls])
                else:
                    anchor = b[c0 + r0 - 1:c0 + r0, pls]
                    qs = q[qrows, pls] * jnp.exp(b[qrows, pls] - anchor)
                    ks = k[krows, pls] * jnp.exp(anchor - b[krows, pls])
                qs2 = jnp.concatenate([jnp.where(lo16, qs, 0.0), jnp.where(lo16, 0.0, qs)], axis=0)
                att = lax.dot_general(qs2.astype(BF16), ks.astype(BF16), (((1,), (1,)), ((), ())),
                                      preferred_element_type=F32)
                trow = lax.broadcasted_iota(I32, (2 * GLA_SUB, n), 0)
                tcol = lax.broadcasted_iota(I32, (2 * GLA_SUB, n), 1)
                t_idx = r0 + jnp.where(trow >= GLA_SUB, trow - GLA_SUB, trow)
                att = jnp.where(tcol <= t_idx, att, 0.0)
                o2 = jnp.dot(att.astype(BF16), vp[krows], preferred_element_type=F32)
                rows.append(jnp.concatenate([o2[:GLA_SUB, :HEAD_W], o2[GLA_SUB:, HEAD_W:]], axis=1))
            outs.append(o_inter + jnp.concatenate(rows, axis=0))
        o = jnp.concatenate(outs, axis=0)
        for hh in range(2):
            head = 2 * p + hh
            hs = slice(HEAD_W * head, HEAD_W * (head + 1))
            oh = o[:, HEAD_W * hh:HEAD_W * (hh + 1)]
            oh = oh * lax.rsqrt(jnp.mean(oh * oh, axis=-1, keepdims=True) + NORM_EPS) * ng_ref[:, hs]
            y_ref[:, hs] = (oh * _silu(r_ref[:, hs])).astype(y_ref.dtype)


def _gla_prompt(proj, mixed, t_len, w_gate_pad, b_gate, norm_g, w_cast, layer):
    L = _pick_tile(t_len, (GLA_STEP_ROWS, GLA_CHUNK))
    side = _side_cast(w_cast, layer, t_len // L)
    return pl.pallas_call(
        _gla_prompt_kernel,
        grid=(t_len // L,),
        in_specs=[
            pl.BlockSpec((L, 256), lambda i: (i, G_Q // 256)),
            pl.BlockSpec((L, 256), lambda i: (i, G_K // 256)),
            pl.BlockSpec((L, GROUP_WIDTH), lambda i: (i, G_V // GROUP_WIDTH)),
            pl.BlockSpec((L, GROUP_WIDTH), lambda i: (i, G_R // GROUP_WIDTH)),
            pl.BlockSpec((L, LANES), lambda i: (i, G_LR // LANES)),
            pl.BlockSpec((LANES, 256), lambda i: (0, 0)),
            pl.BlockSpec((1, 256), lambda i: (0, 0)),
            pl.BlockSpec((1, GROUP_WIDTH), lambda i: (0, 0)),
            pl.BlockSpec(memory_space=pl.ANY),
            side.in_spec,
        ],
        out_specs=[
            pl.BlockSpec((L, GROUP_WIDTH), lambda i: (i, MIX_COL_GLA)),
            pl.BlockSpec((2, LANES, 2 * HEAD_W), lambda i: (0, 0, 0)),
            side.out_spec,
        ],
        out_shape=[
            jax.ShapeDtypeStruct(mixed.shape, mixed.dtype),
            jax.ShapeDtypeStruct((2, LANES, 2 * HEAD_W), F32),
            side.out_shape,
        ],
        input_output_aliases={8: 0},
        compiler_params=_cparams(("arbitrary",)),
        name="gla_prompt",
    )(proj, proj, proj, proj, proj, w_gate_pad, b_gate.reshape(1, 256), norm_g.reshape(1, GROUP_WIDTH), mixed,
      side.operand)


def _conv_norm_act(y, g_ref, be_ref):
    mu = jnp.mean(y, axis=-1, keepdims=True)
    yc = y - mu
    var = jnp.mean(yc * yc, axis=-1, keepdims=True)
    return _silu(yc * lax.rsqrt(var + NORM_EPS) * g_ref[...] + be_ref[...])


def _conv_prompt_kernel(ua_ref, ug_ref, ha_ref, hg_ref, w_ref, b_ref, g_ref, be_ref, mixed_in_ref, wc_in_ref,
                        y_ref, tail_ref, wc_out_ref, buf_ref, sh_ref):
    del mixed_in_ref
    wc_out_ref[...] = wc_in_ref[...].astype(wc_out_ref.dtype)
    tt = ua_ref.shape[0]
    span = tt + CONV_HALO
    halo = ha_ref[...] * _sigmoid(hg_ref[...])
    buf_ref[0:CONV_HALO, :] = jnp.where(pl.program_id(0) > 0, halo, 0.0)
    buf_ref[CONV_HALO:span, :] = ua_ref[...] * _sigmoid(ug_ref[...])
    buf_ref[span:span + SUBLANES, :] = jnp.zeros((SUBLANES, GROUP_WIDTH), F32)
    for k in range(1, SUBLANES):
        sh_ref[k] = buf_ref[k:k + span, :]
    base = CONV_HALO - (CONV_WIDTH - 1)
    for r in range(tt // CONV_ROWS):
        acc = jnp.zeros((CONV_ROWS, GROUP_WIDTH), F32)
        for j in range(CONV_WIDTH):
            s0 = r * CONV_ROWS + base + j
            k = s0 % SUBLANES
            a0 = s0 - k
            win = buf_ref[a0:a0 + CONV_ROWS, :] if k == 0 else sh_ref[k, a0:a0 + CONV_ROWS, :]
            acc = acc + w_ref[j:j + 1, :] * win
        y = _conv_norm_act(acc + b_ref[...], g_ref, be_ref)
        y_ref[r * CONV_ROWS:(r + 1) * CONV_ROWS, :] = y.astype(y_ref.dtype)
    tail_ref[...] = buf_ref[tt:span, :]


def _conv_prompt(proj, mixed, t_len, w, b, g, beta, w_cast, layer):
    tt = _pick_tile(t_len, (CONV_TILE, 128, 64))
    ratio = tt // CONV_HALO
    side = _side_cast(w_cast, layer, t_len // tt)
    vec = lambda: pl.BlockSpec((1, GROUP_WIDTH), lambda i: (0, 0))
    return pl.pallas_call(
        _conv_prompt_kernel,
        grid=(t_len // tt,),
        in_specs=[
            pl.BlockSpec((tt, GROUP_WIDTH), lambda i: (i, C_A // GROUP_WIDTH)),
            pl.BlockSpec((tt, GROUP_WIDTH), lambda i: (i, C_G // GROUP_WIDTH)),
            pl.BlockSpec((CONV_HALO, GROUP_WIDTH), lambda i: (jnp.maximum(i * ratio - 1, 0), C_A // GROUP_WIDTH)),
            pl.BlockSpec((CONV_HALO, GROUP_WIDTH), lambda i: (jnp.maximum(i * ratio - 1, 0), C_G // GROUP_WIDTH)),
            pl.BlockSpec((CONV_WIDTH, GROUP_WIDTH), lambda i: (0, 0)),
            vec(), vec(), vec(),
            pl.BlockSpec(memory_space=pl.ANY),
            side.in_spec,
        ],
        out_specs=[
            pl.BlockSpec((tt, GROUP_WIDTH), lambda i: (i, MIX_COL_CONV)),
            pl.BlockSpec((CONV_HALO, GROUP_WIDTH), lambda i: (0, 0)),
            side.out_spec,
        ],
        out_shape=[
            jax.ShapeDtypeStruct(mixed.shape, mixed.dtype),
            jax.ShapeDtypeStruct((CONV_HALO, GROUP_WIDTH), F32),
            side.out_shape,
        ],
        scratch_shapes=[pltpu.VMEM((tt + CONV_HALO + SUBLANES, GROUP_WIDTH), F32),
                        pltpu.VMEM((SUBLANES, tt + CONV_HALO, GROUP_WIDTH), F32)],
        input_output_aliases={8: 0},
        compiler_params=_cparams(("arbitrary",)),
        name="conv_prompt",
    )(proj, proj, proj, proj, w, b.reshape(1, -1), g.reshape(1, -1), beta.reshape(1, -1), mixed, side.operand)


def _swa_prompt_kernel(sink_ref, q_ref, kc_ref, vc_ref, kp_ref, vp_ref, mixed_in_ref, *rest):
    del mixed_in_ref
    if len(rest) == 3:
        w_ref, y_ref, wpad_ref = rest
        _pad_w_in_rows(w_ref, wpad_ref)
    else:
        (y_ref,) = rest
    bq = WINDOW
    n_blk = q_ref.shape[0] // bq
    first = pl.program_id(0) == 0
    k_full = jnp.concatenate([kp_ref[...], kc_ref[...]], axis=0)
    v_full = jnp.concatenate([vp_ref[...], vc_ref[...]], axis=0)
    k_sw_full = pltpu.roll(k_full, SWA_HEAD_DIM, 1).astype(BF16)
    v_sw_full = pltpu.roll(v_full, SWA_HEAD_DIM, 1).astype(BF16)
    k_full = k_full.astype(BF16)
    v_full = v_full.astype(BF16)
    tq = lax.broadcasted_iota(I32, (bq, 2 * bq), 0)
    kj = lax.broadcasted_iota(I32, (bq, 2 * bq), 1)
    band = (kj > tq) & (kj <= tq + WINDOW)
    lane = lax.broadcasted_iota(I32, (bq, LANES), 1)
    lo = lane < SWA_HEAD_DIM
    rep = SWA_HEADS // SWA_KV_HEADS
    for blk in range(n_blk):
        qs = slice(bq * blk, bq * (blk + 1))
        ks = slice(bq * blk, bq * (blk + 2))
        valid = band & (kj >= jnp.where(first, bq, 0)) if blk == 0 else band
        k_all, v_all, k_sw, v_sw = k_full[ks], v_full[ks], k_sw_full[ks], v_sw_full[ks]
        for c in range(SWA_HEADS // 2):
            qc = q_ref[qs, LANES * c:LANES * (c + 1)] * (SWA_HEAD_DIM ** -0.5)
            outs = []
            for hh in range(2):
                h = 2 * c + hh
                g = h // rep
                qm = jnp.where(lo if hh == 0 else jnp.logical_not(lo), qc, 0.0).astype(BF16)
                k_use = k_all if g == hh else k_sw
                v_use = v_all if g == hh else v_sw
                s = lax.dot_general(qm, k_use, (((1,), (1,)), ((), ())), preferred_element_type=F32)
                s = jnp.where(valid, s, -jnp.inf)
                sink = sink_ref[h]
                mx = jnp.maximum(jnp.max(s, axis=-1, keepdims=True), sink)
                p = jnp.exp(s - mx)
                den = jnp.sum(p, axis=-1, keepdims=True) + jnp.exp(sink - mx)
                p = (p / den).astype(BF16)
                outs.append(jnp.dot(p, v_use, preferred_element_type=F32))
            y_ref[qs, LANES * c:LANES * (c + 1)] = jnp.where(lo, outs[0], outs[1]).astype(y_ref.dtype)


def _swa_prompt(proj, mixed, t_len, sinks, w_in_all=None, next_layer=None):
    bq = WINDOW
    rows = _pick_tile(t_len, (SWA_STEP_ROWS, bq))
    ratio = rows // bq
    cur = lambda off: pl.BlockSpec((rows, LANES), lambda i: (i, off // LANES))
    prev = lambda off: pl.BlockSpec((bq, LANES), lambda i: (jnp.maximum(i * ratio - 1, 0), off // LANES))
    in_specs = [
        pl.BlockSpec(memory_space=pltpu.SMEM),
        pl.BlockSpec((rows, GROUP_WIDTH), lambda i: (i, S_Q // GROUP_WIDTH)),
        cur(S_K), cur(S_V), prev(S_K), prev(S_V),
        pl.BlockSpec(memory_space=pl.ANY),
    ]
    out_specs = [pl.BlockSpec((rows, GROUP_WIDTH), lambda i: (i, MIX_COL_SWA))]
    out_shape = [jax.ShapeDtypeStruct(mixed.shape, mixed.dtype)]
    operands = [sinks, proj, proj, proj, proj, proj, mixed]
    if next_layer is not None:
        side = _w_in_side(w_in_all, next_layer, t_len // rows)
        in_specs.append(side.in_spec)
        out_specs.append(side.out_spec)
        out_shape.append(side.out_shape)
        operands.append(side.operand)
    outs = pl.pallas_call(
        _swa_prompt_kernel,
        grid=(t_len // rows,),
        in_specs=in_specs, out_specs=out_specs, out_shape=out_shape,
        input_output_aliases={6: 0},
        compiler_params=_cparams(("arbitrary",)),
        name="swa_prompt",
    )(*operands)
    return outs[0], (outs[1] if next_layer is not None else None)


_T_MK, _T_MQ = 0, 512
_T_GA, _T_GK, _T_GQ = 1024, 1280, 1536
_T_ROWS = 1792


def _sample_kernel(ps_ref, mm_ref, bias_ref, n0_ref, c0_ref, s0_ref, cv0_ref, k0_ref, v0_ref,
                   qm_ref, sink_ref, wg_ref, bg_ref, mng_ref, gng_ref, cw_ref, cb_ref, cg_ref, cbe_ref,
                   mix_ref, c1_ref, n1_ref, m1_ref, s1_ref, cv1_ref, k1_ref, v1_ref,
                   tt_ref, bc_ref, num_ref, go_ref, yc_ref, yd_ref, ybuf, ybf, zbuf, zsem, ysem):
    i = pl.program_id(0)
    nb = ps_ref.shape[0]
    bb = c0_ref.shape[0]
    t_len = mix_ref.shape[0] - nb
    zr = zbuf.shape[0]

    def zero_copy(t):
        return pltpu.make_async_copy(zbuf, mix_ref.at[pl.ds(t * zr, zr)], zsem.at[0])

    def gla_gate(lr):
        gp = jnp.dot(lr.astype(BF16), wg_ref[...], preferred_element_type=F32) + bg_ref[...]
        return jnp.exp(_log_sigmoid(gp) * (1.0 / GLA_TAU))

    @pl.when(i == 0)
    def _():
        zbuf[...] = jnp.zeros_like(zbuf)
        for t in range(t_len // zr):
            zero_copy(t).start()
        for h in range(MLSTM_HEADS):
            kk = ps_ref[:, A_K + HEAD_W * h:A_K + HEAD_W * (h + 1)] * (HEAD_W ** -0.5)
            tt_ref[_T_MK + HEAD_W * h:_T_MK + HEAD_W * (h + 1), :] = kk.T.astype(BF16)
            qq = ps_ref[:, A_Q + HEAD_W * h:A_Q + HEAD_W * (h + 1)]
            tt_ref[_T_MQ + HEAD_W * h:_T_MQ + HEAD_W * (h + 1), :] = qq.T.astype(BF16)
        a_all = gla_gate(ps_ref[:, G_LR:G_LR + LANES])
        for p in range(2):
            pls = slice(LANES * p, LANES * (p + 1))
            tt_ref[_T_GA + LANES * p:_T_GA + LANES * (p + 1), :] = a_all[:, pls].T.astype(BF16)
            kk = ps_ref[:, G_K + LANES * p:G_K + LANES * (p + 1)]
            tt_ref[_T_GK + LANES * p:_T_GK + LANES * (p + 1), :] = kk.T.astype(BF16)
            qq = ps_ref[:, G_Q + LANES * p:G_Q + LANES * (p + 1)] * (GLA_DK ** -0.5)
            tt_ref[_T_GQ + LANES * p:_T_GQ + LANES * (p + 1), :] = qq.T.astype(BF16)

    r0 = pl.multiple_of(i * bb, bb)
    rows = pl.ds(r0, bb)

    pre = ps_ref[rows, A_G:A_G + LANES] + bias_ref[...]
    lfm = _log_sigmoid(pre) + mm_ref[...]
    f_al = pltpu.roll(lfm, LANES - MLSTM_HEADS, 1)
    m_t = jnp.maximum(f_al, pre)
    g_st = jnp.exp(f_al - m_t)
    w_k = jnp.exp(pre - m_t)
    m1_ref[...] = m_t
    n_new = []
    for h in range(MLSTM_HEADS):
        kk = ps_ref[rows, A_K + HEAD_W * h:A_K + HEAD_W * (h + 1)] * (HEAD_W ** -0.5)
        nn = g_st[:, h:h + 1] * n0_ref[:, HEAD_W * h:HEAD_W * (h + 1)] + w_k[:, h:h + 1] * kk
        n1_ref[:, HEAD_W * h:HEAD_W * (h + 1)] = nn
        n_new.append(nn)

    glu = ps_ref[rows, C_A:C_A + GROUP_WIDTH] * _sigmoid(ps_ref[rows, C_G:C_G + GROUP_WIDTH])
    yc_ref[...] = glu * cw_ref[CONV_WIDTH - 1:CONV_WIDTH, :]
    a_v = ps_ref[rows, A_V:A_V + GROUP_WIDTH]
    g_v = ps_ref[rows, G_V:G_V + GROUP_WIDTH]
    s_k = ps_ref[rows, S_K:S_K + LANES]
    s_v = ps_ref[rows, S_V:S_V + LANES]

    lane_b = lax.broadcasted_iota(I32, (nb, LANES), 0)
    key_row = lax.broadcasted_iota(I32, (SWA_HEADS, WINDOW), 1)
    lo_row = lax.broadcasted_iota(I32, (1, LANES), 1) < SWA_HEAD_DIM
    sink_col = sink_ref[:, 0:1]

    for j in range(bb):
        onehot = (lane_b == r0 + j).astype(BF16)
        bc_ref[...] = jnp.dot(tt_ref[...], onehot, preferred_element_type=F32)
        jrow = slice(j, j + 1)
        for h in range(MLSTM_HEADS):
            hs = slice(HEAD_W * h, HEAD_W * (h + 1))
            kbc = bc_ref[_T_MK + HEAD_W * h:_T_MK + HEAD_W * (h + 1), :]
            qbc = bc_ref[_T_MQ + HEAD_W * h:_T_MQ + HEAD_W * (h + 1), :]
            g1 = g_st[jrow, h:h + 1]
            w1 = w_k[jrow, h:h + 1]
            v_row = a_v[jrow, hs]
            c_new = g1 * c0_ref[j, h] + kbc * (w1 * v_row)
            c1_ref[j, h] = c_new
            num_ref[jrow, hs] = jnp.sum(qbc * c_new, axis=0, keepdims=True)
        for p in range(2):
            abc = bc_ref[_T_GA + LANES * p:_T_GA + LANES * (p + 1), :]
            kbc = bc_ref[_T_GK + LANES * p:_T_GK + LANES * (p + 1), :]
            qbc = bc_ref[_T_GQ + LANES * p:_T_GQ + LANES * (p + 1), :]
            for hh in range(2):
                head = 2 * p + hh
                hs = slice(HEAD_W * head, HEAD_W * (head + 1))
                ds_ = slice(GLA_DK * hh, GLA_DK * (hh + 1))
                v_row = g_v[jrow, hs]
                s_new = abc[ds_, :] * s0_ref[j, head] + kbc[ds_, :] * v_row
                s1_ref[j, head] = s_new
                go_ref[jrow, hs] = jnp.sum(qbc[ds_, :] * s_new, axis=0, keepdims=True)
        cache = cv0_ref[j]
        yc_ref[jrow, :] = yc_ref[jrow, :] + jnp.sum(cache * cw_ref[0:CONV_WIDTH - 1, :], axis=0, keepdims=True)
        cv1_ref[j, 0:CONV_WIDTH - 2, :] = cv0_ref[j, 1:CONV_WIDTH - 1, :]
        cv1_ref[j, CONV_WIDTH - 2:CONV_WIDTH - 1, :] = glu[jrow, :]
        k_new = s_k[jrow, :]
        v_new = s_v[jrow, :]
        k1_ref[j, 0:WINDOW - 1, :] = k0_ref[j, 1:WINDOW, :]
        k1_ref[j, WINDOW - 1:WINDOW, :] = k_new
        v1_ref[j, 0:WINDOW - 1, :] = v0_ref[j, 1:WINDOW, :]
        v1_ref[j, WINDOW - 1:WINDOW, :] = v_new
        qmat = qm_ref[j] * (SWA_HEAD_DIM ** -0.5)
        s_old = lax.dot_general(qmat.astype(BF16), k0_ref[j].astype(BF16), (((1,), (1,)), ((), ())),
                                preferred_element_type=F32)
        s_old = jnp.where(key_row >= 1, s_old, -jnp.inf)
        s_cur = jnp.sum(qmat * k_new, axis=-1, keepdims=True)
        mx = jnp.maximum(jnp.maximum(jnp.max(s_old, axis=-1, keepdims=True), s_cur), sink_col)
        p_old = jnp.exp(s_old - mx)
        p_cur = jnp.exp(s_cur - mx)
        den = jnp.sum(p_old, axis=-1, keepdims=True) + p_cur + jnp.exp(sink_col - mx)
        o = jnp.dot((p_old / den).astype(BF16), v0_ref[j].astype(BF16), preferred_element_type=F32)
        o = o + (p_cur / den) * v_new
        o_sw = pltpu.roll(o, SWA_HEAD_DIM, 1)
        for c in range(SWA_HEADS // 2):
            g = c // (SWA_HEADS // SWA_KV_HEADS // 2)
            left = (o if g == 0 else o_sw)[2 * c:2 * c + 1, :]
            right = (o_sw if g == 0 else o)[2 * c + 1:2 * c + 2, :]
            yd_ref[jrow, LANES * c:LANES * (c + 1)] = jnp.where(lo_row, left, right)

    for h in range(MLSTM_HEADS):
        hs = slice(HEAD_W * h, HEAD_W * (h + 1))
        qq = ps_ref[rows, A_Q + HEAD_W * h:A_Q + HEAD_W * (h + 1)]
        den = jnp.sum(qq * n_new[h], axis=-1, keepdims=True)
        hh = num_ref[:, hs] / jnp.maximum(jnp.abs(den), jnp.exp(-m_t[:, h:h + 1]))
        hh = _sigmoid(ps_ref[rows, A_O + HEAD_W * h:A_O + HEAD_W * (h + 1)]) * hh
        hh = hh * lax.rsqrt(jnp.mean(hh * hh, axis=-1, keepdims=True) + NORM_EPS) * mng_ref[:, hs]
        ybuf[rows, hs] = hh
    ybuf[rows, GROUP_WIDTH:2 * GROUP_WIDTH] = _conv_norm_act(yc_ref[...] + cb_ref[...], cg_ref, cbe_ref)
    for head in range(GLA_HEADS):
        hs = slice(HEAD_W * head, HEAD_W * (head + 1))
        oh = go_ref[:, hs]
        oh = oh * lax.rsqrt(jnp.mean(oh * oh, axis=-1, keepdims=True) + NORM_EPS) * gng_ref[:, hs]
        gr = ps_ref[rows, G_R + HEAD_W * head:G_R + HEAD_W * (head + 1)]
        ybuf[rows, 2 * GROUP_WIDTH + HEAD_W * head:2 * GROUP_WIDTH + HEAD_W * (head + 1)] = oh * _silu(gr)
    ybuf[rows, 3 * GROUP_WIDTH:4 * GROUP_WIDTH] = yd_ref[...]

    @pl.when(i == pl.num_programs(0) - 1)
    def _():
        ybf[...] = ybuf[...].astype(BF16)
        cp = pltpu.make_async_copy(ybf, mix_ref.at[pl.ds(t_len, nb)], ysem.at[0])
        cp.start()
        for t in range(t_len // zr):
            zero_copy(t).wait()
        cp.wait()


def _sample_mixers(proj_s, t_len, states, layer, prm):
    c_all, n_all, m_all, s_all, cv_all, k_all, v_all = states
    depth = c_all.shape[0]
    nb = proj_s.shape[0]
    bb = SAMPLE_BB
    assert nb == LANES and nb % bb == 0
    lb = layer * (nb // bb)
    n0, m0 = n_all[layer], m_all[layer]
    mm = jnp.concatenate([m0, m0, jnp.zeros((nb, LANES - 2 * MLSTM_HEADS), F32)], axis=1)
    n0f = n0.reshape(nb, GROUP_WIDTH)
    c0 = c_all.reshape((depth * nb,) + c_all.shape[2:])
    s0 = s_all.reshape((depth * nb,) + s_all.shape[2:])
    cv0 = cv_all.reshape((depth * nb,) + cv_all.shape[2:])
    k0f = k_all.reshape(depth * nb, WINDOW, LANES)
    v0f = v_all.reshape(depth * nb, WINDOW, LANES)
    sq = proj_s[:, S_Q:S_Q + GROUP_WIDTH].reshape(nb, SWA_KV_HEADS, SWA_HEADS // SWA_KV_HEADS, SWA_HEAD_DIM)
    zq = jnp.zeros_like(sq[:, 0])
    qm = jnp.concatenate([jnp.concatenate([sq[:, 0], zq], axis=-1), jnp.concatenate([zq, sq[:, 1]], axis=-1)], axis=1)
    sink_b = jnp.broadcast_to(prm['swa_sinks'].astype(F32)[:, None], (SWA_HEADS, LANES))

    full = lambda shape: pl.BlockSpec(shape, lambda i: (0,) * len(shape))
    rowb = lambda w: pl.BlockSpec((bb, w), lambda i: (i, 0))
    in_specs = [
        full((nb, N_PROJ)), rowb(LANES), full((1, LANES)), rowb(GROUP_WIDTH),
        pl.BlockSpec((bb, MLSTM_HEADS, HEAD_W, HEAD_W), lambda i: (lb + i, 0, 0, 0)),
        pl.BlockSpec((bb, GLA_HEADS, GLA_DK, HEAD_W), lambda i: (lb + i, 0, 0, 0)),
        pl.BlockSpec((bb, CONV_WIDTH - 1, GROUP_WIDTH), lambda i: (lb + i, 0, 0)),
        pl.BlockSpec((bb, WINDOW, LANES), lambda i: (lb + i, 0, 0)),
        pl.BlockSpec((bb, WINDOW, LANES), lambda i: (lb + i, 0, 0)),
        pl.BlockSpec((bb, SWA_HEADS, LANES), lambda i: (i, 0, 0)),
        full((SWA_HEADS, LANES)), full((LANES, 256)), full((1, 256)),
        full((1, GROUP_WIDTH)), full((1, GROUP_WIDTH)),
        full((CONV_WIDTH, GROUP_WIDTH)), full((1, GROUP_WIDTH)), full((1, GROUP_WIDTH)), full((1, GROUP_WIDTH)),
    ]
    out_specs = [
        pl.BlockSpec(memory_space=pl.ANY),
        pl.BlockSpec((bb, MLSTM_HEADS, HEAD_W, HEAD_W), lambda i: (i, 0, 0, 0)),
        rowb(GROUP_WIDTH), rowb(LANES),
        pl.BlockSpec((bb, GLA_HEADS, GLA_DK, HEAD_W), lambda i: (i, 0, 0, 0)),
        pl.BlockSpec((bb, CONV_WIDTH - 1, GROUP_WIDTH), lambda i: (i, 0, 0)),
        pl.BlockSpec((bb, WINDOW, LANES), lambda i: (i, 0, 0)),
        pl.BlockSpec((bb, WINDOW, LANES), lambda i: (i, 0, 0)),
    ]
    out_shape = [
        jax.ShapeDtypeStruct((t_len + nb, D_MODEL), BF16),
        jax.ShapeDtypeStruct(c_all.shape[1:], F32),
        jax.ShapeDtypeStruct((nb, GROUP_WIDTH), F32),
        jax.ShapeDtypeStruct((nb, LANES), F32),
        jax.ShapeDtypeStruct(s_all.shape[1:], F32),
        jax.ShapeDtypeStruct(cv_all.shape[1:], F32),
        jax.ShapeDtypeStruct((nb, WINDOW, LANES), F32),
        jax.ShapeDtypeStruct((nb, WINDOW, LANES), F32),
    ]
    scratch = [
        pltpu.VMEM((_T_ROWS, nb), BF16), pltpu.VMEM((_T_ROWS, LANES), F32),
        pltpu.VMEM((bb, GROUP_WIDTH), F32), pltpu.VMEM((bb, GROUP_WIDTH), F32), pltpu.VMEM((bb, GROUP_WIDTH), F32),
        pltpu.VMEM((bb, GROUP_WIDTH), F32),
        pltpu.VMEM((nb, D_MODEL), F32), pltpu.VMEM((nb, D_MODEL), BF16),
        pltpu.VMEM((_pick_tile(t_len, (512, 256, 128, 64, 16)), D_MODEL), BF16),
        pltpu.SemaphoreType.DMA((1,)), pltpu.SemaphoreType.DMA((1,)),
    ]
    mixed, c1, n1, m1, s1, cv1, k1, v1 = pl.pallas_call(
        _sample_kernel,
        grid=(nb // bb,),
        in_specs=in_specs, out_specs=out_specs, out_shape=out_shape, scratch_shapes=scratch,
        compiler_params=_cparams(("arbitrary",)),
        name="sample_mixers",
    )(proj_s, mm, prm['gate_bias'], n0f, c0, s0, cv0, k0f, v0f, qm, sink_b,
      prm['gla_w_gate_pad'], prm['gla_b_gate'].reshape(1, 256),
      prm['mlstm_norm_g'].reshape(1, -1), prm['gla_norm_g'].reshape(1, -1),
      prm['conv_w'], prm['conv_b'].reshape(1, -1), prm['conv_norm_g'].reshape(1, -1),
      prm['conv_norm_b'].reshape(1, -1))
    new_state = (c1, n1.reshape(n0.shape), m1[:, :MLSTM_HEADS], s1, cv1,
                 k1.reshape(k_all.shape[1:]), v1.reshape(v_all.shape[1:]))
    return mixed, new_state


def _outproj_router_kernel(x_ref, mix_ref, w_ref, g_ref, rw_ref, rb_ref,
                           x1_ref, hn_ref, ri_ref, rf_ref, cnt_ref):
    @pl.when(pl.program_id(0) == 0)
    def _():
        cnt_ref[...] = jnp.zeros_like(cnt_ref)

    counts = _route_rows(x_ref[...], mix_ref[...], w_ref, g_ref, rw_ref, rb_ref, cnt_ref[0:1, :],
                         x1_ref, hn_ref, ri_ref, rf_ref)
    cnt_ref[...] = jnp.broadcast_to(counts, cnt_ref.shape)


def _route_rows(x, mix, w_ref, g_ref, rw_ref, rb_ref, counts, x1_ref, hn_ref, ri_ref, rf_ref):
    tm = x.shape[0]
    x1 = x + jnp.dot(mix, w_ref[...], preferred_element_type=F32)
    x1_ref[...] = x1
    ms = jnp.mean(x1 * x1, axis=-1, keepdims=True)
    hn = x1 * lax.rsqrt(ms + NORM_EPS) * g_ref[...]
    hn_ref[...] = hn
    logits = jnp.dot(hn.astype(BF16), rw_ref[...], preferred_element_type=F32) + rb_ref[...]
    lane = lax.broadcasted_iota(I32, (tm, LANES), 1)
    big = jnp.int32(LANES)
    gl = jnp.where(lane < N_GROUPS, logits, -jnp.inf)
    gmax = jnp.max(gl, axis=-1, keepdims=True)
    g_sel = jnp.min(jnp.where(gl == gmax, lane, big), axis=-1, keepdims=True)
    g_w = 1.0 / jnp.sum(jnp.exp(gl - gmax), axis=-1, keepdims=True)
    e_lane = lane - N_GROUPS
    in_grp = (e_lane >= 0) & (e_lane < N_EXPERTS) & ((e_lane // EXPERTS_PER_GROUP) == g_sel)
    el = jnp.where(in_grp, logits, -jnp.inf)
    m1 = jnp.max(el, axis=-1, keepdims=True)
    i1 = jnp.min(jnp.where(el == m1, lane, big), axis=-1, keepdims=True)
    el2 = jnp.where(lane == i1, -jnp.inf, el)
    m2 = jnp.max(el2, axis=-1, keepdims=True)
    i2 = jnp.min(jnp.where(el2 == m2, lane, big), axis=-1, keepdims=True)
    r = jnp.exp(m2 - m1)
    p1 = 1.0 / (1.0 + r)
    gate1 = g_w * p1
    gate2 = g_w * (r * p1)
    sel1 = lane == i1
    sel2 = lane == i2
    onehot = jnp.where(sel1 | sel2, 1.0, 0.0)
    row = lax.broadcasted_iota(I32, (tm, tm), 0)
    col = lax.broadcasted_iota(I32, (tm, tm), 1)
    strict = jnp.where(col < row, 1.0, 0.0).astype(BF16)
    cum = jnp.dot(strict, onehot.astype(BF16), preferred_element_type=F32) + counts
    rank1 = jnp.sum(jnp.where(sel1, cum, 0.0), axis=-1, keepdims=True).astype(I32)
    rank2 = jnp.sum(jnp.where(sel2, cum, 0.0), axis=-1, keepdims=True).astype(I32)
    ri = jnp.where(lane == 0, i1 - N_GROUPS, jnp.where(lane == 1, i2 - N_GROUPS,
                   jnp.where(lane == 2, rank1, jnp.where(lane == 3, rank2, 0))))
    ri_ref[...] = ri
    rf_ref[...] = jnp.where(lane == 0, gate1, jnp.where(lane == 1, gate2, 0.0))
    return counts + jnp.sum(onehot, axis=0, keepdims=True)


def _outproj_router(x, mixed, w_out_bf16, norm_g, rw_pad, rb_pad):
    n = x.shape[0]
    tm = _pick_tile(n, (320, 256, 128, 64, 16))
    full = lambda shape: pl.BlockSpec(shape, lambda i: (0,) * len(shape))
    rowb = lambda w: pl.BlockSpec((tm, w), lambda i: (i, 0))
    return pl.pallas_call(
        _outproj_router_kernel,
        grid=(n // tm,),
        in_specs=[rowb(D_MODEL), rowb(D_MODEL), full((D_MODEL, D_MODEL)), full((1, D_MODEL)),
                  full((D_MODEL, LANES)), full((1, LANES))],
        out_specs=[rowb(D_MODEL), rowb(D_MODEL), rowb(LANES), rowb(LANES), full((SUBLANES, LANES))],
        out_shape=[
            jax.ShapeDtypeStruct((n, D_MODEL), F32),
            jax.ShapeDtypeStruct((n, D_MODEL), F32),
            jax.ShapeDtypeStruct((n, LANES), I32),
            jax.ShapeDtypeStruct((n, LANES), F32),
            jax.ShapeDtypeStruct((SUBLANES, LANES), F32),
        ],
        compiler_params=_cparams(("arbitrary",)),
        name="outproj_router",
    )(x, mixed, w_out_bf16, norm_g.reshape(1, D_MODEL), rw_pad, rb_pad)


def _dispatch_kernel(pos_ref, zt_ref, hn_ref, xs_ref, zbuf, sem, zsem):
    tm = hn_ref.shape[0]
    tile = zbuf.shape[0]
    base = pl.program_id(0) * tm

    @pl.when(pl.program_id(0) == 0)
    def _():
        zbuf[...] = jnp.zeros_like(zbuf)

        def zero_tile(k, carry):
            @pl.when(zt_ref[k] >= 0)
            def _():
                row = pl.multiple_of(zt_ref[k] * tile, tile)
                cp = pltpu.make_async_copy(zbuf, xs_ref.at[pl.ds(row, tile)], zsem.at[0])
                cp.start()
                cp.wait()
            return carry

        lax.fori_loop(0, zt_ref.shape[0], zero_tile, 0)

    def issue(r, carry):
        tok = base + r
        src = hn_ref.at[pl.ds(r, 1)]
        pltpu.make_async_copy(src, xs_ref.at[pl.ds(pos_ref[2 * tok], 1)], sem.at[0]).start(priority=0)
        pltpu.make_async_copy(src, xs_ref.at[pl.ds(pos_ref[2 * tok + 1], 1)], sem.at[1]).start(priority=1)
        return carry

    lax.fori_loop(0, tm, issue, 0, unroll=4)
    pltpu.make_async_copy(hn_ref, xs_ref.at[pl.ds(0, tm)], sem.at[0]).wait()
    pltpu.make_async_copy(hn_ref, xs_ref.at[pl.ds(0, tm)], sem.at[1]).wait()


def _dispatch(pos, zero_tiles, hn, n_rows, tile):
    n = hn.shape[0]
    tm = _pick_tile(n, ROW_DMA_TILES)
    return pl.pallas_call(
        _dispatch_kernel,
        grid_spec=pltpu.PrefetchScalarGridSpec(
            num_scalar_prefetch=2,
            grid=(n // tm,),
            in_specs=[pl.BlockSpec((tm, D_MODEL), lambda i, p, z: (i, 0))],
            out_specs=pl.BlockSpec(memory_space=pl.ANY),
            scratch_shapes=[pltpu.VMEM((tile, D_MODEL), F32), pltpu.SemaphoreType.DMA((2,)),
                            pltpu.SemaphoreType.DMA((1,))],
        ),
        out_shape=jax.ShapeDtypeStruct((n_rows, D_MODEL), F32),
        compiler_params=_cparams(("arbitrary",)),
        name="moe_dispatch",
    )(pos, zero_tiles, hn)


def _expert_kernel(te_ref, nt_ref, x_ref, wg_ref, wu_ref, wd_ref, y_ref):
    del te_ref
    used = pl.program_id(0) < nt_ref[0]

    @pl.when(used)
    def _():
        x = x_ref[...].astype(BF16)
        a = jnp.dot(x, wg_ref[0], preferred_element_type=F32)
        u = jnp.dot(x, wu_ref[0], preferred_element_type=F32)
        hmid = (_silu(a) * u).astype(BF16)
        y_ref[...] = jnp.dot(hmid, wd_ref[0], preferred_element_type=F32)

    @pl.when(jnp.logical_not(used))
    def _():
        y_ref[...] = jnp.zeros_like(y_ref)


def _expert_mlp(tile_expert, n_tiles_used, xs, wg, wu, wd, tile):
    n_tiles = xs.shape[0] // tile

    def row_map(t, te, nt):
        return (jnp.minimum(t, nt[0] - 1), 0)

    def w_map(t, te, nt):
        return (te[jnp.minimum(t, nt[0] - 1)], 0, 0)

    return pl.pallas_call(
        _expert_kernel,
        grid_spec=pltpu.PrefetchScalarGridSpec(
            num_scalar_prefetch=2,
            grid=(n_tiles,),
            in_specs=[
                pl.BlockSpec((tile, D_MODEL), row_map),
                pl.BlockSpec((1, D_MODEL, D_EXPERT), w_map),
                pl.BlockSpec((1, D_MODEL, D_EXPERT), w_map),
                pl.BlockSpec((1, D_EXPERT, D_MODEL), w_map),
            ],
            out_specs=pl.BlockSpec((tile, D_MODEL), lambda t, te, nt: (t, 0)),
        ),
        out_shape=jax.ShapeDtypeStruct(xs.shape, F32),
        compiler_params=_cparams(("arbitrary",)),
        name="expert_mlp",
    )(tile_expert, n_tiles_used, xs, wg, wu, wd)


def _combine_kernel(pos_ref, x1_ref, rf_ref, fg_ref, ys_ref, o_ref, buf_a, buf_b, sem, *, final_norm, row0):
    tm = x1_ref.shape[0]
    base = row0 + pl.program_id(0) * tm

    def issue(r, carry):
        tok = base + r
        pltpu.make_async_copy(ys_ref.at[pl.ds(pos_ref[2 * tok], 1)], buf_a.at[pl.ds(r, 1)],
                              sem.at[0]).start(priority=0)
        pltpu.make_async_copy(ys_ref.at[pl.ds(pos_ref[2 * tok + 1], 1)], buf_b.at[pl.ds(r, 1)],
                              sem.at[1]).start(priority=1)
        return carry

    lax.fori_loop(0, tm, issue, 0, unroll=4)
    pltpu.make_async_copy(ys_ref.at[pl.ds(0, tm)], buf_a, sem.at[0]).wait()
    pltpu.make_async_copy(ys_ref.at[pl.ds(0, tm)], buf_b, sem.at[1]).wait()
    rf = rf_ref[...]
    x2 = x1_ref[...] + rf[:, 0:1] * buf_a[...] + rf[:, 1:2] * buf_b[...]
    if final_norm:
        ms = jnp.mean(x2 * x2, axis=-1, keepdims=True)
        x2 = x2 * lax.rsqrt(ms + NORM_EPS) * fg_ref[...]
    o_ref[...] = x2


def _combine(pos, x1, rf, ys, final_g, final_norm, row0=0, n_rows=None):
    n = x1.shape[0]
    n_rows = n if n_rows is None else n_rows
    tm = _pick_tile(n_rows, ROW_DMA_TILES)
    assert row0 % tm == 0
    b0 = row0 // tm
    return pl.pallas_call(
        functools.partial(_combine_kernel, final_norm=final_norm, row0=row0),
        grid_spec=pltpu.PrefetchScalarGridSpec(
            num_scalar_prefetch=1,
            grid=(n_rows // tm,),
            in_specs=[
                pl.BlockSpec((tm, D_MODEL), lambda i, p: (b0 + i, 0)),
                pl.BlockSpec((tm, LANES), lambda i, p: (b0 + i, 0)),
                pl.BlockSpec((1, D_MODEL), lambda i, p: (0, 0)),
                pl.BlockSpec(memory_space=pl.ANY),
            ],
            out_specs=pl.BlockSpec((tm, D_MODEL), lambda i, p: (i, 0)),
            scratch_shapes=[pltpu.VMEM((tm, D_MODEL), F32), pltpu.VMEM((tm, D_MODEL), F32),
                            pltpu.SemaphoreType.DMA((2,))],
        ),
        out_shape=jax.ShapeDtypeStruct((n_rows, D_MODEL), F32),
        compiler_params=_cparams(("arbitrary",)),
        name="moe_combine",
    )(pos, x1, rf, final_g.reshape(1, D_MODEL), ys)


W_IN_COL_BLOCK = LANES


def _pad_w_in_rows(w_ref, o_ref):
    cols = o_ref.shape[1]
    at = 0
    for lo, hi, dst in sorted(_W_IN_SEGMENTS, key=lambda s: s[2]):
        if dst > at:
            o_ref[at:dst, :] = jnp.zeros((dst - at, cols), o_ref.dtype)
        o_ref[dst:dst + (hi - lo), :] = w_ref[0, lo:hi, :].astype(o_ref.dtype)
        at = dst + (hi - lo)
    if at < N_PROJ:
        o_ref[at:N_PROJ, :] = jnp.zeros((N_PROJ - at, cols), o_ref.dtype)


def _w_in_side(w_in_t, layer, n_steps):
    depth, n_in, d = w_in_t.shape
    n_blocks = min(n_steps, d // W_IN_COL_BLOCK)
    assert d % (n_blocks * LANES) == 0
    cols = d // n_blocks
    blk = lambda i: jnp.minimum(i, n_blocks - 1)
    return _SideCast(
        operand=w_in_t,
        in_spec=pl.BlockSpec((1, n_in, cols), lambda i: (layer, 0, blk(i))),
        out_spec=pl.BlockSpec((N_PROJ, cols), lambda i: (0, blk(i))),
        out_shape=jax.ShapeDtypeStruct((N_PROJ, d), BF16),
        shape=(N_PROJ, d),
    )


def _pad_w_in(w_in_t, layer):
    n_steps = w_in_t.shape[2] // W_IN_COL_BLOCK
    side = _w_in_side(w_in_t, layer, n_steps)
    return pl.pallas_call(
        _pad_w_in_rows,
        grid=(n_steps,),
        in_specs=[side.in_spec], out_specs=side.out_spec, out_shape=side.out_shape,
        compiler_params=_cparams(("arbitrary",)),
        name="pad_w_in",
    )(side.operand)


def _moe(x1, hn, ri, rf, counts, experts, final_g, final_norm, tile, t_len):
    n = x1.shape[0]
    n_tiles = -(-2 * n // tile) + N_EXPERTS
    cnt = counts[0, N_GROUPS:N_GROUPS + N_EXPERTS].astype(I32)
    tiles_per = (cnt + tile - 1) // tile
    tile_end = jnp.cumsum(tiles_per)
    row_off = (tile_end - tiles_per) * tile
    pos = (row_off[ri[:, 0:2]] + ri[:, 2:4]).reshape(2 * n)
    tile_ids = jnp.arange(n_tiles, dtype=I32)
    tile_expert = jnp.minimum(jnp.sum((tile_ids[:, None] >= tile_end[None, :]).astype(I32), axis=1), N_EXPERTS - 1)
    n_used = tile_end[N_EXPERTS - 1:N_EXPERTS].astype(I32)
    last_tile = jnp.where(tiles_per > 0, tile_end - 1, -1).astype(I32)
    zero_tiles = jnp.concatenate([last_tile, jnp.where(tile_ids >= n_used[0], tile_ids, -1)])
    xs = _dispatch(pos, zero_tiles, hn, n_tiles * tile, tile)
    ys = _expert_mlp(tile_expert, n_used, xs, *experts, tile)
    if final_norm:
        return (_combine(pos, x1, rf, ys, final_g, True, 0, t_len),
                _combine(pos, x1, rf, ys, final_g, True, t_len, n - t_len))
    return _combine(pos, x1, rf, ys, final_g, False)


def _layer(x, t_len, states, layer, prm, w_in_pad, final_g, final_norm, moe_tile):
    proj = _in_projection(x, prm['norm_mix_g'], w_in_pad)
    mixed, new_s = _sample_mixers(proj[t_len:], t_len, states, layer, prm)
    e_shape = prm['expert_w_gate_all'].shape[1:]
    mixed, cn, m_p, wg = _mlstm_prompt(proj, mixed, t_len, prm['gate_bias'], prm['mlstm_norm_g'],
                                       prm['expert_w_gate_all'], layer)
    mixed, conv_tail, wu = _conv_prompt(proj, mixed, t_len, prm['conv_w'], prm['conv_b'], prm['conv_norm_g'],
                                        prm['conv_norm_b'], prm['expert_w_up_all'], layer)
    mixed, sp, wd = _gla_prompt(proj, mixed, t_len, prm['gla_w_gate_pad'], prm['gla_b_gate'], prm['gla_norm_g'],
                                prm['expert_w_down_all'], layer)
    mixed, next_w_in_pad = _swa_prompt(proj, mixed, t_len, prm['swa_sinks'], prm['w_in_all'],
                                       None if final_norm else layer + 1)
    experts = (wg.reshape(e_shape), wu.reshape(e_shape), wd.reshape(prm['expert_w_down_all'].shape[1:]))
    x1, hn, ri, rf, counts = _outproj_router(x, mixed, prm['w_out'], prm['norm_ffn_g'], prm['router_w'], prm['router_b'])
    x2 = _moe(x1, hn, ri, rf, counts, experts, final_g, final_norm, moe_tile, t_len)
    p_c = cn[None, :, :, :HEAD_W]
    p_n = cn[None, :, :, HEAD_W]
    p_m = m_p[None, :MLSTM_HEADS, 0]
    p_s = jnp.stack([sp[0, :GLA_DK, :HEAD_W], sp[0, GLA_DK:, HEAD_W:],
                     sp[1, :GLA_DK, :HEAD_W], sp[1, GLA_DK:, HEAD_W:]])[None]
    p_conv = conv_tail[None, CONV_HALO - (CONV_WIDTH - 1):]
    p_k = proj[t_len - WINDOW:t_len, S_K:S_K + LANES].reshape(1, WINDOW, SWA_KV_HEADS, SWA_HEAD_DIM)
    p_v = proj[t_len - WINDOW:t_len, S_V:S_V + LANES].reshape(1, WINDOW, SWA_KV_HEADS, SWA_HEAD_DIM)
    return x2, (p_c, p_n, p_m, p_s, p_conv, p_k, p_v), new_s, next_w_in_pad


def _forward(x_prompt, x_sample, states, layer_params, final_norm_g, moe_tile=MOE_TILE):
    t_len = x_prompt.shape[1]
    x = jnp.concatenate([x_prompt[0], x_sample[:, 0]], axis=0)
    new_p, new_s = [], []
    depth = len(layer_params)
    w_in_pad = _pad_w_in(layer_params[0]['w_in_all'], 0)
    for l, prm in enumerate(layer_params):
        x, sp, ss, w_in_pad = _layer(x, t_len, states, l, prm, w_in_pad, final_norm_g, l == depth - 1, moe_tile)
        new_p.append(sp)
        new_s.append(ss)
    y_prompt, y_sample = x
    y_prompt = y_prompt[None]
    y_sample = y_sample[:, None]
    p_states = [jnp.stack(parts) for parts in zip(*new_p)]
    s_states = [jnp.stack(parts) for parts in zip(*new_s)]
    return (y_prompt, y_sample, *p_states, *s_states)


def _prep_layer_params(l, norm_mix_g, w_in, mlstm_b_i, mlstm_b_f, mlstm_norm_g, conv_w, conv_b, conv_norm_g,
                       conv_norm_b, gla_w_gate, gla_b_gate, gla_norm_g, swa_sinks, w_out, norm_ffn_g,
                       router_group_w, router_group_b, router_expert_w, router_expert_b, expert_w_gate,
                       expert_w_up, expert_w_down):
    gate_bias = jnp.concatenate([mlstm_b_i[l], mlstm_b_f[l], jnp.zeros((LANES - 2 * MLSTM_HEADS,), F32)])
    rw = jnp.concatenate([router_group_w[l], router_expert_w[l],
                          jnp.zeros((D_MODEL, LANES - N_GROUPS - N_EXPERTS), F32)], axis=1)
    rb = jnp.concatenate([router_group_b[l], router_expert_b[l],
                          jnp.zeros((LANES - N_GROUPS - N_EXPERTS,), F32)])
    return {
        'norm_mix_g': norm_mix_g[l],
        'w_in_all': jnp.swapaxes(w_in, 1, 2),
        'gate_bias': gate_bias.reshape(1, LANES),
        'mlstm_norm_g': mlstm_norm_g[l],
        'conv_w': conv_w[l], 'conv_b': conv_b[l], 'conv_norm_g': conv_norm_g[l], 'conv_norm_b': conv_norm_b[l],
        'gla_w_gate_pad': jnp.concatenate(
            [gla_w_gate[l], jnp.zeros((LANES - GLA_LOWRANK, GLA_HEADS * GLA_DK), F32)], axis=0).astype(BF16),
        'gla_b_gate': gla_b_gate[l], 'gla_norm_g': gla_norm_g[l],
        'swa_sinks': swa_sinks[l],
        'w_out': w_out[l].astype(BF16),
        'norm_ffn_g': norm_ffn_g[l],
        'router_w': rw.astype(BF16), 'router_b': rb.reshape(1, LANES),
        'expert_w_gate_all': expert_w_gate, 'expert_w_up_all': expert_w_up, 'expert_w_down_all': expert_w_down,
    }


def kernel(x_prompt, x_sample, state_mlstm_C, state_mlstm_n, state_mlstm_m, state_gla_S, cache_conv, cache_swa_k, cache_swa_v, norm_mix_g, w_in, mlstm_b_i, mlstm_b_f, mlstm_norm_g, conv_w, conv_b, conv_norm_g, conv_norm_b, gla_w_gate, gla_b_gate, gla_norm_g, swa_sinks, w_out, norm_ffn_g, router_group_w, router_group_b, router_expert_w, router_expert_b, expert_w_gate, expert_w_up, expert_w_down, final_norm_g):
    depth = w_in.shape[0]
    weights = (norm_mix_g, w_in, mlstm_b_i, mlstm_b_f, mlstm_norm_g, conv_w, conv_b, conv_norm_g, conv_norm_b,
               gla_w_gate, gla_b_gate, gla_norm_g, swa_sinks, w_out, norm_ffn_g, router_group_w, router_group_b,
               router_expert_w, router_expert_b, expert_w_gate, expert_w_up, expert_w_down)
    layer_params = [_prep_layer_params(l, *weights) for l in range(depth)]
    states = (state_mlstm_C, state_mlstm_n, state_mlstm_m, state_gla_S, cache_conv, cache_swa_k, cache_swa_v)
    return _forward(x_prompt, x_sample, states, layer_params, final_norm_g)
```

```python
import functools
from typing import NamedTuple

import jax
import jax.numpy as jnp
from jax import lax
from jax.experimental import pallas as pl
from jax.experimental.pallas import tpu as pltpu

F32 = jnp.float32
BF16 = jnp.bfloat16
I32 = jnp.int32

D_MODEL = 2048
GROUP_WIDTH = 512
HEAD_W = 128
MLSTM_HEADS = 4
GLA_HEADS = 4
GLA_DK = 64
GLA_LOWRANK = 16
GLA_TAU = 16.0
CONV_WIDTH = 31
SWA_HEADS = 8
SWA_KV_HEADS = 2
SWA_HEAD_DIM = 64
WINDOW = 128
N_GROUPS = 4
EXPERTS_PER_GROUP = 4
N_EXPERTS = 16
D_EXPERT = 1024
NORM_EPS = 1e-6
LANES = 128
SUBLANES = 8

A_Q, A_K, A_V, A_O = 0, 512, 1024, 1536
C_A, C_G = 2048, 2560
G_V, G_R = 3072, 3584
S_Q = 4096
G_Q, G_K = 4608, 4864
A_G, G_LR, S_K, S_V = 5120, 5248, 5376, 5504
N_PROJ = 5632
MIX_COL_MLSTM, MIX_COL_CONV, MIX_COL_GLA, MIX_COL_SWA = 0, 1, 2, 3
_W_IN_SEGMENTS = (
    (0, 512, A_Q), (512, 1024, A_K), (1024, 1536, A_V), (1536, 2048, A_O),
    (2048, 2056, A_G),
    (2056, 2568, C_A), (2568, 3080, C_G),
    (3080, 3336, G_Q), (3336, 3592, G_K), (3592, 4104, G_V), (4104, 4616, G_R),
    (4616, 4632, G_LR),
    (4632, 5144, S_Q), (5144, 5272, S_K), (5272, 5400, S_V),
)

MLSTM_CHUNK = 128
MLSTM_STEP_ROWS = 256
GLA_CHUNK = 64
GLA_STEP_ROWS = 256
GLA_SUB = 16
SWA_STEP_ROWS = 256
CONV_TILE = 256
CONV_ROWS = 64
CONV_HALO = 32
SAMPLE_BB = 8
MOE_TILE = 256
ROW_DMA_TILES = (640, 512, 256, 128, 64, 8)
VMEM_LIMIT = 56 * 1024 * 1024


def _cparams(sem, vmem=VMEM_LIMIT):
    return pltpu.CompilerParams(dimension_semantics=sem, vmem_limit_bytes=vmem)


def _log_sigmoid(x):
    return jnp.minimum(x, 0.0) - jnp.log1p(jnp.exp(-jnp.abs(x)))


def _sigmoid(x):
    return 1.0 / (1.0 + jnp.exp(-x))


def _silu(x):
    return x * _sigmoid(x)


def _masked_row_sums(mask, x):
    m = jnp.where(mask, 1.0, 0.0).astype(BF16)
    hi = x.astype(BF16)
    r1 = x - hi.astype(F32)
    mid = r1.astype(BF16)
    lo = (r1 - mid.astype(F32)).astype(BF16)
    out = jnp.dot(m, hi, preferred_element_type=F32)
    out = out + jnp.dot(m, mid, preferred_element_type=F32)
    return out + jnp.dot(m, lo, preferred_element_type=F32)


def _pick_tile(n, candidates):
    for c in candidates:
        if n % c == 0:
            return c
    raise ValueError(f"no tile for {n} in {candidates}")


class _SideCast(NamedTuple):
    operand: jax.Array
    in_spec: pl.BlockSpec
    out_spec: pl.BlockSpec
    out_shape: jax.ShapeDtypeStruct
    shape: tuple


def _side_cast(w_all, layer, n_steps):
    depth, n_e, k, f = w_all.shape
    rows = n_e * k
    assert rows % n_steps == 0, (rows, n_steps)
    tr = rows // n_steps
    return _SideCast(
        operand=w_all.reshape(depth * rows, f),
        in_spec=pl.BlockSpec((tr, f), lambda i: (layer * n_steps + i, 0)),
        out_spec=pl.BlockSpec((tr, f), lambda i: (i, 0)),
        out_shape=jax.ShapeDtypeStruct((rows, f), BF16),
        shape=(n_e, k, f),
    )


def _proj_kernel(x_ref, g_ref, w_ref, o_ref, hn_ref):
    @pl.when(pl.program_id(1) == 0)
    def _():
        x = x_ref[...]
        ms = jnp.mean(x * x, axis=-1, keepdims=True)
        hn_ref[...] = (x * lax.rsqrt(ms + NORM_EPS) * g_ref[...]).astype(BF16)

    o_ref[...] = lax.dot_general(hn_ref[...], w_ref[...], (((1,), (1,)), ((), ())), preferred_element_type=F32)


def _in_projection(x, g, w_bf16):
    n = x.shape[0]
    tm = _pick_tile(n, (832, 640, 512, 256, 128, 64, 8))
    tn = N_PROJ // 4
    return pl.pallas_call(
        _proj_kernel,
        grid=(n // tm, N_PROJ // tn),
        in_specs=[
            pl.BlockSpec((tm, D_MODEL), lambda i, j: (i, 0)),
            pl.BlockSpec((1, D_MODEL), lambda i, j: (0, 0)),
            pl.BlockSpec((tn, D_MODEL), lambda i, j: (j, 0)),
        ],
        out_specs=pl.BlockSpec((tm, tn), lambda i, j: (i, j)),
        out_shape=jax.ShapeDtypeStruct((n, N_PROJ), F32),
        scratch_shapes=[pltpu.VMEM((tm, D_MODEL), BF16)],
        compiler_params=_cparams(("parallel", "arbitrary")),
        name="in_projection",
    )(x, g.reshape(1, D_MODEL), w_bf16)


def _mlstm_prompt_kernel(q_ref, k_ref, v_ref, o_ref, gt_ref, bias_ref, ng_ref, mixed_in_ref, wc_in_ref,
                         y_ref, cn_ref, m_ref, wc_out_ref):
    del mixed_in_ref
    wc_out_ref[...] = wc_in_ref[...].astype(wc_out_ref.dtype)
    rows_step = q_ref.shape[0]
    L = min(MLSTM_CHUNK, rows_step)
    n_chunks = rows_step // L

    @pl.when(pl.program_id(0) == 0)
    def _():
        cn_ref[...] = jnp.zeros_like(cn_ref)
        m_ref[...] = jnp.zeros_like(m_ref)

    pre = gt_ref[...] + bias_ref[...]
    lf = _log_sigmoid(pre)
    row = lax.broadcasted_iota(I32, (rows_step, rows_step), 0)
    col = lax.broadcasted_iota(I32, (rows_step, rows_step), 1)
    cum = ((row // L) == (col // L)) & (col <= row)
    b_all = _masked_row_sums(cum, lf)
    pre_t = pre.T
    b_t = b_all.T
    trow = lax.broadcasted_iota(I32, (L, L), 0)
    tcol = lax.broadcasted_iota(I32, (L, L), 1)
    tri = tcol <= trow
    lane = lax.broadcasted_iota(I32, (L, HEAD_W), 1)
    ones_col = (lane == 0).astype(BF16)
    for h in range(MLSTM_HEADS):
        sl = slice(HEAD_W * h, HEAD_W * (h + 1))
        m_prev = m_ref[h:h + 1, 0:1]
        cn = cn_ref[h]
        for c in range(n_chunks):
            cs = slice(L * c, L * (c + 1))
            q = q_ref[cs, sl]
            k = k_ref[cs, sl] * (HEAD_W ** -0.5)
            v = v_ref[cs, sl]
            b_col = b_all[cs, 4 + h:5 + h]
            i_col = pre[cs, h:h + 1]
            b_row = b_t[4 + h:5 + h, cs]
            i_row = pre_t[h:h + 1, cs]
            log_d = jnp.where(tri, b_col - b_row + i_row, -jnp.inf)
            log_inter = b_col + m_prev
            m_t = jnp.maximum(log_inter, jnp.max(log_d, axis=-1, keepdims=True))
            d_mat = jnp.exp(log_d - m_t)
            g_inter = jnp.exp(log_inter - m_t)
            qb = q.astype(BF16)
            kb = k.astype(BF16)
            s = lax.dot_general(qb, kb, (((1,), (1,)), ((), ())), preferred_element_type=F32)
            w = (s * d_mat).astype(BF16)
            v1 = jnp.concatenate([v.astype(BF16), ones_col], axis=1)
            nd = g_inter * jnp.dot(qb, cn.astype(BF16), preferred_element_type=F32)
            nd = nd + jnp.dot(w, v1, preferred_element_type=F32)
            num = nd[:, :HEAD_W]
            den = nd[:, HEAD_W:HEAD_W + 1]
            hh = num / jnp.maximum(jnp.abs(den), jnp.exp(-m_t))
            hh = _sigmoid(o_ref[cs, sl]) * hh
            hh = hh * lax.rsqrt(jnp.mean(hh * hh, axis=-1, keepdims=True) + NORM_EPS) * ng_ref[:, sl]
            y_ref[cs, sl] = hh.astype(y_ref.dtype)
            m_last = m_t[L - 1:L, :]
            b_last = b_col[L - 1:L, :]
            g_state = jnp.exp(b_last + m_prev - m_last)
            w_k = jnp.exp(b_last - b_col + i_col - m_last)
            kw = (k * w_k).astype(BF16)
            upd = lax.dot_general(kw, v1, (((0,), (0,)), ((), ())), preferred_element_type=F32)
            cn = g_state * cn + upd
            m_prev = m_last
        cn_ref[h] = cn
        m_ref[h:h + 1, :] = jnp.broadcast_to(m_prev, (1, LANES))


def _mlstm_prompt(proj, mixed, t_len, bias_row, norm_g, w_cast, layer):
    L = _pick_tile(t_len, (MLSTM_STEP_ROWS, MLSTM_CHUNK, 64, 32, 16, 8))
    side = _side_cast(w_cast, layer, t_len // L)

    def col(off):
        return pl.BlockSpec((L, GROUP_WIDTH), lambda i, o=off: (i, o // GROUP_WIDTH))

    return pl.pallas_call(
        _mlstm_prompt_kernel,
        grid=(t_len // L,),
        in_specs=[
            col(A_Q), col(A_K), col(A_V), col(A_O),
            pl.BlockSpec((L, LANES), lambda i: (i, A_G // LANES)),
            pl.BlockSpec((1, LANES), lambda i: (0, 0)),
            pl.BlockSpec((1, GROUP_WIDTH), lambda i: (0, 0)),
            pl.BlockSpec(memory_space=pl.ANY),
            side.in_spec,
        ],
        out_specs=[
            pl.BlockSpec((L, GROUP_WIDTH), lambda i: (i, MIX_COL_MLSTM)),
            pl.BlockSpec((MLSTM_HEADS, HEAD_W, 2 * HEAD_W), lambda i: (0, 0, 0)),
            pl.BlockSpec((SUBLANES, LANES), lambda i: (0, 0)),
            side.out_spec,
        ],
        out_shape=[
            jax.ShapeDtypeStruct(mixed.shape, mixed.dtype),
            jax.ShapeDtypeStruct((MLSTM_HEADS, HEAD_W, 2 * HEAD_W), F32),
            jax.ShapeDtypeStruct((SUBLANES, LANES), F32),
            side.out_shape,
        ],
        input_output_aliases={7: 0},
        compiler_params=_cparams(("arbitrary",)),
        name="mlstm_prompt",
    )(proj, proj, proj, proj, proj, bias_row, norm_g.reshape(1, GROUP_WIDTH), mixed, side.operand)


def _gla_prompt_kernel(q_ref, k_ref, v_ref, r_ref, lr_ref, wg_ref, bg_ref, ng_ref, mixed_in_ref, wc_in_ref,
                       y_ref, sp_ref, wc_out_ref):
    del mixed_in_ref
    wc_out_ref[...] = wc_in_ref[...].astype(wc_out_ref.dtype)
    rows_step = q_ref.shape[0]
    L = min(GLA_CHUNK, rows_step)
    n_chunks = rows_step // L
    n_sub = L // GLA_SUB

    @pl.when(pl.program_id(0) == 0)
    def _():
        sp_ref[...] = jnp.zeros_like(sp_ref)

    gate_pre = jnp.dot(lr_ref[...].astype(BF16), wg_ref[...], preferred_element_type=F32) + bg_ref[...]
    log_a = _log_sigmoid(gate_pre) * (1.0 / GLA_TAU)
    row = lax.broadcasted_iota(I32, (rows_step, rows_step), 0)
    col = lax.broadcasted_iota(I32, (rows_step, rows_step), 1)
    tri = ((row // L) == (col // L)) & (col <= row)
    b = _masked_row_sums(tri, log_a)
    q = q_ref[...] * (GLA_DK ** -0.5)
    k = k_ref[...]
    b_last = jnp.concatenate(
        [jnp.broadcast_to(b[L * c + L - 1:L * c + L, :], (L, b.shape[1])) for c in range(n_chunks)], axis=0)
    q_in = q * jnp.exp(b)
    k_dec = k * jnp.exp(b_last - b)
    lane16 = lax.broadcasted_iota(I32, (GLA_SUB, LANES), 1)
    lo16 = lane16 < GLA_DK
    srow = lax.broadcasted_iota(I32, (LANES, 2 * HEAD_W), 0)
    scol = lax.broadcasted_iota(I32, (LANES, 2 * HEAD_W), 1)
    block_diag = (srow < GLA_DK) == (scol < HEAD_W)
    for p in range(2):
        pls = slice(LANES * p, LANES * (p + 1))
        vp = v_ref[:, 2 * HEAD_W * p:2 * HEAD_W * (p + 1)].astype(BF16)
        b_t = b[:, pls].T
        sp = sp_ref[p]
        states = []
        for c in range(n_chunks):
            cs = slice(L * c, L * (c + 1))
            states.append(sp)
            dec_col = jnp.exp(b_t[:, L * c + L - 1:L * c + L])
            upd = lax.dot_general(k_dec[cs, pls].astype(BF16), vp[cs], (((0,), (0,)), ((), ())),
                                  preferred_element_type=F32)
            sp = jnp.where(block_diag, dec_col * sp + upd, 0.0)
        sp_ref[p] = sp
        outs = []
        for c in range(n_chunks):
            c0 = L * c
            o_inter = jnp.dot(q_in[c0:c0 + L, pls].astype(BF16), states[c].astype(BF16),
                              preferred_element_type=F32)
            rows = []
            for blk in range(n_sub):
                r0 = GLA_SUB * blk
                n = GLA_SUB * (blk + 1)
                qrows = slice(c0 + r0, c0 + r0 + GLA_SUB)
                krows = slice(c0, c0 + n)
                if blk == 0:
                    qs = q[qrows, pls] * jnp.exp(b[qrows, pls])
                    ks = k[krows, pls] * jnp.exp(-b[krows, pls])
                else:
                    anchor = b[c0 + r0 - 1:c0 + r0, pls]
                    qs = q[qrows, pls] * jnp.exp(b[qrows, pls] - anchor)
                    ks = k[krows, pls] * jnp.exp(anchor - b[krows, pls])
                qs2 = jnp.concatenate([jnp.where(lo16, qs, 0.0), jnp.where(lo16, 0.0, qs)], axis=0)
                att = lax.dot_general(qs2.astype(BF16), ks.astype(BF16), (((1,), (1,)), ((), ())),
                                      preferred_element_type=F32)
                trow = lax.broadcasted_iota(I32, (2 * GLA_SUB, n), 0)
                tcol = lax.broadcasted_iota(I32, (2 * GLA_SUB, n), 1)
                t_idx = r0 + jnp.where(trow >= GLA_SUB, trow - GLA_SUB, trow)
                att = jnp.where(tcol <= t_idx, att, 0.0)
                o2 = jnp.dot(att.astype(BF16), vp[krows], preferred_element_type=F32)
                rows.append(jnp.concatenate([o2[:GLA_SUB, :HEAD_W], o2[GLA_SUB:, HEAD_W:]], axis=1))
            outs.append(o_inter + jnp.concatenate(rows, axis=0))
        o = jnp.concatenate(outs, axis=0)
        for hh in range(2):
            head = 2 * p + hh
            hs = slice(HEAD_W * head, HEAD_W * (head + 1))
            oh = o[:, HEAD_W * hh:HEAD_W * (hh + 1)]
            oh = oh * lax.rsqrt(jnp.mean(oh * oh, axis=-1, keepdims=True) + NORM_EPS) * ng_ref[:, hs]
            y_ref[:, hs] = (oh * _silu(r_ref[:, hs])).astype(y_ref.dtype)


def _gla_prompt(proj, mixed, t_len, w_gate_pad, b_gate, norm_g, w_cast, layer):
    L = _pick_tile(t_len, (GLA_STEP_ROWS, GLA_CHUNK))
    side = _side_cast(w_cast, layer, t_len // L)
    return pl.pallas_call(
        _gla_prompt_kernel,
        grid=(t_len // L,),
        in_specs=[
            pl.BlockSpec((L, 256), lambda i: (i, G_Q // 256)),
            pl.BlockSpec((L, 256), lambda i: (i, G_K // 256)),
            pl.BlockSpec((L, GROUP_WIDTH), lambda i: (i, G_V // GROUP_WIDTH)),
            pl.BlockSpec((L, GROUP_WIDTH), lambda i: (i, G_R // GROUP_WIDTH)),
            pl.BlockSpec((L, LANES), lambda i: (i, G_LR // LANES)),
            pl.BlockSpec((LANES, 256), lambda i: (0, 0)),
            pl.BlockSpec((1, 256), lambda i: (0, 0)),
            pl.BlockSpec((1, GROUP_WIDTH), lambda i: (0, 0)),
            pl.BlockSpec(memory_space=pl.ANY),
            side.in_spec,
        ],
        out_specs=[
            pl.BlockSpec((L, GROUP_WIDTH), lambda i: (i, MIX_COL_GLA)),
            pl.BlockSpec((2, LANES, 2 * HEAD_W), lambda i: (0, 0, 0)),
            side.out_spec,
        ],
        out_shape=[
            jax.ShapeDtypeStruct(mixed.shape, mixed.dtype),
            jax.ShapeDtypeStruct((2, LANES, 2 * HEAD_W), F32),
            side.out_shape,
        ],
        input_output_aliases={8: 0},
        compiler_params=_cparams(("arbitrary",)),
        name="gla_prompt",
    )(proj, proj, proj, proj, proj, w_gate_pad, b_gate.reshape(1, 256), norm_g.reshape(1, GROUP_WIDTH), mixed,
      side.operand)


def _conv_norm_act(y, g_ref, be_ref):
    mu = jnp.mean(y, axis=-1, keepdims=True)
    yc = y - mu
    var = jnp.mean(yc * yc, axis=-1, keepdims=True)
    return _silu(yc * lax.rsqrt(var + NORM_EPS) * g_ref[...] + be_ref[...])


def _conv_prompt_kernel(ua_ref, ug_ref, ha_ref, hg_ref, w_ref, b_ref, g_ref, be_ref, mixed_in_ref, wc_in_ref,
                        y_ref, tail_ref, wc_out_ref, buf_ref, sh_ref):
    del mixed_in_ref
    wc_out_ref[...] = wc_in_ref[...].astype(wc_out_ref.dtype)
    tt = ua_ref.shape[0]
    span = tt + CONV_HALO
    halo = ha_ref[...] * _sigmoid(hg_ref[...])
    buf_ref[0:CONV_HALO, :] = jnp.where(pl.program_id(0) > 0, halo, 0.0)
    buf_ref[CONV_HALO:span, :] = ua_ref[...] * _sigmoid(ug_ref[...])
    buf_ref[span:span + SUBLANES, :] = jnp.zeros((SUBLANES, GROUP_WIDTH), F32)
    for k in range(1, SUBLANES):
        sh_ref[k] = buf_ref[k:k + span, :]
    base = CONV_HALO - (CONV_WIDTH - 1)
    for r in range(tt // CONV_ROWS):
        acc = jnp.zeros((CONV_ROWS, GROUP_WIDTH), F32)
        for j in range(CONV_WIDTH):
            s0 = r * CONV_ROWS + base + j
            k = s0 % SUBLANES
            a0 = s0 - k
            win = buf_ref[a0:a0 + CONV_ROWS, :] if k == 0 else sh_ref[k, a0:a0 + CONV_ROWS, :]
            acc = acc + w_ref[j:j + 1, :] * win
        y = _conv_norm_act(acc + b_ref[...], g_ref, be_ref)
        y_ref[r * CONV_ROWS:(r + 1) * CONV_ROWS, :] = y.astype(y_ref.dtype)
    tail_ref[...] = buf_ref[tt:span, :]


def _conv_prompt(proj, mixed, t_len, w, b, g, beta, w_cast, layer):
    tt = _pick_tile(t_len, (CONV_TILE, 128, 64))
    ratio = tt // CONV_HALO
    side = _side_cast(w_cast, layer, t_len // tt)
    vec = lambda: pl.BlockSpec((1, GROUP_WIDTH), lambda i: (0, 0))
    return pl.pallas_call(
        _conv_prompt_kernel,
        grid=(t_len // tt,),
        in_specs=[
            pl.BlockSpec((tt, GROUP_WIDTH), lambda i: (i, C_A // GROUP_WIDTH)),
            pl.BlockSpec((tt, GROUP_WIDTH), lambda i: (i, C_G // GROUP_WIDTH)),
            pl.BlockSpec((CONV_HALO, GROUP_WIDTH), lambda i: (jnp.maximum(i * ratio - 1, 0), C_A // GROUP_WIDTH)),
            pl.BlockSpec((CONV_HALO, GROUP_WIDTH), lambda i: (jnp.maximum(i * ratio - 1, 0), C_G // GROUP_WIDTH)),
            pl.BlockSpec((CONV_WIDTH, GROUP_WIDTH), lambda i: (0, 0)),
            vec(), vec(), vec(),
            pl.BlockSpec(memory_space=pl.ANY),
            side.in_spec,
        ],
        out_specs=[
            pl.BlockSpec((tt, GROUP_WIDTH), lambda i: (i, MIX_COL_CONV)),
            pl.BlockSpec((CONV_HALO, GROUP_WIDTH), lambda i: (0, 0)),
            side.out_spec,
        ],
        out_shape=[
            jax.ShapeDtypeStruct(mixed.shape, mixed.dtype),
            jax.ShapeDtypeStruct((CONV_HALO, GROUP_WIDTH), F32),
            side.out_shape,
        ],
        scratch_shapes=[pltpu.VMEM((tt + CONV_HALO + SUBLANES, GROUP_WIDTH), F32),
                        pltpu.VMEM((SUBLANES, tt + CONV_HALO, GROUP_WIDTH), F32)],
        input_output_aliases={8: 0},
        compiler_params=_cparams(("arbitrary",)),
        name="conv_prompt",
    )(proj, proj, proj, proj, w, b.reshape(1, -1), g.reshape(1, -1), beta.reshape(1, -1), mixed, side.operand)


def _swa_prompt_kernel(sink_ref, q_ref, kc_ref, vc_ref, kp_ref, vp_ref, mixed_in_ref, *rest):
    del mixed_in_ref
    if len(rest) == 3:
        w_ref, y_ref, wpad_ref = rest
        _pad_w_in_rows(w_ref, wpad_ref)
    else:
        (y_ref,) = rest
    bq = WINDOW
    n_blk = q_ref.shape[0] // bq
    first = pl.program_id(0) == 0
    k_full = jnp.concatenate([kp_ref[...], kc_ref[...]], axis=0)
    v_full = jnp.concatenate([vp_ref[...], vc_ref[...]], axis=0)
    k_sw_full = pltpu.roll(k_full, SWA_HEAD_DIM, 1).astype(BF16)
    v_sw_full = pltpu.roll(v_full, SWA_HEAD_DIM, 1).astype(BF16)
    k_full = k_full.astype(BF16)
    v_full = v_full.astype(BF16)
    tq = lax.broadcasted_iota(I32, (bq, 2 * bq), 0)
    kj = lax.broadcasted_iota(I32, (bq, 2 * bq), 1)
    band = (kj > tq) & (kj <= tq + WINDOW)
    lane = lax.broadcasted_iota(I32, (bq, LANES), 1)
    lo = lane < SWA_HEAD_DIM
    rep = SWA_HEADS // SWA_KV_HEADS
    for blk in range(n_blk):
        qs = slice(bq * blk, bq * (blk + 1))
        ks = slice(bq * blk, bq * (blk + 2))
        valid = band & (kj >= jnp.where(first, bq, 0)) if blk == 0 else band
        k_all, v_all, k_sw, v_sw = k_full[ks], v_full[ks], k_sw_full[ks], v_sw_full[ks]
        for c in range(SWA_HEADS // 2):
            qc = q_ref[qs, LANES * c:LANES * (c + 1)] * (SWA_HEAD_DIM ** -0.5)
            outs = []
            for hh in range(2):
                h = 2 * c + hh
                g = h // rep
                qm = jnp.where(lo if hh == 0 else jnp.logical_not(lo), qc, 0.0).astype(BF16)
                k_use = k_all if g == hh else k_sw
                v_use = v_all if g == hh else v_sw
                s = lax.dot_general(qm, k_use, (((1,), (1,)), ((), ())), preferred_element_type=F32)
                s = jnp.where(valid, s, -jnp.inf)
                sink = sink_ref[h]
                mx = jnp.maximum(jnp.max(s, axis=-1, keepdims=True), sink)
                p = jnp.exp(s - mx)
                den = jnp.sum(p, axis=-1, keepdims=True) + jnp.exp(sink - mx)
                p = (p / den).astype(BF16)
                outs.append(jnp.dot(p, v_use, preferred_element_type=F32))
            y_ref[qs, LANES * c:LANES * (c + 1)] = jnp.where(lo, outs[0], outs[1]).astype(y_ref.dtype)


def _swa_prompt(proj, mixed, t_len, sinks, w_in_all=None, next_layer=None):
    bq = WINDOW
    rows = _pick_tile(t_len, (SWA_STEP_ROWS, bq))
    ratio = rows // bq
    cur = lambda off: pl.BlockSpec((rows, LANES), lambda i: (i, off // LANES))
    prev = lambda off: pl.BlockSpec((bq, LANES), lambda i: (jnp.maximum(i * ratio - 1, 0), off // LANES))
    in_specs = [
        pl.BlockSpec(memory_space=pltpu.SMEM),
        pl.BlockSpec((rows, GROUP_WIDTH), lambda i: (i, S_Q // GROUP_WIDTH)),
        cur(S_K), cur(S_V), prev(S_K), prev(S_V),
        pl.BlockSpec(memory_space=pl.ANY),
    ]
    out_specs = [pl.BlockSpec((rows, GROUP_WIDTH), lambda i: (i, MIX_COL_SWA))]
    out_shape = [jax.ShapeDtypeStruct(mixed.shape, mixed.dtype)]
    operands = [sinks, proj, proj, proj, proj, proj, mixed]
    if next_layer is not None:
        side = _w_in_side(w_in_all, next_layer, t_len // rows)
        in_specs.append(side.in_spec)
        out_specs.append(side.out_spec)
        out_shape.append(side.out_shape)
        operands.append(side.operand)
    outs = pl.pallas_call(
        _swa_prompt_kernel,
        grid=(t_len // rows,),
        in_specs=in_specs, out_specs=out_specs, out_shape=out_shape,
        input_output_aliases={6: 0},
        compiler_params=_cparams(("arbitrary",)),
        name="swa_prompt",
    )(*operands)
    return outs[0], (outs[1] if next_layer is not None else None)


_T_MK, _T_MQ = 0, 512
_T_GA, _T_GK, _T_GQ = 1024, 1280, 1536
_T_ROWS = 1792


def _sample_kernel(ps_ref, mm_ref, bias_ref, n0_ref, c0_ref, s0_ref, cv0_ref, k0_ref, v0_ref,
                   qm_ref, sink_ref, wg_ref, bg_ref, mng_ref, gng_ref, cw_ref, cb_ref, cg_ref, cbe_ref,
                   mix_ref, c1_ref, n1_ref, m1_ref, s1_ref, cv1_ref, k1_ref, v1_ref,
                   tt_ref, bc_ref, num_ref, go_ref, yc_ref, yd_ref, ybuf, ybf, zbuf, zsem, ysem):
    i = pl.program_id(0)
    nb = ps_ref.shape[0]
    bb = c0_ref.shape[0]
    t_len = mix_ref.shape[0] - nb
    zr = zbuf.shape[0]

    def zero_copy(t):
        return pltpu.make_async_copy(zbuf, mix_ref.at[pl.ds(t * zr, zr)], zsem.at[0])

    def gla_gate(lr):
        gp = jnp.dot(lr.astype(BF16), wg_ref[...], preferred_element_type=F32) + bg_ref[...]
        return jnp.exp(_log_sigmoid(gp) * (1.0 / GLA_TAU))

    @pl.when(i == 0)
    def _():
        zbuf[...] = jnp.zeros_like(zbuf)
        for t in range(t_len // zr):
            zero_copy(t).start()
        for h in range(MLSTM_HEADS):
            kk = ps_ref[:, A_K + HEAD_W * h:A_K + HEAD_W * (h + 1)] * (HEAD_W ** -0.5)
            tt_ref[_T_MK + HEAD_W * h:_T_MK + HEAD_W * (h + 1), :] = kk.T.astype(BF16)
            qq = ps_ref[:, A_Q + HEAD_W * h:A_Q + HEAD_W * (h + 1)]
            tt_ref[_T_MQ + HEAD_W * h:_T_MQ + HEAD_W * (h + 1), :] = qq.T.astype(BF16)
        a_all = gla_gate(ps_ref[:, G_LR:G_LR + LANES])
        for p in range(2):
            pls = slice(LANES * p, LANES * (p + 1))
            tt_ref[_T_GA + LANES * p:_T_GA + LANES * (p + 1), :] = a_all[:, pls].T.astype(BF16)
            kk = ps_ref[:, G_K + LANES * p:G_K + LANES * (p + 1)]
            tt_ref[_T_GK + LANES * p:_T_GK + LANES * (p + 1), :] = kk.T.astype(BF16)
            qq = ps_ref[:, G_Q + LANES * p:G_Q + LANES * (p + 1)] * (GLA_DK ** -0.5)
            tt_ref[_T_GQ + LANES * p:_T_GQ + LANES * (p + 1), :] = qq.T.astype(BF16)

    r0 = pl.multiple_of(i * bb, bb)
    rows = pl.ds(r0, bb)

    pre = ps_ref[rows, A_G:A_G + LANES] + bias_ref[...]
    lfm = _log_sigmoid(pre) + mm_ref[...]
    f_al = pltpu.roll(lfm, LANES - MLSTM_HEADS, 1)
    m_t = jnp.maximum(f_al, pre)
    g_st = jnp.exp(f_al - m_t)
    w_k = jnp.exp(pre - m_t)
    m1_ref[...] = m_t
    n_new = []
    for h in range(MLSTM_HEADS):
        kk = ps_ref[rows, A_K + HEAD_W * h:A_K + HEAD_W * (h + 1)] * (HEAD_W ** -0.5)
        nn = g_st[:, h:h + 1] * n0_ref[:, HEAD_W * h:HEAD_W * (h + 1)] + w_k[:, h:h + 1] * kk
        n1_ref[:, HEAD_W * h:HEAD_W * (h + 1)] = nn
        n_new.append(nn)

    glu = ps_ref[rows, C_A:C_A + GROUP_WIDTH] * _sigmoid(ps_ref[rows, C_G:C_G + GROUP_WIDTH])
    yc_ref[...] = glu * cw_ref[CONV_WIDTH - 1:CONV_WIDTH, :]
    a_v = ps_ref[rows, A_V:A_V + GROUP_WIDTH]
    g_v = ps_ref[rows, G_V:G_V + GROUP_WIDTH]
    s_k = ps_ref[rows, S_K:S_K + LANES]
    s_v = ps_ref[rows, S_V:S_V + LANES]

    lane_b = lax.broadcasted_iota(I32, (nb, LANES), 0)
    key_row = lax.broadcasted_iota(I32, (SWA_HEADS, WINDOW), 1)
    lo_row = lax.broadcasted_iota(I32, (1, LANES), 1) < SWA_HEAD_DIM
    sink_col = sink_ref[:, 0:1]

    for j in range(bb):
        onehot = (lane_b == r0 + j).astype(BF16)
        bc_ref[...] = jnp.dot(tt_ref[...], onehot, preferred_element_type=F32)
        jrow = slice(j, j + 1)
        for h in range(MLSTM_HEADS):
            hs = slice(HEAD_W * h, HEAD_W * (h + 1))
            kbc = bc_ref[_T_MK + HEAD_W * h:_T_MK + HEAD_W * (h + 1), :]
            qbc = bc_ref[_T_MQ + HEAD_W * h:_T_MQ + HEAD_W * (h + 1), :]
            g1 = g_st[jrow, h:h + 1]
            w1 = w_k[jrow, h:h + 1]
            v_row = a_v[jrow, hs]
            c_new = g1 * c0_ref[j, h] + kbc * (w1 * v_row)
            c1_ref[j, h] = c_new
            num_ref[jrow, hs] = jnp.sum(qbc * c_new, axis=0, keepdims=True)
        for p in range(2):
            abc = bc_ref[_T_GA + LANES * p:_T_GA + LANES * (p + 1), :]
            kbc = bc_ref[_T_GK + LANES * p:_T_GK + LANES * (p + 1), :]
            qbc = bc_ref[_T_GQ + LANES * p:_T_GQ + LANES * (p + 1), :]
            for hh in range(2):
                head = 2 * p + hh
                hs = slice(HEAD_W * head, HEAD_W * (head + 1))
                ds_ = slice(GLA_DK * hh, GLA_DK * (hh + 1))
                v_row = g_v[jrow, hs]
                s_new = abc[ds_, :] * s0_ref[j, head] + kbc[ds_, :] * v_row
                s1_ref[j, head] = s_new
                go_ref[jrow, hs] = jnp.sum(qbc[ds_, :] * s_new, axis=0, keepdims=True)
        cache = cv0_ref[j]
        yc_ref[jrow, :] = yc_ref[jrow, :] + jnp.sum(cache * cw_ref[0:CONV_WIDTH - 1, :], axis=0, keepdims=True)
        cv1_ref[j, 0:CONV_WIDTH - 2, :] = cv0_ref[j, 1:CONV_WIDTH - 1, :]
        cv1_ref[j, CONV_WIDTH - 2:CONV_WIDTH - 1, :] = glu[jrow, :]
        k_new = s_k[jrow, :]
        v_new = s_v[jrow, :]
        k1_ref[j, 0:WINDOW - 1, :] = k0_ref[j, 1:WINDOW, :]
        k1_ref[j, WINDOW - 1:WINDOW, :] = k_new
        v1_ref[j, 0:WINDOW - 1, :] = v0_ref[j, 1:WINDOW, :]
        v1_ref[j, WINDOW - 1:WINDOW, :] = v_new
        qmat = qm_ref[j] * (SWA_HEAD_DIM ** -0.5)
        s_old = lax.dot_general(qmat.astype(BF16), k0_ref[j].astype(BF16), (((1,), (1,)), ((), ())),
                                preferred_element_type=F32)
        s_old = jnp.where(key_row >= 1, s_old, -jnp.inf)
        s_cur = jnp.sum(qmat * k_new, axis=-1, keepdims=True)
        mx = jnp.maximum(jnp.maximum(jnp.max(s_old, axis=-1, keepdims=True), s_cur), sink_col)
        p_old = jnp.exp(s_old - mx)
        p_cur = jnp.exp(s_cur - mx)
        den = jnp.sum(p_old, axis=-1, keepdims=True) + p_cur + jnp.exp(sink_col - mx)
        o = jnp.dot((p_old / den).astype(BF16), v0_ref[j].astype(BF16), preferred_element_type=F32)
        o = o + (p_cur / den) * v_new
        o_sw = pltpu.roll(o, SWA_HEAD_DIM, 1)
        for c in range(SWA_HEADS // 2):
            g = c // (SWA_HEADS // SWA_KV_HEADS // 2)
            left = (o if g == 0 else o_sw)[2 * c:2 * c + 1, :]
            right = (o_sw if g == 0 else o)[2 * c + 1:2 * c + 2, :]
            yd_ref[jrow, LANES * c:LANES * (c + 1)] = jnp.where(lo_row, left, right)

    for h in range(MLSTM_HEADS):
        hs = slice(HEAD_W * h, HEAD_W * (h + 1))
        qq = ps_ref[rows, A_Q + HEAD_W * h:A_Q + HEAD_W * (h + 1)]
        den = jnp.sum(qq * n_new[h], axis=-1, keepdims=True)
        hh = num_ref[:, hs] / jnp.maximum(jnp.abs(den), jnp.exp(-m_t[:, h:h + 1]))
        hh = _sigmoid(ps_ref[rows, A_O + HEAD_W * h:A_O + HEAD_W * (h + 1)]) * hh
        hh = hh * lax.rsqrt(jnp.mean(hh * hh, axis=-1, keepdims=True) + NORM_EPS) * mng_ref[:, hs]
        ybuf[rows, hs] = hh
    ybuf[rows, GROUP_WIDTH:2 * GROUP_WIDTH] = _conv_norm_act(yc_ref[...] + cb_ref[...], cg_ref, cbe_ref)
    for head in range(GLA_HEADS):
        hs = slice(HEAD_W * head, HEAD_W * (head + 1))
        oh = go_ref[:, hs]
        oh = oh * lax.rsqrt(jnp.mean(oh * oh, axis=-1, keepdims=True) + NORM_EPS) * gng_ref[:, hs]
        gr = ps_ref[rows, G_R + HEAD_W * head:G_R + HEAD_W * (head + 1)]
        ybuf[rows, 2 * GROUP_WIDTH + HEAD_W * head:2 * GROUP_WIDTH + HEAD_W * (head + 1)] = oh * _silu(gr)
    ybuf[rows, 3 * GROUP_WIDTH:4 * GROUP_WIDTH] = yd_ref[...]

    @pl.when(i == pl.num_programs(0) - 1)
    def _():
        ybf[...] = ybuf[...].astype(BF16)
        cp = pltpu.make_async_copy(ybf, mix_ref.at[pl.ds(t_len, nb)], ysem.at[0])
        cp.start()
        for t in range(t_len // zr):
            zero_copy(t).wait()
        cp.wait()


def _sample_mixers(proj_s, t_len, states, layer, prm):
    c_all, n_all, m_all, s_all, cv_all, k_all, v_all = states
    depth = c_all.shape[0]
    nb = proj_s.shape[0]
    bb = SAMPLE_BB
    assert nb == LANES and nb % bb == 0
    lb = layer * (nb // bb)
    n0, m0 = n_all[layer], m_all[layer]
    mm = jnp.concatenate([m0, m0, jnp.zeros((nb, LANES - 2 * MLSTM_HEADS), F32)], axis=1)
    n0f = n0.reshape(nb, GROUP_WIDTH)
    c0 = c_all.reshape((depth * nb,) + c_all.shape[2:])
    s0 = s_all.reshape((depth * nb,) + s_all.shape[2:])
    cv0 = cv_all.reshape((depth * nb,) + cv_all.shape[2:])
    k0f = k_all.reshape(depth * nb, WINDOW, LANES)
    v0f = v_all.reshape(depth * nb, WINDOW, LANES)
    sq = proj_s[:, S_Q:S_Q + GROUP_WIDTH].reshape(nb, SWA_KV_HEADS, SWA_HEADS // SWA_KV_HEADS, SWA_HEAD_DIM)
    zq = jnp.zeros_like(sq[:, 0])
    qm = jnp.concatenate([jnp.concatenate([sq[:, 0], zq], axis=-1), jnp.concatenate([zq, sq[:, 1]], axis=-1)], axis=1)
    sink_b = jnp.broadcast_to(prm['swa_sinks'].astype(F32)[:, None], (SWA_HEADS, LANES))

    full = lambda shape: pl.BlockSpec(shape, lambda i: (0,) * len(shape))
    rowb = lambda w: pl.BlockSpec((bb, w), lambda i: (i, 0))
    in_specs = [
        full((nb, N_PROJ)), rowb(LANES), full((1, LANES)), rowb(GROUP_WIDTH),
        pl.BlockSpec((bb, MLSTM_HEADS, HEAD_W, HEAD_W), lambda i: (lb + i, 0, 0, 0)),
        pl.BlockSpec((bb, GLA_HEADS, GLA_DK, HEAD_W), lambda i: (lb + i, 0, 0, 0)),
        pl.BlockSpec((bb, CONV_WIDTH - 1, GROUP_WIDTH), lambda i: (lb + i, 0, 0)),
        pl.BlockSpec((bb, WINDOW, LANES), lambda i: (lb + i, 0, 0)),
        pl.BlockSpec((bb, WINDOW, LANES), lambda i: (lb + i, 0, 0)),
        pl.BlockSpec((bb, SWA_HEADS, LANES), lambda i: (i, 0, 0)),
        full((SWA_HEADS, LANES)), full((LANES, 256)), full((1, 256)),
        full((1, GROUP_WIDTH)), full((1, GROUP_WIDTH)),
        full((CONV_WIDTH, GROUP_WIDTH)), full((1, GROUP_WIDTH)), full((1, GROUP_WIDTH)), full((1, GROUP_WIDTH)),
    ]
    out_specs = [
        pl.BlockSpec(memory_space=pl.ANY),
        pl.BlockSpec((bb, MLSTM_HEADS, HEAD_W, HEAD_W), lambda i: (i, 0, 0, 0)),
        rowb(GROUP_WIDTH), rowb(LANES),
        pl.BlockSpec((bb, GLA_HEADS, GLA_DK, HEAD_W), lambda i: (i, 0, 0, 0)),
        pl.BlockSpec((bb, CONV_WIDTH - 1, GROUP_WIDTH), lambda i: (i, 0, 0)),
        pl.BlockSpec((bb, WINDOW, LANES), lambda i: (i, 0, 0)),
        pl.BlockSpec((bb, WINDOW, LANES), lambda i: (i, 0, 0)),
    ]
    out_shape = [
        jax.ShapeDtypeStruct((t_len + nb, D_MODEL), BF16),
        jax.ShapeDtypeStruct(c_all.shape[1:], F32),
        jax.ShapeDtypeStruct((nb, GROUP_WIDTH), F32),
        jax.ShapeDtypeStruct((nb, LANES), F32),
        jax.ShapeDtypeStruct(s_all.shape[1:], F32),
        jax.ShapeDtypeStruct(cv_all.shape[1:], F32),
        jax.ShapeDtypeStruct((nb, WINDOW, LANES), F32),
        jax.ShapeDtypeStruct((nb, WINDOW, LANES), F32),
    ]
    scratch = [
        pltpu.VMEM((_T_ROWS, nb), BF16), pltpu.VMEM((_T_ROWS, LANES), F32),
        pltpu.VMEM((bb, GROUP_WIDTH), F32), pltpu.VMEM((bb, GROUP_WIDTH), F32), pltpu.VMEM((bb, GROUP_WIDTH), F32),
        pltpu.VMEM((bb, GROUP_WIDTH), F32),
        pltpu.VMEM((nb, D_MODEL), F32), pltpu.VMEM((nb, D_MODEL), BF16),
        pltpu.VMEM((_pick_tile(t_len, (512, 256, 128, 64, 16)), D_MODEL), BF16),
        pltpu.SemaphoreType.DMA((1,)), pltpu.SemaphoreType.DMA((1,)),
    ]
    mixed, c1, n1, m1, s1, cv1, k1, v1 = pl.pallas_call(
        _sample_kernel,
        grid=(nb // bb,),
        in_specs=in_specs, out_specs=out_specs, out_shape=out_shape, scratch_shapes=scratch,
        compiler_params=_cparams(("arbitrary",)),
        name="sample_mixers",
    )(proj_s, mm, prm['gate_bias'], n0f, c0, s0, cv0, k0f, v0f, qm, sink_b,
      prm['gla_w_gate_pad'], prm['gla_b_gate'].reshape(1, 256),
      prm['mlstm_norm_g'].reshape(1, -1), prm['gla_norm_g'].reshape(1, -1),
      prm['conv_w'], prm['conv_b'].reshape(1, -1), prm['conv_norm_g'].reshape(1, -1),
      prm['conv_norm_b'].reshape(1, -1))
    new_state = (c1, n1.reshape(n0.shape), m1[:, :MLSTM_HEADS], s1, cv1,
                 k1.reshape(k_all.shape[1:]), v1.reshape(v_all.shape[1:]))
    return mixed, new_state


def _outproj_router_kernel(x_ref, mix_ref, w_ref, g_ref, rw_ref, rb_ref,
                           x1_ref, hn_ref, ri_ref, rf_ref, cnt_ref):
    @pl.when(pl.program_id(0) == 0)
    def _():
        cnt_ref[...] = jnp.zeros_like(cnt_ref)

    counts = _route_rows(x_ref[...], mix_ref[...], w_ref, g_ref, rw_ref, rb_ref, cnt_ref[0:1, :],
                         x1_ref, hn_ref, ri_ref, rf_ref)
    cnt_ref[...] = jnp.broadcast_to(counts, cnt_ref.shape)


def _route_rows(x, mix, w_ref, g_ref, rw_ref, rb_ref, counts, x1_ref, hn_ref, ri_ref, rf_ref):
    tm = x.shape[0]
    x1 = x + jnp.dot(mix, w_ref[...], preferred_element_type=F32)
    x1_ref[...] = x1
    ms = jnp.mean(x1 * x1, axis=-1, keepdims=True)
    hn = x1 * lax.rsqrt(ms + NORM_EPS) * g_ref[...]
    hn_ref[...] = hn
    logits = jnp.dot(hn.astype(BF16), rw_ref[...], preferred_element_type=F32) + rb_ref[...]
    lane = lax.broadcasted_iota(I32, (tm, LANES), 1)
    big = jnp.int32(LANES)
    gl = jnp.where(lane < N_GROUPS, logits, -jnp.inf)
    gmax = jnp.max(gl, axis=-1, keepdims=True)
    g_sel = jnp.min(jnp.where(gl == gmax, lane, big), axis=-1, keepdims=True)
    g_w = 1.0 / jnp.sum(jnp.exp(gl - gmax), axis=-1, keepdims=True)
    e_lane = lane - N_GROUPS
    in_grp = (e_lane >= 0) & (e_lane < N_EXPERTS) & ((e_lane // EXPERTS_PER_GROUP) == g_sel)
    el = jnp.where(in_grp, logits, -jnp.inf)
    m1 = jnp.max(el, axis=-1, keepdims=True)
    i1 = jnp.min(jnp.where(el == m1, lane, big), axis=-1, keepdims=True)
    el2 = jnp.where(lane == i1, -jnp.inf, el)
    m2 = jnp.max(el2, axis=-1, keepdims=True)
    i2 = jnp.min(jnp.where(el2 == m2, lane, big), axis=-1, keepdims=True)
    r = jnp.exp(m2 - m1)
    p1 = 1.0 / (1.0 + r)
    gate1 = g_w * p1
    gate2 = g_w * (r * p1)
    sel1 = lane == i1
    sel2 = lane == i2
    onehot = jnp.where(sel1 | sel2, 1.0, 0.0)
    row = lax.broadcasted_iota(I32, (tm, tm), 0)
    col = lax.broadcasted_iota(I32, (tm, tm), 1)
    strict = jnp.where(col < row, 1.0, 0.0).astype(BF16)
    cum = jnp.dot(strict, onehot.astype(BF16), preferred_element_type=F32) + counts
    rank1 = jnp.sum(jnp.where(sel1, cum, 0.0), axis=-1, keepdims=True).astype(I32)
    rank2 = jnp.sum(jnp.where(sel2, cum, 0.0), axis=-1, keepdims=True).astype(I32)
    ri = jnp.where(lane == 0, i1 - N_GROUPS, jnp.where(lane == 1, i2 - N_GROUPS,
                   jnp.where(lane == 2, rank1, jnp.where(lane == 3, rank2, 0))))
    ri_ref[...] = ri
    rf_ref[...] = jnp.where(lane == 0, gate1, jnp.where(lane == 1, gate2, 0.0))
    return counts + jnp.sum(onehot, axis=0, keepdims=True)


def _outproj_router(x, mixed, w_out_bf16, norm_g, rw_pad, rb_pad):
    n = x.shape[0]
    tm = _pick_tile(n, (320, 256, 128, 64, 16))
    full = lambda shape: pl.BlockSpec(shape, lambda i: (0,) * len(shape))
    rowb = lambda w: pl.BlockSpec((tm, w), lambda i: (i, 0))
    return pl.pallas_call(
        _outproj_router_kernel,
        grid=(n // tm,),
        in_specs=[rowb(D_MODEL), rowb(D_MODEL), full((D_MODEL, D_MODEL)), full((1, D_MODEL)),
                  full((D_MODEL, LANES)), full((1, LANES))],
        out_specs=[rowb(D_MODEL), rowb(D_MODEL), rowb(LANES), rowb(LANES), full((SUBLANES, LANES))],
        out_shape=[
            jax.ShapeDtypeStruct((n, D_MODEL), F32),
            jax.ShapeDtypeStruct((n, D_MODEL), F32),
            jax.ShapeDtypeStruct((n, LANES), I32),
            jax.ShapeDtypeStruct((n, LANES), F32),
            jax.ShapeDtypeStruct((SUBLANES, LANES), F32),
        ],
        compiler_params=_cparams(("arbitrary",)),
        name="outproj_router",
    )(x, mixed, w_out_bf16, norm_g.reshape(1, D_MODEL), rw_pad, rb_pad)


def _dispatch_kernel(pos_ref, zt_ref, hn_ref, xs_ref, zbuf, sem, zsem):
    tm = hn_ref.shape[0]
    tile = zbuf.shape[0]
    base = pl.program_id(0) * tm

    @pl.when(pl.program_id(0) == 0)
    def _():
        zbuf[...] = jnp.zeros_like(zbuf)

        def zero_copy(k):
            row = pl.multiple_of(zt_ref[k] * tile, tile)
            return pltpu.make_async_copy(zbuf, xs_ref.at[pl.ds(row, tile)], zsem.at[0])

        def start_zero(k, carry):
            @pl.when(zt_ref[k] >= 0)
            def _():
                zero_copy(k).start()
            return carry

        def wait_zero(k, carry):
            @pl.when(zt_ref[k] >= 0)
            def _():
                zero_copy(k).wait()
            return carry

        lax.fori_loop(0, zt_ref.shape[0], start_zero, 0)
        lax.fori_loop(0, zt_ref.shape[0], wait_zero, 0)

    def issue(r, carry):
        tok = base + r
        src = hn_ref.at[pl.ds(r, 1)]
        pltpu.make_async_copy(src, xs_ref.at[pl.ds(pos_ref[2 * tok], 1)], sem.at[0]).start(priority=0)
        pltpu.make_async_copy(src, xs_ref.at[pl.ds(pos_ref[2 * tok + 1], 1)], sem.at[1]).start(priority=1)
        return carry

    lax.fori_loop(0, tm, issue, 0, unroll=4)
    pltpu.make_async_copy(hn_ref, xs_ref.at[pl.ds(0, tm)], sem.at[0]).wait()
    pltpu.make_async_copy(hn_ref, xs_ref.at[pl.ds(0, tm)], sem.at[1]).wait()


def _dispatch(pos, zero_tiles, hn, n_rows, tile):
    n = hn.shape[0]
    tm = _pick_tile(n, ROW_DMA_TILES)
    return pl.pallas_call(
        _dispatch_kernel,
        grid_spec=pltpu.PrefetchScalarGridSpec(
            num_scalar_prefetch=2,
            grid=(n // tm,),
            in_specs=[pl.BlockSpec((tm, D_MODEL), lambda i, p, z: (i, 0))],
            out_specs=pl.BlockSpec(memory_space=pl.ANY),
            scratch_shapes=[pltpu.VMEM((tile, D_MODEL), F32), pltpu.SemaphoreType.DMA((2,)),
                            pltpu.SemaphoreType.DMA((1,))],
        ),
        out_shape=jax.ShapeDtypeStruct((n_rows, D_MODEL), F32),
        compiler_params=_cparams(("arbitrary",)),
        name="moe_dispatch",
    )(pos, zero_tiles, hn)


def _expert_kernel(te_ref, nt_ref, x_ref, wg_ref, wu_ref, wd_ref, y_ref):
    del te_ref
    used = pl.program_id(0) < nt_ref[0]

    @pl.when(used)
    def _():
        x = x_ref[...].astype(BF16)
        a = jnp.dot(x, wg_ref[0], preferred_element_type=F32)
        u = jnp.dot(x, wu_ref[0], preferred_element_type=F32)
        hmid = (_silu(a) * u).astype(BF16)
        y_ref[...] = jnp.dot(hmid, wd_ref[0], preferred_element_type=F32)

    @pl.when(jnp.logical_not(used))
    def _():
        y_ref[...] = jnp.zeros_like(y_ref)


def _expert_mlp(tile_expert, n_tiles_used, xs, wg, wu, wd, tile):
    n_tiles = xs.shape[0] // tile

    def row_map(t, te, nt):
        return (jnp.minimum(t, nt[0] - 1), 0)

    def w_map(t, te, nt):
        return (te[jnp.minimum(t, nt[0] - 1)], 0, 0)

    return pl.pallas_call(
        _expert_kernel,
        grid_spec=pltpu.PrefetchScalarGridSpec(
            num_scalar_prefetch=2,
            grid=(n_tiles,),
            in_specs=[
                pl.BlockSpec((tile, D_MODEL), row_map),
                pl.BlockSpec((1, D_MODEL, D_EXPERT), w_map),
                pl.BlockSpec((1, D_MODEL, D_EXPERT), w_map),
                pl.BlockSpec((1, D_EXPERT, D_MODEL), w_map),
            ],
            out_specs=pl.BlockSpec((tile, D_MODEL), lambda t, te, nt: (t, 0)),
        ),
        out_shape=jax.ShapeDtypeStruct(xs.shape, F32),
        compiler_params=_cparams(("arbitrary",)),
        name="expert_mlp",
    )(tile_expert, n_tiles_used, xs, wg, wu, wd)


def _combine_kernel(pos_ref, x1_ref, rf_ref, fg_ref, ys_ref, o_ref, buf_a, buf_b, sem, *, final_norm, row0):
    tm = x1_ref.shape[0]
    base = row0 + pl.program_id(0) * tm

    def issue(r, carry):
        tok = base + r
        pltpu.make_async_copy(ys_ref.at[pl.ds(pos_ref[2 * tok], 1)], buf_a.at[pl.ds(r, 1)],
                              sem.at[0]).start(priority=0)
        pltpu.make_async_copy(ys_ref.at[pl.ds(pos_ref[2 * tok + 1], 1)], buf_b.at[pl.ds(r, 1)],
                              sem.at[1]).start(priority=1)
        return carry

    lax.fori_loop(0, tm, issue, 0, unroll=4)
    pltpu.make_async_copy(ys_ref.at[pl.ds(0, tm)], buf_a, sem.at[0]).wait()
    pltpu.make_async_copy(ys_ref.at[pl.ds(0, tm)], buf_b, sem.at[1]).wait()
    rf = rf_ref[...]
    x2 = x1_ref[...] + rf[:, 0:1] * buf_a[...] + rf[:, 1:2] * buf_b[...]
    if final_norm:
        ms = jnp.mean(x2 * x2, axis=-1, keepdims=True)
        x2 = x2 * lax.rsqrt(ms + NORM_EPS) * fg_ref[...]
    o_ref[...] = x2


def _combine(pos, x1, rf, ys, final_g, final_norm, row0=0, n_rows=None):
    n = x1.shape[0]
    n_rows = n if n_rows is None else n_rows
    tm = _pick_tile(n_rows, ROW_DMA_TILES)
    assert row0 % tm == 0
    b0 = row0 // tm
    return pl.pallas_call(
        functools.partial(_combine_kernel, final_norm=final_norm, row0=row0),
        grid_spec=pltpu.PrefetchScalarGridSpec(
            num_scalar_prefetch=1,
            grid=(n_rows // tm,),
            in_specs=[
                pl.BlockSpec((tm, D_MODEL), lambda i, p: (b0 + i, 0)),
                pl.BlockSpec((tm, LANES), lambda i, p: (b0 + i, 0)),
                pl.BlockSpec((1, D_MODEL), lambda i, p: (0, 0)),
                pl.BlockSpec(memory_space=pl.ANY),
            ],
            out_specs=pl.BlockSpec((tm, D_MODEL), lambda i, p: (i, 0)),
            scratch_shapes=[pltpu.VMEM((tm, D_MODEL), F32), pltpu.VMEM((tm, D_MODEL), F32),
                            pltpu.SemaphoreType.DMA((2,))],
        ),
        out_shape=jax.ShapeDtypeStruct((n_rows, D_MODEL), F32),
        compiler_params=_cparams(("arbitrary",)),
        name="moe_combine",
    )(pos, x1, rf, final_g.reshape(1, D_MODEL), ys)


W_IN_COL_BLOCK = LANES


def _pad_w_in_rows(w_ref, o_ref):
    cols = o_ref.shape[1]
    at = 0
    for lo, hi, dst in sorted(_W_IN_SEGMENTS, key=lambda s: s[2]):
        if dst > at:
            o_ref[at:dst, :] = jnp.zeros((dst - at, cols), o_ref.dtype)
        o_ref[dst:dst + (hi - lo), :] = w_ref[0, lo:hi, :].astype(o_ref.dtype)
        at = dst + (hi - lo)
    if at < N_PROJ:
        o_ref[at:N_PROJ, :] = jnp.zeros((N_PROJ - at, cols), o_ref.dtype)


def _w_in_side(w_in_t, layer, n_steps):
    depth, n_in, d = w_in_t.shape
    n_blocks = min(n_steps, d // W_IN_COL_BLOCK)
    assert d % (n_blocks * LANES) == 0
    cols = d // n_blocks
    blk = lambda i: jnp.minimum(i, n_blocks - 1)
    return _SideCast(
        operand=w_in_t,
        in_spec=pl.BlockSpec((1, n_in, cols), lambda i: (layer, 0, blk(i))),
        out_spec=pl.BlockSpec((N_PROJ, cols), lambda i: (0, blk(i))),
        out_shape=jax.ShapeDtypeStruct((N_PROJ, d), BF16),
        shape=(N_PROJ, d),
    )


def _pad_w_in(w_in_t, layer):
    n_steps = w_in_t.shape[2] // W_IN_COL_BLOCK
    side = _w_in_side(w_in_t, layer, n_steps)
    return pl.pallas_call(
        _pad_w_in_rows,
        grid=(n_steps,),
        in_specs=[side.in_spec], out_specs=side.out_spec, out_shape=side.out_shape,
        compiler_params=_cparams(("arbitrary",)),
        name="pad_w_in",
    )(side.operand)


def _moe(x1, hn, ri, rf, counts, experts, final_g, final_norm, tile, t_len):
    n = x1.shape[0]
    n_tiles = -(-2 * n // tile) + N_EXPERTS
    cnt = counts[0, N_GROUPS:N_GROUPS + N_EXPERTS].astype(I32)
    tiles_per = (cnt + tile - 1) // tile
    tile_end = jnp.cumsum(tiles_per)
    row_off = (tile_end - tiles_per) * tile
    pos = (row_off[ri[:, 0:2]] + ri[:, 2:4]).reshape(2 * n)
    tile_ids = jnp.arange(n_tiles, dtype=I32)
    tile_expert = jnp.minimum(jnp.sum((tile_ids[:, None] >= tile_end[None, :]).astype(I32), axis=1), N_EXPERTS - 1)
    n_used = tile_end[N_EXPERTS - 1:N_EXPERTS].astype(I32)
    last_tile = jnp.where(tiles_per > 0, tile_end - 1, -1).astype(I32)
    zero_tiles = jnp.concatenate([last_tile, jnp.where(tile_ids >= n_used[0], tile_ids, -1)])
    xs = _dispatch(pos, zero_tiles, hn, n_tiles * tile, tile)
    ys = _expert_mlp(tile_expert, n_used, xs, *experts, tile)
    if final_norm:
        return (_combine(pos, x1, rf, ys, final_g, True, 0, t_len),
                _combine(pos, x1, rf, ys, final_g, True, t_len, n - t_len))
    return _combine(pos, x1, rf, ys, final_g, False)


def _layer(x, t_len, states, layer, prm, w_in_pad, final_g, final_norm, moe_tile):
    proj = _in_projection(x, prm['norm_mix_g'], w_in_pad)
    mixed, new_s = _sample_mixers(proj[t_len:], t_len, states, layer, prm)
    e_shape = prm['expert_w_gate_all'].shape[1:]
    mixed, cn, m_p, wg = _mlstm_prompt(proj, mixed, t_len, prm['gate_bias'], prm['mlstm_norm_g'],
                                       prm['expert_w_gate_all'], layer)
    mixed, conv_tail, wu = _conv_prompt(proj, mixed, t_len, prm['conv_w'], prm['conv_b'], prm['conv_norm_g'],
                                        prm['conv_norm_b'], prm['expert_w_up_all'], layer)
    mixed, sp, wd = _gla_prompt(proj, mixed, t_len, prm['gla_w_gate_pad'], prm['gla_b_gate'], prm['gla_norm_g'],
                                prm['expert_w_down_all'], layer)
    mixed, next_w_in_pad = _swa_prompt(proj, mixed, t_len, prm['swa_sinks'], prm['w_in_all'],
                                       None if final_norm else layer + 1)
    experts = (wg.reshape(e_shape), wu.reshape(e_shape), wd.reshape(prm['expert_w_down_all'].shape[1:]))
    x1, hn, ri, rf, counts = _outproj_router(x, mixed, prm['w_out'], prm['norm_ffn_g'], prm['router_w'], prm['router_b'])
    x2 = _moe(x1, hn, ri, rf, counts, experts, final_g, final_norm, moe_tile, t_len)
    p_c = cn[None, :, :, :HEAD_W]
    p_n = cn[None, :, :, HEAD_W]
    p_m = m_p[None, :MLSTM_HEADS, 0]
    p_s = jnp.stack([sp[0, :GLA_DK, :HEAD_W], sp[0, GLA_DK:, HEAD_W:],
                     sp[1, :GLA_DK, :HEAD_W], sp[1, GLA_DK:, HEAD_W:]])[None]
    p_conv = conv_tail[None, CONV_HALO - (CONV_WIDTH - 1):]
    p_k = proj[t_len - WINDOW:t_len, S_K:S_K + LANES].reshape(1, WINDOW, SWA_KV_HEADS, SWA_HEAD_DIM)
    p_v = proj[t_len - WINDOW:t_len, S_V:S_V + LANES].reshape(1, WINDOW, SWA_KV_HEADS, SWA_HEAD_DIM)
    return x2, (p_c, p_n, p_m, p_s, p_conv, p_k, p_v), new_s, next_w_in_pad


def _forward(x_prompt, x_sample, states, layer_params, final_norm_g, moe_tile=MOE_TILE):
    t_len = x_prompt.shape[1]
    x = jnp.concatenate([x_prompt[0], x_sample[:, 0]], axis=0)
    new_p, new_s = [], []
    depth = len(layer_params)
    w_in_pad = _pad_w_in(layer_params[0]['w_in_all'], 0)
    for l, prm in enumerate(layer_params):
        x, sp, ss, w_in_pad = _layer(x, t_len, states, l, prm, w_in_pad, final_norm_g, l == depth - 1, moe_tile)
        new_p.append(sp)
        new_s.append(ss)
    y_prompt, y_sample = x
    y_prompt = y_prompt[None]
    y_sample = y_sample[:, None]
    p_states = [jnp.stack(parts) for parts in zip(*new_p)]
    s_states = [jnp.stack(parts) for parts in zip(*new_s)]
    return (y_prompt, y_sample, *p_states, *s_states)


def _prep_layer_params(l, norm_mix_g, w_in, mlstm_b_i, mlstm_b_f, mlstm_norm_g, conv_w, conv_b, conv_norm_g,
                       conv_norm_b, gla_w_gate, gla_b_gate, gla_norm_g, swa_sinks, w_out, norm_ffn_g,
                       router_group_w, router_group_b, router_expert_w, router_expert_b, expert_w_gate,
                       expert_w_up, expert_w_down):
    gate_bias = jnp.concatenate([mlstm_b_i[l], mlstm_b_f[l], jnp.zeros((LANES - 2 * MLSTM_HEADS,), F32)])
    rw = jnp.concatenate([router_group_w[l], router_expert_w[l],
                          jnp.zeros((D_MODEL, LANES - N_GROUPS - N_EXPERTS), F32)], axis=1)
    rb = jnp.concatenate([router_group_b[l], router_expert_b[l],
                          jnp.zeros((LANES - N_GROUPS - N_EXPERTS,), F32)])
    return {
        'norm_mix_g': norm_mix_g[l],
        'w_in_all': jnp.swapaxes(w_in, 1, 2),
        'gate_bias': gate_bias.reshape(1, LANES),
        'mlstm_norm_g': mlstm_norm_g[l],
        'conv_w': conv_w[l], 'conv_b': conv_b[l], 'conv_norm_g': conv_norm_g[l], 'conv_norm_b': conv_norm_b[l],
        'gla_w_gate_pad': jnp.concatenate(
            [gla_w_gate[l], jnp.zeros((LANES - GLA_LOWRANK, GLA_HEADS * GLA_DK), F32)], axis=0).astype(BF16),
        'gla_b_gate': gla_b_gate[l], 'gla_norm_g': gla_norm_g[l],
        'swa_sinks': swa_sinks[l],
        'w_out': w_out[l].astype(BF16),
        'norm_ffn_g': norm_ffn_g[l],
        'router_w': rw.astype(BF16), 'router_b': rb.reshape(1, LANES),
        'expert_w_gate_all': expert_w_gate, 'expert_w_up_all': expert_w_up, 'expert_w_down_all': expert_w_down,
    }


def kernel(x_prompt, x_sample, state_mlstm_C, state_mlstm_n, state_mlstm_m, state_gla_S, cache_conv, cache_swa_k, cache_swa_v, norm_mix_g, w_in, mlstm_b_i, mlstm_b_f, mlstm_norm_g, conv_w, conv_b, conv_norm_g, conv_norm_b, gla_w_gate, gla_b_gate, gla_norm_g, swa_sinks, w_out, norm_ffn_g, router_group_w, router_group_b, router_expert_w, router_expert_b, expert_w_gate, expert_w_up, expert_w_down, final_norm_g):
    depth = w_in.shape[0]
    weights = (norm_mix_g, w_in, mlstm_b_i, mlstm_b_f, mlstm_norm_g, conv_w, conv_b, conv_norm_g, conv_norm_b,
               gla_w_gate, gla_b_gate, gla_norm_g, swa_sinks, w_out, norm_ffn_g, router_group_w, router_group_b,
               router_expert_w, router_expert_b, expert_w_gate, expert_w_up, expert_w_down)
    layer_params = [_prep_layer_params(l, *weights) for l in range(depth)]
    states = (state_mlstm_C, state_mlstm_n, state_mlstm_m, state_gla_S, cache_conv, cache_swa_k, cache_swa_v)
    return _forward(x_prompt, x_sample, states, layer_params, final_norm_g)
```

```python
import functools
from typing import NamedTuple

import jax
import jax.numpy as jnp
from jax import lax
from jax.experimental import pallas as pl
from jax.experimental.pallas import tpu as pltpu

F32 = jnp.float32
BF16 = jnp.bfloat16
I32 = jnp.int32

D_MODEL = 2048
GROUP_WIDTH = 512
HEAD_W = 128
MLSTM_HEADS = 4
GLA_HEADS = 4
GLA_DK = 64
GLA_LOWRANK = 16
GLA_TAU = 16.0
CONV_WIDTH = 31
SWA_HEADS = 8
SWA_KV_HEADS = 2
SWA_HEAD_DIM = 64
WINDOW = 128
N_GROUPS = 4
EXPERTS_PER_GROUP = 4
N_EXPERTS = 16
D_EXPERT = 1024
NORM_EPS = 1e-6
LANES = 128
SUBLANES = 8

A_Q, A_K, A_V, A_O = 0, 512, 1024, 1536
C_A, C_G = 2048, 2560
G_V, G_R = 3072, 3584
S_Q = 4096
G_Q, G_K = 4608, 4864
A_G, G_LR, S_K, S_V = 5120, 5248, 5376, 5504
N_PROJ = 5632
MIX_COL_MLSTM, MIX_COL_CONV, MIX_COL_GLA, MIX_COL_SWA = 0, 1, 2, 3
_W_IN_SEGMENTS = (
    (0, 512, A_Q), (512, 1024, A_K), (1024, 1536, A_V), (1536, 2048, A_O),
    (2048, 2056, A_G),
    (2056, 2568, C_A), (2568, 3080, C_G),
    (3080, 3336, G_Q), (3336, 3592, G_K), (3592, 4104, G_V), (4104, 4616, G_R),
    (4616, 4632, G_LR),
    (4632, 5144, S_Q), (5144, 5272, S_K), (5272, 5400, S_V),
)

MLSTM_CHUNK = 128
MLSTM_STEP_ROWS = 256
GLA_CHUNK = 64
GLA_STEP_ROWS = 256
GLA_SUB = 16
SWA_STEP_ROWS = 256
CONV_TILE = 256
CONV_ROWS = 64
CONV_HALO = 32
SAMPLE_BB = 8
MOE_TILE = 256
ROW_DMA_TILES = (640, 512, 256, 128, 64, 8)
VMEM_LIMIT = 56 * 1024 * 1024


def _cparams(sem, vmem=VMEM_LIMIT):
    return pltpu.CompilerParams(dimension_semantics=sem, vmem_limit_bytes=vmem)


def _log_sigmoid(x):
    return jnp.minimum(x, 0.0) - jnp.log1p(jnp.exp(-jnp.abs(x)))


def _sigmoid(x):
    return 1.0 / (1.0 + jnp.exp(-x))


def _silu(x):
    return x * _sigmoid(x)


def _masked_row_sums(mask, x):
    m = jnp.where(mask, 1.0, 0.0).astype(BF16)
    hi = x.astype(BF16)
    r1 = x - hi.astype(F32)
    mid = r1.astype(BF16)
    lo = (r1 - mid.astype(F32)).astype(BF16)
    out = jnp.dot(m, hi, preferred_element_type=F32)
    out = out + jnp.dot(m, mid, preferred_element_type=F32)
    return out + jnp.dot(m, lo, preferred_element_type=F32)


def _pick_tile(n, candidates):
    for c in candidates:
        if n % c == 0:
            return c
    raise ValueError(f"no tile for {n} in {candidates}")


class _SideCast(NamedTuple):
    operand: jax.Array
    in_spec: pl.BlockSpec
    out_spec: pl.BlockSpec
    out_shape: jax.ShapeDtypeStruct
    shape: tuple


def _side_cast(w_all, layer, n_steps):
    depth, n_e, k, f = w_all.shape
    rows = n_e * k
    assert rows % n_steps == 0, (rows, n_steps)
    tr = rows // n_steps
    return _SideCast(
        operand=w_all.reshape(depth * rows, f),
        in_spec=pl.BlockSpec((tr, f), lambda i: (layer * n_steps + i, 0)),
        out_spec=pl.BlockSpec((tr, f), lambda i: (i, 0)),
        out_shape=jax.ShapeDtypeStruct((rows, f), BF16),
        shape=(n_e, k, f),
    )


def _proj_kernel(x_ref, g_ref, w_ref, o_ref, hn_ref):
    @pl.when(pl.program_id(1) == 0)
    def _():
        x = x_ref[...]
        ms = jnp.mean(x * x, axis=-1, keepdims=True)
        hn_ref[...] = (x * lax.rsqrt(ms + NORM_EPS) * g_ref[...]).astype(BF16)

    o_ref[...] = lax.dot_general(hn_ref[...], w_ref[...], (((1,), (1,)), ((), ())), preferred_element_type=F32)


def _in_projection(x, g, w_bf16):
    n = x.shape[0]
    tm = _pick_tile(n, (832, 640, 512, 256, 128, 64, 8))
    tn = N_PROJ // 4
    return pl.pallas_call(
        _proj_kernel,
        grid=(n // tm, N_PROJ // tn),
        in_specs=[
            pl.BlockSpec((tm, D_MODEL), lambda i, j: (i, 0)),
            pl.BlockSpec((1, D_MODEL), lambda i, j: (0, 0)),
            pl.BlockSpec((tn, D_MODEL), lambda i, j: (j, 0)),
        ],
        out_specs=pl.BlockSpec((tm, tn), lambda i, j: (i, j)),
        out_shape=jax.ShapeDtypeStruct((n, N_PROJ), F32),
        scratch_shapes=[pltpu.VMEM((tm, D_MODEL), BF16)],
        compiler_params=_cparams(("parallel", "arbitrary")),
        name="in_projection",
    )(x, g.reshape(1, D_MODEL), w_bf16)


def _mlstm_prompt_kernel(q_ref, k_ref, v_ref, o_ref, gt_ref, bias_ref, ng_ref, mixed_in_ref, wc_in_ref,
                         y_ref, cn_ref, m_ref, wc_out_ref):
    del mixed_in_ref
    wc_out_ref[...] = wc_in_ref[...].astype(wc_out_ref.dtype)
    rows_step = q_ref.shape[0]
    L = min(MLSTM_CHUNK, rows_step)
    n_chunks = rows_step // L

    @pl.when(pl.program_id(0) == 0)
    def _():
        cn_ref[...] = jnp.zeros_like(cn_ref)
        m_ref[...] = jnp.zeros_like(m_ref)

    pre = gt_ref[...] + bias_ref[...]
    lf = _log_sigmoid(pre)
    row = lax.broadcasted_iota(I32, (rows_step, rows_step), 0)
    col = lax.broadcasted_iota(I32, (rows_step, rows_step), 1)
    cum = ((row // L) == (col // L)) & (col <= row)
    b_all = _masked_row_sums(cum, lf)
    pre_t = pre.T
    b_t = b_all.T
    trow = lax.broadcasted_iota(I32, (L, L), 0)
    tcol = lax.broadcasted_iota(I32, (L, L), 1)
    tri = tcol <= trow
    lane = lax.broadcasted_iota(I32, (L, HEAD_W), 1)
    ones_col = (lane == 0).astype(BF16)
    for h in range(MLSTM_HEADS):
        sl = slice(HEAD_W * h, HEAD_W * (h + 1))
        m_prev = m_ref[h:h + 1, 0:1]
        cn = cn_ref[h]
        for c in range(n_chunks):
            cs = slice(L * c, L * (c + 1))
            q = q_ref[cs, sl]
            k = k_ref[cs, sl] * (HEAD_W ** -0.5)
            v = v_ref[cs, sl]
            b_col = b_all[cs, 4 + h:5 + h]
            i_col = pre[cs, h:h + 1]
            b_row = b_t[4 + h:5 + h, cs]
            i_row = pre_t[h:h + 1, cs]
            log_d = jnp.where(tri, b_col - b_row + i_row, -jnp.inf)
            log_inter = b_col + m_prev
            m_t = jnp.maximum(log_inter, jnp.max(log_d, axis=-1, keepdims=True))
            d_mat = jnp.exp(log_d - m_t)
            g_inter = jnp.exp(log_inter - m_t)
            qb = q.astype(BF16)
            kb = k.astype(BF16)
            s = lax.dot_general(qb, kb, (((1,), (1,)), ((), ())), preferred_element_type=F32)
            w = (s * d_mat).astype(BF16)
            v1 = jnp.concatenate([v.astype(BF16), ones_col], axis=1)
            nd = g_inter * jnp.dot(qb, cn.astype(BF16), preferred_element_type=F32)
            nd = nd + jnp.dot(w, v1, preferred_element_type=F32)
            num = nd[:, :HEAD_W]
            den = nd[:, HEAD_W:HEAD_W + 1]
            hh = num / jnp.maximum(jnp.abs(den), jnp.exp(-m_t))
            hh = _sigmoid(o_ref[cs, sl]) * hh
            hh = hh * lax.rsqrt(jnp.mean(hh * hh, axis=-1, keepdims=True) + NORM_EPS) * ng_ref[:, sl]
            y_ref[cs, sl] = hh.astype(y_ref.dtype)
            m_last = m_t[L - 1:L, :]
            b_last = b_col[L - 1:L, :]
            g_state = jnp.exp(b_last + m_prev - m_last)
            w_k = jnp.exp(b_last - b_col + i_col - m_last)
            kw = (k * w_k).astype(BF16)
            upd = lax.dot_general(kw, v1, (((0,), (0,)), ((), ())), preferred_element_type=F32)
            cn = g_state * cn + upd
            m_prev = m_last
        cn_ref[h] = cn
        m_ref[h:h + 1, :] = jnp.broadcast_to(m_prev, (1, LANES))


def _mlstm_prompt(proj, mixed, t_len, bias_row, norm_g, w_cast, layer):
    L = _pick_tile(t_len, (MLSTM_STEP_ROWS, MLSTM_CHUNK, 64, 32, 16, 8))
    side = _side_cast(w_cast, layer, t_len // L)

    def col(off):
        return pl.BlockSpec((L, GROUP_WIDTH), lambda i, o=off: (i, o // GROUP_WIDTH))

    return pl.pallas_call(
        _mlstm_prompt_kernel,
        grid=(t_len // L,),
        in_specs=[
            col(A_Q), col(A_K), col(A_V), col(A_O),
            pl.BlockSpec((L, LANES), lambda i: (i, A_G // LANES)),
            pl.BlockSpec((1, LANES), lambda i: (0, 0)),
            pl.BlockSpec((1, GROUP_WIDTH), lambda i: (0, 0)),
            pl.BlockSpec(memory_space=pl.ANY),
            side.in_spec,
        ],
        out_specs=[
            pl.BlockSpec((L, GROUP_WIDTH), lambda i: (i, MIX_COL_MLSTM)),
            pl.BlockSpec((MLSTM_HEADS, HEAD_W, 2 * HEAD_W), lambda i: (0, 0, 0)),
            pl.BlockSpec((SUBLANES, LANES), lambda i: (0, 0)),
            side.out_spec,
        ],
        out_shape=[
            jax.ShapeDtypeStruct(mixed.shape, mixed.dtype),
            jax.ShapeDtypeStruct((MLSTM_HEADS, HEAD_W, 2 * HEAD_W), F32),
            jax.ShapeDtypeStruct((SUBLANES, LANES), F32),
            side.out_shape,
        ],
        input_output_aliases={7: 0},
        compiler_params=_cparams(("arbitrary",)),
        name="mlstm_prompt",
    )(proj, proj, proj, proj, proj, bias_row, norm_g.reshape(1, GROUP_WIDTH), mixed, side.operand)


def _gla_prompt_kernel(q_ref, k_ref, v_ref, r_ref, lr_ref, wg_ref, bg_ref, ng_ref, mixed_in_ref, wc_in_ref,
                       y_ref, sp_ref, wc_out_ref):
    del mixed_in_ref
    wc_out_ref[...] = wc_in_ref[...].astype(wc_out_ref.dtype)
    rows_step = q_ref.shape[0]
    L = min(GLA_CHUNK, rows_step)
    n_chunks = rows_step // L
    n_sub = L // GLA_SUB

    @pl.when(pl.program_id(0) == 0)
    def _():
        sp_ref[...] = jnp.zeros_like(sp_ref)

    gate_pre = jnp.dot(lr_ref[...].astype(BF16), wg_ref[...], preferred_element_type=F32) + bg_ref[...]
    log_a = _log_sigmoid(gate_pre) * (1.0 / GLA_TAU)
    row = lax.broadcasted_iota(I32, (rows_step, rows_step), 0)
    col = lax.broadcasted_iota(I32, (rows_step, rows_step), 1)
    tri = ((row // L) == (col // L)) & (col <= row)
    b = _masked_row_sums(tri, log_a)
    q = q_ref[...] * (GLA_DK ** -0.5)
    k = k_ref[...]
    b_last = jnp.concatenate(
        [jnp.broadcast_to(b[L * c + L - 1:L * c + L, :], (L, b.shape[1])) for c in range(n_chunks)], axis=0)
    q_in = q * jnp.exp(b)
    k_dec = k * jnp.exp(b_last - b)
    lane16 = lax.broadcasted_iota(I32, (GLA_SUB, LANES), 1)
    lo16 = lane16 < GLA_DK
    srow = lax.broadcasted_iota(I32, (LANES, 2 * HEAD_W), 0)
    scol = lax.broadcasted_iota(I32, (LANES, 2 * HEAD_W), 1)
    block_diag = (srow < GLA_DK) == (scol < HEAD_W)
    for p in range(2):
        pls = slice(LANES * p, LANES * (p + 1))
        vp = v_ref[:, 2 * HEAD_W * p:2 * HEAD_W * (p + 1)].astype(BF16)
        b_t = b[:, pls].T
        sp = sp_ref[p]
        states = []
        for c in range(n_chunks):
            cs = slice(L * c, L * (c + 1))
            states.append(sp)
            dec_col = jnp.exp(b_t[:, L * c + L - 1:L * c + L])
            upd = lax.dot_general(k_dec[cs, pls].astype(BF16), vp[cs], (((0,), (0,)), ((), ())),
                                  preferred_element_type=F32)
            sp = jnp.where(block_diag, dec_col * sp + upd, 0.0)
        sp_ref[p] = sp
        outs = []
        for c in range(n_chunks):
            c0 = L * c
            o_inter = jnp.dot(q_in[c0:c0 + L, pls].astype(BF16), states[c].astype(BF16),
                              preferred_element_type=F32)
            rows = []
            for blk in range(n_sub):
                r0 = GLA_SUB * blk
                n = GLA_SUB * (blk + 1)
                qrows = slice(c0 + r0, c0 + r0 + GLA_SUB)
                krows = slice(c0, c0 + n)
                if blk == 0:
                    qs = q[qrows, pls] * jnp.exp(b[qrows, pls])
                    ks = k[krows, pls] * jnp.exp(-b[krows, pls])
                else:
                    anchor = b[c0 + r0 - 1:c0 + r0, pls]
                    qs = q[qrows, pls] * jnp.exp(b[qrows, pls] - anchor)
                    ks = k[krows, pls] * jnp.exp(anchor - b[krows, pls])
                qs2 = jnp.concatenate([jnp.where(lo16, qs, 0.0), jnp.where(lo16, 0.0, qs)], axis=0)
                att = lax.dot_general(qs2.astype(BF16), ks.astype(BF16), (((1,), (1,)), ((), ())),
                                      preferred_element_type=F32)
                trow = lax.broadcasted_iota(I32, (2 * GLA_SUB, n), 0)
                tcol = lax.broadcasted_iota(I32, (2 * GLA_SUB, n), 1)
                t_idx = r0 + jnp.where(trow >= GLA_SUB, trow - GLA_SUB, trow)
                att = jnp.where(tcol <= t_idx, att, 0.0)
                o2 = jnp.dot(att.astype(BF16), vp[krows], preferred_element_type=F32)
                rows.append(jnp.concatenate([o2[:GLA_SUB, :HEAD_W], o2[GLA_SUB:, HEAD_W:]], axis=1))
            outs.append(o_inter + jnp.concatenate(rows, axis=0))
        o = jnp.concatenate(outs, axis=0)
        for hh in range(2):
            head = 2 * p + hh
            hs = slice(HEAD_W * head, HEAD_W * (head + 1))
            oh = o[:, HEAD_W * hh:HEAD_W * (hh + 1)]
            oh = oh * lax.rsqrt(jnp.mean(oh * oh, axis=-1, keepdims=True) + NORM_EPS) * ng_ref[:, hs]
            y_ref[:, hs] = (oh * _silu(r_ref[:, hs])).astype(y_ref.dtype)


def _gla_prompt(proj, mixed, t_len, w_gate_pad, b_gate, norm_g, w_cast, layer):
    L = _pick_tile(t_len, (GLA_STEP_ROWS, GLA_CHUNK))
    side = _side_cast(w_cast, layer, t_len // L)
    return pl.pallas_call(
        _gla_prompt_kernel,
        grid=(t_len // L,),
        in_specs=[
            pl.BlockSpec((L, 256), lambda i: (i, G_Q // 256)),
            pl.BlockSpec((L, 256), lambda i: (i, G_K // 256)),
            pl.BlockSpec((L, GROUP_WIDTH), lambda i: (i, G_V // GROUP_WIDTH)),
            pl.BlockSpec((L, GROUP_WIDTH), lambda i: (i, G_R // GROUP_WIDTH)),
            pl.BlockSpec((L, LANES), lambda i: (i, G_LR // LANES)),
            pl.BlockSpec((LANES, 256), lambda i: (0, 0)),
            pl.BlockSpec((1, 256), lambda i: (0, 0)),
            pl.BlockSpec((1, GROUP_WIDTH), lambda i: (0, 0)),
            pl.BlockSpec(memory_space=pl.ANY),
            side.in_spec,
        ],
        out_specs=[
            pl.BlockSpec((L, GROUP_WIDTH), lambda i: (i, MIX_COL_GLA)),
            pl.BlockSpec((2, LANES, 2 * HEAD_W), lambda i: (0, 0, 0)),
            side.out_spec,
        ],
        out_shape=[
            jax.ShapeDtypeStruct(mixed.shape, mixed.dtype),
            jax.ShapeDtypeStruct((2, LANES, 2 * HEAD_W), F32),
            side.out_shape,
        ],
        input_output_aliases={8: 0},
        compiler_params=_cparams(("arbitrary",)),
        name="gla_prompt",
    )(proj, proj, proj, proj, proj, w_gate_pad, b_gate.reshape(1, 256), norm_g.reshape(1, GROUP_WIDTH), mixed,
      side.operand)


def _conv_norm_act(y, g_ref, be_ref):
    mu = jnp.mean(y, axis=-1, keepdims=True)
    yc = y - mu
    var = jnp.mean(yc * yc, axis=-1, keepdims=True)
    return _silu(yc * lax.rsqrt(var + NORM_EPS) * g_ref[...] + be_ref[...])


def _conv_prompt_kernel(ua_ref, ug_ref, ha_ref, hg_ref, w_ref, b_ref, g_ref, be_ref, mixed_in_ref, wc_in_ref,
                        y_ref, tail_ref, wc_out_ref, buf_ref, sh_ref):
    del mixed_in_ref
    wc_out_ref[...] = wc_in_ref[...].astype(wc_out_ref.dtype)
    tt = ua_ref.shape[0]
    span = tt + CONV_HALO
    halo = ha_ref[...] * _sigmoid(hg_ref[...])
    buf_ref[0:CONV_HALO, :] = jnp.where(pl.program_id(0) > 0, halo, 0.0)
    buf_ref[CONV_HALO:span, :] = ua_ref[...] * _sigmoid(ug_ref[...])
    buf_ref[span:span + SUBLANES, :] = jnp.zeros((SUBLANES, GROUP_WIDTH), F32)
    for k in range(1, SUBLANES):
        sh_ref[k] = buf_ref[k:k + span, :]
    base = CONV_HALO - (CONV_WIDTH - 1)
    for r in range(tt // CONV_ROWS):
        acc = jnp.zeros((CONV_ROWS, GROUP_WIDTH), F32)
        for j in range(CONV_WIDTH):
            s0 = r * CONV_ROWS + base + j
            k = s0 % SUBLANES
            a0 = s0 - k
            win = buf_ref[a0:a0 + CONV_ROWS, :] if k == 0 else sh_ref[k, a0:a0 + CONV_ROWS, :]
            acc = acc + w_ref[j:j + 1, :] * win
        y = _conv_norm_act(acc + b_ref[...], g_ref, be_ref)
        y_ref[r * CONV_ROWS:(r + 1) * CONV_ROWS, :] = y.astype(y_ref.dtype)
    tail_ref[...] = buf_ref[tt:span, :]


def _conv_prompt(proj, mixed, t_len, w, b, g, beta, w_cast, layer):
    tt = _pick_tile(t_len, (CONV_TILE, 128, 64))
    ratio = tt // CONV_HALO
    side = _side_cast(w_cast, layer, t_len // tt)
    vec = lambda: pl.BlockSpec((1, GROUP_WIDTH), lambda i: (0, 0))
    return pl.pallas_call(
        _conv_prompt_kernel,
        grid=(t_len // tt,),
        in_specs=[
            pl.BlockSpec((tt, GROUP_WIDTH), lambda i: (i, C_A // GROUP_WIDTH)),
            pl.BlockSpec((tt, GROUP_WIDTH), lambda i: (i, C_G // GROUP_WIDTH)),
            pl.BlockSpec((CONV_HALO, GROUP_WIDTH), lambda i: (jnp.maximum(i * ratio - 1, 0), C_A // GROUP_WIDTH)),
            pl.BlockSpec((CONV_HALO, GROUP_WIDTH), lambda i: (jnp.maximum(i * ratio - 1, 0), C_G // GROUP_WIDTH)),
            pl.BlockSpec((CONV_WIDTH, GROUP_WIDTH), lambda i: (0, 0)),
            vec(), vec(), vec(),
            pl.BlockSpec(memory_space=pl.ANY),
            side.in_spec,
        ],
        out_specs=[
            pl.BlockSpec((tt, GROUP_WIDTH), lambda i: (i, MIX_COL_CONV)),
            pl.BlockSpec((CONV_HALO, GROUP_WIDTH), lambda i: (0, 0)),
            side.out_spec,
        ],
        out_shape=[
            jax.ShapeDtypeStruct(mixed.shape, mixed.dtype),
            jax.ShapeDtypeStruct((CONV_HALO, GROUP_WIDTH), F32),
            side.out_shape,
        ],
        scratch_shapes=[pltpu.VMEM((tt + CONV_HALO + SUBLANES, GROUP_WIDTH), F32),
                        pltpu.VMEM((SUBLANES, tt + CONV_HALO, GROUP_WIDTH), F32)],
        input_output_aliases={8: 0},
        compiler_params=_cparams(("arbitrary",)),
        name="conv_prompt",
    )(proj, proj, proj, proj, w, b.reshape(1, -1), g.reshape(1, -1), beta.reshape(1, -1), mixed, side.operand)


def _swa_prompt_kernel(sink_ref, q_ref, kc_ref, vc_ref, kp_ref, vp_ref, mixed_in_ref, *rest):
    del mixed_in_ref
    if len(rest) == 3:
        w_ref, y_ref, wpad_ref = rest
        _pad_w_in_rows(w_ref, wpad_ref)
    else:
        (y_ref,) = rest
    bq = WINDOW
    n_blk = q_ref.shape[0] // bq
    first = pl.program_id(0) == 0
    k_full = jnp.concatenate([kp_ref[...], kc_ref[...]], axis=0)
    v_full = jnp.concatenate([vp_ref[...], vc_ref[...]], axis=0)
    k_sw_full = pltpu.roll(k_full, SWA_HEAD_DIM, 1).astype(BF16)
    v_sw_full = pltpu.roll(v_full, SWA_HEAD_DIM, 1).astype(BF16)
    k_full = k_full.astype(BF16)
    v_full = v_full.astype(BF16)
    tq = lax.broadcasted_iota(I32, (bq, 2 * bq), 0)
    kj = lax.broadcasted_iota(I32, (bq, 2 * bq), 1)
    band = (kj > tq) & (kj <= tq + WINDOW)
    lane = lax.broadcasted_iota(I32, (bq, LANES), 1)
    lo = lane < SWA_HEAD_DIM
    rep = SWA_HEADS // SWA_KV_HEADS
    for blk in range(n_blk):
        qs = slice(bq * blk, bq * (blk + 1))
        ks = slice(bq * blk, bq * (blk + 2))
        valid = band & (kj >= jnp.where(first, bq, 0)) if blk == 0 else band
        k_all, v_all, k_sw, v_sw = k_full[ks], v_full[ks], k_sw_full[ks], v_sw_full[ks]
        for c in range(SWA_HEADS // 2):
            qc = q_ref[qs, LANES * c:LANES * (c + 1)] * (SWA_HEAD_DIM ** -0.5)
            outs = []
            for hh in range(2):
                h = 2 * c + hh
                g = h // rep
                qm = jnp.where(lo if hh == 0 else jnp.logical_not(lo), qc, 0.0).astype(BF16)
                k_use = k_all if g == hh else k_sw
                v_use = v_all if g == hh else v_sw
                s = lax.dot_general(qm, k_use, (((1,), (1,)), ((), ())), preferred_element_type=F32)
                s = jnp.where(valid, s, -jnp.inf)
                sink = sink_ref[h]
                mx = jnp.maximum(jnp.max(s, axis=-1, keepdims=True), sink)
                p = jnp.exp(s - mx)
                den = jnp.sum(p, axis=-1, keepdims=True) + jnp.exp(sink - mx)
                p = (p / den).astype(BF16)
                outs.append(jnp.dot(p, v_use, preferred_element_type=F32))
            y_ref[qs, LANES * c:LANES * (c + 1)] = jnp.where(lo, outs[0], outs[1]).astype(y_ref.dtype)


def _swa_prompt(proj, mixed, t_len, sinks, w_in_all=None, next_layer=None):
    bq = WINDOW
    rows = _pick_tile(t_len, (SWA_STEP_ROWS, bq))
    ratio = rows // bq
    cur = lambda off: pl.BlockSpec((rows, LANES), lambda i: (i, off // LANES))
    prev = lambda off: pl.BlockSpec((bq, LANES), lambda i: (jnp.maximum(i * ratio - 1, 0), off // LANES))
    in_specs = [
        pl.BlockSpec(memory_space=pltpu.SMEM),
        pl.BlockSpec((rows, GROUP_WIDTH), lambda i: (i, S_Q // GROUP_WIDTH)),
        cur(S_K), cur(S_V), prev(S_K), prev(S_V),
        pl.BlockSpec(memory_space=pl.ANY),
    ]
    out_specs = [pl.BlockSpec((rows, GROUP_WIDTH), lambda i: (i, MIX_COL_SWA))]
    out_shape = [jax.ShapeDtypeStruct(mixed.shape, mixed.dtype)]
    operands = [sinks, proj, proj, proj, proj, proj, mixed]
    if next_layer is not None:
        side = _w_in_side(w_in_all, next_layer, t_len // rows)
        in_specs.append(side.in_spec)
        out_specs.append(side.out_spec)
        out_shape.append(side.out_shape)
        operands.append(side.operand)
    outs = pl.pallas_call(
        _swa_prompt_kernel,
        grid=(t_len // rows,),
        in_specs=in_specs, out_specs=out_specs, out_shape=out_shape,
        input_output_aliases={6: 0},
        compiler_params=_cparams(("arbitrary",)),
        name="swa_prompt",
    )(*operands)
    return outs[0], (outs[1] if next_layer is not None else None)


_T_MK, _T_MQ = 0, 512
_T_GA, _T_GK, _T_GQ = 1024, 1280, 1536
_T_ROWS = 1792


def _sample_kernel(ps_ref, mm_ref, bias_ref, n0_ref, c0_ref, s0_ref, cv0_ref, k0_ref, v0_ref,
                   qm_ref, sink_ref, wg_ref, bg_ref, mng_ref, gng_ref, cw_ref, cb_ref, cg_ref, cbe_ref,
                   mix_ref, c1_ref, n1_ref, m1_ref, s1_ref, cv1_ref, k1_ref, v1_ref,
                   tt_ref, bc_ref, num_ref, go_ref, yc_ref, yd_ref, ybuf, ybf, zbuf, zsem, ysem):
    i = pl.program_id(0)
    nb = ps_ref.shape[0]
    bb = c0_ref.shape[0]
    t_len = mix_ref.shape[0] - nb
    zr = zbuf.shape[0]

    def zero_copy(t):
        return pltpu.make_async_copy(zbuf, mix_ref.at[pl.ds(t * zr, zr)], zsem.at[0])

    def gla_gate(lr):
        gp = jnp.dot(lr.astype(BF16), wg_ref[...], preferred_element_type=F32) + bg_ref[...]
        return jnp.exp(_log_sigmoid(gp) * (1.0 / GLA_TAU))

    @pl.when(i == 0)
    def _():
        zbuf[...] = jnp.zeros_like(zbuf)
        for t in range(t_len // zr):
            zero_copy(t).start()
        for h in range(MLSTM_HEADS):
            kk = ps_ref[:, A_K + HEAD_W * h:A_K + HEAD_W * (h + 1)] * (HEAD_W ** -0.5)
            tt_ref[_T_MK + HEAD_W * h:_T_MK + HEAD_W * (h + 1), :] = kk.T.astype(BF16)
            qq = ps_ref[:, A_Q + HEAD_W * h:A_Q + HEAD_W * (h + 1)]
            tt_ref[_T_MQ + HEAD_W * h:_T_MQ + HEAD_W * (h + 1), :] = qq.T.astype(BF16)
        a_all = gla_gate(ps_ref[:, G_LR:G_LR + LANES])
        for p in range(2):
            pls = slice(LANES * p, LANES * (p + 1))
            tt_ref[_T_GA + LANES * p:_T_GA + LANES * (p + 1), :] = a_all[:, pls].T.astype(BF16)
            kk = ps_ref[:, G_K + LANES * p:G_K + LANES * (p + 1)]
            tt_ref[_T_GK + LANES * p:_T_GK + LANES * (p + 1), :] = kk.T.astype(BF16)
            qq = ps_ref[:, G_Q + LANES * p:G_Q + LANES * (p + 1)] * (GLA_DK ** -0.5)
            tt_ref[_T_GQ + LANES * p:_T_GQ + LANES * (p + 1), :] = qq.T.astype(BF16)

    r0 = pl.multiple_of(i * bb, bb)
    rows = pl.ds(r0, bb)

    pre = ps_ref[rows, A_G:A_G + LANES] + bias_ref[...]
    lfm = _log_sigmoid(pre) + mm_ref[...]
    f_al = pltpu.roll(lfm, LANES - MLSTM_HEADS, 1)
    m_t = jnp.maximum(f_al, pre)
    g_st = jnp.exp(f_al - m_t)
    w_k = jnp.exp(pre - m_t)
    m1_ref[...] = m_t
    n_new = []
    for h in range(MLSTM_HEADS):
        kk = ps_ref[rows, A_K + HEAD_W * h:A_K + HEAD_W * (h + 1)] * (HEAD_W ** -0.5)
        nn = g_st[:, h:h + 1] * n0_ref[:, HEAD_W * h:HEAD_W * (h + 1)] + w_k[:, h:h + 1] * kk
        n1_ref[:, HEAD_W * h:HEAD_W * (h + 1)] = nn
        n_new.append(nn)

    glu = ps_ref[rows, C_A:C_A + GROUP_WIDTH] * _sigmoid(ps_ref[rows, C_G:C_G + GROUP_WIDTH])
    yc_ref[...] = glu * cw_ref[CONV_WIDTH - 1:CONV_WIDTH, :]
    a_v = ps_ref[rows, A_V:A_V + GROUP_WIDTH]
    g_v = ps_ref[rows, G_V:G_V + GROUP_WIDTH]
    s_k = ps_ref[rows, S_K:S_K + LANES]
    s_v = ps_ref[rows, S_V:S_V + LANES]

    lane_b = lax.broadcasted_iota(I32, (nb, LANES), 0)
    key_row = lax.broadcasted_iota(I32, (SWA_HEADS, WINDOW), 1)
    lo_row = lax.broadcasted_iota(I32, (1, LANES), 1) < SWA_HEAD_DIM
    sink_col = sink_ref[:, 0:1]

    for j in range(bb):
        onehot = (lane_b == r0 + j).astype(BF16)
        bc_ref[...] = jnp.dot(tt_ref[...], onehot, preferred_element_type=F32)
        jrow = slice(j, j + 1)
        for h in range(MLSTM_HEADS):
            hs = slice(HEAD_W * h, HEAD_W * (h + 1))
            kbc = bc_ref[_T_MK + HEAD_W * h:_T_MK + HEAD_W * (h + 1), :]
            qbc = bc_ref[_T_MQ + HEAD_W * h:_T_MQ + HEAD_W * (h + 1), :]
            g1 = g_st[jrow, h:h + 1]
            w1 = w_k[jrow, h:h + 1]
            v_row = a_v[jrow, hs]
            c_new = g1 * c0_ref[j, h] + kbc * (w1 * v_row)
            c1_ref[j, h] = c_new
            num_ref[jrow, hs] = jnp.sum(qbc * c_new, axis=0, keepdims=True)
        for p in range(2):
            abc = bc_ref[_T_GA + LANES * p:_T_GA + LANES * (p + 1), :]
            kbc = bc_ref[_T_GK + LANES * p:_T_GK + LANES * (p + 1), :]
            qbc = bc_ref[_T_GQ + LANES * p:_T_GQ + LANES * (p + 1), :]
            for hh in range(2):
                head = 2 * p + hh
                hs = slice(HEAD_W * head, HEAD_W * (head + 1))
                ds_ = slice(GLA_DK * hh, GLA_DK * (hh + 1))
                v_row = g_v[jrow, hs]
                s_new = abc[ds_, :] * s0_ref[j, head] + kbc[ds_, :] * v_row
                s1_ref[j, head] = s_new
                go_ref[jrow, hs] = jnp.sum(qbc[ds_, :] * s_new, axis=0, keepdims=True)
        cache = cv0_ref[j]
        yc_ref[jrow, :] = yc_ref[jrow, :] + jnp.sum(cache * cw_ref[0:CONV_WIDTH - 1, :], axis=0, keepdims=True)
        cv1_ref[j, 0:CONV_WIDTH - 2, :] = cv0_ref[j, 1:CONV_WIDTH - 1, :]
        cv1_ref[j, CONV_WIDTH - 2:CONV_WIDTH - 1, :] = glu[jrow, :]
        k_new = s_k[jrow, :]
        v_new = s_v[jrow, :]
        k1_ref[j, 0:WINDOW - 1, :] = k0_ref[j, 1:WINDOW, :]
        k1_ref[j, WINDOW - 1:WINDOW, :] = k_new
        v1_ref[j, 0:WINDOW - 1, :] = v0_ref[j, 1:WINDOW, :]
        v1_ref[j, WINDOW - 1:WINDOW, :] = v_new
        qmat = qm_ref[j] * (SWA_HEAD_DIM ** -0.5)
        s_old = lax.dot_general(qmat.astype(BF16), k0_ref[j].astype(BF16), (((1,), (1,)), ((), ())),
                                preferred_element_type=F32)
        s_old = jnp.where(key_row >= 1, s_old, -jnp.inf)
        s_cur = jnp.sum(qmat * k_new, axis=-1, keepdims=True)
        mx = jnp.maximum(jnp.maximum(jnp.max(s_old, axis=-1, keepdims=True), s_cur), sink_col)
        p_old = jnp.exp(s_old - mx)
        p_cur = jnp.exp(s_cur - mx)
        den = jnp.sum(p_old, axis=-1, keepdims=True) + p_cur + jnp.exp(sink_col - mx)
        o = jnp.dot((p_old / den).astype(BF16), v0_ref[j].astype(BF16), preferred_element_type=F32)
        o = o + (p_cur / den) * v_new
        o_sw = pltpu.roll(o, SWA_HEAD_DIM, 1)
        for c in range(SWA_HEADS // 2):
            g = c // (SWA_HEADS // SWA_KV_HEADS // 2)
            left = (o if g == 0 else o_sw)[2 * c:2 * c + 1, :]
            right = (o_sw if g == 0 else o)[2 * c + 1:2 * c + 2, :]
            yd_ref[jrow, LANES * c:LANES * (c + 1)] = jnp.where(lo_row, left, right)

    for h in range(MLSTM_HEADS):
        hs = slice(HEAD_W * h, HEAD_W * (h + 1))
        qq = ps_ref[rows, A_Q + HEAD_W * h:A_Q + HEAD_W * (h + 1)]
        den = jnp.sum(qq * n_new[h], axis=-1, keepdims=True)
        hh = num_ref[:, hs] / jnp.maximum(jnp.abs(den), jnp.exp(-m_t[:, h:h + 1]))
        hh = _sigmoid(ps_ref[rows, A_O + HEAD_W * h:A_O + HEAD_W * (h + 1)]) * hh
        hh = hh * lax.rsqrt(jnp.mean(hh * hh, axis=-1, keepdims=True) + NORM_EPS) * mng_ref[:, hs]
        ybuf[rows, hs] = hh
    ybuf[rows, GROUP_WIDTH:2 * GROUP_WIDTH] = _conv_norm_act(yc_ref[...] + cb_ref[...], cg_ref, cbe_ref)
    for head in range(GLA_HEADS):
        hs = slice(HEAD_W * head, HEAD_W * (head + 1))
        oh = go_ref[:, hs]
        oh = oh * lax.rsqrt(jnp.mean(oh * oh, axis=-1, keepdims=True) + NORM_EPS) * gng_ref[:, hs]
        gr = ps_ref[rows, G_R + HEAD_W * head:G_R + HEAD_W * (head + 1)]
        ybuf[rows, 2 * GROUP_WIDTH + HEAD_W * head:2 * GROUP_WIDTH + HEAD_W * (head + 1)] = oh * _silu(gr)
    ybuf[rows, 3 * GROUP_WIDTH:4 * GROUP_WIDTH] = yd_ref[...]

    @pl.when(i == pl.num_programs(0) - 1)
    def _():
        ybf[...] = ybuf[...].astype(BF16)
        cp = pltpu.make_async_copy(ybf, mix_ref.at[pl.ds(t_len, nb)], ysem.at[0])
        cp.start()
        for t in range(t_len // zr):
            zero_copy(t).wait()
        cp.wait()


def _sample_mixers(proj_s, t_len, states, layer, prm):
    c_all, n_all, m_all, s_all, cv_all, k_all, v_all = states
    depth = c_all.shape[0]
    nb = proj_s.shape[0]
    bb = SAMPLE_BB
    assert nb == LANES and nb % bb == 0
    lb = layer * (nb // bb)
    n0, m0 = n_all[layer], m_all[layer]
    mm = jnp.concatenate([m0, m0, jnp.zeros((nb, LANES - 2 * MLSTM_HEADS), F32)], axis=1)
    n0f = n0.reshape(nb, GROUP_WIDTH)
    c0 = c_all.reshape((depth * nb,) + c_all.shape[2:])
    s0 = s_all.reshape((depth * nb,) + s_all.shape[2:])
    cv0 = cv_all.reshape((depth * nb,) + cv_all.shape[2:])
    k0f = k_all.reshape(depth * nb, WINDOW, LANES)
    v0f = v_all.reshape(depth * nb, WINDOW, LANES)
    sq = proj_s[:, S_Q:S_Q + GROUP_WIDTH].reshape(nb, SWA_KV_HEADS, SWA_HEADS // SWA_KV_HEADS, SWA_HEAD_DIM)
    zq = jnp.zeros_like(sq[:, 0])
    qm = jnp.concatenate([jnp.concatenate([sq[:, 0], zq], axis=-1), jnp.concatenate([zq, sq[:, 1]], axis=-1)], axis=1)
    sink_b = jnp.broadcast_to(prm['swa_sinks'].astype(F32)[:, None], (SWA_HEADS, LANES))

    full = lambda shape: pl.BlockSpec(shape, lambda i: (0,) * len(shape))
    rowb = lambda w: pl.BlockSpec((bb, w), lambda i: (i, 0))
    in_specs = [
        full((nb, N_PROJ)), rowb(LANES), full((1, LANES)), rowb(GROUP_WIDTH),
        pl.BlockSpec((bb, MLSTM_HEADS, HEAD_W, HEAD_W), lambda i: (lb + i, 0, 0, 0)),
        pl.BlockSpec((bb, GLA_HEADS, GLA_DK, HEAD_W), lambda i: (lb + i, 0, 0, 0)),
        pl.BlockSpec((bb, CONV_WIDTH - 1, GROUP_WIDTH), lambda i: (lb + i, 0, 0)),
        pl.BlockSpec((bb, WINDOW, LANES), lambda i: (lb + i, 0, 0)),
        pl.BlockSpec((bb, WINDOW, LANES), lambda i: (lb + i, 0, 0)),
        pl.BlockSpec((bb, SWA_HEADS, LANES), lambda i: (i, 0, 0)),
        full((SWA_HEADS, LANES)), full((LANES, 256)), full((1, 256)),
        full((1, GROUP_WIDTH)), full((1, GROUP_WIDTH)),
        full((CONV_WIDTH, GROUP_WIDTH)), full((1, GROUP_WIDTH)), full((1, GROUP_WIDTH)), full((1, GROUP_WIDTH)),
    ]
    out_specs = [
        pl.BlockSpec(memory_space=pl.ANY),
        pl.BlockSpec((bb, MLSTM_HEADS, HEAD_W, HEAD_W), lambda i: (i, 0, 0, 0)),
        rowb(GROUP_WIDTH), rowb(LANES),
        pl.BlockSpec((bb, GLA_HEADS, GLA_DK, HEAD_W), lambda i: (i, 0, 0, 0)),
        pl.BlockSpec((bb, CONV_WIDTH - 1, GROUP_WIDTH), lambda i: (i, 0, 0)),
        pl.BlockSpec((bb, WINDOW, LANES), lambda i: (i, 0, 0)),
        pl.BlockSpec((bb, WINDOW, LANES), lambda i: (i, 0, 0)),
    ]
    out_shape = [
        jax.ShapeDtypeStruct((t_len + nb, D_MODEL), BF16),
        jax.ShapeDtypeStruct(c_all.shape[1:], F32),
        jax.ShapeDtypeStruct((nb, GROUP_WIDTH), F32),
        jax.ShapeDtypeStruct((nb, LANES), F32),
        jax.ShapeDtypeStruct(s_all.shape[1:], F32),
        jax.ShapeDtypeStruct(cv_all.shape[1:], F32),
        jax.ShapeDtypeStruct((nb, WINDOW, LANES), F32),
        jax.ShapeDtypeStruct((nb, WINDOW, LANES), F32),
    ]
    scratch = [
        pltpu.VMEM((_T_ROWS, nb), BF16), pltpu.VMEM((_T_ROWS, LANES), F32),
        pltpu.VMEM((bb, GROUP_WIDTH), F32), pltpu.VMEM((bb, GROUP_WIDTH), F32), pltpu.VMEM((bb, GROUP_WIDTH), F32),
        pltpu.VMEM((bb, GROUP_WIDTH), F32),
        pltpu.VMEM((nb, D_MODEL), F32), pltpu.VMEM((nb, D_MODEL), BF16),
        pltpu.VMEM((_pick_tile(t_len, (512, 256, 128, 64, 16)), D_MODEL), BF16),
        pltpu.SemaphoreType.DMA((1,)), pltpu.SemaphoreType.DMA((1,)),
    ]
    mixed, c1, n1, m1, s1, cv1, k1, v1 = pl.pallas_call(
        _sample_kernel,
        grid=(nb // bb,),
        in_specs=in_specs, out_specs=out_specs, out_shape=out_shape, scratch_shapes=scratch,
        compiler_params=_cparams(("arbitrary",)),
        name="sample_mixers",
    )(proj_s, mm, prm['gate_bias'], n0f, c0, s0, cv0, k0f, v0f, qm, sink_b,
      prm['gla_w_gate_pad'], prm['gla_b_gate'].reshape(1, 256),
      prm['mlstm_norm_g'].reshape(1, -1), prm['gla_norm_g'].reshape(1, -1),
      prm['conv_w'], prm['conv_b'].reshape(1, -1), prm['conv_norm_g'].reshape(1, -1),
      prm['conv_norm_b'].reshape(1, -1))
    new_state = (c1, n1.reshape(n0.shape), m1[:, :MLSTM_HEADS], s1, cv1,
                 k1.reshape(k_all.shape[1:]), v1.reshape(v_all.shape[1:]))
    return mixed, new_state


def _outproj_router_kernel(x_ref, mix_ref, w_ref, g_ref, rw_ref, rb_ref,
                           x1_ref, hn_ref, ri_ref, rf_ref, cnt_ref):
    @pl.when(pl.program_id(0) == 0)
    def _():
        cnt_ref[...] = jnp.zeros_like(cnt_ref)

    counts = _route_rows(x_ref[...], mix_ref[...], w_ref, g_ref, rw_ref, rb_ref, cnt_ref[0:1, :],
                         x1_ref, hn_ref, ri_ref, rf_ref)
    cnt_ref[...] = jnp.broadcast_to(counts, cnt_ref.shape)


def _route_rows(x, mix, w_ref, g_ref, rw_ref, rb_ref, counts, x1_ref, hn_ref, ri_ref, rf_ref):
    tm = x.shape[0]
    x1 = x + jnp.dot(mix, w_ref[...], preferred_element_type=F32)
    x1_ref[...] = x1
    ms = jnp.mean(x1 * x1, axis=-1, keepdims=True)
    hn = x1 * lax.rsqrt(ms + NORM_EPS) * g_ref[...]
    hn_ref[...] = hn
    logits = jnp.dot(hn.astype(BF16), rw_ref[...], preferred_element_type=F32) + rb_ref[...]
    lane = lax.broadcasted_iota(I32, (tm, LANES), 1)
    big = jnp.int32(LANES)
    gl = jnp.where(lane < N_GROUPS, logits, -jnp.inf)
    gmax = jnp.max(gl, axis=-1, keepdims=True)
    g_sel = jnp.min(jnp.where(gl == gmax, lane, big), axis=-1, keepdims=True)
    g_w = 1.0 / jnp.sum(jnp.exp(gl - gmax), axis=-1, keepdims=True)
    e_lane = lane - N_GROUPS
    in_grp = (e_lane >= 0) & (e_lane < N_EXPERTS) & ((e_lane // EXPERTS_PER_GROUP) == g_sel)
    el = jnp.where(in_grp, logits, -jnp.inf)
    m1 = jnp.max(el, axis=-1, keepdims=True)
    i1 = jnp.min(jnp.where(el == m1, lane, big), axis=-1, keepdims=True)
    el2 = jnp.where(lane == i1, -jnp.inf, el)
    m2 = jnp.max(el2, axis=-1, keepdims=True)
    i2 = jnp.min(jnp.where(el2 == m2, lane, big), axis=-1, keepdims=True)
    r = jnp.exp(m2 - m1)
    p1 = 1.0 / (1.0 + r)
    gate1 = g_w * p1
    gate2 = g_w * (r * p1)
    sel1 = lane == i1
    sel2 = lane == i2
    onehot = jnp.where(sel1 | sel2, 1.0, 0.0)
    row = lax.broadcasted_iota(I32, (tm, tm), 0)
    col = lax.broadcasted_iota(I32, (tm, tm), 1)
    strict = jnp.where(col < row, 1.0, 0.0).astype(BF16)
    cum = jnp.dot(strict, onehot.astype(BF16), preferred_element_type=F32) + counts
    rank1 = jnp.sum(jnp.where(sel1, cum, 0.0), axis=-1, keepdims=True).astype(I32)
    rank2 = jnp.sum(jnp.where(sel2, cum, 0.0), axis=-1, keepdims=True).astype(I32)
    ri = jnp.where(lane == 0, i1 - N_GROUPS, jnp.where(lane == 1, i2 - N_GROUPS,
                   jnp.where(lane == 2, rank1, jnp.where(lane == 3, rank2, 0))))
    ri_ref[...] = ri
    rf_ref[...] = jnp.where(lane == 0, gate1, jnp.where(lane == 1, gate2, 0.0))
    return counts + jnp.sum(onehot, axis=0, keepdims=True)


def _outproj_router(x, mixed, w_out_bf16, norm_g, rw_pad, rb_pad):
    n = x.shape[0]
    tm = _pick_tile(n, (320, 256, 128, 64, 16))
    full = lambda shape: pl.BlockSpec(shape, lambda i: (0,) * len(shape))
    rowb = lambda w: pl.BlockSpec((tm, w), lambda i: (i, 0))
    return pl.pallas_call(
        _outproj_router_kernel,
        grid=(n // tm,),
        in_specs=[rowb(D_MODEL), rowb(D_MODEL), full((D_MODEL, D_MODEL)), full((1, D_MODEL)),
                  full((D_MODEL, LANES)), full((1, LANES))],
        out_specs=[rowb(D_MODEL), rowb(D_MODEL), rowb(LANES), rowb(LANES), full((SUBLANES, LANES))],
        out_shape=[
            jax.ShapeDtypeStruct((n, D_MODEL), F32),
            jax.ShapeDtypeStruct((n, D_MODEL), F32),
            jax.ShapeDtypeStruct((n, LANES), I32),
            jax.ShapeDtypeStruct((n, LANES), F32),
            jax.ShapeDtypeStruct((SUBLANES, LANES), F32),
        ],
        compiler_params=_cparams(("arbitrary",)),
        name="outproj_router",
    )(x, mixed, w_out_bf16, norm_g.reshape(1, D_MODEL), rw_pad, rb_pad)


def _dispatch_kernel(pos_ref, zt_ref, hn_ref, xs_ref, zbuf, sem, zsem):
    tm = hn_ref.shape[0]
    tile = zbuf.shape[0]
    base = pl.program_id(0) * tm

    @pl.when(pl.program_id(0) == 0)
    def _():
        zbuf[...] = jnp.zeros_like(zbuf)

        def zero_copy(k):
            row = pl.multiple_of(zt_ref[k] * tile, tile)
            return pltpu.make_async_copy(zbuf, xs_ref.at[pl.ds(row, tile)], zsem.at[0])

        def start_zero(k, carry):
            @pl.when(zt_ref[k] >= 0)
            def _():
                zero_copy(k).start()
            return carry

        def wait_zero(k, carry):
            @pl.when(zt_ref[k] >= 0)
            def _():
                zero_copy(k).wait()
            return carry

        lax.fori_loop(0, zt_ref.shape[0], start_zero, 0)
        lax.fori_loop(0, zt_ref.shape[0], wait_zero, 0)

    def issue(r, carry):
        tok = base + r
        src = hn_ref.at[pl.ds(r, 1)]
        pltpu.make_async_copy(src, xs_ref.at[pl.ds(pos_ref[2 * tok], 1)], sem.at[0]).start(priority=0)
        pltpu.make_async_copy(src, xs_ref.at[pl.ds(pos_ref[2 * tok + 1], 1)], sem.at[1]).start(priority=1)
        return carry

    lax.fori_loop(0, tm, issue, 0, unroll=4)
    pltpu.make_async_copy(hn_ref, xs_ref.at[pl.ds(0, tm)], sem.at[0]).wait()
    pltpu.make_async_copy(hn_ref, xs_ref.at[pl.ds(0, tm)], sem.at[1]).wait()


def _dispatch(pos, zero_tiles, hn, n_rows, tile):
    n = hn.shape[0]
    tm = _pick_tile(n, ROW_DMA_TILES)
    return pl.pallas_call(
        _dispatch_kernel,
        grid_spec=pltpu.PrefetchScalarGridSpec(
            num_scalar_prefetch=2,
            grid=(n // tm,),
            in_specs=[pl.BlockSpec((tm, D_MODEL), lambda i, p, z: (i, 0))],
            out_specs=pl.BlockSpec(memory_space=pl.ANY),
            scratch_shapes=[pltpu.VMEM((tile, D_MODEL), F32), pltpu.SemaphoreType.DMA((2,)),
                            pltpu.SemaphoreType.DMA((1,))],
        ),
        out_shape=jax.ShapeDtypeStruct((n_rows, D_MODEL), F32),
        compiler_params=_cparams(("arbitrary",)),
        name="moe_dispatch",
    )(pos, zero_tiles, hn)


def _expert_kernel(te_ref, nt_ref, x_ref, wg_ref, wu_ref, wd_ref, y_ref):
    del te_ref
    used = pl.program_id(0) < nt_ref[0]

    @pl.when(used)
    def _():
        x = x_ref[...].astype(BF16)
        a = jnp.dot(x, wg_ref[0], preferred_element_type=F32)
        u = jnp.dot(x, wu_ref[0], preferred_element_type=F32)
        hmid = (_silu(a) * u).astype(BF16)
        y_ref[...] = jnp.dot(hmid, wd_ref[0], preferred_element_type=F32)

    @pl.when(jnp.logical_not(used))
    def _():
        y_ref[...] = jnp.zeros_like(y_ref)


def _expert_mlp(tile_expert, n_tiles_used, xs, wg, wu, wd, tile):
    n_tiles = xs.shape[0] // tile

    def row_map(t, te, nt):
        return (jnp.minimum(t, nt[0] - 1), 0)

    def w_map(t, te, nt):
        return (te[jnp.minimum(t, nt[0] - 1)], 0, 0)

    return pl.pallas_call(
        _expert_kernel,
        grid_spec=pltpu.PrefetchScalarGridSpec(
            num_scalar_prefetch=2,
            grid=(n_tiles,),
            in_specs=[
                pl.BlockSpec((tile, D_MODEL), row_map),
                pl.BlockSpec((1, D_MODEL, D_EXPERT), w_map),
                pl.BlockSpec((1, D_MODEL, D_EXPERT), w_map),
                pl.BlockSpec((1, D_EXPERT, D_MODEL), w_map),
            ],
            out_specs=pl.BlockSpec((tile, D_MODEL), lambda t, te, nt: (t, 0)),
        ),
        out_shape=jax.ShapeDtypeStruct(xs.shape, F32),
        compiler_params=_cparams(("arbitrary",)),
        name="expert_mlp",
    )(tile_expert, n_tiles_used, xs, wg, wu, wd)


def _combine_kernel(pos_ref, x1_ref, rf_ref, fg_ref, ys_ref, o_ref, buf_a, buf_b, sem, *, final_norm, row0):
    tm = x1_ref.shape[0]
    step = pl.program_id(0)
    n_steps = pl.num_programs(0)

    def gather(tile_idx, slot):
        base = row0 + tile_idx * tm

        def issue(r, carry):
            tok = base + r
            pltpu.make_async_copy(ys_ref.at[pl.ds(pos_ref[2 * tok], 1)], buf_a.at[slot, pl.ds(r, 1)],
                                  sem.at[slot, 0]).start(priority=0)
            pltpu.make_async_copy(ys_ref.at[pl.ds(pos_ref[2 * tok + 1], 1)], buf_b.at[slot, pl.ds(r, 1)],
                                  sem.at[slot, 1]).start(priority=1)
            return carry

        lax.fori_loop(0, tm, issue, 0, unroll=4)

    @pl.when(step == 0)
    def _():
        gather(0, 0)

    @pl.when(step + 1 < n_steps)
    def _():
        gather(step + 1, (step + 1) % 2)

    slot = step % 2
    pltpu.make_async_copy(ys_ref.at[pl.ds(0, tm)], buf_a.at[slot], sem.at[slot, 0]).wait()
    pltpu.make_async_copy(ys_ref.at[pl.ds(0, tm)], buf_b.at[slot], sem.at[slot, 1]).wait()
    rf = rf_ref[...]
    x2 = x1_ref[...] + rf[:, 0:1] * buf_a[slot] + rf[:, 1:2] * buf_b[slot]
    if final_norm:
        ms = jnp.mean(x2 * x2, axis=-1, keepdims=True)
        x2 = x2 * lax.rsqrt(ms + NORM_EPS) * fg_ref[...]
    o_ref[...] = x2


def _combine(pos, x1, rf, ys, final_g, final_norm, row0=0, n_rows=None):
    n = x1.shape[0]
    n_rows = n if n_rows is None else n_rows
    tm = _pick_tile(n_rows, ROW_DMA_TILES)
    assert row0 % tm == 0
    b0 = row0 // tm
    return pl.pallas_call(
        functools.partial(_combine_kernel, final_norm=final_norm, row0=row0),
        grid_spec=pltpu.PrefetchScalarGridSpec(
            num_scalar_prefetch=1,
            grid=(n_rows // tm,),
            in_specs=[
                pl.BlockSpec((tm, D_MODEL), lambda i, p: (b0 + i, 0)),
                pl.BlockSpec((tm, LANES), lambda i, p: (b0 + i, 0)),
                pl.BlockSpec((1, D_MODEL), lambda i, p: (0, 0)),
                pl.BlockSpec(memory_space=pl.ANY),
            ],
            out_specs=pl.BlockSpec((tm, D_MODEL), lambda i, p: (i, 0)),
            scratch_shapes=[pltpu.VMEM((2, tm, D_MODEL), F32), pltpu.VMEM((2, tm, D_MODEL), F32),
                            pltpu.SemaphoreType.DMA((2, 2))],
        ),
        out_shape=jax.ShapeDtypeStruct((n_rows, D_MODEL), F32),
        compiler_params=_cparams(("arbitrary",)),
        name="moe_combine",
    )(pos, x1, rf, final_g.reshape(1, D_MODEL), ys)


W_IN_COL_BLOCK = LANES


def _pad_w_in_rows(w_ref, o_ref):
    cols = o_ref.shape[1]
    at = 0
    for lo, hi, dst in sorted(_W_IN_SEGMENTS, key=lambda s: s[2]):
        if dst > at:
            o_ref[at:dst, :] = jnp.zeros((dst - at, cols), o_ref.dtype)
        o_ref[dst:dst + (hi - lo), :] = w_ref[0, lo:hi, :].astype(o_ref.dtype)
        at = dst + (hi - lo)
    if at < N_PROJ:
        o_ref[at:N_PROJ, :] = jnp.zeros((N_PROJ - at, cols), o_ref.dtype)


def _w_in_side(w_in_t, layer, n_steps):
    depth, n_in, d = w_in_t.shape
    n_blocks = min(n_steps, d // W_IN_COL_BLOCK)
    assert d % (n_blocks * LANES) == 0
    cols = d // n_blocks
    blk = lambda i: jnp.minimum(i, n_blocks - 1)
    return _SideCast(
        operand=w_in_t,
        in_spec=pl.BlockSpec((1, n_in, cols), lambda i: (layer, 0, blk(i))),
        out_spec=pl.BlockSpec((N_PROJ, cols), lambda i: (0, blk(i))),
        out_shape=jax.ShapeDtypeStruct((N_PROJ, d), BF16),
        shape=(N_PROJ, d),
    )


def _pad_w_in(w_in_t, layer):
    n_steps = w_in_t.shape[2] // W_IN_COL_BLOCK
    side = _w_in_side(w_in_t, layer, n_steps)
    return pl.pallas_call(
        _pad_w_in_rows,
        grid=(n_steps,),
        in_specs=[side.in_spec], out_specs=side.out_spec, out_shape=side.out_shape,
        compiler_params=_cparams(("arbitrary",)),
        name="pad_w_in",
    )(side.operand)


def _moe(x1, hn, ri, rf, counts, experts, final_g, final_norm, tile, t_len):
    n = x1.shape[0]
    n_tiles = -(-2 * n // tile) + N_EXPERTS
    cnt = counts[0, N_GROUPS:N_GROUPS + N_EXPERTS].astype(I32)
    tiles_per = (cnt + tile - 1) // tile
    tile_end = jnp.cumsum(tiles_per)
    row_off = (tile_end - tiles_per) * tile
    pos = (row_off[ri[:, 0:2]] + ri[:, 2:4]).reshape(2 * n)
    tile_ids = jnp.arange(n_tiles, dtype=I32)
    tile_expert = jnp.minimum(jnp.sum((tile_ids[:, None] >= tile_end[None, :]).astype(I32), axis=1), N_EXPERTS - 1)
    n_used = tile_end[N_EXPERTS - 1:N_EXPERTS].astype(I32)
    last_tile = jnp.where(tiles_per > 0, tile_end - 1, -1).astype(I32)
    zero_tiles = jnp.concatenate([last_tile, jnp.where(tile_ids >= n_used[0], tile_ids, -1)])
    xs = _dispatch(pos, zero_tiles, hn, n_tiles * tile, tile)
    ys = _expert_mlp(tile_expert, n_used, xs, *experts, tile)
    if final_norm:
        return (_combine(pos, x1, rf, ys, final_g, True, 0, t_len),
                _combine(pos, x1, rf, ys, final_g, True, t_len, n - t_len))
    return _combine(pos, x1, rf, ys, final_g, False)


def _layer(x, t_len, states, layer, prm, w_in_pad, final_g, final_norm, moe_tile):
    proj = _in_projection(x, prm['norm_mix_g'], w_in_pad)
    mixed, new_s = _sample_mixers(proj[t_len:], t_len, states, layer, prm)
    e_shape = prm['expert_w_gate_all'].shape[1:]
    mixed, cn, m_p, wg = _mlstm_prompt(proj, mixed, t_len, prm['gate_bias'], prm['mlstm_norm_g'],
                                       prm['expert_w_gate_all'], layer)
    mixed, conv_tail, wu = _conv_prompt(proj, mixed, t_len, prm['conv_w'], prm['conv_b'], prm['conv_norm_g'],
                                        prm['conv_norm_b'], prm['expert_w_up_all'], layer)
    mixed, sp, wd = _gla_prompt(proj, mixed, t_len, prm['gla_w_gate_pad'], prm['gla_b_gate'], prm['gla_norm_g'],
                                prm['expert_w_down_all'], layer)
    mixed, next_w_in_pad = _swa_prompt(proj, mixed, t_len, prm['swa_sinks'], prm['w_in_all'],
                                       None if final_norm else layer + 1)
    experts = (wg.reshape(e_shape), wu.reshape(e_shape), wd.reshape(prm['expert_w_down_all'].shape[1:]))
    x1, hn, ri, rf, counts = _outproj_router(x, mixed, prm['w_out'], prm['norm_ffn_g'], prm['router_w'], prm['router_b'])
    x2 = _moe(x1, hn, ri, rf, counts, experts, final_g, final_norm, moe_tile, t_len)
    p_c = cn[None, :, :, :HEAD_W]
    p_n = cn[None, :, :, HEAD_W]
    p_m = m_p[None, :MLSTM_HEADS, 0]
    p_s = jnp.stack([sp[0, :GLA_DK, :HEAD_W], sp[0, GLA_DK:, HEAD_W:],
                     sp[1, :GLA_DK, :HEAD_W], sp[1, GLA_DK:, HEAD_W:]])[None]
    p_conv = conv_tail[None, CONV_HALO - (CONV_WIDTH - 1):]
    p_k = proj[t_len - WINDOW:t_len, S_K:S_K + LANES].reshape(1, WINDOW, SWA_KV_HEADS, SWA_HEAD_DIM)
    p_v = proj[t_len - WINDOW:t_len, S_V:S_V + LANES].reshape(1, WINDOW, SWA_KV_HEADS, SWA_HEAD_DIM)
    return x2, (p_c, p_n, p_m, p_s, p_conv, p_k, p_v), new_s, next_w_in_pad


def _forward(x_prompt, x_sample, states, layer_params, final_norm_g, moe_tile=MOE_TILE):
    t_len = x_prompt.shape[1]
    x = jnp.concatenate([x_prompt[0], x_sample[:, 0]], axis=0)
    new_p, new_s = [], []
    depth = len(layer_params)
    w_in_pad = _pad_w_in(layer_params[0]['w_in_all'], 0)
    for l, prm in enumerate(layer_params):
        x, sp, ss, w_in_pad = _layer(x, t_len, states, l, prm, w_in_pad, final_norm_g, l == depth - 1, moe_tile)
        new_p.append(sp)
        new_s.append(ss)
    y_prompt, y_sample = x
    y_prompt = y_prompt[None]
    y_sample = y_sample[:, None]
    p_states = [jnp.stack(parts) for parts in zip(*new_p)]
    s_states = [jnp.stack(parts) for parts in zip(*new_s)]
    return (y_prompt, y_sample, *p_states, *s_states)


def _prep_layer_params(l, norm_mix_g, w_in, mlstm_b_i, mlstm_b_f, mlstm_norm_g, conv_w, conv_b, conv_norm_g,
                       conv_norm_b, gla_w_gate, gla_b_gate, gla_norm_g, swa_sinks, w_out, norm_ffn_g,
                       router_group_w, router_group_b, router_expert_w, router_expert_b, expert_w_gate,
                       expert_w_up, expert_w_down):
    gate_bias = jnp.concatenate([mlstm_b_i[l], mlstm_b_f[l], jnp.zeros((LANES - 2 * MLSTM_HEADS,), F32)])
    rw = jnp.concatenate([router_group_w[l], router_expert_w[l],
                          jnp.zeros((D_MODEL, LANES - N_GROUPS - N_EXPERTS), F32)], axis=1)
    rb = jnp.concatenate([router_group_b[l], router_expert_b[l],
                          jnp.zeros((LANES - N_GROUPS - N_EXPERTS,), F32)])
    return {
        'norm_mix_g': norm_mix_g[l],
        'w_in_all': jnp.swapaxes(w_in, 1, 2),
        'gate_bias': gate_bias.reshape(1, LANES),
        'mlstm_norm_g': mlstm_norm_g[l],
        'conv_w': conv_w[l], 'conv_b': conv_b[l], 'conv_norm_g': conv_norm_g[l], 'conv_norm_b': conv_norm_b[l],
        'gla_w_gate_pad': jnp.concatenate(
            [gla_w_gate[l], jnp.zeros((LANES - GLA_LOWRANK, GLA_HEADS * GLA_DK), F32)], axis=0).astype(BF16),
        'gla_b_gate': gla_b_gate[l], 'gla_norm_g': gla_norm_g[l],
        'swa_sinks': swa_sinks[l],
        'w_out': w_out[l].astype(BF16),
        'norm_ffn_g': norm_ffn_g[l],
        'router_w': rw.astype(BF16), 'router_b': rb.reshape(1, LANES),
        'expert_w_gate_all': expert_w_gate, 'expert_w_up_all': expert_w_up, 'expert_w_down_all': expert_w_down,
    }


def kernel(x_prompt, x_sample, state_mlstm_C, state_mlstm_n, state_mlstm_m, state_gla_S, cache_conv, cache_swa_k, cache_swa_v, norm_mix_g, w_in, mlstm_b_i, mlstm_b_f, mlstm_norm_g, conv_w, conv_b, conv_norm_g, conv_norm_b, gla_w_gate, gla_b_gate, gla_norm_g, swa_sinks, w_out, norm_ffn_g, router_group_w, router_group_b, router_expert_w, router_expert_b, expert_w_gate, expert_w_up, expert_w_down, final_norm_g):
    depth = w_in.shape[0]
    weights = (norm_mix_g, w_in, mlstm_b_i, mlstm_b_f, mlstm_norm_g, conv_w, conv_b, conv_norm_g, conv_norm_b,
               gla_w_gate, gla_b_gate, gla_norm_g, swa_sinks, w_out, norm_ffn_g, router_group_w, router_group_b,
               router_expert_w, router_expert_b, expert_w_gate, expert_w_up, expert_w_down)
    layer_params = [_prep_layer_params(l, *weights) for l in range(depth)]
    states = (state_mlstm_C, state_mlstm_n, state_mlstm_m, state_gla_S, cache_conv, cache_swa_k, cache_swa_v)
    return _forward(x_prompt, x_sample, states, layer_params, final_norm_g)
```
